```python
import math
import jax
import jax.numpy as jnp
from jax import lax
import numpy as np

D_MODEL = 2048
BATCH = 8
SEQ = 4096
DEPTH = 2

RMS_EPS = 1e-6
Q_BLOCK = 128

SSD_HEADS = 32
SSD_HEAD_DIM = 64
SSD_WIDTH = SSD_HEADS * SSD_HEAD_DIM
SSD_GROUPS = 4
SSD_STATE = 128
SSD_CONV = 4
SSD_CHUNK = 128
SSD_CONV_CH = SSD_WIDTH + 2 * SSD_GROUPS * SSD_STATE

MLA_HEADS = 16
MLA_Q_RANK = 512
MLA_KV_RANK = 512
MLA_NOPE = 128
MLA_ROPE = 64
MLA_V = 128
MLA_WIDTH = MLA_HEADS * MLA_V
ROPE_THETA = 10000.0

SB_HEADS = 16
SB_HEAD_DIM = 128
SB_WIDTH = SB_HEADS * SB_HEAD_DIM

FFN_HIDDEN = -(-(8 * D_MODEL) // (3 * 256)) * 256

IN_SPLITS = (SSD_WIDTH, SSD_CONV_CH, SSD_HEADS, MLA_Q_RANK, MLA_KV_RANK, MLA_ROPE)
IN_WIDTH = sum(IN_SPLITS)
IN_OFFSETS = tuple(int(v) for v in np.cumsum(IN_SPLITS)[:-1])
MIX_WIDTH = SSD_WIDTH + MLA_WIDTH

N_EVEN = (DEPTH + 1) // 2
N_ODD = DEPTH // 2

kernel_name = 'hybrid_ssd_mla_stickbreaking_block'


def rms_norm(x, g):
    xf = x.astype(jnp.float32)
    y = xf * lax.rsqrt(jnp.mean(xf * xf, axis=-1, keepdims=True) + RMS_EPS)
    return (y * g.astype(jnp.float32)).astype(x.dtype)


def rope_tables(seq, dtype):
    half = MLA_ROPE // 2
    inv_freq = ROPE_THETA ** (-jnp.arange(half, dtype=jnp.float32) / half)
    ang = jnp.arange(seq, dtype=jnp.float32)[:, None] * inv_freq[None, :]
    return jnp.cos(ang).astype(dtype), jnp.sin(ang).astype(dtype)


def apply_rope(x, cos, sin):
    x1, x2 = jnp.split(x, 2, axis=-1)
    return jnp.concatenate([x1 * cos - x2 * sin, x1 * sin + x2 * cos], axis=-1)


def swiglu_ffn(h, w_gate, w_up, w_down):
    return (jax.nn.silu(h @ w_gate) * (h @ w_up)) @ w_down


def ssd_chunked_scan(x, dt, a, b_in, c_in):
    bsz, seq, nh, hd = x.shape
    g = SSD_GROUPS
    hpg = nh // g
    l = SSD_CHUNK
    nc = seq // l
    xdt = (x * dt[..., None]).reshape(bsz, nc, l, g, hpg, hd)
    adt = (dt * a).reshape(bsz, nc, l, g, hpg).transpose(0, 3, 4, 1, 2)
    a_cs = jnp.cumsum(adt, axis=-1)
    bc = b_in.reshape(bsz, nc, l, g, SSD_STATE)
    cc = c_in.reshape(bsz, nc, l, g, SSD_STATE)
    causal = jnp.tril(jnp.ones((l, l), dtype=bool))
    seg = a_cs[..., :, None] - a_cs[..., None, :]
    decay = jnp.exp(jnp.where(causal, seg, -jnp.inf))
    cb = jnp.einsum('bclgn,bcsgn->bgcls', cc, bc)
    y_diag = jnp.einsum('bghcls,bcsghp->bclghp', cb[:, :, None] * decay, xdt)
    decay_to_end = jnp.exp(a_cs[..., -1:] - a_cs)
    states = jnp.einsum('bclgn,bghcl,bclghp->bcghpn', bc, decay_to_end, xdt)
    chunk_decay = jnp.exp(a_cs[..., -1])

    def step(carry, inp):
        s_c, d_c = inp
        return carry * d_c[..., None, None] + s_c, carry

    init = jnp.zeros(states.shape[:1] + states.shape[2:], states.dtype)
    _, prev = lax.scan(step, init, (jnp.moveaxis(states, 1, 0), jnp.moveaxis(chunk_decay, 3, 0)))
    prev = jnp.moveaxis(prev, 0, 1)
    y_off = jnp.einsum('bclgn,bcghpn,bghcl->bclghp', cc, prev, jnp.exp(a_cs))
    return (y_diag + y_off).reshape(bsz, seq, nh, hd)


def ssd_mixer(z, xbc, dt_raw, conv_w, conv_b, dt_bias, a_log, d_skip, norm_g):
    bsz, seq, _ = xbc.shape
    xbc = lax.conv_general_dilated(
        xbc, conv_w[:, None, :].astype(xbc.dtype), window_strides=(1,),
        padding=[(SSD_CONV - 1, 0)], dimension_numbers=('NWC', 'WIO', 'NWC'),
        feature_group_count=SSD_CONV_CH)
    xbc = jax.nn.silu(xbc + conv_b).astype(jnp.float32)
    x_ssm, b_in, c_in = jnp.split(xbc, [SSD_WIDTH, SSD_WIDTH + SSD_GROUPS * SSD_STATE], axis=-1)
    x_ssm = x_ssm.reshape(bsz, seq, SSD_HEADS, SSD_HEAD_DIM)
    b_in = b_in.reshape(bsz, seq, SSD_GROUPS, SSD_STATE)
    c_in = c_in.reshape(bsz, seq, SSD_GROUPS, SSD_STATE)
    dt = jax.nn.softplus(dt_raw.astype(jnp.float32) + dt_bias.astype(jnp.float32))
    a = -jnp.exp(a_log.astype(jnp.float32))
    y = ssd_chunked_scan(x_ssm, dt, a, b_in, c_in) + d_skip.astype(jnp.float32)[:, None] * x_ssm
    y = y.reshape(bsz, seq, SSD_WIDTH)
    return rms_norm(y * jax.nn.silu(z.astype(jnp.float32)), norm_g).astype(z.dtype)


def mla_mixer(c_q, c_kv, k_rope, q_norm, kv_norm, w_uq, w_ukv, cos, sin):
    bsz, seq, _ = c_q.shape
    q = (rms_norm(c_q, q_norm) @ w_uq).reshape(bsz, seq, MLA_HEADS, MLA_NOPE + MLA_ROPE)
    kv = (rms_norm(c_kv, kv_norm) @ w_ukv).reshape(bsz, seq, MLA_HEADS, MLA_NOPE + MLA_V)
    q_nope, q_pe = jnp.split(q, [MLA_NOPE], axis=-1)
    k_nope, v = jnp.split(kv, [MLA_NOPE], axis=-1)
    q_pe = apply_rope(q_pe, cos[:, None, :], sin[:, None, :])
    k_pe = apply_rope(k_rope, cos, sin)
    scale = (MLA_NOPE + MLA_ROPE) ** -0.5
    outs = []
    for i in range(seq // Q_BLOCK):
        q0, q1 = i * Q_BLOCK, (i + 1) * Q_BLOCK
        s = (jnp.einsum('bqhd,bkhd->bhqk', q_nope[:, q0:q1], k_nope[:, :q1])
             + jnp.einsum('bqhr,bkr->bhqk', q_pe[:, q0:q1], k_pe[:, :q1])).astype(jnp.float32) * scale
        mask = jnp.arange(q1)[None, :] <= jnp.arange(q0, q1)[:, None]
        p = jax.nn.softmax(jnp.where(mask, s, -jnp.inf), axis=-1)
        outs.append(jnp.einsum('bhqk,bkhd->bqhd', p.astype(v.dtype), v[:, :q1]))
    return jnp.concatenate(outs, axis=1).reshape(bsz, seq, MLA_WIDTH)


def ssd_mla_mixer(h, w_in, conv_w, conv_b, dt_bias, a_log, d_skip, ssd_norm,
                  q_norm, kv_norm, w_uq, w_ukv, w_out, cos, sin):
    z, xbc, dt_raw, c_q, c_kv, k_rope = jnp.split(h @ w_in, IN_OFFSETS, axis=-1)
    y_ssd = ssd_mixer(z, xbc, dt_raw, conv_w, conv_b, dt_bias, a_log, d_skip, ssd_norm)
    y_mla = mla_mixer(c_q, c_kv, k_rope, q_norm, kv_norm, w_uq, w_ukv, cos, sin)
    return jnp.concatenate([y_ssd, y_mla], axis=-1) @ w_out


def stick_breaking_mixer(h, w_qkv, w_out):
    bsz, seq, _ = h.shape
    qkv = (h @ w_qkv).reshape(bsz, seq, 3, SB_HEADS, SB_HEAD_DIM)
    q, k, v = qkv[:, :, 0], qkv[:, :, 1], qkv[:, :, 2]
    scale = SB_HEAD_DIM ** -0.5
    outs = []
    for i in range(seq // Q_BLOCK):
        q0, q1 = i * Q_BLOCK, (i + 1) * Q_BLOCK
        z = jnp.einsum('bqhd,bkhd->bhqk', q[:, q0:q1], k[:, :q1]).astype(jnp.float32) * scale
        mask = jnp.arange(q1)[None, :] < jnp.arange(q0, q1)[:, None]
        log_keep = jnp.where(mask, -jax.nn.softplus(z), 0.0)
        later = lax.cumsum(log_keep, axis=log_keep.ndim - 1, reverse=True) - log_keep
        weight = jnp.where(mask, jnp.exp(jax.nn.log_sigmoid(z) + later), 0.0)
        outs.append(jnp.einsum('bhqk,bkhd->bqhd', weight.astype(v.dtype), v[:, :q1]))
    return jnp.concatenate(outs, axis=1).reshape(bsz, seq, SB_WIDTH) @ w_out


def _fwd_setup_inputs(seed: int = 0) -> dict:
    key = jax.random.key(seed)
    ks = jax.random.split(key, 24)
    f32 = jnp.float32

    def dense(k, shape, fan_in):
        return jax.random.normal(k, shape, f32) * fan_in ** -0.5

    def gain(k, shape):
        return 1.0 + 0.02 * jax.random.normal(k, shape, f32)

    dt0 = jnp.exp(jax.random.uniform(ks[6], (N_EVEN, SSD_HEADS), f32, math.log(1e-3), math.log(1e-1)))
    return {
        'x': jax.random.normal(ks[0], (BATCH, SEQ, D_MODEL), f32),
        'mix_norm': gain(ks[1], (DEPTH, D_MODEL)),
        'ffn_norm': gain(ks[2], (DEPTH, D_MODEL)),
        'w_in': dense(ks[3], (N_EVEN, D_MODEL, IN_WIDTH), D_MODEL),
        'conv_w': dense(ks[4], (N_EVEN, SSD_CONV, SSD_CONV_CH), SSD_CONV),
        'conv_b': 0.02 * jax.random.normal(ks[5], (N_EVEN, SSD_CONV_CH), f32),
        'dt_bias': dt0 + jnp.log(-jnp.expm1(-dt0)),
        'a_log': jnp.log(jax.random.uniform(ks[7], (N_EVEN, SSD_HEADS), f32, 1.0, 16.0)),
        'd_skip': gain(ks[8], (N_EVEN, SSD_HEADS)),
        'ssd_norm': gain(ks[9], (N_EVEN, SSD_WIDTH)),
        'q_norm': gain(ks[10], (N_EVEN, MLA_Q_RANK)),
        'kv_norm': gain(ks[11], (N_EVEN, MLA_KV_RANK)),
        'w_uq': dense(ks[12], (N_EVEN, MLA_Q_RANK, MLA_HEADS * (MLA_NOPE + MLA_ROPE)), MLA_Q_RANK),
        'w_ukv': dense(ks[13], (N_EVEN, MLA_KV_RANK, MLA_HEADS * (MLA_NOPE + MLA_V)), MLA_KV_RANK),
        'w_out_even': dense(ks[14], (N_EVEN, MIX_WIDTH, D_MODEL), MIX_WIDTH),
        'w_qkv': dense(ks[15], (N_ODD, D_MODEL, 3 * SB_WIDTH), D_MODEL),
        'w_out_odd': dense(ks[16], (N_ODD, SB_WIDTH, D_MODEL), SB_WIDTH),
        'w_gate': dense(ks[17], (DEPTH, D_MODEL, FFN_HIDDEN), D_MODEL),
        'w_up': dense(ks[18], (DEPTH, D_MODEL, FFN_HIDDEN), D_MODEL),
        'w_down': dense(ks[19], (DEPTH, FFN_HIDDEN, D_MODEL), FFN_HIDDEN),
        'final_norm': gain(ks[20], (D_MODEL,)),
    }


def _fwd_reference(x, mix_norm, ffn_norm, w_in, conv_w, conv_b, dt_bias, a_log, d_skip, ssd_norm,
              q_norm, kv_norm, w_uq, w_ukv, w_out_even, w_qkv, w_out_odd,
              w_gate, w_up, w_down, final_norm):
    cos, sin = rope_tables(x.shape[1], x.dtype)
    for layer in range(DEPTH):
        h = rms_norm(x, mix_norm[layer])
        if layer % 2 == 0:
            e = layer // 2
            mix = ssd_mla_mixer(h, w_in[e], conv_w[e], conv_b[e], dt_bias[e], a_log[e], d_skip[e],
                                ssd_norm[e], q_norm[e], kv_norm[e], w_uq[e], w_ukv[e],
                                w_out_even[e], cos, sin)
        else:
            o = layer // 2
            mix = stick_breaking_mixer(h, w_qkv[o], w_out_odd[o])
        x = x + mix
        x = x + swiglu_ffn(rms_norm(x, ffn_norm[layer]), w_gate[layer], w_up[layer], w_down[layer])
    return rms_norm(x, final_norm)


import jax as _jax
import jax.numpy as _jnp

TWIN_FORMAT = 'train_step'
FWD_PARAMS = ['x', 'mix_norm', 'ffn_norm', 'w_in', 'conv_w', 'conv_b', 'dt_bias', 'a_log', 'd_skip', 'ssd_norm', 'q_norm', 'kv_norm', 'w_uq', 'w_ukv', 'w_out_even', 'w_qkv', 'w_out_odd', 'w_gate', 'w_up', 'w_down', 'final_norm']
TWIN_WEIGHTS = ['mix_norm', 'ffn_norm', 'w_in', 'conv_w', 'conv_b', 'dt_bias', 'a_log', 'd_skip', 'ssd_norm', 'q_norm', 'kv_norm', 'w_uq', 'w_ukv', 'w_out_even', 'w_qkv', 'w_out_odd', 'w_gate', 'w_up', 'w_down', 'final_norm']
TWIN_DIFF_INPUT = 'x'
TWIN_INPUTS = ['x', 'mix_norm', 'ffn_norm', 'w_in', 'conv_w', 'conv_b', 'dt_bias', 'a_log', 'd_skip', 'ssd_norm', 'q_norm', 'kv_norm', 'w_uq', 'w_ukv', 'w_out_even', 'w_qkv', 'w_out_odd', 'w_gate', 'w_up', 'w_down', 'final_norm', 'loss_target', 'm_mix_norm', 'm_ffn_norm', 'm_w_in', 'm_conv_w', 'm_conv_b', 'm_dt_bias', 'm_a_log', 'm_d_skip', 'm_ssd_norm', 'm_q_norm', 'm_kv_norm', 'm_w_uq', 'm_w_ukv', 'm_w_out_even', 'm_w_qkv', 'm_w_out_odd', 'm_w_gate', 'm_w_up', 'm_w_down', 'm_final_norm', 'v_mix_norm', 'v_ffn_norm', 'v_w_in', 'v_conv_w', 'v_conv_b', 'v_dt_bias', 'v_a_log', 'v_d_skip', 'v_ssd_norm', 'v_q_norm', 'v_kv_norm', 'v_w_uq', 'v_w_ukv', 'v_w_out_even', 'v_w_qkv', 'v_w_out_odd', 'v_w_gate', 'v_w_up', 'v_w_down', 'v_final_norm']
TWIN_OUTPUTS = ['loss', 'grad_x', 'grad_mix_norm', 'grad_ffn_norm', 'grad_w_in', 'grad_conv_w', 'grad_conv_b', 'grad_dt_bias', 'grad_a_log', 'grad_d_skip', 'grad_ssd_norm', 'grad_q_norm', 'grad_kv_norm', 'grad_w_uq', 'grad_w_ukv', 'grad_w_out_even', 'grad_w_qkv', 'grad_w_out_odd', 'grad_w_gate', 'grad_w_up', 'grad_w_down', 'grad_final_norm', 'delta_mix_norm', 'delta_ffn_norm', 'delta_w_in', 'delta_conv_w', 'delta_conv_b', 'delta_dt_bias', 'delta_a_log', 'delta_d_skip', 'delta_ssd_norm', 'delta_q_norm', 'delta_kv_norm', 'delta_w_uq', 'delta_w_ukv', 'delta_w_out_even', 'delta_w_qkv', 'delta_w_out_odd', 'delta_w_gate', 'delta_w_up', 'delta_w_down', 'delta_final_norm', 'new_m_mix_norm', 'new_m_ffn_norm', 'new_m_w_in', 'new_m_conv_w', 'new_m_conv_b', 'new_m_dt_bias', 'new_m_a_log', 'new_m_d_skip', 'new_m_ssd_norm', 'new_m_q_norm', 'new_m_kv_norm', 'new_m_w_uq', 'new_m_w_ukv', 'new_m_w_out_even', 'new_m_w_qkv', 'new_m_w_out_odd', 'new_m_w_gate', 'new_m_w_up', 'new_m_w_down', 'new_m_final_norm', 'new_v_mix_norm', 'new_v_ffn_norm', 'new_v_w_in', 'new_v_conv_w', 'new_v_conv_b', 'new_v_dt_bias', 'new_v_a_log', 'new_v_d_skip', 'new_v_ssd_norm', 'new_v_q_norm', 'new_v_kv_norm', 'new_v_w_uq', 'new_v_w_ukv', 'new_v_w_out_even', 'new_v_w_qkv', 'new_v_w_out_odd', 'new_v_w_gate', 'new_v_w_up', 'new_v_w_down', 'new_v_final_norm']
TWIN_LEAF_KINDS = {'loss': 'loss', 'grad_x': 'grad_x', 'grad_mix_norm': 'grad_w', 'grad_ffn_norm': 'grad_w', 'grad_w_in': 'grad_w', 'grad_conv_w': 'grad_w', 'grad_conv_b': 'grad_w', 'grad_dt_bias': 'grad_w', 'grad_a_log': 'grad_w', 'grad_d_skip': 'grad_w', 'grad_ssd_norm': 'grad_w', 'grad_q_norm': 'grad_w', 'grad_kv_norm': 'grad_w', 'grad_w_uq': 'grad_w', 'grad_w_ukv': 'grad_w', 'grad_w_out_even': 'grad_w', 'grad_w_qkv': 'grad_w', 'grad_w_out_odd': 'grad_w', 'grad_w_gate': 'grad_w', 'grad_w_up': 'grad_w', 'grad_w_down': 'grad_w', 'grad_final_norm': 'grad_w', 'delta_mix_norm': 'delta_w', 'delta_ffn_norm': 'delta_w', 'delta_w_in': 'delta_w', 'delta_conv_w': 'delta_w', 'delta_conv_b': 'delta_w', 'delta_dt_bias': 'delta_w', 'delta_a_log': 'delta_w', 'delta_d_skip': 'delta_w', 'delta_ssd_norm': 'delta_w', 'delta_q_norm': 'delta_w', 'delta_kv_norm': 'delta_w', 'delta_w_uq': 'delta_w', 'delta_w_ukv': 'delta_w', 'delta_w_out_even': 'delta_w', 'delta_w_qkv': 'delta_w', 'delta_w_out_odd': 'delta_w', 'delta_w_gate': 'delta_w', 'delta_w_up': 'delta_w', 'delta_w_down': 'delta_w', 'delta_final_norm': 'delta_w', 'new_m_mix_norm': 'new_m', 'new_m_ffn_norm': 'new_m', 'new_m_w_in': 'new_m', 'new_m_conv_w': 'new_m', 'new_m_conv_b': 'new_m', 'new_m_dt_bias': 'new_m', 'new_m_a_log': 'new_m', 'new_m_d_skip': 'new_m', 'new_m_ssd_norm': 'new_m', 'new_m_q_norm': 'new_m', 'new_m_kv_norm': 'new_m', 'new_m_w_uq': 'new_m', 'new_m_w_ukv': 'new_m', 'new_m_w_out_even': 'new_m', 'new_m_w_qkv': 'new_m', 'new_m_w_out_odd': 'new_m', 'new_m_w_gate': 'new_m', 'new_m_w_up': 'new_m', 'new_m_w_down': 'new_m', 'new_m_final_norm': 'new_m', 'new_v_mix_norm': 'new_v', 'new_v_ffn_norm': 'new_v', 'new_v_w_in': 'new_v', 'new_v_conv_w': 'new_v', 'new_v_conv_b': 'new_v', 'new_v_dt_bias': 'new_v', 'new_v_a_log': 'new_v', 'new_v_d_skip': 'new_v', 'new_v_ssd_norm': 'new_v', 'new_v_q_norm': 'new_v', 'new_v_kv_norm': 'new_v', 'new_v_w_uq': 'new_v', 'new_v_w_ukv': 'new_v', 'new_v_w_out_even': 'new_v', 'new_v_w_qkv': 'new_v', 'new_v_w_out_odd': 'new_v', 'new_v_w_gate': 'new_v', 'new_v_w_up': 'new_v', 'new_v_w_down': 'new_v', 'new_v_final_norm': 'new_v'}


def _forward(args):
    return _fwd_reference(*[args[k] for k in FWD_PARAMS])


def _output_shape():
    def fwd():
        inp = _fwd_setup_inputs(0)
        return _fwd_reference(*[inp[k] for k in FWD_PARAMS])
    out = _jax.eval_shape(fwd)
    return out.shape, out.dtype

N_MICROBATCH = 1
ADAM_LR = 0.001
ADAM_B1 = 0.9
ADAM_B2 = 0.999
ADAM_EPS = 1e-08
ADAM_WD = 0.01
ADAM_STEP = 10
PER_EXAMPLE_BATCH_AXIS = {'x': 0, 'loss_target': 0}
SHARED_INPUTS = []
_WEIGHT_DTYPES = {'mix_norm': _jnp.float32, 'ffn_norm': _jnp.float32, 'w_in': _jnp.float32, 'conv_w': _jnp.float32, 'conv_b': _jnp.float32, 'dt_bias': _jnp.float32, 'a_log': _jnp.float32, 'd_skip': _jnp.float32, 'ssd_norm': _jnp.float32, 'q_norm': _jnp.float32, 'kv_norm': _jnp.float32, 'w_uq': _jnp.float32, 'w_ukv': _jnp.float32, 'w_out_even': _jnp.float32, 'w_qkv': _jnp.float32, 'w_out_odd': _jnp.float32, 'w_gate': _jnp.float32, 'w_up': _jnp.float32, 'w_down': _jnp.float32, 'final_norm': _jnp.float32}
MOMENT_SCALE = {'mix_norm': 7.708297e-02, 'ffn_norm': 5.549362e-02, 'w_in': 5.525786e-02, 'conv_w': 5.431375e-02, 'conv_b': 7.585373e-02, 'dt_bias': 2.024016e-01, 'a_log': 2.602265e-01, 'd_skip': 3.064905e-01, 'ssd_norm': 6.321534e-02, 'q_norm': 2.208154e-02, 'kv_norm': 3.269985e-02, 'w_uq': 8.907482e-03, 'w_ukv': 1.144470e-02, 'w_out_even': 6.328536e-02, 'w_qkv': 2.822170e-02, 'w_out_odd': 4.124693e-02, 'w_gate': 2.379145e-02, 'w_up': 2.306487e-02, 'w_down': 3.823738e-02, 'final_norm': 1.601752e+01}


def _to_microbatches(a, axis):
    t = _jnp.moveaxis(a, axis, 0)
    t = t.reshape((N_MICROBATCH, t.shape[0] // N_MICROBATCH) + t.shape[1:])
    return _jnp.moveaxis(t, 1, axis + 1)


def setup_inputs(seed: int = 0) -> dict:
    inp = _fwd_setup_inputs(seed)
    key = _jax.random.fold_in(_jax.random.key(seed), 7919)
    shape, _ = _output_shape()
    out = dict(inp)
    out["loss_target"] = _jax.random.normal(_jax.random.fold_in(key, 0), shape, _jnp.float32)
    for i, name in enumerate(TWIN_WEIGHTS):
        w = inp[name].astype(_jnp.float32)
        if MOMENT_SCALE is None:
            s = _jnp.sqrt(_jnp.mean(_jnp.square(w)) + 1e-30)
        else:
            s = MOMENT_SCALE[name]
        km, kv = _jax.random.split(_jax.random.fold_in(key, i + 1))
        out[name] = w
        out["m_" + name] = s * _jax.random.normal(km, w.shape, _jnp.float32)
        out["v_" + name] = (s * s) * _jax.random.uniform(kv, w.shape, _jnp.float32, 0.5, 1.5)
    if N_MICROBATCH > 1:
        for name, axis in PER_EXAMPLE_BATCH_AXIS.items():
            out[name] = _to_microbatches(out[name], axis)
    return {'x': out['x'], 'mix_norm': out['mix_norm'], 'ffn_norm': out['ffn_norm'], 'w_in': out['w_in'], 'conv_w': out['conv_w'], 'conv_b': out['conv_b'], 'dt_bias': out['dt_bias'], 'a_log': out['a_log'], 'd_skip': out['d_skip'], 'ssd_norm': out['ssd_norm'], 'q_norm': out['q_norm'], 'kv_norm': out['kv_norm'], 'w_uq': out['w_uq'], 'w_ukv': out['w_ukv'], 'w_out_even': out['w_out_even'], 'w_qkv': out['w_qkv'], 'w_out_odd': out['w_out_odd'], 'w_gate': out['w_gate'], 'w_up': out['w_up'], 'w_down': out['w_down'], 'final_norm': out['final_norm'], 'loss_target': out['loss_target'], 'm_mix_norm': out['m_mix_norm'], 'm_ffn_norm': out['m_ffn_norm'], 'm_w_in': out['m_w_in'], 'm_conv_w': out['m_conv_w'], 'm_conv_b': out['m_conv_b'], 'm_dt_bias': out['m_dt_bias'], 'm_a_log': out['m_a_log'], 'm_d_skip': out['m_d_skip'], 'm_ssd_norm': out['m_ssd_norm'], 'm_q_norm': out['m_q_norm'], 'm_kv_norm': out['m_kv_norm'], 'm_w_uq': out['m_w_uq'], 'm_w_ukv': out['m_w_ukv'], 'm_w_out_even': out['m_w_out_even'], 'm_w_qkv': out['m_w_qkv'], 'm_w_out_odd': out['m_w_out_odd'], 'm_w_gate': out['m_w_gate'], 'm_w_up': out['m_w_up'], 'm_w_down': out['m_w_down'], 'm_final_norm': out['m_final_norm'], 'v_mix_norm': out['v_mix_norm'], 'v_ffn_norm': out['v_ffn_norm'], 'v_w_in': out['v_w_in'], 'v_conv_w': out['v_conv_w'], 'v_conv_b': out['v_conv_b'], 'v_dt_bias': out['v_dt_bias'], 'v_a_log': out['v_a_log'], 'v_d_skip': out['v_d_skip'], 'v_ssd_norm': out['v_ssd_norm'], 'v_q_norm': out['v_q_norm'], 'v_kv_norm': out['v_kv_norm'], 'v_w_uq': out['v_w_uq'], 'v_w_ukv': out['v_w_ukv'], 'v_w_out_even': out['v_w_out_even'], 'v_w_qkv': out['v_w_qkv'], 'v_w_out_odd': out['v_w_out_odd'], 'v_w_gate': out['v_w_gate'], 'v_w_up': out['v_w_up'], 'v_w_down': out['v_w_down'], 'v_final_norm': out['v_final_norm']}


def _loss(weights, diff, rest, loss_target):
    with _jax.named_scope("forward"):
        args = {**rest, TWIN_DIFF_INPUT: diff, **{k: w.astype(_WEIGHT_DTYPES[k]) for k, w in weights.items()}}
        y = _forward(args)
    with _jax.named_scope("loss_head"):
        err = _jnp.square(y.astype(_jnp.float32) - loss_target)
        return 0.5 * _jnp.sum(_jnp.mean(err, axis=-1)) if err.ndim else 0.5 * err


def _adamw(w, g, m, v):
    m = ADAM_B1 * m + (1.0 - ADAM_B1) * g
    v = ADAM_B2 * v + (1.0 - ADAM_B2) * _jnp.square(g)
    m_hat = m / (1.0 - ADAM_B1 ** ADAM_STEP)
    v_hat = v / (1.0 - ADAM_B2 ** ADAM_STEP)
    delta = -ADAM_LR * (m_hat / (_jnp.sqrt(v_hat) + ADAM_EPS) + ADAM_WD * w)
    return delta, m, v


def reference(x, mix_norm, ffn_norm, w_in, conv_w, conv_b, dt_bias, a_log, d_skip, ssd_norm, q_norm, kv_norm, w_uq, w_ukv, w_out_even, w_qkv, w_out_odd, w_gate, w_up, w_down, final_norm, loss_target, m_mix_norm, m_ffn_norm, m_w_in, m_conv_w, m_conv_b, m_dt_bias, m_a_log, m_d_skip, m_ssd_norm, m_q_norm, m_kv_norm, m_w_uq, m_w_ukv, m_w_out_even, m_w_qkv, m_w_out_odd, m_w_gate, m_w_up, m_w_down, m_final_norm, v_mix_norm, v_ffn_norm, v_w_in, v_conv_w, v_conv_b, v_dt_bias, v_a_log, v_d_skip, v_ssd_norm, v_q_norm, v_kv_norm, v_w_uq, v_w_ukv, v_w_out_even, v_w_qkv, v_w_out_odd, v_w_gate, v_w_up, v_w_down, v_final_norm):
    given = dict(x=x, mix_norm=mix_norm, ffn_norm=ffn_norm, w_in=w_in, conv_w=conv_w, conv_b=conv_b, dt_bias=dt_bias, a_log=a_log, d_skip=d_skip, ssd_norm=ssd_norm, q_norm=q_norm, kv_norm=kv_norm, w_uq=w_uq, w_ukv=w_ukv, w_out_even=w_out_even, w_qkv=w_qkv, w_out_odd=w_out_odd, w_gate=w_gate, w_up=w_up, w_down=w_down, final_norm=final_norm, loss_target=loss_target, m_mix_norm=m_mix_norm, m_ffn_norm=m_ffn_norm, m_w_in=m_w_in, m_conv_w=m_conv_w, m_conv_b=m_conv_b, m_dt_bias=m_dt_bias, m_a_log=m_a_log, m_d_skip=m_d_skip, m_ssd_norm=m_ssd_norm, m_q_norm=m_q_norm, m_kv_norm=m_kv_norm, m_w_uq=m_w_uq, m_w_ukv=m_w_ukv, m_w_out_even=m_w_out_even, m_w_qkv=m_w_qkv, m_w_out_odd=m_w_out_odd, m_w_gate=m_w_gate, m_w_up=m_w_up, m_w_down=m_w_down, m_final_norm=m_final_norm, v_mix_norm=v_mix_norm, v_ffn_norm=v_ffn_norm, v_w_in=v_w_in, v_conv_w=v_conv_w, v_conv_b=v_conv_b, v_dt_bias=v_dt_bias, v_a_log=v_a_log, v_d_skip=v_d_skip, v_ssd_norm=v_ssd_norm, v_q_norm=v_q_norm, v_kv_norm=v_kv_norm, v_w_uq=v_w_uq, v_w_ukv=v_w_ukv, v_w_out_even=v_w_out_even, v_w_qkv=v_w_qkv, v_w_out_odd=v_w_out_odd, v_w_gate=v_w_gate, v_w_up=v_w_up, v_w_down=v_w_down, v_final_norm=v_final_norm)
    weights = {n: given[n] for n in TWIN_WEIGHTS}
    shared = {n: given[n] for n in SHARED_INPUTS}
    per_example = {n: given[n] for n in ['x']}
    grad_fn = _jax.value_and_grad(_loss, argnums=(0, 1))

    def one_microbatch(ex, loss_target):
        ex = dict(ex)
        diff = ex.pop(TWIN_DIFF_INPUT)
        return grad_fn(weights, diff, {**shared, **ex}, loss_target)

    if N_MICROBATCH == 1:
        loss, (grad_w, grad_x) = one_microbatch(per_example, given["loss_target"])
    else:
        def body(carry, xs):
            loss_sum, grad_sum = carry
            l_k, (gw_k, gx_k) = one_microbatch(xs[0], xs[1])
            with _jax.named_scope("update"):
                return (loss_sum + l_k, _jax.tree.map(_jnp.add, grad_sum, gw_k)), gx_k

        init = (_jnp.zeros((), _jnp.float32), _jax.tree.map(_jnp.zeros_like, weights))
        (loss, grad_w), grad_x = _jax.lax.scan(body, init, (per_example, given["loss_target"]))
    with _jax.named_scope("update"):
        delta_w, new_m, new_v = {}, {}, {}
        for n in TWIN_WEIGHTS:
            delta_w[n], new_m[n], new_v[n] = _adamw(weights[n], grad_w[n], given["m_" + n], given["v_" + n])
    return (loss, grad_x, *[grad_w[n] for n in TWIN_WEIGHTS], *[delta_w[n] for n in TWIN_WEIGHTS],
            *[new_m[n] for n in TWIN_WEIGHTS], *[new_v[n] for n in TWIN_WEIGHTS])
```

```python
import functools
import math

import jax
import jax.numpy as jnp
import numpy as np
from jax import lax
from jax.experimental import pallas as pl
from jax.experimental.pallas import tpu as pltpu

F32, BF16 = jnp.float32, jnp.bfloat16

RMS_EPS = 1e-6
SSD_HEADS, SSD_HEAD_DIM, SSD_GROUPS, SSD_STATE, SSD_CONV, SSD_CHUNK = 32, 64, 4, 128, 4, 128
MLA_HEADS, MLA_Q_RANK, MLA_KV_RANK, MLA_NOPE, MLA_ROPE, MLA_V = 16, 512, 512, 128, 64, 128
ROPE_THETA = 10000.0
SB_HEADS, SB_HEAD_DIM = 16, 128
ADAM_LR, ADAM_B1, ADAM_B2, ADAM_EPS, ADAM_WD, ADAM_STEP = 0.001, 0.9, 0.999, 1e-08, 0.01, 10

LANES = 128
VMEM_LIMIT_BYTES = 56 * 1024 * 1024
ATT_BLK = 256
ROW_TILE = 256
NEG = -1e30

MESH_T = pl.DeviceIdType.MESH
WEIGHTS = ['mix_norm', 'ffn_norm', 'w_in', 'conv_w', 'conv_b', 'dt_bias', 'a_log', 'd_skip', 'ssd_norm', 'q_norm',
           'kv_norm', 'w_uq', 'w_ukv', 'w_out_even', 'w_qkv', 'w_out_odd', 'w_gate', 'w_up', 'w_down', 'final_norm']
SHARDED = ['w_in', 'conv_w', 'w_uq', 'w_ukv', 'w_out_even', 'w_qkv', 'w_out_odd', 'w_gate', 'w_up', 'w_down']
COL_SHARDED = ['w_in', 'conv_w', 'w_uq', 'w_ukv', 'w_qkv', 'w_gate', 'w_up']
SMALL = [n for n in WEIGHTS if n not in SHARDED]


def _tile(n, cands):
    for c in cands:
        if n % c == 0:
            return c
    return n


def _params(ngrid):
    return pltpu.CompilerParams(dimension_semantics=("arbitrary",) * ngrid, vmem_limit_bytes=VMEM_LIMIT_BYTES)


def _dot(a, b, mode="nn"):
    dims = {"nn": (((1,), (0,)), ((), ())), "nt": (((1,), (1,)), ((), ())), "tn": (((0,), (0,)), ((), ()))}[mode]
    return lax.dot_general(a, b, dims, preferred_element_type=F32)


def _split(x, parts):
    out, r = [], x
    for _ in range(parts):
        p = r.astype(BF16)
        out.append(p)
        r = r - p.astype(F32)
    return out


def _xdot(x, e, parts=3):
    acc = None
    for p in _split(x, parts):
        t = _dot(p, e)
        acc = t if acc is None else acc + t
    return acc


def _xdot_l(e, x, parts=3):
    acc = None
    for p in _split(x, parts):
        t = _dot(e, p)
        acc = t if acc is None else acc + t
    return acc


def _iota(shape, dim):
    return lax.broadcasted_iota(jnp.int32, shape, dim)


def _softplus(z):
    return jnp.maximum(z, 0.0) + jnp.log(1.0 + jnp.exp(-jnp.abs(z)))


def _sigmoid(z):
    return 1.0 / (1.0 + jnp.exp(-z))


def _acc_rows(ref, first, val):
    @pl.when(first)
    def _():
        ref[...] = jnp.zeros_like(ref)
    ref[...] += jnp.broadcast_to(val, ref.shape)


def _mm(a, b, mode, name, out_dtype=F32, add=None):
    if mode == "nn":
        (m, k), n = a.shape, b.shape[1]
    elif mode == "nt":
        (m, k), n = a.shape, b.shape[0]
    else:
        (k, m), n = a.shape, b.shape[1]
    tm = _tile(m, (512, 256, 128))
    tn = _tile(n, (1024, 512, 640, 256, 128))
    tk = k if k <= 2048 else _tile(k, (1024, 512, 640, 256, 128))
    nk = k // tk
    a_spec = {"nn": pl.BlockSpec((tm, tk), lambda i, j, kk: (i, kk)), "nt": pl.BlockSpec((tm, tk), lambda i, j, kk: (i, kk)),
              "tn": pl.BlockSpec((tk, tm), lambda i, j, kk: (kk, i))}[mode]
    b_spec = {"nn": pl.BlockSpec((tk, tn), lambda i, j, kk: (kk, j)), "nt": pl.BlockSpec((tn, tk), lambda i, j, kk: (j, kk)),
              "tn": pl.BlockSpec((tk, tn), lambda i, j, kk: (kk, j))}[mode]
    o_spec = pl.BlockSpec((tm, tn), lambda i, j, kk: (i, j))
    has_add = add is not None

    def body(*refs):
        a_ref, b_ref = refs[0], refs[1]
        add_ref = refs[2] if has_add else None
        o_ref = refs[2 + has_add]
        part = _dot(a_ref[...].astype(BF16), b_ref[...].astype(BF16), mode)
        if nk == 1:
            if has_add:
                part = part + add_ref[...]
            o_ref[...] = part.astype(o_ref.dtype)
            return
        acc_ref = refs[3 + has_add]
        kk = pl.program_id(2)

        @pl.when(kk == 0)
        def _():
            acc_ref[...] = jnp.zeros_like(acc_ref)
        acc_ref[...] += part

        @pl.when(kk == nk - 1)
        def _():
            r = acc_ref[...]
            if has_add:
                r = r + add_ref[...]
            o_ref[...] = r.astype(o_ref.dtype)

    ins = [a, b] + ([add] if has_add else [])
    specs = [a_spec, b_spec] + ([o_spec] if has_add else [])
    return pl.pallas_call(
        body, name=name, grid=(m // tm, n // tn, nk), in_specs=specs, out_specs=o_spec,
        out_shape=jax.ShapeDtypeStruct((m, n), out_dtype),
        scratch_shapes=[pltpu.VMEM((tm, tn), F32)] if nk > 1 else [],
        compiler_params=_params(3))(*ins)


def _rows(t):
    return _tile(t, (ROW_TILE, 128, 64, 32, 16, 8))


def _rms_fwd(x, g, name):
    t, c = x.shape
    tr = _rows(t)

    def body(x_ref, g_ref, h_ref, r_ref):
        xv = x_ref[...]
        r = lax.rsqrt(jnp.mean(xv * xv, axis=-1, keepdims=True) + RMS_EPS)
        h_ref[...] = (xv * r * g_ref[...]).astype(h_ref.dtype)
        r_ref[...] = r

    return pl.pallas_call(
        body, name=name, grid=(t // tr,),
        in_specs=[pl.BlockSpec((tr, c), lambda i: (i, 0)), pl.BlockSpec((1, c), lambda i: (0, 0))],
        out_specs=[pl.BlockSpec((tr, c), lambda i: (i, 0)), pl.BlockSpec((tr, 1), lambda i: (i, 0))],
        out_shape=[jax.ShapeDtypeStruct((t, c), BF16), jax.ShapeDtypeStruct((t, 1), F32)],
        compiler_params=_params(1))(x, g)


def _rms_bwd(x, g, r, dh, name, dres=None):
    t, c = x.shape
    tr = _rows(t)
    has_res = dres is not None

    def body(*refs):
        x_ref, g_ref, r_ref, dh_ref = refs[:4]
        res_ref = refs[4] if has_res else None
        dx_ref, dg_ref = refs[4 + has_res], refs[5 + has_res]
        rv = r_ref[...]
        xh = x_ref[...] * rv
        dhv = dh_ref[...]
        dxh = dhv * g_ref[...]
        cm = jnp.mean(dxh * xh, axis=-1, keepdims=True)
        dx = (dxh - xh * cm) * rv
        if has_res:
            dx = dx + res_ref[...]
        dx_ref[...] = dx
        _acc_rows(dg_ref, pl.program_id(0) == 0, jnp.sum(dhv * xh, axis=0, keepdims=True))

    row = pl.BlockSpec((tr, c), lambda i: (i, 0))
    ins = [x, g, r, dh] + ([dres] if has_res else [])
    specs = [row, pl.BlockSpec((1, c), lambda i: (0, 0)), pl.BlockSpec((tr, 1), lambda i: (i, 0)), row] + ([row] if has_res else [])
    dx, dg = pl.pallas_call(
        body, name=name, grid=(t // tr,), in_specs=specs,
        out_specs=[row, pl.BlockSpec((8, c), lambda i: (0, 0))],
        out_shape=[jax.ShapeDtypeStruct((t, c), F32), jax.ShapeDtypeStruct((8, c), F32)],
        compiler_params=_params(1))(*ins)
    return dx, dg[0]


def _gated_fwd(y, z, g, name):
    t, c = y.shape
    tr = _rows(t)

    def body(y_ref, z_ref, g_ref, o_ref, r_ref):
        zv = z_ref[...]
        v = y_ref[...] * zv * _sigmoid(zv)
        r = lax.rsqrt(jnp.mean(v * v, axis=-1, keepdims=True) + RMS_EPS)
        o_ref[...] = (v * r * g_ref[...]).astype(o_ref.dtype)
        r_ref[...] = r

    row = pl.BlockSpec((tr, c), lambda i: (i, 0))
    return pl.pallas_call(
        body, name=name, grid=(t // tr,), in_specs=[row, row, pl.BlockSpec((1, c), lambda i: (0, 0))],
        out_specs=[row, pl.BlockSpec((tr, 1), lambda i: (i, 0))],
        out_shape=[jax.ShapeDtypeStruct((t, c), BF16), jax.ShapeDtypeStruct((t, 1), F32)],
        compiler_params=_params(1))(y, z, g)


def _gated_bwd(y, z, g, r, dout, name):
    t, c = y.shape
    tr = _rows(t)

    def body(y_ref, z_ref, g_ref, r_ref, do_ref, dy_ref, dz_ref, dg_ref):
        yv, zv, rv, dov = y_ref[...], z_ref[...], r_ref[...], do_ref[...]
        s = _sigmoid(zv)
        sz = zv * s
        xh = yv * sz * rv
        dxh = dov * g_ref[...]
        cm = jnp.mean(dxh * xh, axis=-1, keepdims=True)
        dv = (dxh - xh * cm) * rv
        dy_ref[...] = dv * sz
        dz_ref[...] = dv * yv * s * (1.0 + zv * (1.0 - s))
        _acc_rows(dg_ref, pl.program_id(0) == 0, jnp.sum(dov * xh, axis=0, keepdims=True))

    row = pl.BlockSpec((tr, c), lambda i: (i, 0))
    dy, dz, dg = pl.pallas_call(
        body, name=name, grid=(t // tr,),
        in_specs=[row, row, pl.BlockSpec((1, c), lambda i: (0, 0)), pl.BlockSpec((tr, 1), lambda i: (i, 0)), row],
        out_specs=[row, row, pl.BlockSpec((8, c), lambda i: (0, 0))],
        out_shape=[jax.ShapeDtypeStruct((t, c), F32), jax.ShapeDtypeStruct((t, c), F32), jax.ShapeDtypeStruct((8, c), F32)],
        compiler_params=_params(1))(y, z, g, r, dout)
    return dy, dz, dg[0]


def _swiglu_fwd(gate, up, name):
    t, c = gate.shape
    tr, tc = _rows(t), _tile(c, (2816, 1408, 1024, 512, 256, 128))

    def body(g_ref, u_ref, o_ref):
        gv = g_ref[...]
        o_ref[...] = (gv * _sigmoid(gv) * u_ref[...]).astype(o_ref.dtype)

    blk = pl.BlockSpec((tr, tc), lambda i, j: (i, j))
    return pl.pallas_call(body, name=name, grid=(t // tr, c // tc), in_specs=[blk, blk], out_specs=blk,
                          out_shape=jax.ShapeDtypeStruct((t, c), BF16), compiler_params=_params(2))(gate, up)


def _swiglu_bwd(gate, up, dact, name):
    t, c = gate.shape
    tr, tc = _rows(t), _tile(c, (2816, 1408, 1024, 512, 256, 128))

    def body(g_ref, u_ref, d_ref, dg_ref, du_ref):
        gv, dv = g_ref[...], d_ref[...]
        s = _sigmoid(gv)
        dg_ref[...] = (dv * u_ref[...] * s * (1.0 + gv * (1.0 - s))).astype(dg_ref.dtype)
        du_ref[...] = (dv * gv * s).astype(du_ref.dtype)

    blk = pl.BlockSpec((tr, tc), lambda i, j: (i, j))
    return pl.pallas_call(body, name=name, grid=(t // tr, c // tc), in_specs=[blk, blk, blk], out_specs=[blk, blk],
                          out_shape=[jax.ShapeDtypeStruct((t, c), BF16)] * 2, compiler_params=_params(2))(gate, up, dact)


def _loss_fwd_bwd(x, g, tgt, name):
    t, c = x.shape
    tr = _rows(t)

    def body(x_ref, g_ref, t_ref, l_ref, dx_ref, dg_ref):
        xv, gv = x_ref[...], g_ref[...]
        r = lax.rsqrt(jnp.mean(xv * xv, axis=-1, keepdims=True) + RMS_EPS)
        xh = xv * r
        err = xh * gv - t_ref[...]
        per_row = jnp.mean(err * err, axis=-1, keepdims=True)
        dy = err * (1.0 / c)
        dxh = dy * gv
        cm = jnp.mean(dxh * xh, axis=-1, keepdims=True)
        dx_ref[...] = (dxh - xh * cm) * r
        first = pl.program_id(0) == 0
        _acc_rows(dg_ref, first, jnp.sum(dy * xh, axis=0, keepdims=True))
        _acc_rows(l_ref, first, jnp.broadcast_to(0.5 * jnp.sum(per_row, axis=0, keepdims=True), (1, LANES)))

    row = pl.BlockSpec((tr, c), lambda i: (i, 0))
    lo, dx, dg = pl.pallas_call(
        body, name=name, grid=(t // tr,), in_specs=[row, pl.BlockSpec((1, c), lambda i: (0, 0)), row],
        out_specs=[pl.BlockSpec((8, LANES), lambda i: (0, 0)), row, pl.BlockSpec((8, c), lambda i: (0, 0))],
        out_shape=[jax.ShapeDtypeStruct((8, LANES), F32), jax.ShapeDtypeStruct((t, c), F32), jax.ShapeDtypeStruct((8, c), F32)],
        compiler_params=_params(1))(x, g, tgt)
    return lo[0, 0], dx, dg[0]


def _conv_specs(t, c):
    tr = _rows(t)
    tc = _tile(c, (1024, 768, 512, 256, 128))
    h8 = tr // 8
    tile = pl.BlockSpec((tr, tc), lambda j, i: (i, j))
    prev = pl.BlockSpec((8, tc), lambda j, i: (jnp.maximum(i * h8 - 1, 0), j))
    nxt = pl.BlockSpec((8, tc), lambda j, i: (jnp.minimum((i + 1) * h8, t // 8 - 1), j))
    return tr, tc, tile, prev, nxt


def _conv_fwd(xbc, w, b, name):
    t, c = xbc.shape
    tr, tc, tile, prev, _ = _conv_specs(t, c)

    def body(x_ref, p_ref, w_ref, b_ref, o_ref, buf):
        i = pl.program_id(1)
        buf[0:8, :] = jnp.where(i > 0, p_ref[...], 0.0)
        buf[8:, :] = x_ref[...]
        pre = b_ref[...]
        for k in range(SSD_CONV):
            pre = pre + w_ref[k:k + 1, :] * buf[pl.ds(8 - (SSD_CONV - 1) + k, tr), :]
        o_ref[...] = pre * _sigmoid(pre)

    return pl.pallas_call(
        body, name=name, grid=(c // tc, t // tr),
        in_specs=[tile, prev, pl.BlockSpec((8, tc), lambda j, i: (0, j)), pl.BlockSpec((1, tc), lambda j, i: (0, j))],
        out_specs=tile, out_shape=jax.ShapeDtypeStruct((t, c), F32),
        scratch_shapes=[pltpu.VMEM((tr + 8, tc), F32)], compiler_params=_params(2))(xbc, xbc, w, b)


def _conv_bwd(xbc, w, b, dout, name):
    t, c = xbc.shape
    tr, tc, tile, prev, nxt = _conv_specs(t, c)
    nt = t // tr
    kc = SSD_CONV

    def body(x_ref, p_ref, n_ref, w_ref, b_ref, d_ref, dn_ref, dx_ref, dw_ref, db_ref, buf, dbuf):
        i = pl.program_id(1)
        last = i == nt - 1
        buf[0:8, :] = jnp.where(i > 0, p_ref[...], 0.0)
        buf[8:8 + tr, :] = x_ref[...]
        buf[8 + tr:, :] = jnp.where(last, 0.0, n_ref[...])
        pre = b_ref[...]
        for k in range(kc):
            pre = pre + w_ref[k:k + 1, :] * buf[pl.ds(8 - (kc - 1) + k, tr + 8), :]
        s = _sigmoid(pre)
        dsilu = s * (1.0 + pre * (1.0 - s))
        dbuf[0:tr, :] = d_ref[...] * dsilu[0:tr, :]
        dbuf[tr:, :] = jnp.where(last, 0.0, dn_ref[...]) * dsilu[tr:, :]
        dpre = dbuf[0:tr, :]
        dx = jnp.zeros((tr, tc), F32)
        first = i == 0
        for k in range(kc):
            dx = dx + w_ref[k:k + 1, :] * dbuf[pl.ds(kc - 1 - k, tr), :]
        dx_ref[...] = dx

        @pl.when(first)
        def _():
            dw_ref[...] = jnp.zeros_like(dw_ref)
        for k in range(kc):
            dw_ref[k:k + 1, :] += jnp.sum(dpre * buf[pl.ds(8 - (kc - 1) + k, tr), :], axis=0, keepdims=True)
        _acc_rows(db_ref, first, jnp.sum(dpre, axis=0, keepdims=True))

    par = pl.BlockSpec((8, tc), lambda j, i: (0, j))
    dx, dw, db = pl.pallas_call(
        body, name=name, grid=(c // tc, nt),
        in_specs=[tile, prev, nxt, par, pl.BlockSpec((1, tc), lambda j, i: (0, j)), tile, nxt],
        out_specs=[tile, par, par],
        out_shape=[jax.ShapeDtypeStruct((t, c), F32), jax.ShapeDtypeStruct((8, c), F32), jax.ShapeDtypeStruct((8, c), F32)],
        scratch_shapes=[pltpu.VMEM((tr + 16, tc), F32), pltpu.VMEM((tr + 8, tc), F32)],
        compiler_params=_params(2))(xbc, xbc, xbc, w, b, dout, dout)
    return dx, dw[:kc], db[0]


def _ssd_consts():
    h, p, ln = SSD_HEADS, SSD_HEAD_DIM, SSD_CHUNK
    w = h * p
    hrow, jcol = _iota((LANES, w), 0), _iota((LANES, w), 1)
    e = ((jcol >= hrow * p) & (jcol < (hrow + 1) * p)).astype(BF16)
    jrow, hcol = _iota((w, LANES), 0), _iota((w, LANES), 1)
    et = ((jrow >= hcol * p) & (jrow < (hcol + 1) * p)).astype(BF16)
    row, col = _iota((ln, ln), 0), _iota((ln, ln), 1)
    return e, et, row, col


def _ssd_common(dtraw_ref, dtb_ref, alog_ref, e):
    ln = SSD_CHUNK
    raw = dtraw_ref[...] + dtb_ref[...]
    dt = _softplus(raw)
    a = -jnp.exp(alog_ref[...])
    adt = dt * a
    row, col = _iota((ln, ln), 0), _iota((ln, ln), 1)
    cs = _xdot_l((col <= row).astype(BF16), adt)
    cl = jnp.sum(adt, axis=0, keepdims=True)
    ex = _xdot(jnp.concatenate([dt, cs, jnp.broadcast_to(cl, (ln, LANES))], axis=0), e)
    return raw, dt, a, cs, ex[:ln], ex[ln:2 * ln], ex[2 * ln:]


def _head_decay(cs, h, causal):
    ln = SSD_CHUNK
    lane = _iota((1, LANES), 1)
    colv = jnp.sum(jnp.where(lane == h, cs, 0.0), axis=1, keepdims=True)
    cb = jnp.broadcast_to(colv, (ln, ln))
    return jnp.exp(jnp.where(causal, cb - cb.T, NEG))


def _ssd_fwd(dtraw, xc, dtb, alog, dskip_x, name):
    t = dtraw.shape[0]
    h, p, g, n, ln = SSD_HEADS, SSD_HEAD_DIM, SSD_GROUPS, SSD_STATE, SSD_CHUNK
    w, gn, gw, hpg = h * p, g * n, (h // g) * p, h // g
    assert n == ln and gw % LANES == 0 and w % gn == 0
    nc = t // ln

    def body(dtraw_ref, x_ref, b_ref, c_ref, dtb_ref, alog_ref, dsk_ref, y_ref, sp_ref, s_ref):
        @pl.when(pl.program_id(0) == 0)
        def _():
            s_ref[...] = jnp.zeros_like(s_ref)
        e, _, row, col = _ssd_consts()
        causal = col <= row
        _, _, _, cs, dt_x, cs_x, cl_x = _ssd_common(dtraw_ref, dtb_ref, alog_ref, e)
        xv = x_ref[...]
        xd = xv * dt_x
        xdb = xd.astype(BF16)
        sv = s_ref[...]
        sp_ref[0] = sv
        el_x = jnp.exp(cs_x)
        zb = (xd * jnp.exp(cl_x - cs_x)).astype(BF16)
        cd_x = jnp.exp(cl_x)
        dsk = dsk_ref[...]
        half = _iota((1, LANES), 1) >= p
        for gi in range(g):
            gs = slice(gi * gw, (gi + 1) * gw)
            bg = b_ref[:, gi * n:(gi + 1) * n].astype(BF16)
            cg = c_ref[:, gi * n:(gi + 1) * n].astype(BF16)
            gm = _dot(cg, bg, "nt")
            sg = sv[:, gs]
            yoff = _dot(cg, sg.astype(BF16)) * el_x[:, gs]
            s_ref[:, gs] = sg * cd_x[:, gs] + _dot(bg, zb[:, gs], "tn")
            for pp in range(gw // LANES):
                ls = slice(gi * gw + pp * LANES, gi * gw + (pp + 1) * LANES)
                xp = xdb[:, ls]
                yp = yoff[:, pp * LANES:(pp + 1) * LANES] + dsk[:, ls] * xv[:, ls]
                for hh in range(LANES // p):
                    hd = gi * hpg + pp * (LANES // p) + hh
                    wm = (gm * _head_decay(cs, hd, causal)).astype(BF16)
                    yp = yp + _dot(wm, jnp.where(half == (hh == 1), xp, jnp.zeros_like(xp)))
                y_ref[:, ls] = yp

    nar = pl.BlockSpec((ln, LANES), lambda c: (c, 0))
    one = pl.BlockSpec((1, LANES), lambda c: (0, 0))
    return pl.pallas_call(
        body, name=name, grid=(nc,),
        in_specs=[nar, pl.BlockSpec((ln, w), lambda c: (c, 0)), pl.BlockSpec((ln, gn), lambda c: (c, w // gn)),
                  pl.BlockSpec((ln, gn), lambda c: (c, w // gn + 1)), one, one, pl.BlockSpec((1, w), lambda c: (0, 0))],
        out_specs=[pl.BlockSpec((ln, w), lambda c: (c, 0)), pl.BlockSpec((1, n, w), lambda c: (c, 0, 0))],
        out_shape=[jax.ShapeDtypeStruct((t, w), F32), jax.ShapeDtypeStruct((nc, n, w), F32)],
        scratch_shapes=[pltpu.VMEM((n, w), F32)], compiler_params=_params(1))(dtraw, xc, xc, xc, dtb, alog, dskip_x)


def _ssd_bwd(dtraw, xc, dtb, alog, dskip_x, sprev, dy, name):
    t = dtraw.shape[0]
    h, p, g, n, ln = SSD_HEADS, SSD_HEAD_DIM, SSD_GROUPS, SSD_STATE, SSD_CHUNK
    w, gn, gw, hpg = h * p, g * n, (h // g) * p, h // g
    nc = t // ln

    def body(dtraw_ref, x_ref, b_ref, c_ref, dtb_ref, alog_ref, dsk_ref, sp_ref, dy_ref,
             dx_ref, db_ref, dc_ref, ddt_ref, dbias_ref, dalog_ref, ddsk_ref, ds_ref, dxd_ref, qcs_ref):
        first = pl.program_id(0) == 0

        @pl.when(first)
        def _():
            ds_ref[...] = jnp.zeros_like(ds_ref)
        e, et, row, col = _ssd_consts()
        causal = col <= row
        raw, dt, a, cs, dt_x, cs_x, cl_x = _ssd_common(dtraw_ref, dtb_ref, alog_ref, e)
        xv = x_ref[...]
        xd = xv * dt_x
        xdb = xd.astype(BF16)
        sv = sp_ref[0]
        dyv = dy_ref[...]
        dyb = dyv.astype(BF16)
        dsn = ds_ref[...]
        el_x = jnp.exp(cs_x)
        dte_x = jnp.exp(cl_x - cs_x)
        cd_x = jnp.exp(cl_x)
        zf = xd * dte_x
        lane = _iota((1, LANES), 1)
        half = lane >= p
        lastrow = _iota((ln, 1), 0) == ln - 1
        dcs = jnp.zeros((ln, LANES), F32)
        for gi in range(g):
            gs = slice(gi * gw, (gi + 1) * gw)
            ns = slice(gi * n, (gi + 1) * n)
            bg = b_ref[:, ns].astype(BF16)
            cg = c_ref[:, ns].astype(BF16)
            gm = _dot(cg, bg, "nt")
            sgb = sv[:, gs].astype(BF16)
            dsg = dsn[:, gs]
            dsgb = dsg.astype(BF16)
            yoff = _dot(cg, sgb) * el_x[:, gs]
            drb = (el_x[:, gs] * dyv[:, gs]).astype(BF16)
            dcg = _dot(drb, sgb, "nt")
            ds_ref[:, gs] = cd_x[:, gs] * dsg + _dot(cg, drb, "tn")
            dz = _dot(bg, dsgb)
            zg = zf[:, gs]
            dbg = _dot(zg.astype(BF16), dsgb, "nt")
            dzz = dz * zg
            qcl = jnp.sum(dzz + cd_x[:, gs] * dsg * sv[:, gs], axis=0, keepdims=True)
            qcs_ref[:, gs] = dyv[:, gs] * yoff - dzz + jnp.where(lastrow, jnp.broadcast_to(qcl, (ln, gw)), 0.0)
            dgm = jnp.zeros((ln, ln), F32)
            for pp in range(gw // LANES):
                ls = slice(gi * gw + pp * LANES, gi * gw + (pp + 1) * LANES)
                xp = xdb[:, ls]
                dxp = dz[:, pp * LANES:(pp + 1) * LANES] * dte_x[:, ls]
                for hh in range(LANES // p):
                    hd = gi * hpg + pp * (LANES // p) + hh
                    dm = _head_decay(cs, hd, causal)
                    wf = gm * dm
                    dym = jnp.where(half == (hh == 1), dyb[:, ls], jnp.zeros_like(xp))
                    dw = _dot(dym, xp, "nt")
                    dxp = dxp + _dot(wf.astype(BF16), dym, "tn")
                    dgm = dgm + dw * dm
                    mm = dw * wf
                    rc = jnp.sum(mm, axis=1, keepdims=True) - jnp.sum(mm.T, axis=1, keepdims=True)
                    dcs = dcs + rc * (lane == hd).astype(F32)
                dxd_ref[:, ls] = dxp
            dgb = dgm.astype(BF16)
            dc_ref[:, ns] = dcg + _dot(dgb, bg)
            db_ref[:, ns] = dbg + _dot(dgb, cg, "tn")
        dxd = dxd_ref[...]
        dx_ref[...] = dxd * dt_x + dsk_ref[...] * dyv
        red = _xdot(jnp.concatenate([qcs_ref[...], dxd * xv, dyv * xv], axis=0), et)
        dcs = dcs + red[:ln]
        dadt = _xdot_l((row <= col).astype(BF16), dcs)
        ddt = red[ln:2 * ln] + dadt * a
        draw = ddt * _sigmoid(raw)
        ddt_ref[...] = draw
        _acc_rows(dbias_ref, first, jnp.sum(draw, axis=0, keepdims=True))
        _acc_rows(dalog_ref, first, jnp.sum(dadt * dt, axis=0, keepdims=True) * a)
        _acc_rows(ddsk_ref, first, jnp.sum(red[2 * ln:], axis=0, keepdims=True))

    rev = lambda c: nc - 1 - c
    nar = pl.BlockSpec((ln, LANES), lambda c: (rev(c), 0))
    one = pl.BlockSpec((1, LANES), lambda c: (0, 0))
    wide = pl.BlockSpec((ln, w), lambda c: (rev(c), 0))
    bcs = pl.BlockSpec((ln, gn), lambda c: (rev(c), 0))
    acc = pl.BlockSpec((8, LANES), lambda c: (0, 0))
    outs = pl.pallas_call(
        body, name=name, grid=(nc,),
        in_specs=[nar, wide, pl.BlockSpec((ln, gn), lambda c: (rev(c), w // gn)), pl.BlockSpec((ln, gn), lambda c: (rev(c), w // gn + 1)),
                  one, one, pl.BlockSpec((1, w), lambda c: (0, 0)), pl.BlockSpec((1, n, w), lambda c: (rev(c), 0, 0)), wide],
        out_specs=[wide, bcs, bcs, nar, acc, acc, acc],
        out_shape=[jax.ShapeDtypeStruct((t, w), F32), jax.ShapeDtypeStruct((t, gn), F32), jax.ShapeDtypeStruct((t, gn), F32),
                   jax.ShapeDtypeStruct((t, LANES), F32)] + [jax.ShapeDtypeStruct((8, LANES), F32)] * 3,
        scratch_shapes=[pltpu.VMEM((n, w), F32), pltpu.VMEM((ln, w), F32), pltpu.VMEM((ln, w), F32)],
        compiler_params=_params(1))(dtraw, xc, xc, xc, dtb, alog, dskip_x, sprev, dy)
    dx, db, dc, ddt, dbias, dalog, ddsk = outs
    return dx, db, dc, ddt, dbias[0], dalog[0], ddsk[0]


def _rope_tables(t):
    half = MLA_ROPE // 2
    inv_freq = ROPE_THETA ** (-jnp.arange(half, dtype=F32) / half)
    ang = jnp.arange(t, dtype=F32)[:, None] * inv_freq[None, :]
    cos, sin = jnp.cos(ang), jnp.sin(ang)
    pad = LANES - MLA_ROPE
    cos_r = jnp.concatenate([cos, cos, jnp.ones((t, pad), F32)], axis=1)
    sin_r = jnp.concatenate([sin, sin, jnp.zeros((t, pad), F32)], axis=1)
    return jnp.stack([jnp.ones((t, LANES), F32), cos_r]), jnp.stack([jnp.zeros((t, LANES), F32), sin_r])


def _rot_matrix():
    half = MLA_ROPE // 2
    i, j = _iota((LANES, LANES), 0), _iota((LANES, LANES), 1)
    neg = (j < half) & (i == j + half)
    pos = (j >= half) & (j < 2 * half) & (i == j - half)
    return (pos.astype(F32) - neg.astype(F32)).astype(BF16)


def _rope(xs, cos, sin, tabsel, transpose, out_dtype, name):
    two = len(xs) == 2
    t = xs[0].shape[0]
    nb = xs[0].shape[1] // LANES * (2 if two else 1)
    tr = _rows(t)

    def body(*refs):
        c_ref, s_ref, o_ref = refs[-3], refs[-2], refs[-1]
        rot = _rot_matrix()

        def emit(xv):
            cv, sv = c_ref[0], s_ref[0]
            if transpose:
                o_ref[...] = (xv * cv - _xdot(xv * sv, rot, 2)).astype(o_ref.dtype)
            else:
                o_ref[...] = (xv * cv + _xdot(xv, rot, 2) * sv).astype(o_ref.dtype)
        if two:
            even = pl.program_id(1) % 2 == 0

            @pl.when(even)
            def _():
                emit(refs[0][...])

            @pl.when(jnp.logical_not(even))
            def _():
                emit(refs[1][...])
        else:
            emit(refs[0][...])

    if two:
        xspecs = [pl.BlockSpec((tr, LANES), lambda i, j: (i, j // 2))] * 2
    else:
        xspecs = [pl.BlockSpec((tr, LANES), lambda i, j: (i, j))]
    tab = pl.BlockSpec((1, tr, LANES), lambda i, j: (tabsel(j), i, 0))
    return pl.pallas_call(
        body, name=name, grid=(t // tr, nb), in_specs=xspecs + [tab, tab],
        out_specs=pl.BlockSpec((tr, LANES), lambda i, j: (i, j)),
        out_shape=jax.ShapeDtypeStruct((t, nb * LANES), out_dtype), compiler_params=_params(2))(*xs, cos, sin)


def _att_masks(blk):
    return _iota((blk, blk), 0), _iota((blk, blk), 1)


def _mla_fwd(q, kr, kv, name):
    t = q.shape[0]
    nh, blk = MLA_HEADS, min(ATT_BLK, t)
    scale = (MLA_NOPE + MLA_ROPE) ** -0.5

    def body(qn_ref, qp_ref, kn_ref, v_ref, kr_ref, o_ref, lse_ref):
        i = pl.program_id(1)
        qn, qp = qn_ref[...], qp_ref[...]
        row, col = _att_masks(blk)

        def scores(kb):
            ks = pl.ds(pl.multiple_of(kb * blk, blk), blk)
            return (_dot(qn, kn_ref[ks, :], "nt") + _dot(qp, kr_ref[ks, :], "nt")) * scale, v_ref[ks, :]
        s, v = scores(i)
        s = jnp.where(col <= row, s, NEG)
        m = jnp.max(s, axis=1, keepdims=True)
        pr = jnp.exp(s - m)
        l = jnp.sum(pr, axis=1, keepdims=True)
        acc = _dot(pr.astype(BF16), v)

        def step(kb, carry):
            m, l, acc = carry
            s, v = scores(kb)
            m2 = jnp.maximum(m, jnp.max(s, axis=1, keepdims=True))
            al = jnp.exp(m - m2)
            pr = jnp.exp(s - m2)
            return m2, al * l + jnp.sum(pr, axis=1, keepdims=True), al * acc + _dot(pr.astype(BF16), v)
        m, l, acc = lax.fori_loop(0, i, step, (m, l, acc))
        o_ref[...] = acc / l
        lse_ref[0] = m + jnp.log(l)

    full = lambda f: pl.BlockSpec((t, LANES), f)
    return pl.pallas_call(
        body, name=name, grid=(nh, t // blk),
        in_specs=[pl.BlockSpec((blk, LANES), lambda h, i: (i, 2 * h)), pl.BlockSpec((blk, LANES), lambda h, i: (i, 2 * h + 1)),
                  full(lambda h, i: (0, 2 * h)), full(lambda h, i: (0, 2 * h + 1)), full(lambda h, i: (0, 0))],
        out_specs=[pl.BlockSpec((blk, LANES), lambda h, i: (i, h)), pl.BlockSpec((1, blk, 1), lambda h, i: (h, i, 0))],
        out_shape=[jax.ShapeDtypeStruct((t, nh * LANES), F32), jax.ShapeDtypeStruct((nh, t, 1), F32)],
        compiler_params=_params(2))(q, q, kv, kv, kr)


def _mla_bwd(q, kr, kv, o, do, lse, name):
    t = q.shape[0]
    nh, blk = MLA_HEADS, min(ATT_BLK, t)
    scale = (MLA_NOPE + MLA_ROPE) ** -0.5

    def body(qn_ref, qp_ref, kn_ref, v_ref, kr_ref, o_ref, do_ref, lse_ref, dqn_ref, dqp_ref, dkn_ref, dv_ref, dkr_ref):
        hh, i = pl.program_id(0), pl.program_id(1)

        @pl.when(i == 0)
        def _():
            dkn_ref[...] = jnp.zeros_like(dkn_ref)
            dv_ref[...] = jnp.zeros_like(dv_ref)

        @pl.when((i == 0) & (hh == 0))
        def _():
            dkr_ref[...] = jnp.zeros_like(dkr_ref)
        qn, qp = qn_ref[...], qp_ref[...]
        dov = do_ref[...]
        dob = dov.astype(BF16)
        delta = jnp.sum(dov * o_ref[...], axis=1, keepdims=True)
        lse_v = lse_ref[0]
        row, col = _att_masks(blk)

        def tile(kb, carry, masked):
            dqn, dqp = carry
            ks = pl.ds(pl.multiple_of(kb * blk, blk), blk)
            kn, krv, v = kn_ref[ks, :], kr_ref[ks, :], v_ref[ks, :]
            s = (_dot(qn, kn, "nt") + _dot(qp, krv, "nt")) * scale
            pr = jnp.exp(s - lse_v)
            if masked:
                pr = jnp.where(col <= row, pr, 0.0)
            dv_ref[ks, :] += _dot(pr.astype(BF16), dob, "tn")
            dp = _dot(dob, v, "nt")
            dsb = (pr * (dp - delta) * scale).astype(BF16)
            dkn_ref[ks, :] += _dot(dsb, qn, "tn")
            dkr_ref[ks, :] += _dot(dsb, qp, "tn")
            return dqn + _dot(dsb, kn), dqp + _dot(dsb, krv)
        zero = jnp.zeros((blk, LANES), F32)
        carry = lax.fori_loop(0, i, lambda kb, c: tile(kb, c, False), (zero, zero))
        dqn, dqp = tile(i, carry, True)
        dqn_ref[...] = dqn
        dqp_ref[...] = dqp

    full = lambda f: pl.BlockSpec((t, LANES), f)
    qb = lambda f: pl.BlockSpec((blk, LANES), f)
    wide = jax.ShapeDtypeStruct((t, nh * LANES), F32)
    return pl.pallas_call(
        body, name=name, grid=(nh, t // blk),
        in_specs=[qb(lambda h, i: (i, 2 * h)), qb(lambda h, i: (i, 2 * h + 1)),
                  full(lambda h, i: (0, 2 * h)), full(lambda h, i: (0, 2 * h + 1)), full(lambda h, i: (0, 0)),
                  qb(lambda h, i: (i, h)), qb(lambda h, i: (i, h)), pl.BlockSpec((1, blk, 1), lambda h, i: (h, i, 0))],
        out_specs=[qb(lambda h, i: (i, h)), qb(lambda h, i: (i, h)), full(lambda h, i: (0, h)), full(lambda h, i: (0, h)),
                   full(lambda h, i: (0, 0))],
        out_shape=[wide, wide, wide, wide, jax.ShapeDtypeStruct((t, LANES), F32)],
        compiler_params=_params(2))(q, q, kv, kv, kr, o, do, lse)


def _sb_fwd(qkv, name):
    t = qkv.shape[0]
    nh, blk = SB_HEADS, min(ATT_BLK, t)
    scale = SB_HEAD_DIM ** -0.5

    def body(q_ref, k_ref, v_ref, o_ref, lt_ref):
        i = pl.program_id(1)
        qv = q_ref[...]
        row, col = _att_masks(blk)
        usuf = (row > col).astype(BF16)

        def tile(kb, carry, masked):
            acc, run = carry
            ks = pl.ds(pl.multiple_of(kb * blk, blk), blk)
            z = _dot(qv, k_ref[ks, :], "nt") * scale
            lk = -_softplus(z)
            ls = lk + z
            if masked:
                lk = jnp.where(col < row, lk, 0.0)
            wt = jnp.exp(ls + _xdot(lk, usuf, 2) + run)
            if masked:
                wt = jnp.where(col < row, wt, 0.0)
            return acc + _dot(wt.astype(BF16), v_ref[ks, :]), run + jnp.sum(lk, axis=1, keepdims=True)
        carry = tile(i, (jnp.zeros((blk, LANES), F32), jnp.zeros((blk, 1), F32)), True)
        acc, run = lax.fori_loop(0, i, lambda j, c: tile(i - 1 - j, c, False), carry)
        o_ref[...] = acc
        lt_ref[0] = run

    full = lambda f: pl.BlockSpec((t, LANES), f)
    return pl.pallas_call(
        body, name=name, grid=(nh, t // blk),
        in_specs=[pl.BlockSpec((blk, LANES), lambda h, i: (i, h)), full(lambda h, i: (0, nh + h)), full(lambda h, i: (0, 2 * nh + h))],
        out_specs=[pl.BlockSpec((blk, LANES), lambda h, i: (i, h)), pl.BlockSpec((1, blk, 1), lambda h, i: (h, i, 0))],
        out_shape=[jax.ShapeDtypeStruct((t, nh * LANES), F32), jax.ShapeDtypeStruct((nh, t, 1), F32)],
        compiler_params=_params(2))(qkv, qkv, qkv)


def _sb_bwd(qkv, do, ltot, name):
    t = qkv.shape[0]
    nh, blk = SB_HEADS, min(ATT_BLK, t)
    scale = SB_HEAD_DIM ** -0.5

    def body(q_ref, k_ref, v_ref, do_ref, lt_ref, dq_ref, dk_ref, dv_ref):
        i = pl.program_id(1)

        @pl.when(i == 0)
        def _():
            dk_ref[...] = jnp.zeros_like(dk_ref)
            dv_ref[...] = jnp.zeros_like(dv_ref)
        qv = q_ref[...]
        dob = do_ref[...].astype(BF16)
        lt = lt_ref[0]
        row, col = _att_masks(blk)
        uinc = (row <= col).astype(BF16)
        uexc = (row < col).astype(BF16)

        def tile(kb, carry, masked):
            dq, pre, ec = carry
            ks = pl.ds(pl.multiple_of(kb * blk, blk), blk)
            kv_, vv = k_ref[ks, :], v_ref[ks, :]
            z = _dot(qv, kv_, "nt") * scale
            lk = -_softplus(z)
            ls = lk + z
            sig = jnp.exp(ls)
            if masked:
                lk = jnp.where(col < row, lk, 0.0)
            later = lt - (pre + _xdot(lk, uinc, 2))
            wt = jnp.exp(ls + later)
            if masked:
                wt = jnp.where(col < row, wt, 0.0)
            dv_ref[ks, :] += _dot(wt.astype(BF16), dob, "tn")
            ev = _dot(dob, vv, "nt") * wt
            ecs = _xdot(ev, uexc, 2) + ec
            dz = ev * (1.0 - sig) - ecs * sig
            if masked:
                dz = jnp.where(col < row, dz, 0.0)
            dzb = (dz * scale).astype(BF16)
            dk_ref[ks, :] += _dot(dzb, qv, "tn")
            return dq + _dot(dzb, kv_), pre + jnp.sum(lk, axis=1, keepdims=True), ec + jnp.sum(ev, axis=1, keepdims=True)
        z1 = jnp.zeros((blk, 1), F32)
        carry = lax.fori_loop(0, i, lambda kb, c: tile(kb, c, False), (jnp.zeros((blk, LANES), F32), z1, z1))
        dq_ref[...] = tile(i, carry, True)[0]

    full = lambda f: pl.BlockSpec((t, LANES), f)
    qb = pl.BlockSpec((blk, LANES), lambda h, i: (i, h))
    wide = jax.ShapeDtypeStruct((t, nh * LANES), F32)
    return pl.pallas_call(
        body, name=name, grid=(nh, t // blk),
        in_specs=[qb, full(lambda h, i: (0, nh + h)), full(lambda h, i: (0, 2 * nh + h)), qb,
                  pl.BlockSpec((1, blk, 1), lambda h, i: (h, i, 0))],
        out_specs=[qb, full(lambda h, i: (0, h)), full(lambda h, i: (0, h))],
        out_shape=[wide, wide, wide], compiler_params=_params(2))(qkv, qkv, qkv, do, ltot)


def _as2d(a):
    return a.reshape((-1, a.shape[-1]))


def _sum4(parts, name):
    _, r, c = parts.shape
    tr = _tile(r, (256, 128, 64, 32, 16, 8))

    def body(p_ref, o_ref):
        acc = p_ref[0].astype(F32)
        for j in range(1, 4):
            acc = acc + p_ref[j].astype(F32)
        o_ref[...] = acc

    return pl.pallas_call(body, name=name, grid=(r // tr,), in_specs=[pl.BlockSpec((4, tr, c), lambda i: (0, i, 0))],
                          out_specs=pl.BlockSpec((tr, c), lambda i: (i, 0)), out_shape=jax.ShapeDtypeStruct((r, c), F32),
                          compiler_params=_params(1))(parts)


def _adamw(w, ga, gb, m, v, name):
    r, c = w.shape
    tr = _tile(r, (128, 64, 32, 16, 8))
    c1, c2 = 1.0 - ADAM_B1 ** ADAM_STEP, 1.0 - ADAM_B2 ** ADAM_STEP
    two = gb is not None

    def body(*refs):
        w_ref, ga_ref = refs[0], refs[1]
        gb_ref = refs[2] if two else None
        m_ref, v_ref, g_out, d_out, m_out, v_out = refs[2 + two:]
        gv = ga_ref[...]
        if two:
            gv = gv + gb_ref[...]
        mn = ADAM_B1 * m_ref[...] + (1.0 - ADAM_B1) * gv
        vn = ADAM_B2 * v_ref[...] + (1.0 - ADAM_B2) * (gv * gv)
        g_out[...] = gv
        m_out[...] = mn
        v_out[...] = vn
        d_out[...] = -ADAM_LR * ((mn / c1) / (jnp.sqrt(vn / c2) + ADAM_EPS) + ADAM_WD * w_ref[...])

    blk = pl.BlockSpec((tr, c), lambda i: (i, 0))
    ins = [w, ga] + ([gb] if two else []) + [m, v]
    return pl.pallas_call(body, name=name, grid=(r // tr,), in_specs=[blk] * len(ins), out_specs=[blk] * 4,
                          out_shape=[jax.ShapeDtypeStruct((r, c), F32)] * 4, compiler_params=_params(1))(*ins)


def _chip_exchange(arrs, whole, name):
    n = len(arrs)

    def body(*refs):
        ins, outs = refs[:n], refs[n:2 * n]
        send_sems, recv_sems, loc_sems = refs[2 * n:]
        x, y, c = lax.axis_index("x"), lax.axis_index("y"), lax.axis_index("c")
        me = 2 * x + y
        peers = [(1 - x, y), (x, 1 - y), (1 - x, 1 - y)]
        locs, sends = [], []
        for wi in range(n):
            src = ins[wi] if whole else ins[wi].at[me]
            cp = pltpu.make_async_copy(src, outs[wi].at[me], loc_sems.at[wi])
            cp.start()
            locs.append(cp)
        for wi in range(n):
            for k, (px, py) in enumerate(peers):
                src = ins[wi] if whole else ins[wi].at[2 * px + py]
                cp = pltpu.make_async_remote_copy(
                    src_ref=src, dst_ref=outs[wi].at[me], send_sem=send_sems.at[wi * 3 + k], recv_sem=recv_sems.at[wi * 3 + k],
                    device_id=(px, py, c), device_id_type=MESH_T)
                cp.start()
                sends.append(cp)
        for wi in range(n):
            for k, (px, py) in enumerate(peers):
                src = ins[wi] if whole else ins[wi].at[me]
                pltpu.make_async_remote_copy(
                    src_ref=src, dst_ref=outs[wi].at[2 * px + py], send_sem=send_sems.at[wi * 3 + k], recv_sem=recv_sems.at[wi * 3 + k],
                    device_id=(px, py, c), device_id_type=MESH_T).wait_recv()
        for cp in sends:
            cp.wait_send()
        for cp in locs:
            cp.wait()

    shapes = [jax.ShapeDtypeStruct(((4,) + a.shape) if whole else a.shape, a.dtype) for a in arrs]
    anyspec = pl.BlockSpec(memory_space=pl.ANY)
    return pl.pallas_call(
        body, name=name, in_specs=[anyspec] * n, out_specs=[anyspec] * n, out_shape=shapes,
        scratch_shapes=[pltpu.SemaphoreType.DMA((3 * n,)), pltpu.SemaphoreType.DMA((3 * n,)), pltpu.SemaphoreType.DMA((n,))],
        compiler_params=pltpu.CompilerParams(has_side_effects=True))(*arrs)


def _sibling_exchange(arrs, name):
    n = len(arrs)

    def body(*refs):
        ins, outs = refs[:n], refs[n:2 * n]
        send_sems, recv_sems = refs[2 * n:]
        sib = (lax.axis_index("x"), lax.axis_index("y"), 1 - lax.axis_index("c"))
        cps = [pltpu.make_async_remote_copy(src_ref=ins[wi], dst_ref=outs[wi], send_sem=send_sems.at[wi], recv_sem=recv_sems.at[wi],
                                            device_id=sib, device_id_type=MESH_T) for wi in range(n)]
        for cp in cps:
            cp.start()
        for cp in cps:
            cp.wait_recv()
        for cp in cps:
            cp.wait_send()

    anyspec = pl.BlockSpec(memory_space=pl.ANY)
    return pl.pallas_call(
        body, name=name, in_specs=[anyspec] * n, out_specs=[anyspec] * n,
        out_shape=[jax.ShapeDtypeStruct(a.shape, a.dtype) for a in arrs],
        scratch_shapes=[pltpu.SemaphoreType.DMA((n,)), pltpu.SemaphoreType.DMA((n,))],
        compiler_params=pltpu.CompilerParams(has_side_effects=True))(*arrs)


def _allreduce_small(packed, name):
    r = packed.shape[0]

    def body(in_ref, out_ref, land, send_sems, recv_sems):
        x, y, c = lax.axis_index("x"), lax.axis_index("y"), lax.axis_index("c")
        me = 4 * x + 2 * y + c
        land[me] = in_ref[...]
        rel = [(dx, dy, dc) for dx in (0, 1) for dy in (0, 1) for dc in (0, 1)][1:]
        peers = [((1 - x) if dx else x, (1 - y) if dy else y, (1 - c) if dc else c) for dx, dy, dc in rel]
        sends = []
        for k, peer in enumerate(peers):
            cp = pltpu.make_async_remote_copy(src_ref=in_ref, dst_ref=land.at[me], send_sem=send_sems.at[k], recv_sem=recv_sems.at[k],
                                              device_id=peer, device_id_type=MESH_T)
            cp.start()
            sends.append(cp)
        for k, (px, py, pc) in enumerate(peers):
            pltpu.make_async_remote_copy(src_ref=in_ref, dst_ref=land.at[4 * px + 2 * py + pc], send_sem=send_sems.at[k],
                                         recv_sem=recv_sems.at[k], device_id=(px, py, pc), device_id_type=MESH_T).wait_recv()
        for cp in sends:
            cp.wait_send()
        acc = land[0]
        for j in range(1, 8):
            acc = acc + land[j]
        out_ref[...] = acc

    vm = pl.BlockSpec(memory_space=pltpu.VMEM)
    return pl.pallas_call(
        body, name=name, in_specs=[vm], out_specs=vm, out_shape=jax.ShapeDtypeStruct((r, LANES), F32),
        scratch_shapes=[pltpu.VMEM((8, r, LANES), F32), pltpu.SemaphoreType.DMA((7,)), pltpu.SemaphoreType.DMA((7,))],
        compiler_params=pltpu.CompilerParams(has_side_effects=True))(packed)


def _in_splits():
    w = SSD_HEADS * SSD_HEAD_DIM
    cc = w + 2 * SSD_GROUPS * SSD_STATE
    return [w, cc, SSD_HEADS, MLA_Q_RANK, MLA_KV_RANK, MLA_ROPE]


def _padc(a, n):
    return jnp.pad(a, ((0, 0), (0, n - a.shape[1])))


def _win_pad(wm):
    offs = np.cumsum(_in_splits())[:-1]
    z, xbc, dt, cq, ckv, kr = jnp.split(wm, offs, axis=1)
    return jnp.concatenate([z, xbc, cq, ckv, _padc(kr, LANES), _padc(dt, LANES)], axis=1)


def _win_unpad(g):
    w, cc, nh, qr, kvr, rp = _in_splits()
    offs = np.cumsum([w, cc, qr, kvr, LANES])
    z, xbc, cq, ckv, kr, dt = jnp.split(g, offs, axis=1)
    return jnp.concatenate([z, xbc, dt[:, :nh], cq, ckv, kr[:, :rp]], axis=1)


def _wuq_pad(wm):
    r = wm.shape[0]
    w3 = wm.reshape(r, MLA_HEADS, MLA_NOPE + MLA_ROPE)
    return jnp.pad(w3, ((0, 0), (0, 0), (0, 2 * LANES - MLA_NOPE - MLA_ROPE))).reshape(r, MLA_HEADS * 2 * LANES)


def _wuq_unpad(g):
    r = g.shape[0]
    return g.reshape(r, MLA_HEADS, 2 * LANES)[:, :, :MLA_NOPE + MLA_ROPE].reshape(r, MLA_HEADS * (MLA_NOPE + MLA_ROPE))


def _full_from_gather(name, g):
    col = name in COL_SHARDED
    layers = g.shape[1]
    return [jnp.concatenate([g[j, l] for j in range(4)], axis=1 if col else 0) for l in range(layers)]


def _shards_from_full(name, mats):
    col = name in COL_SHARDED
    per = []
    for mt in mats:
        r, c = mt.shape
        per.append(mt.reshape(r, 4, c // 4).transpose(1, 0, 2) if col else mt.reshape(4, r // 4, c))
    return jnp.stack(per, axis=1).astype(BF16)


def _pack_small(vals):
    flat = jnp.concatenate([vals[n].reshape(-1).astype(F32) for n in SMALL])
    rows = -(-flat.shape[0] // (8 * LANES)) * 8
    return jnp.pad(flat, (0, rows * LANES - flat.shape[0])).reshape(rows, LANES)


def _unpack_small(packed, like):
    flat, out, off = packed.reshape(-1), {}, 0
    for n in SMALL:
        sz = like[n].size
        out[n] = flat[off:off + sz].reshape(like[n].shape)
        off += sz
    return out


def _ffn_fwd(x, norm_g, wg, wu, wd, tag):
    h, r = _rms_fwd(x, norm_g, f"ffn_norm_{tag}")
    gate = _mm(h, wg, "nn", f"ffn_gate_{tag}")
    up = _mm(h, wu, "nn", f"ffn_up_{tag}")
    act = _swiglu_fwd(gate, up, f"swiglu_{tag}")
    out = _mm(act, wd, "nn", f"ffn_down_{tag}", add=x)
    return out, (x, r, h, gate, up, act)


def _ffn_bwd(dout, norm_g, wg, wu, wd, saved, tag):
    x, r, h, gate, up, act = saved
    dact = _mm(dout, wd, "nt", f"ffn_dact_{tag}")
    dgate, dup = _swiglu_bwd(gate, up, dact, f"swiglu_bwd_{tag}")
    d_wd = _mm(act, dout, "tn", f"ffn_dwd_{tag}", out_dtype=BF16)
    d_wg = _mm(h, dgate, "tn", f"ffn_dwg_{tag}", out_dtype=BF16)
    d_wu = _mm(h, dup, "tn", f"ffn_dwu_{tag}", out_dtype=BF16)
    dh = _mm(dgate, wg, "nt", f"ffn_dh1_{tag}")
    dh = _mm(dup, wu, "nt", f"ffn_dh2_{tag}", add=dh)
    dx, dnorm = _rms_bwd(x, norm_g, r, dh, f"ffn_norm_bwd_{tag}", dres=dout)
    return dx, dnorm, d_wg, d_wu, d_wd


def kernel(x, mix_norm, ffn_norm, w_in, conv_w, conv_b, dt_bias, a_log, d_skip, ssd_norm, q_norm, kv_norm, w_uq, w_ukv, w_out_even, w_qkv, w_out_odd, w_gate, w_up, w_down, final_norm, loss_target, m_mix_norm, m_ffn_norm, m_w_in, m_conv_w, m_conv_b, m_dt_bias, m_a_log, m_d_skip, m_ssd_norm, m_q_norm, m_kv_norm, m_w_uq, m_w_ukv, m_w_out_even, m_w_qkv, m_w_out_odd, m_w_gate, m_w_up, m_w_down, m_final_norm, v_mix_norm, v_ffn_norm, v_w_in, v_conv_w, v_conv_b, v_dt_bias, v_a_log, v_d_skip, v_ssd_norm, v_q_norm, v_kv_norm, v_w_uq, v_w_ukv, v_w_out_even, v_w_qkv, v_w_out_odd, v_w_gate, v_w_up, v_w_down, v_final_norm):
    given = dict(locals())
    wts = {n: given[n] for n in WEIGHTS}
    x0 = x[0]
    tgt = loss_target[0]
    t, d = x0.shape
    hw = SSD_HEADS * SSD_HEAD_DIM
    gn = SSD_GROUPS * SSD_STATE
    cc = hw + 2 * gn
    qr, kvr = MLA_Q_RANK, MLA_KV_RANK

    shards = [wts[n] if n == 'conv_w' else wts[n].astype(BF16) for n in SHARDED]
    gathered = dict(zip(SHARDED, _chip_exchange(shards, True, "gather_weights")))
    full = {n: _full_from_gather(n, gathered[n]) for n in SHARDED}
    win = _win_pad(full['w_in'][0])
    wuq = _wuq_pad(full['w_uq'][0])
    wukv, woe, wqkv, woo = full['w_ukv'][0], full['w_out_even'][0], full['w_qkv'][0], full['w_out_odd'][0]
    cw = _padc(full['conv_w'][0].T, 8).T
    o_cq, o_ckv, o_kr, o_dt = hw + cc, hw + cc + qr, hw + cc + qr + kvr, hw + cc + qr + kvr + LANES

    row = lambda v: v.reshape(1, -1)
    narrow = lambda v: _padc(v.reshape(1, -1), LANES)
    dtb, alog = narrow(dt_bias[0]), narrow(a_log[0])
    dsk_x = jnp.repeat(d_skip[0], SSD_HEAD_DIM).reshape(1, hw)
    cos, sin = _rope_tables(t)

    h0, r0 = _rms_fwd(x0, row(mix_norm[0]), "mix_norm_0")
    u = _mm(h0, win, "nn", "in_proj")
    z, xbc, c_q, c_kv = u[:, :hw], u[:, hw:hw + cc], u[:, o_cq:o_ckv], u[:, o_ckv:o_kr]
    kr_raw, dtraw = u[:, o_kr:o_dt], u[:, o_dt:]
    xc = _conv_fwd(xbc, cw, row(conv_b[0]), "conv")
    y_ssd, sprev = _ssd_fwd(dtraw, xc, dtb, alog, dsk_x, "ssd")
    yg, r_g = _gated_fwd(y_ssd, z, row(ssd_norm[0]), "ssd_gate_norm")
    q_lat, r_q = _rms_fwd(c_q, row(q_norm[0]), "q_norm")
    kv_lat, r_kv = _rms_fwd(c_kv, row(kv_norm[0]), "kv_norm")
    qf = _mm(q_lat, wuq, "nn", "q_up")
    kvb = _mm(kv_lat, wukv, "nn", "kv_up", out_dtype=BF16)
    q_r = _rope([qf], cos, sin, lambda j: j % 2, False, BF16, "rope_q")
    k_r = _rope([kr_raw], cos, sin, lambda j: 1, False, BF16, "rope_k")
    o_mla, lse = _mla_fwd(q_r, k_r, kvb, "mla")
    cat = jnp.concatenate([yg, o_mla.astype(BF16)], axis=1)
    x1 = _mm(cat, woe, "nn", "mix_out_0", add=x0)
    x2, ffn0 = _ffn_fwd(x1, row(ffn_norm[0]), full['w_gate'][0], full['w_up'][0], full['w_down'][0], "0")

    h1, r1 = _rms_fwd(x2, row(mix_norm[1]), "mix_norm_1")
    qkv = _mm(h1, wqkv, "nn", "qkv_proj", out_dtype=BF16)
    o_sb, ltot = _sb_fwd(qkv, "sb")
    o_sbb = o_sb.astype(BF16)
    x3 = _mm(o_sbb, woo, "nn", "mix_out_1", add=x2)
    x4, ffn1 = _ffn_fwd(x3, row(ffn_norm[1]), full['w_gate'][1], full['w_up'][1], full['w_down'][1], "1")

    loss_part, dx4, d_final = _loss_fwd_bwd(x4, row(final_norm), tgt, "loss")
    loss = lax.psum(loss_part, ("x", "y", "c"))

    dx3, d_ffn1, d_wg1, d_wu1, d_wd1 = _ffn_bwd(dx4, row(ffn_norm[1]), full['w_gate'][1], full['w_up'][1], full['w_down'][1], ffn1, "1")
    do_sb = _mm(dx3, woo, "nt", "sb_dout")
    d_woo = _mm(o_sbb, dx3, "tn", "d_w_out_odd", out_dtype=BF16)
    dq, dk, dv = _sb_bwd(qkv, do_sb, ltot, "sb_bwd")
    dqkv = jnp.concatenate([dq, dk, dv], axis=1).astype(BF16)
    d_wqkv = _mm(h1, dqkv, "tn", "d_w_qkv", out_dtype=BF16)
    dh1 = _mm(dqkv, wqkv, "nt", "qkv_dh")
    dx2, d_mix1 = _rms_bwd(x2, row(mix_norm[1]), r1, dh1, "mix_norm_bwd_1", dres=dx3)

    dx1, d_ffn0, d_wg0, d_wu0, d_wd0 = _ffn_bwd(dx2, row(ffn_norm[0]), full['w_gate'][0], full['w_up'][0], full['w_down'][0], ffn0, "0")
    dcat = _mm(dx1, woe, "nt", "mix_dcat")
    d_woe = _mm(cat, dx1, "tn", "d_w_out_even", out_dtype=BF16)
    dy_ssd, dz, d_ssdn = _gated_bwd(y_ssd, z, row(ssd_norm[0]), r_g, dcat[:, :hw], "ssd_gate_norm_bwd")
    dxs, db_, dc_, ddtraw, d_dtb, d_alog, d_dsk = _ssd_bwd(dtraw, xc, dtb, alog, dsk_x, sprev, dy_ssd, "ssd_bwd")
    dxbc, d_cw, d_cb = _conv_bwd(xbc, cw, row(conv_b[0]), jnp.concatenate([dxs, db_, dc_], axis=1), "conv_bwd")
    dqn, dqp, dkn, dvv, dkr = _mla_bwd(q_r, k_r, kvb, o_mla, dcat[:, hw:], lse, "mla_bwd")
    dqf = _rope([dqn, dqp], cos, sin, lambda j: j % 2, True, BF16, "rope_q_bwd")
    dkr_raw = _rope([dkr], cos, sin, lambda j: 1, True, F32, "rope_k_bwd")
    nhm = MLA_HEADS
    dkvf = jnp.stack([dkn.reshape(t, nhm, LANES), dvv.reshape(t, nhm, LANES)], axis=2).reshape(t, 2 * nhm * LANES).astype(BF16)
    d_wuq = _mm(q_lat, dqf, "tn", "d_w_uq", out_dtype=BF16)
    d_wukv = _mm(kv_lat, dkvf, "tn", "d_w_ukv", out_dtype=BF16)
    dq_lat = _mm(dqf, wuq, "nt", "q_up_bwd")
    dkv_lat = _mm(dkvf, wukv, "nt", "kv_up_bwd")
    dc_q, d_qn = _rms_bwd(c_q, row(q_norm[0]), r_q, dq_lat, "q_norm_bwd")
    dc_kv, d_kvn = _rms_bwd(c_kv, row(kv_norm[0]), r_kv, dkv_lat, "kv_norm_bwd")
    du = jnp.concatenate([dz, dxbc, dc_q, dc_kv, dkr_raw, ddtraw], axis=1).astype(BF16)
    d_win = _mm(h0, du, "tn", "d_w_in", out_dtype=BF16)
    dh0 = _mm(du, win, "nt", "in_proj_bwd")
    grad_x, d_mix0 = _rms_bwd(x0, row(mix_norm[0]), r0, dh0, "mix_norm_bwd_0", dres=dx1)

    gfull = {'w_in': [_win_unpad(d_win)], 'conv_w': [d_cw], 'w_uq': [_wuq_unpad(d_wuq)], 'w_ukv': [d_wukv], 'w_out_even': [d_woe],
             'w_qkv': [d_wqkv], 'w_out_odd': [d_woo], 'w_gate': [d_wg0, d_wg1], 'w_up': [d_wu0, d_wu1], 'w_down': [d_wd0, d_wd1]}
    parts = _chip_exchange([_shards_from_full(n, gfull[n]) for n in SHARDED], False, "exchange_grads")
    psum = [_sum4(p.reshape(4, -1, p.shape[-1]), f"sum_{n}") for n, p in zip(SHARDED, parts)]
    sib = _sibling_exchange(psum, "exchange_sibling")

    nhs = SSD_HEADS
    small_g = {'mix_norm': jnp.stack([d_mix0, d_mix1]), 'ffn_norm': jnp.stack([d_ffn0, d_ffn1]), 'conv_b': d_cb[None],
               'dt_bias': d_dtb[None, :nhs], 'a_log': d_alog[None, :nhs], 'd_skip': d_dsk[None, :nhs], 'ssd_norm': d_ssdn[None],
               'q_norm': d_qn[None], 'kv_norm': d_kvn[None], 'final_norm': d_final}
    g_small = _allreduce_small(_pack_small(small_g), "allreduce_small")
    pk = lambda pre: _pack_small({n: given[pre + n] for n in SMALL})
    sg, sd, sm, sv = _adamw(pk(''), g_small, None, pk('m_'), pk('v_'), "adamw_small")

    grads, deltas, new_m, new_v = {}, {}, {}, {}
    for n, ga, gb in zip(SHARDED, psum, sib):
        shp = wts[n].shape
        res = _adamw(_as2d(wts[n]), ga, gb, _as2d(given['m_' + n]), _as2d(given['v_' + n]), f"adamw_{n}")
        grads[n], deltas[n], new_m[n], new_v[n] = [r.reshape(shp) for r in res]
    for dst, src in ((grads, sg), (deltas, sd), (new_m, sm), (new_v, sv)):
        dst.update(_unpack_small(src, wts))

    outs = [loss, grad_x[None]]
    for dct in (grads, deltas, new_m, new_v):
        outs += [dct[n] for n in WEIGHTS]
    return tuple(outs)
```

```python
import functools
import math

import jax
import jax.numpy as jnp
import numpy as np
from jax import lax
from jax.experimental import pallas as pl
from jax.experimental.pallas import tpu as pltpu

F32, BF16 = jnp.float32, jnp.bfloat16

RMS_EPS = 1e-6
SSD_HEADS, SSD_HEAD_DIM, SSD_GROUPS, SSD_STATE, SSD_CONV, SSD_CHUNK = 32, 64, 4, 128, 4, 128
MLA_HEADS, MLA_Q_RANK, MLA_KV_RANK, MLA_NOPE, MLA_ROPE, MLA_V = 16, 512, 512, 128, 64, 128
ROPE_THETA = 10000.0
SB_HEADS, SB_HEAD_DIM = 16, 128
ADAM_LR, ADAM_B1, ADAM_B2, ADAM_EPS, ADAM_WD, ADAM_STEP = 0.001, 0.9, 0.999, 1e-08, 0.01, 10

LANES = 128
VMEM_LIMIT_BYTES = 56 * 1024 * 1024
MM_VMEM_BUDGET = 40 * 1024 * 1024
ATT_BLK = 256
ROW_TILE = 256
NEG = -1e30

MESH_T = pl.DeviceIdType.MESH
WEIGHTS = ['mix_norm', 'ffn_norm', 'w_in', 'conv_w', 'conv_b', 'dt_bias', 'a_log', 'd_skip', 'ssd_norm', 'q_norm',
           'kv_norm', 'w_uq', 'w_ukv', 'w_out_even', 'w_qkv', 'w_out_odd', 'w_gate', 'w_up', 'w_down', 'final_norm']
SHARDED = ['w_in', 'conv_w', 'w_uq', 'w_ukv', 'w_out_even', 'w_qkv', 'w_out_odd', 'w_gate', 'w_up', 'w_down']
COL_SHARDED = ['w_in', 'conv_w', 'w_uq', 'w_ukv', 'w_qkv', 'w_gate', 'w_up']
SMALL = [n for n in WEIGHTS if n not in SHARDED]


def _tile(n, cands):
    for c in cands:
        if n % c == 0:
            return c
    return n


def _params(ngrid):
    return pltpu.CompilerParams(dimension_semantics=("arbitrary",) * ngrid, vmem_limit_bytes=VMEM_LIMIT_BYTES)


def _dot(a, b, mode="nn"):
    dims = {"nn": (((1,), (0,)), ((), ())), "nt": (((1,), (1,)), ((), ())), "tn": (((0,), (0,)), ((), ()))}[mode]
    return lax.dot_general(a, b, dims, preferred_element_type=F32)


def _split(x, parts):
    out, r = [], x
    for _ in range(parts):
        p = r.astype(BF16)
        out.append(p)
        r = r - p.astype(F32)
    return out


def _xdot(x, e, parts=3):
    acc = None
    for p in _split(x, parts):
        t = _dot(p, e)
        acc = t if acc is None else acc + t
    return acc


def _xdot_l(e, x, parts=3):
    acc = None
    for p in _split(x, parts):
        t = _dot(e, p)
        acc = t if acc is None else acc + t
    return acc


def _iota(shape, dim):
    return lax.broadcasted_iota(jnp.int32, shape, dim)


def _softplus(z):
    return jnp.maximum(z, 0.0) + jnp.log(1.0 + jnp.exp(-jnp.abs(z)))


def _sigmoid(z):
    return 1.0 / (1.0 + jnp.exp(-z))


def _acc_rows(ref, first, val):
    @pl.when(first)
    def _():
        ref[...] = jnp.zeros_like(ref)
    ref[...] += jnp.broadcast_to(val, ref.shape)


def _mm(a, b, mode, name, out_dtype=F32, add=None):
    if mode == "nn":
        (m, k), n = a.shape, b.shape[1]
    elif mode == "nt":
        (m, k), n = a.shape, b.shape[0]
    else:
        (k, m), n = a.shape, b.shape[1]
    tn = _tile(n, (1536, 1408, 1280, 1024, 768, 640, 512, 256, 128))
    tk = k if k <= 2048 else _tile(k, (2048, 1536, 1408, 1280, 1024, 512, 256, 128))
    ob = jnp.dtype(out_dtype).itemsize

    def need(tm_):
        per = tm_ * tk * a.dtype.itemsize + tk * tn * b.dtype.itemsize + tm_ * tn * ob + (tm_ * tn * 4 if add is not None else 0)
        return 2 * per + (tm_ * tn * 4 if k > tk else 0)
    tm = m
    for cand in (1024, 512, 256, 128):
        if m % cand == 0:
            tm = cand
            if need(cand) <= MM_VMEM_BUDGET:
                break
    nk = k // tk
    a_spec = {"nn": pl.BlockSpec((tm, tk), lambda i, j, kk: (i, kk)), "nt": pl.BlockSpec((tm, tk), lambda i, j, kk: (i, kk)),
              "tn": pl.BlockSpec((tk, tm), lambda i, j, kk: (kk, i))}[mode]
    b_spec = {"nn": pl.BlockSpec((tk, tn), lambda i, j, kk: (kk, j)), "nt": pl.BlockSpec((tn, tk), lambda i, j, kk: (j, kk)),
              "tn": pl.BlockSpec((tk, tn), lambda i, j, kk: (kk, j))}[mode]
    o_spec = pl.BlockSpec((tm, tn), lambda i, j, kk: (i, j))
    has_add = add is not None

    def body(*refs):
        a_ref, b_ref = refs[0], refs[1]
        add_ref = refs[2] if has_add else None
        o_ref = refs[2 + has_add]
        part = _dot(a_ref[...].astype(BF16), b_ref[...].astype(BF16), mode)
        if nk == 1:
            if has_add:
                part = part + add_ref[...]
            o_ref[...] = part.astype(o_ref.dtype)
            return
        acc_ref = refs[3 + has_add]
        kk = pl.program_id(2)

        @pl.when(kk == 0)
        def _():
            acc_ref[...] = jnp.zeros_like(acc_ref)
        acc_ref[...] += part

        @pl.when(kk == nk - 1)
        def _():
            r = acc_ref[...]
            if has_add:
                r = r + add_ref[...]
            o_ref[...] = r.astype(o_ref.dtype)

    ins = [a, b] + ([add] if has_add else [])
    specs = [a_spec, b_spec] + ([o_spec] if has_add else [])
    return pl.pallas_call(
        body, name=name, grid=(m // tm, n // tn, nk), in_specs=specs, out_specs=o_spec,
        out_shape=jax.ShapeDtypeStruct((m, n), out_dtype),
        scratch_shapes=[pltpu.VMEM((tm, tn), F32)] if nk > 1 else [],
        compiler_params=_params(3))(*ins)


def _rows(t):
    return _tile(t, (ROW_TILE, 128, 64, 32, 16, 8))


def _rms_fwd(x, g, name):
    t, c = x.shape
    tr = _rows(t)

    def body(x_ref, g_ref, h_ref, r_ref):
        xv = x_ref[...]
        r = lax.rsqrt(jnp.mean(xv * xv, axis=-1, keepdims=True) + RMS_EPS)
        h_ref[...] = (xv * r * g_ref[...]).astype(h_ref.dtype)
        r_ref[...] = r

    return pl.pallas_call(
        body, name=name, grid=(t // tr,),
        in_specs=[pl.BlockSpec((tr, c), lambda i: (i, 0)), pl.BlockSpec((1, c), lambda i: (0, 0))],
        out_specs=[pl.BlockSpec((tr, c), lambda i: (i, 0)), pl.BlockSpec((tr, 1), lambda i: (i, 0))],
        out_shape=[jax.ShapeDtypeStruct((t, c), BF16), jax.ShapeDtypeStruct((t, 1), F32)],
        compiler_params=_params(1))(x, g)


def _rms_bwd(x, g, r, dh, name, dres=None):
    t, c = x.shape
    tr = _rows(t)
    has_res = dres is not None

    def body(*refs):
        x_ref, g_ref, r_ref, dh_ref = refs[:4]
        res_ref = refs[4] if has_res else None
        dx_ref, dxb_ref, dg_ref = refs[4 + has_res:]
        rv = r_ref[...]
        xh = x_ref[...] * rv
        dhv = dh_ref[...]
        dxh = dhv * g_ref[...]
        cm = jnp.mean(dxh * xh, axis=-1, keepdims=True)
        dx = (dxh - xh * cm) * rv
        if has_res:
            dx = dx + res_ref[...]
        dx_ref[...] = dx
        dxb_ref[...] = dx.astype(BF16)
        _acc_rows(dg_ref, pl.program_id(0) == 0, jnp.sum(dhv * xh, axis=0, keepdims=True))

    row = pl.BlockSpec((tr, c), lambda i: (i, 0))
    ins = [x, g, r, dh] + ([dres] if has_res else [])
    specs = [row, pl.BlockSpec((1, c), lambda i: (0, 0)), pl.BlockSpec((tr, 1), lambda i: (i, 0)), row] + ([row] if has_res else [])
    dx, dxb, dg = pl.pallas_call(
        body, name=name, grid=(t // tr,), in_specs=specs,
        out_specs=[row, row, pl.BlockSpec((8, c), lambda i: (0, 0))],
        out_shape=[jax.ShapeDtypeStruct((t, c), F32), jax.ShapeDtypeStruct((t, c), BF16), jax.ShapeDtypeStruct((8, c), F32)],
        compiler_params=_params(1))(*ins)
    return dx, dxb, dg[0]


def _gated_fwd(y, z, g, name):
    t, c = y.shape
    tr = _rows(t)

    def body(y_ref, z_ref, g_ref, o_ref, r_ref):
        zv = z_ref[...]
        v = y_ref[...] * zv * _sigmoid(zv)
        r = lax.rsqrt(jnp.mean(v * v, axis=-1, keepdims=True) + RMS_EPS)
        o_ref[...] = (v * r * g_ref[...]).astype(o_ref.dtype)
        r_ref[...] = r

    row = pl.BlockSpec((tr, c), lambda i: (i, 0))
    return pl.pallas_call(
        body, name=name, grid=(t // tr,), in_specs=[row, row, pl.BlockSpec((1, c), lambda i: (0, 0))],
        out_specs=[row, pl.BlockSpec((tr, 1), lambda i: (i, 0))],
        out_shape=[jax.ShapeDtypeStruct((t, c), BF16), jax.ShapeDtypeStruct((t, 1), F32)],
        compiler_params=_params(1))(y, z, g)


def _gated_bwd(y, z, g, r, dout, name):
    t, c = y.shape
    tr = _rows(t)

    def body(y_ref, z_ref, g_ref, r_ref, do_ref, dy_ref, dz_ref, dg_ref):
        yv, zv, rv, dov = y_ref[...], z_ref[...], r_ref[...], do_ref[...]
        s = _sigmoid(zv)
        sz = zv * s
        xh = yv * sz * rv
        dxh = dov * g_ref[...]
        cm = jnp.mean(dxh * xh, axis=-1, keepdims=True)
        dv = (dxh - xh * cm) * rv
        dy_ref[...] = dv * sz
        dz_ref[...] = dv * yv * s * (1.0 + zv * (1.0 - s))
        _acc_rows(dg_ref, pl.program_id(0) == 0, jnp.sum(dov * xh, axis=0, keepdims=True))

    row = pl.BlockSpec((tr, c), lambda i: (i, 0))
    dy, dz, dg = pl.pallas_call(
        body, name=name, grid=(t // tr,),
        in_specs=[row, row, pl.BlockSpec((1, c), lambda i: (0, 0)), pl.BlockSpec((tr, 1), lambda i: (i, 0)), row],
        out_specs=[row, row, pl.BlockSpec((8, c), lambda i: (0, 0))],
        out_shape=[jax.ShapeDtypeStruct((t, c), F32), jax.ShapeDtypeStruct((t, c), F32), jax.ShapeDtypeStruct((8, c), F32)],
        compiler_params=_params(1))(y, z, g, r, dout)
    return dy, dz, dg[0]


def _swiglu_fwd(gate, up, name):
    t, c = gate.shape
    tr, tc = _rows(t), _tile(c, (2816, 1408, 1024, 512, 256, 128))

    def body(g_ref, u_ref, o_ref):
        gv = g_ref[...]
        o_ref[...] = (gv * _sigmoid(gv) * u_ref[...]).astype(o_ref.dtype)

    blk = pl.BlockSpec((tr, tc), lambda i, j: (i, j))
    return pl.pallas_call(body, name=name, grid=(t // tr, c // tc), in_specs=[blk, blk], out_specs=blk,
                          out_shape=jax.ShapeDtypeStruct((t, c), BF16), compiler_params=_params(2))(gate, up)


def _swiglu_bwd(gate, up, dact, name):
    t, c = gate.shape
    tr, tc = _rows(t), _tile(c, (2816, 1408, 1024, 512, 256, 128))

    def body(g_ref, u_ref, d_ref, dg_ref, du_ref):
        gv, dv = g_ref[...], d_ref[...]
        s = _sigmoid(gv)
        dg_ref[...] = (dv * u_ref[...] * s * (1.0 + gv * (1.0 - s))).astype(dg_ref.dtype)
        du_ref[...] = (dv * gv * s).astype(du_ref.dtype)

    blk = pl.BlockSpec((tr, tc), lambda i, j: (i, j))
    return pl.pallas_call(body, name=name, grid=(t // tr, c // tc), in_specs=[blk, blk, blk], out_specs=[blk, blk],
                          out_shape=[jax.ShapeDtypeStruct((t, c), BF16)] * 2, compiler_params=_params(2))(gate, up, dact)


def _loss_fwd_bwd(x, g, tgt, name):
    t, c = x.shape
    tr = _rows(t)

    def body(x_ref, g_ref, t_ref, l_ref, dx_ref, dxb_ref, dg_ref):
        xv, gv = x_ref[...], g_ref[...]
        r = lax.rsqrt(jnp.mean(xv * xv, axis=-1, keepdims=True) + RMS_EPS)
        xh = xv * r
        err = xh * gv - t_ref[...]
        per_row = jnp.mean(err * err, axis=-1, keepdims=True)
        dy = err * (1.0 / c)
        dxh = dy * gv
        cm = jnp.mean(dxh * xh, axis=-1, keepdims=True)
        dx = (dxh - xh * cm) * r
        dx_ref[...] = dx
        dxb_ref[...] = dx.astype(BF16)
        first = pl.program_id(0) == 0
        _acc_rows(dg_ref, first, jnp.sum(dy * xh, axis=0, keepdims=True))
        _acc_rows(l_ref, first, jnp.broadcast_to(0.5 * jnp.sum(per_row, axis=0, keepdims=True), (1, LANES)))

    row = pl.BlockSpec((tr, c), lambda i: (i, 0))
    lo, dx, dxb, dg = pl.pallas_call(
        body, name=name, grid=(t // tr,), in_specs=[row, pl.BlockSpec((1, c), lambda i: (0, 0)), row],
        out_specs=[pl.BlockSpec((8, LANES), lambda i: (0, 0)), row, row, pl.BlockSpec((8, c), lambda i: (0, 0))],
        out_shape=[jax.ShapeDtypeStruct((8, LANES), F32), jax.ShapeDtypeStruct((t, c), F32), jax.ShapeDtypeStruct((t, c), BF16),
                   jax.ShapeDtypeStruct((8, c), F32)],
        compiler_params=_params(1))(x, g, tgt)
    return lo[0, 0], dx, dxb, dg[0]


def _conv_specs(t, c):
    tr = _rows(t)
    tc = _tile(c, (1024, 768, 512, 256, 128))
    h8 = tr // 8
    tile = pl.BlockSpec((tr, tc), lambda j, i: (i, j))
    prev = pl.BlockSpec((8, tc), lambda j, i: (jnp.maximum(i * h8 - 1, 0), j))
    nxt = pl.BlockSpec((8, tc), lambda j, i: (jnp.minimum((i + 1) * h8, t // 8 - 1), j))
    return tr, tc, tile, prev, nxt


def _conv_fwd(xbc, w, b, name):
    t, c = xbc.shape
    tr, tc, tile, prev, _ = _conv_specs(t, c)

    def body(x_ref, p_ref, w_ref, b_ref, o_ref, buf):
        i = pl.program_id(1)
        buf[0:8, :] = jnp.where(i > 0, p_ref[...], 0.0)
        buf[8:, :] = x_ref[...]
        pre = b_ref[...]
        for k in range(SSD_CONV):
            pre = pre + w_ref[k:k + 1, :] * buf[pl.ds(8 - (SSD_CONV - 1) + k, tr), :]
        o_ref[...] = pre * _sigmoid(pre)

    return pl.pallas_call(
        body, name=name, grid=(c // tc, t // tr),
        in_specs=[tile, prev, pl.BlockSpec((8, tc), lambda j, i: (0, j)), pl.BlockSpec((1, tc), lambda j, i: (0, j))],
        out_specs=tile, out_shape=jax.ShapeDtypeStruct((t, c), F32),
        scratch_shapes=[pltpu.VMEM((tr + 8, tc), F32)], compiler_params=_params(2))(xbc, xbc, w, b)


def _conv_bwd(xbc, w, b, dout, name):
    t, c = xbc.shape
    tr, tc, tile, prev, nxt = _conv_specs(t, c)
    nt = t // tr
    kc = SSD_CONV

    def body(x_ref, p_ref, n_ref, w_ref, b_ref, d_ref, dn_ref, dx_ref, dw_ref, db_ref, buf, dbuf):
        i = pl.program_id(1)
        last = i == nt - 1
        buf[0:8, :] = jnp.where(i > 0, p_ref[...], 0.0)
        buf[8:8 + tr, :] = x_ref[...]
        buf[8 + tr:, :] = jnp.where(last, 0.0, n_ref[...])
        pre = b_ref[...]
        for k in range(kc):
            pre = pre + w_ref[k:k + 1, :] * buf[pl.ds(8 - (kc - 1) + k, tr + 8), :]
        s = _sigmoid(pre)
        dsilu = s * (1.0 + pre * (1.0 - s))
        dbuf[0:tr, :] = d_ref[...] * dsilu[0:tr, :]
        dbuf[tr:, :] = jnp.where(last, 0.0, dn_ref[...]) * dsilu[tr:, :]
        dpre = dbuf[0:tr, :]
        dx = jnp.zeros((tr, tc), F32)
        first = i == 0
        for k in range(kc):
            dx = dx + w_ref[k:k + 1, :] * dbuf[pl.ds(kc - 1 - k, tr), :]
        dx_ref[...] = dx

        @pl.when(first)
        def _():
            dw_ref[...] = jnp.zeros_like(dw_ref)
        for k in range(kc):
            dw_ref[k:k + 1, :] += jnp.sum(dpre * buf[pl.ds(8 - (kc - 1) + k, tr), :], axis=0, keepdims=True)
        _acc_rows(db_ref, first, jnp.sum(dpre, axis=0, keepdims=True))

    par = pl.BlockSpec((8, tc), lambda j, i: (0, j))
    dx, dw, db = pl.pallas_call(
        body, name=name, grid=(c // tc, nt),
        in_specs=[tile, prev, nxt, par, pl.BlockSpec((1, tc), lambda j, i: (0, j)), tile, nxt],
        out_specs=[tile, par, par],
        out_shape=[jax.ShapeDtypeStruct((t, c), F32), jax.ShapeDtypeStruct((8, c), F32), jax.ShapeDtypeStruct((8, c), F32)],
        scratch_shapes=[pltpu.VMEM((tr + 16, tc), F32), pltpu.VMEM((tr + 8, tc), F32)],
        compiler_params=_params(2))(xbc, xbc, xbc, w, b, dout, dout)
    return dx, dw[:kc], db[0]


def _ssd_consts():
    h, p, ln = SSD_HEADS, SSD_HEAD_DIM, SSD_CHUNK
    w = h * p
    hrow, jcol = _iota((LANES, w), 0), _iota((LANES, w), 1)
    e = ((jcol >= hrow * p) & (jcol < (hrow + 1) * p)).astype(BF16)
    jrow, hcol = _iota((w, LANES), 0), _iota((w, LANES), 1)
    et = ((jrow >= hcol * p) & (jrow < (hcol + 1) * p)).astype(BF16)
    row, col = _iota((ln, ln), 0), _iota((ln, ln), 1)
    return e, et, row, col


def _ssd_common(dtraw_ref, dtb_ref, alog_ref, e):
    ln = SSD_CHUNK
    raw = dtraw_ref[...] + dtb_ref[...]
    dt = _softplus(raw)
    a = -jnp.exp(alog_ref[...])
    adt = dt * a
    row, col = _iota((ln, ln), 0), _iota((ln, ln), 1)
    cs = _xdot_l((col <= row).astype(BF16), adt)
    cl = jnp.sum(adt, axis=0, keepdims=True)
    ex = _xdot(jnp.concatenate([dt, cs, jnp.broadcast_to(cl, (ln, LANES))], axis=0), e)
    return raw, dt, a, cs, ex[:ln], ex[ln:2 * ln], ex[2 * ln:]


def _head_decay(cs, h, causal):
    ln = SSD_CHUNK
    lane = _iota((1, LANES), 1)
    colv = jnp.sum(jnp.where(lane == h, cs, 0.0), axis=1, keepdims=True)
    cb = jnp.broadcast_to(colv, (ln, ln))
    return jnp.exp(jnp.where(causal, cb - cb.T, NEG))


def _ssd_fwd(dtraw, xc, dtb, alog, dskip_x, name):
    t = dtraw.shape[0]
    h, p, g, n, ln = SSD_HEADS, SSD_HEAD_DIM, SSD_GROUPS, SSD_STATE, SSD_CHUNK
    w, gn, gw, hpg = h * p, g * n, (h // g) * p, h // g
    assert n == ln and gw % LANES == 0 and w % gn == 0
    nc = t // ln

    def body(dtraw_ref, x_ref, b_ref, c_ref, dtb_ref, alog_ref, dsk_ref, y_ref, sp_ref, s_ref):
        @pl.when(pl.program_id(0) == 0)
        def _():
            s_ref[...] = jnp.zeros_like(s_ref)
        e, _, row, col = _ssd_consts()
        causal = col <= row
        _, _, _, cs, dt_x, cs_x, cl_x = _ssd_common(dtraw_ref, dtb_ref, alog_ref, e)
        xv = x_ref[...]
        xd = xv * dt_x
        xdb = xd.astype(BF16)
        sv = s_ref[...]
        sp_ref[0] = sv
        el_x = jnp.exp(cs_x)
        zb = (xd * jnp.exp(cl_x - cs_x)).astype(BF16)
        cd_x = jnp.exp(cl_x)
        dsk = dsk_ref[...]
        half = _iota((1, LANES), 1) >= p
        for gi in range(g):
            gs = slice(gi * gw, (gi + 1) * gw)
            bg = b_ref[:, gi * n:(gi + 1) * n].astype(BF16)
            cg = c_ref[:, gi * n:(gi + 1) * n].astype(BF16)
            gm = _dot(cg, bg, "nt")
            sg = sv[:, gs]
            yoff = _dot(cg, sg.astype(BF16)) * el_x[:, gs]
            s_ref[:, gs] = sg * cd_x[:, gs] + _dot(bg, zb[:, gs], "tn")
            for pp in range(gw // LANES):
                ls = slice(gi * gw + pp * LANES, gi * gw + (pp + 1) * LANES)
                xp = xdb[:, ls]
                yp = yoff[:, pp * LANES:(pp + 1) * LANES] + dsk[:, ls] * xv[:, ls]
                for hh in range(LANES // p):
                    hd = gi * hpg + pp * (LANES // p) + hh
                    wm = (gm * _head_decay(cs, hd, causal)).astype(BF16)
                    yp = yp + _dot(wm, jnp.where(half == (hh == 1), xp, jnp.zeros_like(xp)))
                y_ref[:, ls] = yp

    nar = pl.BlockSpec((ln, LANES), lambda c: (c, 0))
    one = pl.BlockSpec((1, LANES), lambda c: (0, 0))
    return pl.pallas_call(
        body, name=name, grid=(nc,),
        in_specs=[nar, pl.BlockSpec((ln, w), lambda c: (c, 0)), pl.BlockSpec((ln, gn), lambda c: (c, w // gn)),
                  pl.BlockSpec((ln, gn), lambda c: (c, w // gn + 1)), one, one, pl.BlockSpec((1, w), lambda c: (0, 0))],
        out_specs=[pl.BlockSpec((ln, w), lambda c: (c, 0)), pl.BlockSpec((1, n, w), lambda c: (c, 0, 0))],
        out_shape=[jax.ShapeDtypeStruct((t, w), F32), jax.ShapeDtypeStruct((nc, n, w), F32)],
        scratch_shapes=[pltpu.VMEM((n, w), F32)], compiler_params=_params(1))(dtraw, xc, xc, xc, dtb, alog, dskip_x)


def _ssd_bwd(dtraw, xc, dtb, alog, dskip_x, sprev, dy, name):
    t = dtraw.shape[0]
    h, p, g, n, ln = SSD_HEADS, SSD_HEAD_DIM, SSD_GROUPS, SSD_STATE, SSD_CHUNK
    w, gn, gw, hpg = h * p, g * n, (h // g) * p, h // g
    nc = t // ln

    def body(dtraw_ref, x_ref, b_ref, c_ref, dtb_ref, alog_ref, dsk_ref, sp_ref, dy_ref,
             dx_ref, db_ref, dc_ref, ddt_ref, dbias_ref, dalog_ref, ddsk_ref, ds_ref, dxd_ref, qcs_ref):
        first = pl.program_id(0) == 0

        @pl.when(first)
        def _():
            ds_ref[...] = jnp.zeros_like(ds_ref)
        e, et, row, col = _ssd_consts()
        causal = col <= row
        raw, dt, a, cs, dt_x, cs_x, cl_x = _ssd_common(dtraw_ref, dtb_ref, alog_ref, e)
        xv = x_ref[...]
        xd = xv * dt_x
        xdb = xd.astype(BF16)
        sv = sp_ref[0]
        dyv = dy_ref[...]
        dyb = dyv.astype(BF16)
        dsn = ds_ref[...]
        el_x = jnp.exp(cs_x)
        dte_x = jnp.exp(cl_x - cs_x)
        cd_x = jnp.exp(cl_x)
        zf = xd * dte_x
        lane = _iota((1, LANES), 1)
        half = lane >= p
        lastrow = _iota((ln, 1), 0) == ln - 1
        dcs = jnp.zeros((ln, LANES), F32)
        for gi in range(g):
            gs = slice(gi * gw, (gi + 1) * gw)
            ns = slice(gi * n, (gi + 1) * n)
            bg = b_ref[:, ns].astype(BF16)
            cg = c_ref[:, ns].astype(BF16)
            gm = _dot(cg, bg, "nt")
            sgb = sv[:, gs].astype(BF16)
            dsg = dsn[:, gs]
            dsgb = dsg.astype(BF16)
            yoff = _dot(cg, sgb) * el_x[:, gs]
            drb = (el_x[:, gs] * dyv[:, gs]).astype(BF16)
            dcg = _dot(drb, sgb, "nt")
            ds_ref[:, gs] = cd_x[:, gs] * dsg + _dot(cg, drb, "tn")
            dz = _dot(bg, dsgb)
            zg = zf[:, gs]
            dbg = _dot(zg.astype(BF16), dsgb, "nt")
            dzz = dz * zg
            qcl = jnp.sum(dzz + cd_x[:, gs] * dsg * sv[:, gs], axis=0, keepdims=True)
            qcs_ref[:, gs] = dyv[:, gs] * yoff - dzz + jnp.where(lastrow, jnp.broadcast_to(qcl, (ln, gw)), 0.0)
            dgm = jnp.zeros((ln, ln), F32)
            for pp in range(gw // LANES):
                ls = slice(gi * gw + pp * LANES, gi * gw + (pp + 1) * LANES)
                xp = xdb[:, ls]
                dxp = dz[:, pp * LANES:(pp + 1) * LANES] * dte_x[:, ls]
                for hh in range(LANES // p):
                    hd = gi * hpg + pp * (LANES // p) + hh
                    dm = _head_decay(cs, hd, causal)
                    wf = gm * dm
                    dym = jnp.where(half == (hh == 1), dyb[:, ls], jnp.zeros_like(xp))
                    dw = _dot(dym, xp, "nt")
                    dxp = dxp + _dot(wf.astype(BF16), dym, "tn")
                    dgm = dgm + dw * dm
                    mm = dw * wf
                    rc = jnp.sum(mm, axis=1, keepdims=True) - jnp.sum(mm.T, axis=1, keepdims=True)
                    dcs = dcs + rc * (lane == hd).astype(F32)
                dxd_ref[:, ls] = dxp
            dgb = dgm.astype(BF16)
            dc_ref[:, ns] = dcg + _dot(dgb, bg)
            db_ref[:, ns] = dbg + _dot(dgb, cg, "tn")
        dxd = dxd_ref[...]
        dx_ref[...] = dxd * dt_x + dsk_ref[...] * dyv
        red = _xdot(jnp.concatenate([qcs_ref[...], dxd * xv, dyv * xv], axis=0), et)
        dcs = dcs + red[:ln]
        dadt = _xdot_l((row <= col).astype(BF16), dcs)
        ddt = red[ln:2 * ln] + dadt * a
        draw = ddt * _sigmoid(raw)
        ddt_ref[...] = draw
        _acc_rows(dbias_ref, first, jnp.sum(draw, axis=0, keepdims=True))
        _acc_rows(dalog_ref, first, jnp.sum(dadt * dt, axis=0, keepdims=True) * a)
        _acc_rows(ddsk_ref, first, jnp.sum(red[2 * ln:], axis=0, keepdims=True))

    rev = lambda c: nc - 1 - c
    nar = pl.BlockSpec((ln, LANES), lambda c: (rev(c), 0))
    one = pl.BlockSpec((1, LANES), lambda c: (0, 0))
    wide = pl.BlockSpec((ln, w), lambda c: (rev(c), 0))
    bcs = pl.BlockSpec((ln, gn), lambda c: (rev(c), 0))
    acc = pl.BlockSpec((8, LANES), lambda c: (0, 0))
    outs = pl.pallas_call(
        body, name=name, grid=(nc,),
        in_specs=[nar, wide, pl.BlockSpec((ln, gn), lambda c: (rev(c), w // gn)), pl.BlockSpec((ln, gn), lambda c: (rev(c), w // gn + 1)),
                  one, one, pl.BlockSpec((1, w), lambda c: (0, 0)), pl.BlockSpec((1, n, w), lambda c: (rev(c), 0, 0)), wide],
        out_specs=[wide, bcs, bcs, nar, acc, acc, acc],
        out_shape=[jax.ShapeDtypeStruct((t, w), F32), jax.ShapeDtypeStruct((t, gn), F32), jax.ShapeDtypeStruct((t, gn), F32),
                   jax.ShapeDtypeStruct((t, LANES), F32)] + [jax.ShapeDtypeStruct((8, LANES), F32)] * 3,
        scratch_shapes=[pltpu.VMEM((n, w), F32), pltpu.VMEM((ln, w), F32), pltpu.VMEM((ln, w), F32)],
        compiler_params=_params(1))(dtraw, xc, xc, xc, dtb, alog, dskip_x, sprev, dy)
    dx, db, dc, ddt, dbias, dalog, ddsk = outs
    return dx, db, dc, ddt, dbias[0], dalog[0], ddsk[0]


def _rope_tables(t):
    half = MLA_ROPE // 2
    inv_freq = ROPE_THETA ** (-jnp.arange(half, dtype=F32) / half)
    ang = jnp.arange(t, dtype=F32)[:, None] * inv_freq[None, :]
    cos, sin = jnp.cos(ang), jnp.sin(ang)
    pad = LANES - MLA_ROPE
    cos_r = jnp.concatenate([cos, cos, jnp.ones((t, pad), F32)], axis=1)
    sin_r = jnp.concatenate([sin, sin, jnp.zeros((t, pad), F32)], axis=1)
    return cos_r, sin_r


def _rot_matrix():
    half = MLA_ROPE // 2
    i, j = _iota((LANES, LANES), 0), _iota((LANES, LANES), 1)
    neg = (j < half) & (i == j + half)
    pos = (j >= half) & (j < 2 * half) & (i == j - half)
    return (pos.astype(F32) - neg.astype(F32)).astype(BF16)


def _rope(x, cos, sin, every, transpose, out_dtype, name):
    t, w = x.shape
    tr = _rows(t)

    def body(x_ref, c_ref, s_ref, o_ref):
        rot = _rot_matrix()
        cv, sv = c_ref[...], s_ref[...]
        for j in range(w // LANES):
            ls = slice(j * LANES, (j + 1) * LANES)
            xv = x_ref[:, ls]
            if j % every != every - 1:
                o_ref[:, ls] = xv.astype(o_ref.dtype)
            elif transpose:
                o_ref[:, ls] = (xv * cv - _xdot(xv * sv, rot, 2)).astype(o_ref.dtype)
            else:
                o_ref[:, ls] = (xv * cv + _xdot(xv, rot, 2) * sv).astype(o_ref.dtype)

    wide = pl.BlockSpec((tr, w), lambda i: (i, 0))
    tab = pl.BlockSpec((tr, LANES), lambda i: (i, 0))
    return pl.pallas_call(
        body, name=name, grid=(t // tr,), in_specs=[wide, tab, tab], out_specs=wide,
        out_shape=jax.ShapeDtypeStruct((t, w), out_dtype), compiler_params=_params(1))(x, cos, sin)


def _att_masks(blk):
    return _iota((blk, blk), 0), _iota((blk, blk), 1)


def _lanes(j):
    return slice(j * LANES, (j + 1) * LANES)


def _mla_fwd(q, kr, kv, name):
    t = q.shape[0]
    nh, blk = MLA_HEADS, min(ATT_BLK, t)
    scale = (MLA_NOPE + MLA_ROPE) ** -0.5

    def body(q_ref, kv_ref, kr_ref, o_ref, lse_ref):
        i = pl.program_id(1)
        row, col = _att_masks(blk)
        qs = [(q_ref[:, _lanes(2 * hh)], q_ref[:, _lanes(2 * hh + 1)]) for hh in range(2)]

        def scores(kb, hh):
            ks = pl.ds(pl.multiple_of(kb * blk, blk), blk)
            s = (_dot(qs[hh][0], kv_ref[ks, _lanes(2 * hh)], "nt") + _dot(qs[hh][1], kr_ref[ks, :], "nt")) * scale
            return s, kv_ref[ks, _lanes(2 * hh + 1)]
        init = []
        for hh in range(2):
            s, v = scores(i, hh)
            s = jnp.where(col <= row, s, NEG)
            m = jnp.max(s, axis=1, keepdims=True)
            pr = jnp.exp(s - m)
            init += [m, jnp.sum(pr, axis=1, keepdims=True), _dot(pr.astype(BF16), v)]

        def step(kb, carry):
            out = []
            for hh in range(2):
                m, l, acc = carry[3 * hh:3 * hh + 3]
                s, v = scores(kb, hh)
                m2 = jnp.maximum(m, jnp.max(s, axis=1, keepdims=True))
                al = jnp.exp(m - m2)
                pr = jnp.exp(s - m2)
                out += [m2, al * l + jnp.sum(pr, axis=1, keepdims=True), al * acc + _dot(pr.astype(BF16), v)]
            return tuple(out)
        res = lax.fori_loop(0, i, step, tuple(init))
        for hh in range(2):
            m, l, acc = res[3 * hh:3 * hh + 3]
            o_ref[:, _lanes(hh)] = acc / l
            lse_ref[hh] = m + jnp.log(l)

    return pl.pallas_call(
        body, name=name, grid=(nh // 2, t // blk),
        in_specs=[pl.BlockSpec((blk, 4 * LANES), lambda h, i: (i, h)), pl.BlockSpec((t, 4 * LANES), lambda h, i: (0, h)),
                  pl.BlockSpec((t, LANES), lambda h, i: (0, 0))],
        out_specs=[pl.BlockSpec((blk, 2 * LANES), lambda h, i: (i, h)), pl.BlockSpec((2, blk, 1), lambda h, i: (h, i, 0))],
        out_shape=[jax.ShapeDtypeStruct((t, nh * LANES), F32), jax.ShapeDtypeStruct((nh, t, 1), F32)],
        compiler_params=_params(2))(q, kv, kr)


def _mla_bwd(q, kr, kv, o, do, lse, name):
    t = q.shape[0]
    nh, blk = MLA_HEADS, min(ATT_BLK, t)
    scale = (MLA_NOPE + MLA_ROPE) ** -0.5

    def body(q_ref, kv_ref, kr_ref, o_ref, do_ref, lse_ref, dq_ref, dkv_ref, dkr_ref):
        hp, i = pl.program_id(0), pl.program_id(1)

        @pl.when(i == 0)
        def _():
            dkv_ref[...] = jnp.zeros_like(dkv_ref)

        @pl.when((i == 0) & (hp == 0))
        def _():
            dkr_ref[...] = jnp.zeros_like(dkr_ref)
        row, col = _att_masks(blk)
        qs = [(q_ref[:, _lanes(2 * hh)], q_ref[:, _lanes(2 * hh + 1)]) for hh in range(2)]
        dobs = [do_ref[:, _lanes(hh)].astype(BF16) for hh in range(2)]
        deltas = [jnp.sum(do_ref[:, _lanes(hh)] * o_ref[:, _lanes(hh)], axis=1, keepdims=True) for hh in range(2)]
        lses = [lse_ref[hh] for hh in range(2)]

        def tile(kb, carry, masked):
            ks = pl.ds(pl.multiple_of(kb * blk, blk), blk)
            krv = kr_ref[ks, :]
            out, dkr = [], None
            for hh in range(2):
                qn, qp = qs[hh]
                kn, v = kv_ref[ks, _lanes(2 * hh)], kv_ref[ks, _lanes(2 * hh + 1)]
                s = (_dot(qn, kn, "nt") + _dot(qp, krv, "nt")) * scale
                pr = jnp.exp(s - lses[hh])
                if masked:
                    pr = jnp.where(col <= row, pr, 0.0)
                dkv_ref[ks, _lanes(2 * hh + 1)] += _dot(pr.astype(BF16), dobs[hh], "tn")
                dp = _dot(dobs[hh], v, "nt")
                dsb = (pr * (dp - deltas[hh]) * scale).astype(BF16)
                dkv_ref[ks, _lanes(2 * hh)] += _dot(dsb, qn, "tn")
                part = _dot(dsb, qp, "tn")
                dkr = part if dkr is None else dkr + part
                out += [carry[2 * hh] + _dot(dsb, kn), carry[2 * hh + 1] + _dot(dsb, krv)]
            dkr_ref[ks, :] += dkr
            return tuple(out)
        zero = jnp.zeros((blk, LANES), F32)
        carry = lax.fori_loop(0, i, lambda kb, c: tile(kb, c, False), (zero,) * 4)
        res = tile(i, carry, True)
        for j in range(4):
            dq_ref[:, _lanes(j)] = res[j]

    qb = lambda w: pl.BlockSpec((blk, w * LANES), lambda h, i: (i, h))
    wide = jax.ShapeDtypeStruct((t, nh * 2 * LANES), F32)
    return pl.pallas_call(
        body, name=name, grid=(nh // 2, t // blk),
        in_specs=[qb(4), pl.BlockSpec((t, 4 * LANES), lambda h, i: (0, h)), pl.BlockSpec((t, LANES), lambda h, i: (0, 0)),
                  qb(2), qb(2), pl.BlockSpec((2, blk, 1), lambda h, i: (h, i, 0))],
        out_specs=[qb(4), pl.BlockSpec((t, 4 * LANES), lambda h, i: (0, h)), pl.BlockSpec((t, LANES), lambda h, i: (0, 0))],
        out_shape=[wide, wide, jax.ShapeDtypeStruct((t, LANES), F32)],
        compiler_params=_params(2))(q, kv, kr, o, do, lse)


def _sb_fwd(qkv, name):
    t = qkv.shape[0]
    nh, blk = SB_HEADS, min(ATT_BLK, t)
    nq = t // blk
    scale = SB_HEAD_DIM ** -0.5

    def body(q_ref, k_ref, v_ref, o_ref, lt_ref):
        i = pl.program_id(1)
        row, col = _att_masks(blk)
        usuf = (row > col).astype(BF16)
        qs = [q_ref[:, _lanes(hh)] for hh in range(2)]

        def tile(kb, carry, masked):
            ks = pl.ds(pl.multiple_of(kb * blk, blk), blk)
            out = []
            for hh in range(2):
                acc, run = carry[2 * hh], carry[2 * hh + 1]
                z = _dot(qs[hh], k_ref[ks, _lanes(hh)], "nt") * scale
                lk = -_softplus(z)
                ls = lk + z
                if masked:
                    lk = jnp.where(col < row, lk, 0.0)
                wt = jnp.exp(ls + _xdot(lk, usuf, 2) + run)
                if masked:
                    wt = jnp.where(col < row, wt, 0.0)
                out += [acc + _dot(wt.astype(BF16), v_ref[ks, _lanes(hh)]), run + jnp.sum(lk, axis=1, keepdims=True)]
            return tuple(out)
        za, zr = jnp.zeros((blk, LANES), F32), jnp.zeros((blk, 1), F32)
        carry = tile(i, (za, zr, za, zr), True)

        carry = lax.fori_loop(0, i, lambda j, c: tile(i - 1 - j, c, False), carry)
        for hh in range(2):
            o_ref[:, _lanes(hh)] = carry[2 * hh]
            lt_ref[hh] = carry[2 * hh + 1]

    full = lambda f: pl.BlockSpec((t, 2 * LANES), f)
    return pl.pallas_call(
        body, name=name, grid=(nh // 2, nq),
        in_specs=[pl.BlockSpec((blk, 2 * LANES), lambda h, i: (i, h)), full(lambda h, i: (0, nh // 2 + h)), full(lambda h, i: (0, nh + h))],
        out_specs=[pl.BlockSpec((blk, 2 * LANES), lambda h, i: (i, h)), pl.BlockSpec((2, blk, 1), lambda h, i: (h, i, 0))],
        out_shape=[jax.ShapeDtypeStruct((t, nh * LANES), F32), jax.ShapeDtypeStruct((nh, t, 1), F32)],
        compiler_params=_params(2))(qkv, qkv, qkv)


def _sb_bwd(qkv, do, ltot, name):
    t = qkv.shape[0]
    nh, blk = SB_HEADS, min(ATT_BLK, t)
    nq = t // blk
    scale = SB_HEAD_DIM ** -0.5

    def body(q_ref, k_ref, v_ref, do_ref, lt_ref, dq_ref, dk_ref, dv_ref):
        i = pl.program_id(1)

        @pl.when(i == 0)
        def _():
            dk_ref[...] = jnp.zeros_like(dk_ref)
            dv_ref[...] = jnp.zeros_like(dv_ref)
        row, col = _att_masks(blk)
        uinc = (row <= col).astype(BF16)
        uexc = (row < col).astype(BF16)
        qs = [q_ref[:, _lanes(hh)] for hh in range(2)]
        dobs = [do_ref[:, _lanes(hh)].astype(BF16) for hh in range(2)]
        lts = [lt_ref[hh] for hh in range(2)]

        def tile(kb, carry, masked):
            ks = pl.ds(pl.multiple_of(kb * blk, blk), blk)
            out = []
            for hh in range(2):
                dq, pre, ec = carry[3 * hh:3 * hh + 3]
                kv_, vv = k_ref[ks, _lanes(hh)], v_ref[ks, _lanes(hh)]
                z = _dot(qs[hh], kv_, "nt") * scale
                lk = -_softplus(z)
                ls = lk + z
                sig = jnp.exp(ls)
                if masked:
                    lk = jnp.where(col < row, lk, 0.0)
                later = lts[hh] - (pre + _xdot(lk, uinc, 2))
                wt = jnp.exp(ls + later)
                if masked:
                    wt = jnp.where(col < row, wt, 0.0)
                dv_ref[ks, _lanes(hh)] += _dot(wt.astype(BF16), dobs[hh], "tn")
                ev = _dot(dobs[hh], vv, "nt") * wt
                ecs = _xdot(ev, uexc, 2) + ec
                dz = ev * (1.0 - sig) - ecs * sig
                if masked:
                    dz = jnp.where(col < row, dz, 0.0)
                dzb = (dz * scale).astype(BF16)
                dk_ref[ks, _lanes(hh)] += _dot(dzb, qs[hh], "tn")
                out += [dq + _dot(dzb, kv_), pre + jnp.sum(lk, axis=1, keepdims=True), ec + jnp.sum(ev, axis=1, keepdims=True)]
            return tuple(out)
        za, z1 = jnp.zeros((blk, LANES), F32), jnp.zeros((blk, 1), F32)
        carry = lax.fori_loop(0, i, lambda kb, c: tile(kb, c, False), (za, z1, z1, za, z1, z1))
        res = tile(i, carry, True)
        for hh in range(2):
            dq_ref[:, _lanes(hh)] = res[3 * hh]

    full = lambda f: pl.BlockSpec((t, 2 * LANES), f)
    qb = pl.BlockSpec((blk, 2 * LANES), lambda h, i: (i, h))
    wide = jax.ShapeDtypeStruct((t, nh * LANES), F32)
    return pl.pallas_call(
        body, name=name, grid=(nh // 2, nq),
        in_specs=[qb, full(lambda h, i: (0, nh // 2 + h)), full(lambda h, i: (0, nh + h)), qb,
                  pl.BlockSpec((2, blk, 1), lambda h, i: (h, i, 0))],
        out_specs=[qb, full(lambda h, i: (0, h)), full(lambda h, i: (0, h))],
        out_shape=[wide, wide, wide], compiler_params=_params(2))(qkv, qkv, qkv, do, ltot)


def _as2d(a):
    return a.reshape((-1, a.shape[-1]))


def _sum4(parts, name):
    _, r, c = parts.shape
    tr = _tile(r, (256, 128, 64, 32, 16, 8))

    def body(p_ref, o_ref):
        acc = p_ref[0].astype(F32)
        for j in range(1, 4):
            acc = acc + p_ref[j].astype(F32)
        o_ref[...] = acc

    return pl.pallas_call(body, name=name, grid=(r // tr,), in_specs=[pl.BlockSpec((4, tr, c), lambda i: (0, i, 0))],
                          out_specs=pl.BlockSpec((tr, c), lambda i: (i, 0)), out_shape=jax.ShapeDtypeStruct((r, c), F32),
                          compiler_params=_params(1))(parts)


def _adamw(w, ga, gb, m, v, name):
    r, c = w.shape
    tr = _tile(r, (128, 64, 32, 16, 8))
    c1, c2 = 1.0 - ADAM_B1 ** ADAM_STEP, 1.0 - ADAM_B2 ** ADAM_STEP
    two = gb is not None

    def body(*refs):
        w_ref, ga_ref = refs[0], refs[1]
        gb_ref = refs[2] if two else None
        m_ref, v_ref, g_out, d_out, m_out, v_out = refs[2 + two:]
        gv = ga_ref[...]
        if two:
            gv = gv + gb_ref[...]
        mn = ADAM_B1 * m_ref[...] + (1.0 - ADAM_B1) * gv
        vn = ADAM_B2 * v_ref[...] + (1.0 - ADAM_B2) * (gv * gv)
        g_out[...] = gv
        m_out[...] = mn
        v_out[...] = vn
        d_out[...] = -ADAM_LR * ((mn / c1) / (jnp.sqrt(vn / c2) + ADAM_EPS) + ADAM_WD * w_ref[...])

    blk = pl.BlockSpec((tr, c), lambda i: (i, 0))
    ins = [w, ga] + ([gb] if two else []) + [m, v]
    return pl.pallas_call(body, name=name, grid=(r // tr,), in_specs=[blk] * len(ins), out_specs=[blk] * 4,
                          out_shape=[jax.ShapeDtypeStruct((r, c), F32)] * 4, compiler_params=_params(1))(*ins)


HBM_SPEC = pl.BlockSpec(memory_space=pltpu.HBM)
SEM_SPEC = pl.BlockSpec(memory_space=pltpu.SEMAPHORE)
EFFECT = pltpu.SideEffectType.DATAFLOW_SIDE_EFFECTING


def _chip_copies(ins, lands, send_sems, recv_sems, whole):
    x, y, c = lax.axis_index("x"), lax.axis_index("y"), lax.axis_index("c")
    me = 2 * x + y
    out = []
    for wi in range(len(ins)):
        for k, (px, py) in enumerate([(1 - x, y), (x, 1 - y), (1 - x, 1 - y)]):
            sems = dict(send_sem=send_sems[wi * 3 + k], recv_sem=recv_sems[wi * 3 + k], device_id=(px, py, c), device_id_type=MESH_T)
            peer = 2 * px + py
            sent = pltpu.make_async_remote_copy(src_ref=ins[wi] if whole else ins[wi].at[peer], dst_ref=lands[wi].at[me], **sems)
            got = pltpu.make_async_remote_copy(src_ref=ins[wi] if whole else ins[wi].at[me], dst_ref=lands[wi].at[peer], **sems)
            out.append((sent, got))
    return out


def _xstart(arrs, whole, after, name):
    n, na = len(arrs), len(after)
    me = 2 * lax.axis_index("x") + lax.axis_index("y")
    lands = []
    for a in arrs:
        own = a[None] if whole else lax.dynamic_slice_in_dim(a, me, 1, axis=0)
        empty = lax.empty(((4,) + a.shape) if whole else a.shape, a.dtype)
        lands.append(lax.dynamic_update_slice_in_dim(empty, own, me, axis=0))

    def body(*refs):
        ins, lands_in = refs[:n], refs[n:2 * n]
        outs = refs[2 * n + na:]
        for sent, _ in _chip_copies(ins, lands_in, outs[:3 * n], outs[3 * n:6 * n], whole):
            sent.start()
        outs[8 * n][...] = jnp.zeros((8, LANES), F32)

    hbm = lambda a: pltpu.HBM(a.shape, a.dtype)
    res = pl.pallas_call(
        body, name=name,
        out_shape=[pltpu.SemaphoreType.DMA(())] * (6 * n) + [hbm(a) for a in arrs] + [hbm(a) for a in lands]
        + [jax.ShapeDtypeStruct((8, LANES), F32)],
        in_specs=[HBM_SPEC] * (2 * n) + [pl.BlockSpec(memory_space=pl.ANY)] * na,
        out_specs=[SEM_SPEC] * (6 * n) + [HBM_SPEC] * (2 * n) + [pl.BlockSpec(memory_space=pltpu.VMEM)],
        input_output_aliases={i: 6 * n + i for i in range(2 * n)},
        compiler_params=pltpu.CompilerParams(has_side_effects=EFFECT),
    )(*[pltpu.with_memory_space_constraint(a, pltpu.HBM) for a in list(arrs) + lands], *after)
    return res


def _xwait(handle, whole, after, name):
    n = (len(handle) - 1) // 8
    sems, thru = handle[:6 * n], handle[6 * n:8 * n]

    def body(*refs):
        ins, lands_in = refs[:n], refs[n:2 * n]
        for sent, got in _chip_copies(ins, lands_in, refs[2 * n:5 * n], refs[5 * n:8 * n], whole):
            sent.wait_send()
            got.wait_recv()

    hbm = lambda a: pltpu.HBM(a.shape, a.dtype)
    res = pl.pallas_call(
        body, name=name, out_shape=[hbm(a) for a in thru],
        in_specs=[HBM_SPEC] * (2 * n) + [SEM_SPEC] * (6 * n) + [pl.BlockSpec(memory_space=pl.ANY)],
        out_specs=[HBM_SPEC] * (2 * n), input_output_aliases={i: i for i in range(2 * n)},
        compiler_params=pltpu.CompilerParams(has_side_effects=EFFECT),
    )(*thru, *sems, after)
    return res[n:]


def _sibling_exchange(arrs, name):
    n = len(arrs)

    def body(*refs):
        ins, outs = refs[:n], refs[n:2 * n]
        send_sems, recv_sems = refs[2 * n:]
        sib = (lax.axis_index("x"), lax.axis_index("y"), 1 - lax.axis_index("c"))
        cps = [pltpu.make_async_remote_copy(src_ref=ins[wi], dst_ref=outs[wi], send_sem=send_sems.at[wi], recv_sem=recv_sems.at[wi],
                                            device_id=sib, device_id_type=MESH_T) for wi in range(n)]
        for cp in cps:
            cp.start()
        for cp in cps:
            cp.wait_recv()
        for cp in cps:
            cp.wait_send()

    anyspec = pl.BlockSpec(memory_space=pl.ANY)
    return pl.pallas_call(
        body, name=name, in_specs=[anyspec] * n, out_specs=[anyspec] * n,
        out_shape=[jax.ShapeDtypeStruct(a.shape, a.dtype) for a in arrs],
        scratch_shapes=[pltpu.SemaphoreType.DMA((n,)), pltpu.SemaphoreType.DMA((n,))],
        compiler_params=pltpu.CompilerParams(has_side_effects=True))(*arrs)


def _allreduce_small(packed, name):
    r = packed.shape[0]

    def body(in_ref, out_ref, land, send_sems, recv_sems):
        x, y, c = lax.axis_index("x"), lax.axis_index("y"), lax.axis_index("c")
        me = 4 * x + 2 * y + c
        land[me] = in_ref[...]
        rel = [(dx, dy, dc) for dx in (0, 1) for dy in (0, 1) for dc in (0, 1)][1:]
        peers = [((1 - x) if dx else x, (1 - y) if dy else y, (1 - c) if dc else c) for dx, dy, dc in rel]
        sends = []
        for k, peer in enumerate(peers):
            cp = pltpu.make_async_remote_copy(src_ref=in_ref, dst_ref=land.at[me], send_sem=send_sems.at[k], recv_sem=recv_sems.at[k],
                                              device_id=peer, device_id_type=MESH_T)
            cp.start()
            sends.append(cp)
        for k, (px, py, pc) in enumerate(peers):
            pltpu.make_async_remote_copy(src_ref=in_ref, dst_ref=land.at[4 * px + 2 * py + pc], send_sem=send_sems.at[k],
                                         recv_sem=recv_sems.at[k], device_id=(px, py, pc), device_id_type=MESH_T).wait_recv()
        for cp in sends:
            cp.wait_send()
        acc = land[0]
        for j in range(1, 8):
            acc = acc + land[j]
        out_ref[...] = acc

    vm = pl.BlockSpec(memory_space=pltpu.VMEM)
    return pl.pallas_call(
        body, name=name, in_specs=[vm], out_specs=vm, out_shape=jax.ShapeDtypeStruct((r, LANES), F32),
        scratch_shapes=[pltpu.VMEM((8, r, LANES), F32), pltpu.SemaphoreType.DMA((7,)), pltpu.SemaphoreType.DMA((7,))],
        compiler_params=pltpu.CompilerParams(has_side_effects=True))(packed)


def _in_splits():
    w = SSD_HEADS * SSD_HEAD_DIM
    cc = w + 2 * SSD_GROUPS * SSD_STATE
    return [w, cc, SSD_HEADS, MLA_Q_RANK, MLA_KV_RANK, MLA_ROPE]


def _padc(a, n):
    return jnp.pad(a, ((0, 0), (0, n - a.shape[1])))


def _win_pad(wm):
    offs = np.cumsum(_in_splits())[:-1]
    z, xbc, dt, cq, ckv, kr = jnp.split(wm, offs, axis=1)
    return jnp.concatenate([z, xbc, cq, ckv, _padc(kr, LANES), _padc(dt, LANES)], axis=1)


def _win_unpad(g):
    w, cc, nh, qr, kvr, rp = _in_splits()
    offs = np.cumsum([w, cc, qr, kvr, LANES])
    z, xbc, cq, ckv, kr, dt = jnp.split(g, offs, axis=1)
    return jnp.concatenate([z, xbc, dt[:, :nh], cq, ckv, kr[:, :rp]], axis=1)


def _wuq_pad(wm):
    r = wm.shape[0]
    w3 = wm.reshape(r, MLA_HEADS, MLA_NOPE + MLA_ROPE)
    return jnp.pad(w3, ((0, 0), (0, 0), (0, 2 * LANES - MLA_NOPE - MLA_ROPE))).reshape(r, MLA_HEADS * 2 * LANES)


def _wuq_unpad(g):
    r = g.shape[0]
    return g.reshape(r, MLA_HEADS, 2 * LANES)[:, :, :MLA_NOPE + MLA_ROPE].reshape(r, MLA_HEADS * (MLA_NOPE + MLA_ROPE))


def _full_from_gather(name, g):
    col = name in COL_SHARDED
    layers = g.shape[1]
    return [jnp.concatenate([g[j, l] for j in range(4)], axis=1 if col else 0) for l in range(layers)]


def _shards_from_full(name, mats):
    col = name in COL_SHARDED
    per = []
    for mt in mats:
        r, c = mt.shape
        per.append(mt.reshape(r, 4, c // 4).transpose(1, 0, 2) if col else mt.reshape(4, r // 4, c))
    return jnp.stack(per, axis=1).astype(BF16)


def _pack_small(vals):
    flat = jnp.concatenate([vals[n].reshape(-1).astype(F32) for n in SMALL])
    rows = -(-flat.shape[0] // (8 * LANES)) * 8
    return jnp.pad(flat, (0, rows * LANES - flat.shape[0])).reshape(rows, LANES)


def _unpack_small(packed, like):
    flat, out, off = packed.reshape(-1), {}, 0
    for n in SMALL:
        sz = like[n].size
        out[n] = flat[off:off + sz].reshape(like[n].shape)
        off += sz
    return out


def _ffn_fwd(x, norm_g, wg, wu, wd, tag):
    h, r = _rms_fwd(x, norm_g, f"ffn_norm_{tag}")
    gate = _mm(h, wg, "nn", f"ffn_gate_{tag}")
    up = _mm(h, wu, "nn", f"ffn_up_{tag}")
    act = _swiglu_fwd(gate, up, f"swiglu_{tag}")
    out = _mm(act, wd, "nn", f"ffn_down_{tag}", add=x)
    return out, (x, r, h, gate, up, act)


def _ffn_bwd(dout, doutb, norm_g, wg, wu, wd, saved, tag):
    x, r, h, gate, up, act = saved
    dact = _mm(doutb, wd.T, "nn", f"ffn_dact_{tag}")
    dgate, dup = _swiglu_bwd(gate, up, dact, f"swiglu_bwd_{tag}")
    ht = h.T
    d_wd = _mm(act.T, doutb, "nn", f"ffn_dwd_{tag}", out_dtype=BF16)
    d_wg = _mm(ht, dgate, "nn", f"ffn_dwg_{tag}", out_dtype=BF16)
    d_wu = _mm(ht, dup, "nn", f"ffn_dwu_{tag}", out_dtype=BF16)
    dh = _mm(dgate, wg.T, "nn", f"ffn_dh1_{tag}")
    dh = _mm(dup, wu.T, "nn", f"ffn_dh2_{tag}", add=dh)
    dx, dxb, dnorm = _rms_bwd(x, norm_g, r, dh, f"ffn_norm_bwd_{tag}", dres=dout)
    return dx, dxb, dnorm, d_wg, d_wu, d_wd


def kernel(x, mix_norm, ffn_norm, w_in, conv_w, conv_b, dt_bias, a_log, d_skip, ssd_norm, q_norm, kv_norm, w_uq, w_ukv, w_out_even, w_qkv, w_out_odd, w_gate, w_up, w_down, final_norm, loss_target, m_mix_norm, m_ffn_norm, m_w_in, m_conv_w, m_conv_b, m_dt_bias, m_a_log, m_d_skip, m_ssd_norm, m_q_norm, m_kv_norm, m_w_uq, m_w_ukv, m_w_out_even, m_w_qkv, m_w_out_odd, m_w_gate, m_w_up, m_w_down, m_final_norm, v_mix_norm, v_ffn_norm, v_w_in, v_conv_w, v_conv_b, v_dt_bias, v_a_log, v_d_skip, v_ssd_norm, v_q_norm, v_kv_norm, v_w_uq, v_w_ukv, v_w_out_even, v_w_qkv, v_w_out_odd, v_w_gate, v_w_up, v_w_down, v_final_norm):
    given = dict(locals())
    wts = {n: given[n] for n in WEIGHTS}
    x0 = x[0]
    tgt = loss_target[0]
    t, d = x0.shape
    hw = SSD_HEADS * SSD_HEAD_DIM
    gn = SSD_GROUPS * SSD_STATE
    cc = hw + 2 * gn
    qr, kvr = MLA_Q_RANK, MLA_KV_RANK

    def shard(n, layer=None):
        a = wts[n] if layer is None else wts[n][layer:layer + 1]
        return a if n == 'conv_w' else a.astype(BF16)
    g0_names = ['w_in', 'conv_w', 'w_uq', 'w_ukv']
    g1_names = [('w_out_even', None), ('w_gate', 0), ('w_up', 0), ('w_down', 0)]
    g2_names = [('w_qkv', None), ('w_out_odd', None), ('w_gate', 1), ('w_up', 1), ('w_down', 1)]
    hg0 = _xstart([shard(n) for n in g0_names], True, [], "gather0_start")
    hg1 = _xstart([shard(n, l) for n, l in g1_names], True, [hg0[-1]], "gather1_start")
    hg2 = _xstart([shard(n, l) for n, l in g2_names], True, [hg1[-1]], "gather2_start")
    full = lambda n, g: _full_from_gather(n, g)[0]
    g0 = dict(zip(g0_names, _xwait(hg0, True, hg2[-1], "gather0_wait")))
    win = _win_pad(full('w_in', g0['w_in']))
    wuq = _wuq_pad(full('w_uq', g0['w_uq']))
    wukv = full('w_ukv', g0['w_ukv'])
    cw = _padc(full('conv_w', g0['conv_w']).T, 8).T
    o_cq, o_ckv, o_kr, o_dt = hw + cc, hw + cc + qr, hw + cc + qr + kvr, hw + cc + qr + kvr + LANES

    row = lambda v: v.reshape(1, -1)
    narrow = lambda v: _padc(v.reshape(1, -1), LANES)
    dtb, alog = narrow(dt_bias[0]), narrow(a_log[0])
    dsk_x = jnp.repeat(d_skip[0], SSD_HEAD_DIM).reshape(1, hw)
    cos, sin = _rope_tables(t)

    h0, r0 = _rms_fwd(x0, row(mix_norm[0]), "mix_norm_0")
    u = _mm(h0, win, "nn", "in_proj")
    z, xbc, c_q, c_kv = u[:, :hw], u[:, hw:hw + cc], u[:, o_cq:o_ckv], u[:, o_ckv:o_kr]
    kr_raw, dtraw = u[:, o_kr:o_dt], u[:, o_dt:]
    xc = _conv_fwd(xbc, cw, row(conv_b[0]), "conv")
    y_ssd, sprev = _ssd_fwd(dtraw, xc, dtb, alog, dsk_x, "ssd")
    yg, r_g = _gated_fwd(y_ssd, z, row(ssd_norm[0]), "ssd_gate_norm")
    q_lat, r_q = _rms_fwd(c_q, row(q_norm[0]), "q_norm")
    kv_lat, r_kv = _rms_fwd(c_kv, row(kv_norm[0]), "kv_norm")
    qf = _mm(q_lat, wuq, "nn", "q_up")
    kvb = _mm(kv_lat, wukv, "nn", "kv_up", out_dtype=BF16)
    q_r = _rope(qf, cos, sin, 2, False, BF16, "rope_q")
    k_r = _rope(kr_raw, cos, sin, 1, False, BF16, "rope_k")
    o_mla, lse = _mla_fwd(q_r, k_r, kvb, "mla")
    g1 = dict(zip(g1_names, _xwait(hg1, True, lse, "gather1_wait")))
    woe, wg0, wu0, wd0 = [full(k[0], g1[k]) for k in g1_names]
    cat = jnp.concatenate([yg, o_mla.astype(BF16)], axis=1)
    x1 = _mm(cat, woe, "nn", "mix_out_0", add=x0)
    x2, ffn0 = _ffn_fwd(x1, row(ffn_norm[0]), wg0, wu0, wd0, "0")

    g2 = dict(zip(g2_names, _xwait(hg2, True, x2, "gather2_wait")))
    wqkv, woo, wg1, wu1, wd1 = [full(k[0], g2[k]) for k in g2_names]
    h1, r1 = _rms_fwd(x2, row(mix_norm[1]), "mix_norm_1")
    qkv = _mm(h1, wqkv, "nn", "qkv_proj", out_dtype=BF16)
    o_sb, ltot = _sb_fwd(qkv, "sb")
    o_sbb = o_sb.astype(BF16)
    x3 = _mm(o_sbb, woo, "nn", "mix_out_1", add=x2)
    x4, ffn1 = _ffn_fwd(x3, row(ffn_norm[1]), wg1, wu1, wd1, "1")

    loss_part, dx4, dx4b, d_final = _loss_fwd_bwd(x4, row(final_norm), tgt, "loss")
    loss = lax.psum(loss_part, ("x", "y", "c"))

    def grad_shards(pairs):
        return [_shards_from_full(n, [g]) for n, g in pairs]

    dx3, dx3b, d_ffn1, d_wg1, d_wu1, d_wd1 = _ffn_bwd(dx4, dx4b, row(ffn_norm[1]), wg1, wu1, wd1, ffn1, "1")
    do_sb = _mm(dx3b, woo.T, "nn", "sb_dout")
    d_woo = _mm(o_sbb.T, dx3b, "nn", "d_w_out_odd", out_dtype=BF16)
    dq, dk, dv = _sb_bwd(qkv, do_sb, ltot, "sb_bwd")
    dqkv = jnp.concatenate([dq, dk, dv], axis=1).astype(BF16)
    d_wqkv = _mm(h1.T, dqkv, "nn", "d_w_qkv", out_dtype=BF16)
    x2_names = ['w_qkv', 'w_out_odd', 'w_gate', 'w_up', 'w_down']
    hx2 = _xstart(grad_shards(zip(x2_names, [d_wqkv, d_woo, d_wg1, d_wu1, d_wd1])), False, [], "grads2_start")
    dh1 = _mm(dqkv, wqkv.T, "nn", "qkv_dh")
    dx2, dx2b, d_mix1 = _rms_bwd(x2, row(mix_norm[1]) + hx2[-1][0:1, 0:1], r1, dh1, "mix_norm_bwd_1", dres=dx3)

    dx1, dx1b, d_ffn0, d_wg0, d_wu0, d_wd0 = _ffn_bwd(dx2, dx2b, row(ffn_norm[0]), wg0, wu0, wd0, ffn0, "0")
    d_woe = _mm(cat.T, dx1b, "nn", "d_w_out_even", out_dtype=BF16)
    x1_names = ['w_out_even', 'w_gate', 'w_up', 'w_down']
    hx1 = _xstart(grad_shards(zip(x1_names, [d_woe, d_wg0, d_wu0, d_wd0])), False, [], "grads1_start")
    tok1 = hx1[-1][0:1, 0:1]
    dcat = _mm(dx1b, woe.T, "nn", "mix_dcat")
    dy_ssd, dz, d_ssdn = _gated_bwd(y_ssd, z, row(ssd_norm[0]) + tok1, r_g, dcat[:, :hw], "ssd_gate_norm_bwd")
    dxs, db_, dc_, ddtraw, d_dtb, d_alog, d_dsk = _ssd_bwd(dtraw, xc, dtb, alog, dsk_x, sprev, dy_ssd, "ssd_bwd")
    dxbc, d_cw, d_cb = _conv_bwd(xbc, cw, row(conv_b[0]), jnp.concatenate([dxs, db_, dc_], axis=1), "conv_bwd")
    dqm, dkvm, dkr = _mla_bwd(q_r, k_r, kvb, o_mla, dcat[:, hw:], lse + tok1, "mla_bwd")
    dqf = _rope(dqm, cos, sin, 2, True, BF16, "rope_q_bwd")
    dkr_raw = _rope(dkr, cos, sin, 1, True, F32, "rope_k_bwd")
    dkvf = dkvm.astype(BF16)
    d_wuq = _mm(q_lat.T, dqf, "nn", "d_w_uq", out_dtype=BF16)
    d_wukv = _mm(kv_lat.T, dkvf, "nn", "d_w_ukv", out_dtype=BF16)
    dq_lat = _mm(dqf, wuq.T, "nn", "q_up_bwd")
    dkv_lat = _mm(dkvf, wukv.T, "nn", "kv_up_bwd")
    dc_q, _, d_qn = _rms_bwd(c_q, row(q_norm[0]), r_q, dq_lat, "q_norm_bwd")
    dc_kv, _, d_kvn = _rms_bwd(c_kv, row(kv_norm[0]), r_kv, dkv_lat, "kv_norm_bwd")
    du = jnp.concatenate([dz, dxbc, dc_q, dc_kv, dkr_raw, ddtraw], axis=1).astype(BF16)
    d_win = _mm(h0.T, du, "nn", "d_w_in", out_dtype=BF16)
    x0_names = ['w_in', 'conv_w', 'w_uq', 'w_ukv']
    hx0 = _xstart(grad_shards(zip(x0_names, [_win_unpad(d_win), d_cw, _wuq_unpad(d_wuq), d_wukv])), False, [], "grads0_start")
    dh0 = _mm(du, win.T, "nn", "in_proj_bwd")
    grad_x, _, d_mix0 = _rms_bwd(x0, row(mix_norm[0]) + hx0[-1][0:1, 0:1], r0, dh0, "mix_norm_bwd_0", dres=dx1)

    def chip_sums(handle, names, after, tag):
        lands = _xwait(handle, False, after, f"grads{tag}_wait")
        return {n: _sum4(p.reshape(4, -1, p.shape[-1]), f"sum{tag}_{n}") for n, p in zip(names, lands)}
    s2 = chip_sums(hx2, x2_names, grad_x, "2")
    s1 = chip_sums(hx1, x1_names, s2['w_qkv'], "1")
    s0 = chip_sums(hx0, x0_names, s1['w_out_even'], "0")
    by_name = {**s0, 'w_out_even': s1['w_out_even'], 'w_qkv': s2['w_qkv'], 'w_out_odd': s2['w_out_odd']}
    for n in ('w_gate', 'w_up', 'w_down'):
        by_name[n] = jnp.concatenate([s1[n], s2[n]], axis=0)
    psum = [by_name[n] for n in SHARDED]
    sib = _sibling_exchange(psum, "exchange_sibling")

    nhs = SSD_HEADS
    small_g = {'mix_norm': jnp.stack([d_mix0, d_mix1]), 'ffn_norm': jnp.stack([d_ffn0, d_ffn1]), 'conv_b': d_cb[None],
               'dt_bias': d_dtb[None, :nhs], 'a_log': d_alog[None, :nhs], 'd_skip': d_dsk[None, :nhs], 'ssd_norm': d_ssdn[None],
               'q_norm': d_qn[None], 'kv_norm': d_kvn[None], 'final_norm': d_final}
    g_small = _allreduce_small(_pack_small(small_g), "allreduce_small")
    pk = lambda pre: _pack_small({n: given[pre + n] for n in SMALL})
    sg, sd, sm, sv = _adamw(pk(''), g_small, None, pk('m_'), pk('v_'), "adamw_small")

    grads, deltas, new_m, new_v = {}, {}, {}, {}
    for n, ga, gb in zip(SHARDED, psum, sib):
        shp = wts[n].shape
        res = _adamw(_as2d(wts[n]), ga, gb, _as2d(given['m_' + n]), _as2d(given['v_' + n]), f"adamw_{n}")
        grads[n], deltas[n], new_m[n], new_v[n] = [r.reshape(shp) for r in res]
    for dst, src in ((grads, sg), (deltas, sd), (new_m, sm), (new_v, sv)):
        dst.update(_unpack_small(src, wts))

    outs = [loss, grad_x[None]]
    for dct in (grads, deltas, new_m, new_v):
        outs += [dct[n] for n in WEIGHTS]
    return tuple(outs)
```

```python
import functools
import math

import jax
import jax.numpy as jnp
import numpy as np
from jax import lax
from jax.experimental import pallas as pl
from jax.experimental.pallas import tpu as pltpu

F32, BF16 = jnp.float32, jnp.bfloat16

RMS_EPS = 1e-6
SSD_HEADS, SSD_HEAD_DIM, SSD_GROUPS, SSD_STATE, SSD_CONV, SSD_CHUNK = 32, 64, 4, 128, 4, 128
MLA_HEADS, MLA_Q_RANK, MLA_KV_RANK, MLA_NOPE, MLA_ROPE, MLA_V = 16, 512, 512, 128, 64, 128
ROPE_THETA = 10000.0
SB_HEADS, SB_HEAD_DIM = 16, 128
ADAM_LR, ADAM_B1, ADAM_B2, ADAM_EPS, ADAM_WD, ADAM_STEP = 0.001, 0.9, 0.999, 1e-08, 0.01, 10

LANES = 128
VMEM_LIMIT_BYTES = 56 * 1024 * 1024
MM_VMEM_BUDGET = 40 * 1024 * 1024
ATT_BLK = 256
ROW_TILE = 256
NEG = -1e30

MESH_T = pl.DeviceIdType.MESH
WEIGHTS = ['mix_norm', 'ffn_norm', 'w_in', 'conv_w', 'conv_b', 'dt_bias', 'a_log', 'd_skip', 'ssd_norm', 'q_norm',
           'kv_norm', 'w_uq', 'w_ukv', 'w_out_even', 'w_qkv', 'w_out_odd', 'w_gate', 'w_up', 'w_down', 'final_norm']
SHARDED = ['w_in', 'conv_w', 'w_uq', 'w_ukv', 'w_out_even', 'w_qkv', 'w_out_odd', 'w_gate', 'w_up', 'w_down']
COL_SHARDED = ['w_in', 'conv_w', 'w_uq', 'w_ukv', 'w_qkv', 'w_gate', 'w_up']
SMALL = [n for n in WEIGHTS if n not in SHARDED]


def _tile(n, cands):
    for c in cands:
        if n % c == 0:
            return c
    return n


def _params(ngrid):
    return pltpu.CompilerParams(dimension_semantics=("arbitrary",) * ngrid, vmem_limit_bytes=VMEM_LIMIT_BYTES)


def _dot(a, b, mode="nn"):
    dims = {"nn": (((1,), (0,)), ((), ())), "nt": (((1,), (1,)), ((), ())), "tn": (((0,), (0,)), ((), ()))}[mode]
    return lax.dot_general(a, b, dims, preferred_element_type=F32)


def _split(x, parts):
    out, r = [], x
    for _ in range(parts):
        p = r.astype(BF16)
        out.append(p)
        r = r - p.astype(F32)
    return out


def _xdot(x, e, parts=3):
    acc = None
    for p in _split(x, parts):
        t = _dot(p, e)
        acc = t if acc is None else acc + t
    return acc


def _xdot_l(e, x, parts=3):
    acc = None
    for p in _split(x, parts):
        t = _dot(e, p)
        acc = t if acc is None else acc + t
    return acc


def _iota(shape, dim):
    return lax.broadcasted_iota(jnp.int32, shape, dim)


def _softplus(z):
    return jnp.maximum(z, 0.0) + jnp.log(1.0 + jnp.exp(-jnp.abs(z)))


def _sigmoid(z):
    return 1.0 / (1.0 + jnp.exp(-z))


def _acc_rows(ref, first, val):
    @pl.when(first)
    def _():
        ref[...] = jnp.zeros_like(ref)
    ref[...] += jnp.broadcast_to(val, ref.shape)


def _mm(a, b, mode, name, out_dtype=F32, add=None):
    if mode == "nn":
        (m, k), n = a.shape, b.shape[1]
    elif mode == "nt":
        (m, k), n = a.shape, b.shape[0]
    else:
        (k, m), n = a.shape, b.shape[1]
    tn = _tile(n, (1536, 1408, 1280, 1024, 768, 640, 512, 256, 128))
    tk = k if k <= 2048 else _tile(k, (2048, 1536, 1408, 1280, 1024, 512, 256, 128))
    ob = jnp.dtype(out_dtype).itemsize

    def need(tm_):
        per = tm_ * tk * a.dtype.itemsize + tk * tn * b.dtype.itemsize + tm_ * tn * ob + (tm_ * tn * 4 if add is not None else 0)
        return 2 * per + (tm_ * tn * 4 if k > tk else 0)
    tm = m
    for cand in (1024, 512, 256, 128):
        if m % cand == 0:
            tm = cand
            if need(cand) <= MM_VMEM_BUDGET:
                break
    nk = k // tk
    a_spec = {"nn": pl.BlockSpec((tm, tk), lambda i, j, kk: (i, kk)), "nt": pl.BlockSpec((tm, tk), lambda i, j, kk: (i, kk)),
              "tn": pl.BlockSpec((tk, tm), lambda i, j, kk: (kk, i))}[mode]
    b_spec = {"nn": pl.BlockSpec((tk, tn), lambda i, j, kk: (kk, j)), "nt": pl.BlockSpec((tn, tk), lambda i, j, kk: (j, kk)),
              "tn": pl.BlockSpec((tk, tn), lambda i, j, kk: (kk, j))}[mode]
    o_spec = pl.BlockSpec((tm, tn), lambda i, j, kk: (i, j))
    has_add = add is not None

    def body(*refs):
        a_ref, b_ref = refs[0], refs[1]
        add_ref = refs[2] if has_add else None
        o_ref = refs[2 + has_add]
        part = _dot(a_ref[...].astype(BF16), b_ref[...].astype(BF16), mode)
        if nk == 1:
            if has_add:
                part = part + add_ref[...]
            o_ref[...] = part.astype(o_ref.dtype)
            return
        acc_ref = refs[3 + has_add]
        kk = pl.program_id(2)

        @pl.when(kk == 0)
        def _():
            acc_ref[...] = jnp.zeros_like(acc_ref)
        acc_ref[...] += part

        @pl.when(kk == nk - 1)
        def _():
            r = acc_ref[...]
            if has_add:
                r = r + add_ref[...]
            o_ref[...] = r.astype(o_ref.dtype)

    ins = [a, b] + ([add] if has_add else [])
    specs = [a_spec, b_spec] + ([o_spec] if has_add else [])
    return pl.pallas_call(
        body, name=name, grid=(m // tm, n // tn, nk), in_specs=specs, out_specs=o_spec,
        out_shape=jax.ShapeDtypeStruct((m, n), out_dtype),
        scratch_shapes=[pltpu.VMEM((tm, tn), F32)] if nk > 1 else [],
        compiler_params=_params(3))(*ins)


def _rows(t):
    return _tile(t, (ROW_TILE, 128, 64, 32, 16, 8))


def _rms_fwd(x, g, name):
    t, c = x.shape
    tr = _rows(t)

    def body(x_ref, g_ref, h_ref, r_ref):
        xv = x_ref[...]
        r = lax.rsqrt(jnp.mean(xv * xv, axis=-1, keepdims=True) + RMS_EPS)
        h_ref[...] = (xv * r * g_ref[...]).astype(h_ref.dtype)
        r_ref[...] = r

    return pl.pallas_call(
        body, name=name, grid=(t // tr,),
        in_specs=[pl.BlockSpec((tr, c), lambda i: (i, 0)), pl.BlockSpec((1, c), lambda i: (0, 0))],
        out_specs=[pl.BlockSpec((tr, c), lambda i: (i, 0)), pl.BlockSpec((tr, 1), lambda i: (i, 0))],
        out_shape=[jax.ShapeDtypeStruct((t, c), BF16), jax.ShapeDtypeStruct((t, 1), F32)],
        compiler_params=_params(1))(x, g)


def _rms_bwd(x, g, r, dh, name, dres=None):
    t, c = x.shape
    tr = _rows(t)
    has_res = dres is not None

    def body(*refs):
        x_ref, g_ref, r_ref, dh_ref = refs[:4]
        res_ref = refs[4] if has_res else None
        dx_ref, dxb_ref, dg_ref = refs[4 + has_res:]
        rv = r_ref[...]
        xh = x_ref[...] * rv
        dhv = dh_ref[...]
        dxh = dhv * g_ref[...]
        cm = jnp.mean(dxh * xh, axis=-1, keepdims=True)
        dx = (dxh - xh * cm) * rv
        if has_res:
            dx = dx + res_ref[...]
        dx_ref[...] = dx
        dxb_ref[...] = dx.astype(BF16)
        _acc_rows(dg_ref, pl.program_id(0) == 0, jnp.sum(dhv * xh, axis=0, keepdims=True))

    row = pl.BlockSpec((tr, c), lambda i: (i, 0))
    ins = [x, g, r, dh] + ([dres] if has_res else [])
    specs = [row, pl.BlockSpec((1, c), lambda i: (0, 0)), pl.BlockSpec((tr, 1), lambda i: (i, 0)), row] + ([row] if has_res else [])
    dx, dxb, dg = pl.pallas_call(
        body, name=name, grid=(t // tr,), in_specs=specs,
        out_specs=[row, row, pl.BlockSpec((8, c), lambda i: (0, 0))],
        out_shape=[jax.ShapeDtypeStruct((t, c), F32), jax.ShapeDtypeStruct((t, c), BF16), jax.ShapeDtypeStruct((8, c), F32)],
        compiler_params=_params(1))(*ins)
    return dx, dxb, dg[0]


def _gated_fwd(y, z, g, name):
    t, c = y.shape
    tr = _rows(t)

    def body(y_ref, z_ref, g_ref, o_ref, r_ref):
        zv = z_ref[...]
        v = y_ref[...] * zv * _sigmoid(zv)
        r = lax.rsqrt(jnp.mean(v * v, axis=-1, keepdims=True) + RMS_EPS)
        o_ref[...] = (v * r * g_ref[...]).astype(o_ref.dtype)
        r_ref[...] = r

    row = pl.BlockSpec((tr, c), lambda i: (i, 0))
    return pl.pallas_call(
        body, name=name, grid=(t // tr,), in_specs=[row, row, pl.BlockSpec((1, c), lambda i: (0, 0))],
        out_specs=[row, pl.BlockSpec((tr, 1), lambda i: (i, 0))],
        out_shape=[jax.ShapeDtypeStruct((t, c), BF16), jax.ShapeDtypeStruct((t, 1), F32)],
        compiler_params=_params(1))(y, z, g)


def _gated_bwd(y, z, g, r, dout, name):
    t, c = y.shape
    tr = _rows(t)

    def body(y_ref, z_ref, g_ref, r_ref, do_ref, dy_ref, dz_ref, dg_ref):
        yv, zv, rv, dov = y_ref[...], z_ref[...], r_ref[...], do_ref[...]
        s = _sigmoid(zv)
        sz = zv * s
        xh = yv * sz * rv
        dxh = dov * g_ref[...]
        cm = jnp.mean(dxh * xh, axis=-1, keepdims=True)
        dv = (dxh - xh * cm) * rv
        dy_ref[...] = dv * sz
        dz_ref[...] = dv * yv * s * (1.0 + zv * (1.0 - s))
        _acc_rows(dg_ref, pl.program_id(0) == 0, jnp.sum(dov * xh, axis=0, keepdims=True))

    row = pl.BlockSpec((tr, c), lambda i: (i, 0))
    dy, dz, dg = pl.pallas_call(
        body, name=name, grid=(t // tr,),
        in_specs=[row, row, pl.BlockSpec((1, c), lambda i: (0, 0)), pl.BlockSpec((tr, 1), lambda i: (i, 0)), row],
        out_specs=[row, row, pl.BlockSpec((8, c), lambda i: (0, 0))],
        out_shape=[jax.ShapeDtypeStruct((t, c), F32), jax.ShapeDtypeStruct((t, c), F32), jax.ShapeDtypeStruct((8, c), F32)],
        compiler_params=_params(1))(y, z, g, r, dout)
    return dy, dz, dg[0]


def _swiglu_fwd(gate, up, name):
    t, c = gate.shape
    tr, tc = _rows(t), _tile(c, (2816, 1408, 1024, 512, 256, 128))

    def body(g_ref, u_ref, o_ref):
        gv = g_ref[...]
        o_ref[...] = (gv * _sigmoid(gv) * u_ref[...]).astype(o_ref.dtype)

    blk = pl.BlockSpec((tr, tc), lambda i, j: (i, j))
    return pl.pallas_call(body, name=name, grid=(t // tr, c // tc), in_specs=[blk, blk], out_specs=blk,
                          out_shape=jax.ShapeDtypeStruct((t, c), BF16), compiler_params=_params(2))(gate, up)


def _swiglu_bwd(gate, up, dact, name):
    t, c = gate.shape
    tr, tc = _rows(t), _tile(c, (2816, 1408, 1024, 512, 256, 128))

    def body(g_ref, u_ref, d_ref, dg_ref, du_ref):
        gv, dv = g_ref[...], d_ref[...]
        s = _sigmoid(gv)
        dg_ref[...] = (dv * u_ref[...] * s * (1.0 + gv * (1.0 - s))).astype(dg_ref.dtype)
        du_ref[...] = (dv * gv * s).astype(du_ref.dtype)

    blk = pl.BlockSpec((tr, tc), lambda i, j: (i, j))
    return pl.pallas_call(body, name=name, grid=(t // tr, c // tc), in_specs=[blk, blk, blk], out_specs=[blk, blk],
                          out_shape=[jax.ShapeDtypeStruct((t, c), BF16)] * 2, compiler_params=_params(2))(gate, up, dact)


def _loss_fwd_bwd(x, g, tgt, name):
    t, c = x.shape
    tr = _rows(t)

    def body(x_ref, g_ref, t_ref, l_ref, dx_ref, dxb_ref, dg_ref):
        xv, gv = x_ref[...], g_ref[...]
        r = lax.rsqrt(jnp.mean(xv * xv, axis=-1, keepdims=True) + RMS_EPS)
        xh = xv * r
        err = xh * gv - t_ref[...]
        per_row = jnp.mean(err * err, axis=-1, keepdims=True)
        dy = err * (1.0 / c)
        dxh = dy * gv
        cm = jnp.mean(dxh * xh, axis=-1, keepdims=True)
        dx = (dxh - xh * cm) * r
        dx_ref[...] = dx
        dxb_ref[...] = dx.astype(BF16)
        first = pl.program_id(0) == 0
        _acc_rows(dg_ref, first, jnp.sum(dy * xh, axis=0, keepdims=True))
        _acc_rows(l_ref, first, jnp.broadcast_to(0.5 * jnp.sum(per_row, axis=0, keepdims=True), (1, LANES)))

    row = pl.BlockSpec((tr, c), lambda i: (i, 0))
    lo, dx, dxb, dg = pl.pallas_call(
        body, name=name, grid=(t // tr,), in_specs=[row, pl.BlockSpec((1, c), lambda i: (0, 0)), row],
        out_specs=[pl.BlockSpec((8, LANES), lambda i: (0, 0)), row, row, pl.BlockSpec((8, c), lambda i: (0, 0))],
        out_shape=[jax.ShapeDtypeStruct((8, LANES), F32), jax.ShapeDtypeStruct((t, c), F32), jax.ShapeDtypeStruct((t, c), BF16),
                   jax.ShapeDtypeStruct((8, c), F32)],
        compiler_params=_params(1))(x, g, tgt)
    return lo[0, 0], dx, dxb, dg[0]


def _conv_specs(t, c):
    tr = _rows(t)
    tc = _tile(c, (1024, 768, 512, 256, 128))
    h8 = tr // 8
    tile = pl.BlockSpec((tr, tc), lambda j, i: (i, j))
    prev = pl.BlockSpec((8, tc), lambda j, i: (jnp.maximum(i * h8 - 1, 0), j))
    nxt = pl.BlockSpec((8, tc), lambda j, i: (jnp.minimum((i + 1) * h8, t // 8 - 1), j))
    return tr, tc, tile, prev, nxt


def _conv_fwd(xbc, w, b, name):
    t, c = xbc.shape
    tr, tc, tile, prev, _ = _conv_specs(t, c)

    def body(x_ref, p_ref, w_ref, b_ref, o_ref, buf):
        i = pl.program_id(1)
        buf[0:8, :] = jnp.where(i > 0, p_ref[...], 0.0)
        buf[8:, :] = x_ref[...]
        pre = b_ref[...]
        for k in range(SSD_CONV):
            pre = pre + w_ref[k:k + 1, :] * buf[pl.ds(8 - (SSD_CONV - 1) + k, tr), :]
        o_ref[...] = pre * _sigmoid(pre)

    return pl.pallas_call(
        body, name=name, grid=(c // tc, t // tr),
        in_specs=[tile, prev, pl.BlockSpec((8, tc), lambda j, i: (0, j)), pl.BlockSpec((1, tc), lambda j, i: (0, j))],
        out_specs=tile, out_shape=jax.ShapeDtypeStruct((t, c), F32),
        scratch_shapes=[pltpu.VMEM((tr + 8, tc), F32)], compiler_params=_params(2))(xbc, xbc, w, b)


def _conv_bwd(xbc, w, b, dout, name):
    t, c = xbc.shape
    tr, tc, tile, prev, nxt = _conv_specs(t, c)
    nt = t // tr
    kc = SSD_CONV

    def body(x_ref, p_ref, n_ref, w_ref, b_ref, d_ref, dn_ref, dx_ref, dw_ref, db_ref, buf, dbuf):
        i = pl.program_id(1)
        last = i == nt - 1
        buf[0:8, :] = jnp.where(i > 0, p_ref[...], 0.0)
        buf[8:8 + tr, :] = x_ref[...]
        buf[8 + tr:, :] = jnp.where(last, 0.0, n_ref[...])
        pre = b_ref[...]
        for k in range(kc):
            pre = pre + w_ref[k:k + 1, :] * buf[pl.ds(8 - (kc - 1) + k, tr + 8), :]
        s = _sigmoid(pre)
        dsilu = s * (1.0 + pre * (1.0 - s))
        dbuf[0:tr, :] = d_ref[...] * dsilu[0:tr, :]
        dbuf[tr:, :] = jnp.where(last, 0.0, dn_ref[...]) * dsilu[tr:, :]
        dpre = dbuf[0:tr, :]
        dx = jnp.zeros((tr, tc), F32)
        first = i == 0
        for k in range(kc):
            dx = dx + w_ref[k:k + 1, :] * dbuf[pl.ds(kc - 1 - k, tr), :]
        dx_ref[...] = dx

        @pl.when(first)
        def _():
            dw_ref[...] = jnp.zeros_like(dw_ref)
        for k in range(kc):
            dw_ref[k:k + 1, :] += jnp.sum(dpre * buf[pl.ds(8 - (kc - 1) + k, tr), :], axis=0, keepdims=True)
        _acc_rows(db_ref, first, jnp.sum(dpre, axis=0, keepdims=True))

    par = pl.BlockSpec((8, tc), lambda j, i: (0, j))
    dx, dw, db = pl.pallas_call(
        body, name=name, grid=(c // tc, nt),
        in_specs=[tile, prev, nxt, par, pl.BlockSpec((1, tc), lambda j, i: (0, j)), tile, nxt],
        out_specs=[tile, par, par],
        out_shape=[jax.ShapeDtypeStruct((t, c), F32), jax.ShapeDtypeStruct((8, c), F32), jax.ShapeDtypeStruct((8, c), F32)],
        scratch_shapes=[pltpu.VMEM((tr + 16, tc), F32), pltpu.VMEM((tr + 8, tc), F32)],
        compiler_params=_params(2))(xbc, xbc, xbc, w, b, dout, dout)
    return dx, dw[:kc], db[0]


def _ssd_consts():
    h, p, ln = SSD_HEADS, SSD_HEAD_DIM, SSD_CHUNK
    w = h * p
    hrow, jcol = _iota((LANES, w), 0), _iota((LANES, w), 1)
    e = ((jcol >= hrow * p) & (jcol < (hrow + 1) * p)).astype(BF16)
    jrow, hcol = _iota((w, LANES), 0), _iota((w, LANES), 1)
    et = ((jrow >= hcol * p) & (jrow < (hcol + 1) * p)).astype(BF16)
    row, col = _iota((ln, ln), 0), _iota((ln, ln), 1)
    return e, et, row, col


def _ssd_common(dtraw_ref, dtb_ref, alog_ref, e):
    ln = SSD_CHUNK
    raw = dtraw_ref[...] + dtb_ref[...]
    dt = _softplus(raw)
    a = -jnp.exp(alog_ref[...])
    adt = dt * a
    row, col = _iota((ln, ln), 0), _iota((ln, ln), 1)
    cs = _xdot_l((col <= row).astype(BF16), adt)
    cl = jnp.sum(adt, axis=0, keepdims=True)
    ex = _xdot(jnp.concatenate([dt, cs, jnp.broadcast_to(cl, (ln, LANES))], axis=0), e)
    return raw, dt, a, cs, ex[:ln], ex[ln:2 * ln], ex[2 * ln:]


def _head_decay(cs, h, causal):
    ln = SSD_CHUNK
    lane = _iota((1, LANES), 1)
    colv = jnp.sum(jnp.where(lane == h, cs, 0.0), axis=1, keepdims=True)
    cb = jnp.broadcast_to(colv, (ln, ln))
    return jnp.exp(jnp.where(causal, cb - cb.T, NEG))


def _ssd_fwd(dtraw, xc, dtb, alog, dskip_x, name):
    t = dtraw.shape[0]
    h, p, g, n, ln = SSD_HEADS, SSD_HEAD_DIM, SSD_GROUPS, SSD_STATE, SSD_CHUNK
    w, gn, gw, hpg = h * p, g * n, (h // g) * p, h // g
    assert n == ln and gw % LANES == 0 and w % gn == 0
    nc = t // ln

    def body(dtraw_ref, x_ref, b_ref, c_ref, dtb_ref, alog_ref, dsk_ref, y_ref, sp_ref, s_ref):
        @pl.when(pl.program_id(0) == 0)
        def _():
            s_ref[...] = jnp.zeros_like(s_ref)
        e, _, row, col = _ssd_consts()
        causal = col <= row
        _, _, _, cs, dt_x, cs_x, cl_x = _ssd_common(dtraw_ref, dtb_ref, alog_ref, e)
        xv = x_ref[...]
        xd = xv * dt_x
        xdb = xd.astype(BF16)
        sv = s_ref[...]
        sp_ref[0] = sv
        el_x = jnp.exp(cs_x)
        zb = (xd * jnp.exp(cl_x - cs_x)).astype(BF16)
        cd_x = jnp.exp(cl_x)
        dsk = dsk_ref[...]
        half = _iota((1, LANES), 1) >= p
        for gi in range(g):
            gs = slice(gi * gw, (gi + 1) * gw)
            bg = b_ref[:, gi * n:(gi + 1) * n].astype(BF16)
            cg = c_ref[:, gi * n:(gi + 1) * n].astype(BF16)
            gm = _dot(cg, bg, "nt")
            sg = sv[:, gs]
            yoff = _dot(cg, sg.astype(BF16)) * el_x[:, gs]
            s_ref[:, gs] = sg * cd_x[:, gs] + _dot(bg, zb[:, gs], "tn")
            for pp in range(gw // LANES):
                ls = slice(gi * gw + pp * LANES, gi * gw + (pp + 1) * LANES)
                xp = xdb[:, ls]
                yp = yoff[:, pp * LANES:(pp + 1) * LANES] + dsk[:, ls] * xv[:, ls]
                for hh in range(LANES // p):
                    hd = gi * hpg + pp * (LANES // p) + hh
                    wm = (gm * _head_decay(cs, hd, causal)).astype(BF16)
                    yp = yp + _dot(wm, jnp.where(half == (hh == 1), xp, jnp.zeros_like(xp)))
                y_ref[:, ls] = yp

    nar = pl.BlockSpec((ln, LANES), lambda c: (c, 0))
    one = pl.BlockSpec((1, LANES), lambda c: (0, 0))
    return pl.pallas_call(
        body, name=name, grid=(nc,),
        in_specs=[nar, pl.BlockSpec((ln, w), lambda c: (c, 0)), pl.BlockSpec((ln, gn), lambda c: (c, w // gn)),
                  pl.BlockSpec((ln, gn), lambda c: (c, w // gn + 1)), one, one, pl.BlockSpec((1, w), lambda c: (0, 0))],
        out_specs=[pl.BlockSpec((ln, w), lambda c: (c, 0)), pl.BlockSpec((1, n, w), lambda c: (c, 0, 0))],
        out_shape=[jax.ShapeDtypeStruct((t, w), F32), jax.ShapeDtypeStruct((nc, n, w), F32)],
        scratch_shapes=[pltpu.VMEM((n, w), F32)], compiler_params=_params(1))(dtraw, xc, xc, xc, dtb, alog, dskip_x)


def _ssd_bwd(dtraw, xc, dtb, alog, dskip_x, sprev, dy, name):
    t = dtraw.shape[0]
    h, p, g, n, ln = SSD_HEADS, SSD_HEAD_DIM, SSD_GROUPS, SSD_STATE, SSD_CHUNK
    w, gn, gw, hpg = h * p, g * n, (h // g) * p, h // g
    nc = t // ln

    def body(dtraw_ref, x_ref, b_ref, c_ref, dtb_ref, alog_ref, dsk_ref, sp_ref, dy_ref,
             dx_ref, db_ref, dc_ref, ddt_ref, dbias_ref, dalog_ref, ddsk_ref, ds_ref, dxd_ref, qcs_ref):
        first = pl.program_id(0) == 0

        @pl.when(first)
        def _():
            ds_ref[...] = jnp.zeros_like(ds_ref)
        e, et, row, col = _ssd_consts()
        causal = col <= row
        raw, dt, a, cs, dt_x, cs_x, cl_x = _ssd_common(dtraw_ref, dtb_ref, alog_ref, e)
        xv = x_ref[...]
        xd = xv * dt_x
        xdb = xd.astype(BF16)
        sv = sp_ref[0]
        dyv = dy_ref[...]
        dyb = dyv.astype(BF16)
        dsn = ds_ref[...]
        el_x = jnp.exp(cs_x)
        dte_x = jnp.exp(cl_x - cs_x)
        cd_x = jnp.exp(cl_x)
        zf = xd * dte_x
        lane = _iota((1, LANES), 1)
        half = lane >= p
        lastrow = _iota((ln, 1), 0) == ln - 1
        dcs = jnp.zeros((ln, LANES), F32)
        for gi in range(g):
            gs = slice(gi * gw, (gi + 1) * gw)
            ns = slice(gi * n, (gi + 1) * n)
            bg = b_ref[:, ns].astype(BF16)
            cg = c_ref[:, ns].astype(BF16)
            gm = _dot(cg, bg, "nt")
            sgb = sv[:, gs].astype(BF16)
            dsg = dsn[:, gs]
            dsgb = dsg.astype(BF16)
            yoff = _dot(cg, sgb) * el_x[:, gs]
            drb = (el_x[:, gs] * dyv[:, gs]).astype(BF16)
            dcg = _dot(drb, sgb, "nt")
            ds_ref[:, gs] = cd_x[:, gs] * dsg + _dot(cg, drb, "tn")
            dz = _dot(bg, dsgb)
            zg = zf[:, gs]
            dbg = _dot(zg.astype(BF16), dsgb, "nt")
            dzz = dz * zg
            qcl = jnp.sum(dzz + cd_x[:, gs] * dsg * sv[:, gs], axis=0, keepdims=True)
            qcs_ref[:, gs] = dyv[:, gs] * yoff - dzz + jnp.where(lastrow, jnp.broadcast_to(qcl, (ln, gw)), 0.0)
            dgm = jnp.zeros((ln, ln), F32)
            for pp in range(gw // LANES):
                ls = slice(gi * gw + pp * LANES, gi * gw + (pp + 1) * LANES)
                xp = xdb[:, ls]
                dxp = dz[:, pp * LANES:(pp + 1) * LANES] * dte_x[:, ls]
                for hh in range(LANES // p):
                    hd = gi * hpg + pp * (LANES // p) + hh
                    dm = _head_decay(cs, hd, causal)
                    wf = gm * dm
                    dym = jnp.where(half == (hh == 1), dyb[:, ls], jnp.zeros_like(xp))
                    dw = _dot(dym, xp, "nt")
                    dxp = dxp + _dot(wf.astype(BF16), dym, "tn")
                    dgm = dgm + dw * dm
                    mm = dw * wf
                    rc = jnp.sum(mm, axis=1, keepdims=True) - jnp.sum(mm.T, axis=1, keepdims=True)
                    dcs = dcs + rc * (lane == hd).astype(F32)
                dxd_ref[:, ls] = dxp
            dgb = dgm.astype(BF16)
            dc_ref[:, ns] = dcg + _dot(dgb, bg)
            db_ref[:, ns] = dbg + _dot(dgb, cg, "tn")
        dxd = dxd_ref[...]
        dx_ref[...] = dxd * dt_x + dsk_ref[...] * dyv
        red = _xdot(jnp.concatenate([qcs_ref[...], dxd * xv, dyv * xv], axis=0), et)
        dcs = dcs + red[:ln]
        dadt = _xdot_l((row <= col).astype(BF16), dcs)
        ddt = red[ln:2 * ln] + dadt * a
        draw = ddt * _sigmoid(raw)
        ddt_ref[...] = draw
        _acc_rows(dbias_ref, first, jnp.sum(draw, axis=0, keepdims=True))
        _acc_rows(dalog_ref, first, jnp.sum(dadt * dt, axis=0, keepdims=True) * a)
        _acc_rows(ddsk_ref, first, jnp.sum(red[2 * ln:], axis=0, keepdims=True))

    rev = lambda c: nc - 1 - c
    nar = pl.BlockSpec((ln, LANES), lambda c: (rev(c), 0))
    one = pl.BlockSpec((1, LANES), lambda c: (0, 0))
    wide = pl.BlockSpec((ln, w), lambda c: (rev(c), 0))
    bcs = pl.BlockSpec((ln, gn), lambda c: (rev(c), 0))
    acc = pl.BlockSpec((8, LANES), lambda c: (0, 0))
    outs = pl.pallas_call(
        body, name=name, grid=(nc,),
        in_specs=[nar, wide, pl.BlockSpec((ln, gn), lambda c: (rev(c), w // gn)), pl.BlockSpec((ln, gn), lambda c: (rev(c), w // gn + 1)),
                  one, one, pl.BlockSpec((1, w), lambda c: (0, 0)), pl.BlockSpec((1, n, w), lambda c: (rev(c), 0, 0)), wide],
        out_specs=[wide, bcs, bcs, nar, acc, acc, acc],
        out_shape=[jax.ShapeDtypeStruct((t, w), F32), jax.ShapeDtypeStruct((t, gn), F32), jax.ShapeDtypeStruct((t, gn), F32),
                   jax.ShapeDtypeStruct((t, LANES), F32)] + [jax.ShapeDtypeStruct((8, LANES), F32)] * 3,
        scratch_shapes=[pltpu.VMEM((n, w), F32), pltpu.VMEM((ln, w), F32), pltpu.VMEM((ln, w), F32)],
        compiler_params=_params(1))(dtraw, xc, xc, xc, dtb, alog, dskip_x, sprev, dy)
    dx, db, dc, ddt, dbias, dalog, ddsk = outs
    return dx, db, dc, ddt, dbias[0], dalog[0], ddsk[0]


def _rope_tables(t):
    half = MLA_ROPE // 2
    inv_freq = ROPE_THETA ** (-jnp.arange(half, dtype=F32) / half)
    ang = jnp.arange(t, dtype=F32)[:, None] * inv_freq[None, :]
    cos, sin = jnp.cos(ang), jnp.sin(ang)
    pad = LANES - MLA_ROPE
    cos_r = jnp.concatenate([cos, cos, jnp.ones((t, pad), F32)], axis=1)
    sin_r = jnp.concatenate([sin, sin, jnp.zeros((t, pad), F32)], axis=1)
    return cos_r, sin_r


def _rot_matrix():
    half = MLA_ROPE // 2
    i, j = _iota((LANES, LANES), 0), _iota((LANES, LANES), 1)
    neg = (j < half) & (i == j + half)
    pos = (j >= half) & (j < 2 * half) & (i == j - half)
    return (pos.astype(F32) - neg.astype(F32)).astype(BF16)


def _rope(x, cos, sin, every, transpose, out_dtype, name):
    t, w = x.shape
    tr = _rows(t)

    def body(x_ref, c_ref, s_ref, o_ref):
        rot = _rot_matrix()
        cv, sv = c_ref[...], s_ref[...]
        for j in range(w // LANES):
            ls = slice(j * LANES, (j + 1) * LANES)
            xv = x_ref[:, ls]
            if j % every != every - 1:
                o_ref[:, ls] = xv.astype(o_ref.dtype)
            elif transpose:
                o_ref[:, ls] = (xv * cv - _xdot(xv * sv, rot, 2)).astype(o_ref.dtype)
            else:
                o_ref[:, ls] = (xv * cv + _xdot(xv, rot, 2) * sv).astype(o_ref.dtype)

    wide = pl.BlockSpec((tr, w), lambda i: (i, 0))
    tab = pl.BlockSpec((tr, LANES), lambda i: (i, 0))
    return pl.pallas_call(
        body, name=name, grid=(t // tr,), in_specs=[wide, tab, tab], out_specs=wide,
        out_shape=jax.ShapeDtypeStruct((t, w), out_dtype), compiler_params=_params(1))(x, cos, sin)


def _att_masks(blk):
    return _iota((blk, blk), 0), _iota((blk, blk), 1)


def _lanes(j):
    return slice(j * LANES, (j + 1) * LANES)


def _mla_fwd(q, kr, kv, name):
    t = q.shape[0]
    nh, blk = MLA_HEADS, min(ATT_BLK, t)
    scale = (MLA_NOPE + MLA_ROPE) ** -0.5

    def body(q_ref, kv_ref, kr_ref, o_ref, lse_ref):
        i = pl.program_id(1)
        row, col = _att_masks(blk)
        qs = [(q_ref[:, _lanes(2 * hh)], q_ref[:, _lanes(2 * hh + 1)]) for hh in range(2)]

        def scores(kb, hh):
            ks = pl.ds(pl.multiple_of(kb * blk, blk), blk)
            s = (_dot(qs[hh][0], kv_ref[ks, _lanes(2 * hh)], "nt") + _dot(qs[hh][1], kr_ref[ks, :], "nt")) * scale
            return s, kv_ref[ks, _lanes(2 * hh + 1)]
        init = []
        for hh in range(2):
            s, v = scores(i, hh)
            s = jnp.where(col <= row, s, NEG)
            m = jnp.max(s, axis=1, keepdims=True)
            pr = jnp.exp(s - m)
            init += [m, jnp.sum(pr, axis=1, keepdims=True), _dot(pr.astype(BF16), v)]

        def step(kb, carry):
            sv = [scores(kb, hh) for hh in range(2)]
            out, prs = [], []
            for hh in range(2):
                m, l, acc = carry[3 * hh:3 * hh + 3]
                s = sv[hh][0]
                m2 = jnp.maximum(m, jnp.max(s, axis=1, keepdims=True))
                al = jnp.exp(m - m2)
                pr = jnp.exp(s - m2)
                prs.append(pr.astype(BF16))
                out += [m2, al * l + jnp.sum(pr, axis=1, keepdims=True), al * acc]
            for hh in range(2):
                out[3 * hh + 2] = out[3 * hh + 2] + _dot(prs[hh], sv[hh][1])
            return tuple(out)
        res = lax.fori_loop(0, i, step, tuple(init))
        for hh in range(2):
            m, l, acc = res[3 * hh:3 * hh + 3]
            o_ref[:, _lanes(hh)] = acc / l
            lse_ref[hh] = m + jnp.log(l)

    return pl.pallas_call(
        body, name=name, grid=(nh // 2, t // blk),
        in_specs=[pl.BlockSpec((blk, 4 * LANES), lambda h, i: (i, h)), pl.BlockSpec((t, 4 * LANES), lambda h, i: (0, h)),
                  pl.BlockSpec((t, LANES), lambda h, i: (0, 0))],
        out_specs=[pl.BlockSpec((blk, 2 * LANES), lambda h, i: (i, h)), pl.BlockSpec((2, blk, 1), lambda h, i: (h, i, 0))],
        out_shape=[jax.ShapeDtypeStruct((t, nh * LANES), F32), jax.ShapeDtypeStruct((nh, t, 1), F32)],
        compiler_params=_params(2))(q, kv, kr)


def _mla_bwd(q, kr, kv, o, do, lse, name):
    t = q.shape[0]
    nh, blk = MLA_HEADS, min(ATT_BLK, t)
    scale = (MLA_NOPE + MLA_ROPE) ** -0.5

    def body(q_ref, kv_ref, kr_ref, o_ref, do_ref, lse_ref, dq_ref, dkv_ref, dkr_ref):
        hp, i = pl.program_id(0), pl.program_id(1)

        @pl.when(i == 0)
        def _():
            dkv_ref[...] = jnp.zeros_like(dkv_ref)

        @pl.when((i == 0) & (hp == 0))
        def _():
            dkr_ref[...] = jnp.zeros_like(dkr_ref)
        row, col = _att_masks(blk)
        qs = [(q_ref[:, _lanes(2 * hh)], q_ref[:, _lanes(2 * hh + 1)]) for hh in range(2)]
        dobs = [do_ref[:, _lanes(hh)].astype(BF16) for hh in range(2)]
        deltas = [jnp.sum(do_ref[:, _lanes(hh)] * o_ref[:, _lanes(hh)], axis=1, keepdims=True) for hh in range(2)]
        lses = [lse_ref[hh] for hh in range(2)]

        def tile(kb, carry, masked):
            ks = pl.ds(pl.multiple_of(kb * blk, blk), blk)
            krv = kr_ref[ks, :]
            hs = range(2)
            kns = [kv_ref[ks, _lanes(2 * hh)] for hh in hs]
            ss = [_dot(qs[hh][0], kns[hh], "nt") + _dot(qs[hh][1], krv, "nt") for hh in hs]
            dps = [_dot(dobs[hh], kv_ref[ks, _lanes(2 * hh + 1)], "nt") for hh in hs]
            prb, dsb = [], []
            for hh in hs:
                pr = jnp.exp(ss[hh] * scale - lses[hh])
                if masked:
                    pr = jnp.where(col <= row, pr, 0.0)
                prb.append(pr.astype(BF16))
                dsb.append((pr * (dps[hh] - deltas[hh]) * scale).astype(BF16))
            out = []
            for hh in hs:
                dkv_ref[ks, _lanes(2 * hh + 1)] += _dot(prb[hh], dobs[hh], "tn")
                dkv_ref[ks, _lanes(2 * hh)] += _dot(dsb[hh], qs[hh][0], "tn")
                out += [carry[2 * hh] + _dot(dsb[hh], kns[hh]), carry[2 * hh + 1] + _dot(dsb[hh], krv)]
            dkr_ref[ks, :] += _dot(dsb[0], qs[0][1], "tn") + _dot(dsb[1], qs[1][1], "tn")
            return tuple(out)
        zero = jnp.zeros((blk, LANES), F32)
        carry = lax.fori_loop(0, i, lambda kb, c: tile(kb, c, False), (zero,) * 4)
        res = tile(i, carry, True)
        for j in range(4):
            dq_ref[:, _lanes(j)] = res[j]

    qb = lambda w: pl.BlockSpec((blk, w * LANES), lambda h, i: (i, h))
    wide = jax.ShapeDtypeStruct((t, nh * 2 * LANES), F32)
    return pl.pallas_call(
        body, name=name, grid=(nh // 2, t // blk),
        in_specs=[qb(4), pl.BlockSpec((t, 4 * LANES), lambda h, i: (0, h)), pl.BlockSpec((t, LANES), lambda h, i: (0, 0)),
                  qb(2), qb(2), pl.BlockSpec((2, blk, 1), lambda h, i: (h, i, 0))],
        out_specs=[qb(4), pl.BlockSpec((t, 4 * LANES), lambda h, i: (0, h)), pl.BlockSpec((t, LANES), lambda h, i: (0, 0))],
        out_shape=[wide, wide, jax.ShapeDtypeStruct((t, LANES), F32)],
        compiler_params=_params(2))(q, kv, kr, o, do, lse)


def _sb_fwd(qkv, name):
    t = qkv.shape[0]
    nh, blk = SB_HEADS, min(ATT_BLK, t)
    nq = t // blk
    scale = SB_HEAD_DIM ** -0.5

    def body(q_ref, k_ref, v_ref, o_ref, lt_ref):
        i = pl.program_id(1)
        row, col = _att_masks(blk)
        usuf = (row > col).astype(BF16)
        qs = [q_ref[:, _lanes(hh)] for hh in range(2)]

        def tile(kb, carry, masked):
            ks = pl.ds(pl.multiple_of(kb * blk, blk), blk)
            hs = range(2)
            zs = [_dot(qs[hh], k_ref[ks, _lanes(hh)], "nt") for hh in hs]
            lks, lss = [], []
            for hh in hs:
                z = zs[hh] * scale
                lk = -_softplus(z)
                lss.append(lk + z)
                lks.append(jnp.where(col < row, lk, 0.0) if masked else lk)
            pieces = [_split(lks[hh], 2) for hh in hs]
            later = [_dot(pieces[hh][0], usuf) + _dot(pieces[hh][1], usuf) for hh in hs]
            out = []
            for hh in hs:
                wt = jnp.exp(lss[hh] + later[hh] + carry[2 * hh + 1])
                if masked:
                    wt = jnp.where(col < row, wt, 0.0)
                out += [wt.astype(BF16), carry[2 * hh + 1] + jnp.sum(lks[hh], axis=1, keepdims=True)]
            for hh in hs:
                out[2 * hh] = carry[2 * hh] + _dot(out[2 * hh], v_ref[ks, _lanes(hh)])
            return tuple(out)
        za, zr = jnp.zeros((blk, LANES), F32), jnp.zeros((blk, 1), F32)
        carry = tile(i, (za, zr, za, zr), True)
        carry = lax.fori_loop(0, i, lambda j, c: tile(i - 1 - j, c, False), carry)
        for hh in range(2):
            o_ref[:, _lanes(hh)] = carry[2 * hh]
            lt_ref[hh] = carry[2 * hh + 1]

    full = lambda f: pl.BlockSpec((t, 2 * LANES), f)
    return pl.pallas_call(
        body, name=name, grid=(nh // 2, nq),
        in_specs=[pl.BlockSpec((blk, 2 * LANES), lambda h, i: (i, h)), full(lambda h, i: (0, nh // 2 + h)), full(lambda h, i: (0, nh + h))],
        out_specs=[pl.BlockSpec((blk, 2 * LANES), lambda h, i: (i, h)), pl.BlockSpec((2, blk, 1), lambda h, i: (h, i, 0))],
        out_shape=[jax.ShapeDtypeStruct((t, nh * LANES), F32), jax.ShapeDtypeStruct((nh, t, 1), F32)],
        compiler_params=_params(2))(qkv, qkv, qkv)


def _sb_bwd(qkv, do, ltot, name):
    t = qkv.shape[0]
    nh, blk = SB_HEADS, min(ATT_BLK, t)
    nq = t // blk
    scale = SB_HEAD_DIM ** -0.5

    def body(q_ref, k_ref, v_ref, do_ref, lt_ref, dq_ref, dk_ref, dv_ref):
        i = pl.program_id(1)

        @pl.when(i == 0)
        def _():
            dk_ref[...] = jnp.zeros_like(dk_ref)
            dv_ref[...] = jnp.zeros_like(dv_ref)
        row, col = _att_masks(blk)
        uinc = (row <= col).astype(BF16)
        uexc = (row < col).astype(BF16)
        qs = [q_ref[:, _lanes(hh)] for hh in range(2)]
        dobs = [do_ref[:, _lanes(hh)].astype(BF16) for hh in range(2)]
        lts = [lt_ref[hh] for hh in range(2)]

        def tile(kb, carry, masked):
            ks = pl.ds(pl.multiple_of(kb * blk, blk), blk)
            hs = range(2)
            kvs = [k_ref[ks, _lanes(hh)] for hh in hs]
            zs = [_dot(qs[hh], kvs[hh], "nt") for hh in hs]
            dws = [_dot(dobs[hh], v_ref[ks, _lanes(hh)], "nt") for hh in hs]
            lks, lss = [], []
            for hh in hs:
                z = zs[hh] * scale
                lk = -_softplus(z)
                lss.append(lk + z)
                lks.append(jnp.where(col < row, lk, 0.0) if masked else lk)
            pieces = [_split(lks[hh], 2) for hh in hs]
            css = [_dot(pieces[hh][0], uinc) + _dot(pieces[hh][1], uinc) for hh in hs]
            wts, evs = [], []
            for hh in hs:
                wt = jnp.exp(lss[hh] + (lts[hh] - (carry[3 * hh + 1] + css[hh])))
                if masked:
                    wt = jnp.where(col < row, wt, 0.0)
                wts.append(wt.astype(BF16))
                evs.append(dws[hh] * wt)
            epieces = [_split(evs[hh], 2) for hh in hs]
            ecss = [_dot(epieces[hh][0], uexc) + _dot(epieces[hh][1], uexc) for hh in hs]
            for hh in hs:
                dv_ref[ks, _lanes(hh)] += _dot(wts[hh], dobs[hh], "tn")
            dzbs = []
            for hh in hs:
                sig = jnp.exp(lss[hh])
                dz = evs[hh] * (1.0 - sig) - (ecss[hh] + carry[3 * hh + 2]) * sig
                if masked:
                    dz = jnp.where(col < row, dz, 0.0)
                dzbs.append((dz * scale).astype(BF16))
            out = []
            for hh in hs:
                dk_ref[ks, _lanes(hh)] += _dot(dzbs[hh], qs[hh], "tn")
                out += [carry[3 * hh] + _dot(dzbs[hh], kvs[hh]), carry[3 * hh + 1] + jnp.sum(lks[hh], axis=1, keepdims=True),
                        carry[3 * hh + 2] + jnp.sum(evs[hh], axis=1, keepdims=True)]
            return tuple(out)
        za, z1 = jnp.zeros((blk, LANES), F32), jnp.zeros((blk, 1), F32)
        carry = lax.fori_loop(0, i, lambda kb, c: tile(kb, c, False), (za, z1, z1, za, z1, z1))
        res = tile(i, carry, True)
        for hh in range(2):
            dq_ref[:, _lanes(hh)] = res[3 * hh]

    full = lambda f: pl.BlockSpec((t, 2 * LANES), f)
    qb = pl.BlockSpec((blk, 2 * LANES), lambda h, i: (i, h))
    wide = jax.ShapeDtypeStruct((t, nh * LANES), F32)
    return pl.pallas_call(
        body, name=name, grid=(nh // 2, nq),
        in_specs=[qb, full(lambda h, i: (0, nh // 2 + h)), full(lambda h, i: (0, nh + h)), qb,
                  pl.BlockSpec((2, blk, 1), lambda h, i: (h, i, 0))],
        out_specs=[qb, full(lambda h, i: (0, h)), full(lambda h, i: (0, h))],
        out_shape=[wide, wide, wide], compiler_params=_params(2))(qkv, qkv, qkv, do, ltot)


def _as2d(a):
    return a.reshape((-1, a.shape[-1]))


def _sum4(parts, name):
    _, r, c = parts.shape
    tr = _tile(r, (256, 128, 64, 32, 16, 8))

    def body(p_ref, o_ref):
        acc = p_ref[0].astype(F32)
        for j in range(1, 4):
            acc = acc + p_ref[j].astype(F32)
        o_ref[...] = acc

    return pl.pallas_call(body, name=name, grid=(r // tr,), in_specs=[pl.BlockSpec((4, tr, c), lambda i: (0, i, 0))],
                          out_specs=pl.BlockSpec((tr, c), lambda i: (i, 0)), out_shape=jax.ShapeDtypeStruct((r, c), F32),
                          compiler_params=_params(1))(parts)


def _adamw(w, ga, gb, m, v, name):
    r, c = w.shape
    tr = _tile(r, (128, 64, 32, 16, 8))
    c1, c2 = 1.0 - ADAM_B1 ** ADAM_STEP, 1.0 - ADAM_B2 ** ADAM_STEP
    two = gb is not None

    def body(*refs):
        w_ref, ga_ref = refs[0], refs[1]
        gb_ref = refs[2] if two else None
        m_ref, v_ref, g_out, d_out, m_out, v_out = refs[2 + two:]
        gv = ga_ref[...]
        if two:
            gv = gv + gb_ref[...]
        mn = ADAM_B1 * m_ref[...] + (1.0 - ADAM_B1) * gv
        vn = ADAM_B2 * v_ref[...] + (1.0 - ADAM_B2) * (gv * gv)
        g_out[...] = gv
        m_out[...] = mn
        v_out[...] = vn
        d_out[...] = -ADAM_LR * ((mn / c1) / (jnp.sqrt(vn / c2) + ADAM_EPS) + ADAM_WD * w_ref[...])

    blk = pl.BlockSpec((tr, c), lambda i: (i, 0))
    ins = [w, ga] + ([gb] if two else []) + [m, v]
    return pl.pallas_call(body, name=name, grid=(r // tr,), in_specs=[blk] * len(ins), out_specs=[blk] * 4,
                          out_shape=[jax.ShapeDtypeStruct((r, c), F32)] * 4, compiler_params=_params(1))(*ins)


HBM_SPEC = pl.BlockSpec(memory_space=pltpu.HBM)
SEM_SPEC = pl.BlockSpec(memory_space=pltpu.SEMAPHORE)
EFFECT = pltpu.SideEffectType.DATAFLOW_SIDE_EFFECTING


def _chip_copies(ins, lands, send_sems, recv_sems, whole):
    x, y, c = lax.axis_index("x"), lax.axis_index("y"), lax.axis_index("c")
    me = 2 * x + y
    out = []
    for wi in range(len(ins)):
        for k, (px, py) in enumerate([(1 - x, y), (x, 1 - y), (1 - x, 1 - y)]):
            sems = dict(send_sem=send_sems[wi * 3 + k], recv_sem=recv_sems[wi * 3 + k], device_id=(px, py, c), device_id_type=MESH_T)
            peer = 2 * px + py
            sent = pltpu.make_async_remote_copy(src_ref=ins[wi] if whole else ins[wi].at[peer], dst_ref=lands[wi].at[me], **sems)
            got = pltpu.make_async_remote_copy(src_ref=ins[wi] if whole else ins[wi].at[me], dst_ref=lands[wi].at[peer], **sems)
            out.append((sent, got))
    return out


def _xstart(arrs, whole, after, name):
    n, na = len(arrs), len(after)
    me = 2 * lax.axis_index("x") + lax.axis_index("y")
    lands = []
    for a in arrs:
        own = a[None] if whole else lax.dynamic_slice_in_dim(a, me, 1, axis=0)
        empty = lax.empty(((4,) + a.shape) if whole else a.shape, a.dtype)
        lands.append(lax.dynamic_update_slice_in_dim(empty, own, me, axis=0))

    def body(*refs):
        ins, lands_in = refs[:n], refs[n:2 * n]
        outs = refs[2 * n + na:]
        for sent, _ in _chip_copies(ins, lands_in, outs[:3 * n], outs[3 * n:6 * n], whole):
            sent.start()
        outs[8 * n][...] = jnp.zeros((8, LANES), F32)

    hbm = lambda a: pltpu.HBM(a.shape, a.dtype)
    res = pl.pallas_call(
        body, name=name,
        out_shape=[pltpu.SemaphoreType.DMA(())] * (6 * n) + [hbm(a) for a in arrs] + [hbm(a) for a in lands]
        + [jax.ShapeDtypeStruct((8, LANES), F32)],
        in_specs=[HBM_SPEC] * (2 * n) + [pl.BlockSpec(memory_space=pl.ANY)] * na,
        out_specs=[SEM_SPEC] * (6 * n) + [HBM_SPEC] * (2 * n) + [pl.BlockSpec(memory_space=pltpu.VMEM)],
        input_output_aliases={i: 6 * n + i for i in range(2 * n)},
        compiler_params=pltpu.CompilerParams(has_side_effects=EFFECT),
    )(*[pltpu.with_memory_space_constraint(a, pltpu.HBM) for a in list(arrs) + lands], *after)
    return res


def _xwait(handle, whole, after, name):
    n = (len(handle) - 1) // 8
    sems, thru = handle[:6 * n], handle[6 * n:8 * n]

    def body(*refs):
        ins, lands_in = refs[:n], refs[n:2 * n]
        for sent, got in _chip_copies(ins, lands_in, refs[2 * n:5 * n], refs[5 * n:8 * n], whole):
            sent.wait_send()
            got.wait_recv()

    hbm = lambda a: pltpu.HBM(a.shape, a.dtype)
    res = pl.pallas_call(
        body, name=name, out_shape=[hbm(a) for a in thru],
        in_specs=[HBM_SPEC] * (2 * n) + [SEM_SPEC] * (6 * n) + [pl.BlockSpec(memory_space=pl.ANY)],
        out_specs=[HBM_SPEC] * (2 * n), input_output_aliases={i: i for i in range(2 * n)},
        compiler_params=pltpu.CompilerParams(has_side_effects=EFFECT),
    )(*thru, *sems, after)
    return res[n:]


def _sibling_exchange(arrs, name):
    n = len(arrs)

    def body(*refs):
        ins, outs = refs[:n], refs[n:2 * n]
        send_sems, recv_sems = refs[2 * n:]
        sib = (lax.axis_index("x"), lax.axis_index("y"), 1 - lax.axis_index("c"))
        cps = [pltpu.make_async_remote_copy(src_ref=ins[wi], dst_ref=outs[wi], send_sem=send_sems.at[wi], recv_sem=recv_sems.at[wi],
                                            device_id=sib, device_id_type=MESH_T) for wi in range(n)]
        for cp in cps:
            cp.start()
        for cp in cps:
            cp.wait_recv()
        for cp in cps:
            cp.wait_send()

    anyspec = pl.BlockSpec(memory_space=pl.ANY)
    return pl.pallas_call(
        body, name=name, in_specs=[anyspec] * n, out_specs=[anyspec] * n,
        out_shape=[jax.ShapeDtypeStruct(a.shape, a.dtype) for a in arrs],
        scratch_shapes=[pltpu.SemaphoreType.DMA((n,)), pltpu.SemaphoreType.DMA((n,))],
        compiler_params=pltpu.CompilerParams(has_side_effects=True))(*arrs)


def _allreduce_small(packed, name):
    r = packed.shape[0]

    def body(in_ref, out_ref, land, send_sems, recv_sems):
        x, y, c = lax.axis_index("x"), lax.axis_index("y"), lax.axis_index("c")
        me = 4 * x + 2 * y + c
        land[me] = in_ref[...]
        rel = [(dx, dy, dc) for dx in (0, 1) for dy in (0, 1) for dc in (0, 1)][1:]
        peers = [((1 - x) if dx else x, (1 - y) if dy else y, (1 - c) if dc else c) for dx, dy, dc in rel]
        sends = []
        for k, peer in enumerate(peers):
            cp = pltpu.make_async_remote_copy(src_ref=in_ref, dst_ref=land.at[me], send_sem=send_sems.at[k], recv_sem=recv_sems.at[k],
                                              device_id=peer, device_id_type=MESH_T)
            cp.start()
            sends.append(cp)
        for k, (px, py, pc) in enumerate(peers):
            pltpu.make_async_remote_copy(src_ref=in_ref, dst_ref=land.at[4 * px + 2 * py + pc], send_sem=send_sems.at[k],
                                         recv_sem=recv_sems.at[k], device_id=(px, py, pc), device_id_type=MESH_T).wait_recv()
        for cp in sends:
            cp.wait_send()
        acc = land[0]
        for j in range(1, 8):
            acc = acc + land[j]
        out_ref[...] = acc

    vm = pl.BlockSpec(memory_space=pltpu.VMEM)
    return pl.pallas_call(
        body, name=name, in_specs=[vm], out_specs=vm, out_shape=jax.ShapeDtypeStruct((r, LANES), F32),
        scratch_shapes=[pltpu.VMEM((8, r, LANES), F32), pltpu.SemaphoreType.DMA((7,)), pltpu.SemaphoreType.DMA((7,))],
        compiler_params=pltpu.CompilerParams(has_side_effects=True))(packed)


def _in_splits():
    w = SSD_HEADS * SSD_HEAD_DIM
    cc = w + 2 * SSD_GROUPS * SSD_STATE
    return [w, cc, SSD_HEADS, MLA_Q_RANK, MLA_KV_RANK, MLA_ROPE]


def _padc(a, n):
    return jnp.pad(a, ((0, 0), (0, n - a.shape[1])))


def _win_pad(wm):
    offs = np.cumsum(_in_splits())[:-1]
    z, xbc, dt, cq, ckv, kr = jnp.split(wm, offs, axis=1)
    return jnp.concatenate([z, xbc, cq, ckv, _padc(kr, LANES), _padc(dt, LANES)], axis=1)


def _win_unpad(g):
    w, cc, nh, qr, kvr, rp = _in_splits()
    offs = np.cumsum([w, cc, qr, kvr, LANES])
    z, xbc, cq, ckv, kr, dt = jnp.split(g, offs, axis=1)
    return jnp.concatenate([z, xbc, dt[:, :nh], cq, ckv, kr[:, :rp]], axis=1)


def _wuq_pad(wm):
    r = wm.shape[0]
    w3 = wm.reshape(r, MLA_HEADS, MLA_NOPE + MLA_ROPE)
    return jnp.pad(w3, ((0, 0), (0, 0), (0, 2 * LANES - MLA_NOPE - MLA_ROPE))).reshape(r, MLA_HEADS * 2 * LANES)


def _wuq_unpad(g):
    r = g.shape[0]
    return g.reshape(r, MLA_HEADS, 2 * LANES)[:, :, :MLA_NOPE + MLA_ROPE].reshape(r, MLA_HEADS * (MLA_NOPE + MLA_ROPE))


def _full_from_gather(name, g):
    col = name in COL_SHARDED
    layers = g.shape[1]
    return [jnp.concatenate([g[j, l] for j in range(4)], axis=1 if col else 0) for l in range(layers)]


def _shards_from_full(name, mats):
    col = name in COL_SHARDED
    per = []
    for mt in mats:
        r, c = mt.shape
        per.append(mt.reshape(r, 4, c // 4).transpose(1, 0, 2) if col else mt.reshape(4, r // 4, c))
    return jnp.stack(per, axis=1).astype(BF16)


def _pack_small(vals):
    flat = jnp.concatenate([vals[n].reshape(-1).astype(F32) for n in SMALL])
    rows = -(-flat.shape[0] // (8 * LANES)) * 8
    return jnp.pad(flat, (0, rows * LANES - flat.shape[0])).reshape(rows, LANES)


def _unpack_small(packed, like):
    flat, out, off = packed.reshape(-1), {}, 0
    for n in SMALL:
        sz = like[n].size
        out[n] = flat[off:off + sz].reshape(like[n].shape)
        off += sz
    return out


def _ffn_fwd(x, norm_g, wg, wu, wd, tag):
    h, r = _rms_fwd(x, norm_g, f"ffn_norm_{tag}")
    gate = _mm(h, wg, "nn", f"ffn_gate_{tag}")
    up = _mm(h, wu, "nn", f"ffn_up_{tag}")
    act = _swiglu_fwd(gate, up, f"swiglu_{tag}")
    out = _mm(act, wd, "nn", f"ffn_down_{tag}", add=x)
    return out, (x, r, h, gate, up, act)


def _ffn_bwd(dout, doutb, norm_g, wg, wu, wd, saved, tag):
    x, r, h, gate, up, act = saved
    dact = _mm(doutb, wd.T, "nn", f"ffn_dact_{tag}")
    dgate, dup = _swiglu_bwd(gate, up, dact, f"swiglu_bwd_{tag}")
    ht = h.T
    d_wd = _mm(act.T, doutb, "nn", f"ffn_dwd_{tag}", out_dtype=BF16)
    d_wg = _mm(ht, dgate, "nn", f"ffn_dwg_{tag}", out_dtype=BF16)
    d_wu = _mm(ht, dup, "nn", f"ffn_dwu_{tag}", out_dtype=BF16)
    dh = _mm(dgate, wg.T, "nn", f"ffn_dh1_{tag}")
    dh = _mm(dup, wu.T, "nn", f"ffn_dh2_{tag}", add=dh)
    dx, dxb, dnorm = _rms_bwd(x, norm_g, r, dh, f"ffn_norm_bwd_{tag}", dres=dout)
    return dx, dxb, dnorm, d_wg, d_wu, d_wd


def kernel(x, mix_norm, ffn_norm, w_in, conv_w, conv_b, dt_bias, a_log, d_skip, ssd_norm, q_norm, kv_norm, w_uq, w_ukv, w_out_even, w_qkv, w_out_odd, w_gate, w_up, w_down, final_norm, loss_target, m_mix_norm, m_ffn_norm, m_w_in, m_conv_w, m_conv_b, m_dt_bias, m_a_log, m_d_skip, m_ssd_norm, m_q_norm, m_kv_norm, m_w_uq, m_w_ukv, m_w_out_even, m_w_qkv, m_w_out_odd, m_w_gate, m_w_up, m_w_down, m_final_norm, v_mix_norm, v_ffn_norm, v_w_in, v_conv_w, v_conv_b, v_dt_bias, v_a_log, v_d_skip, v_ssd_norm, v_q_norm, v_kv_norm, v_w_uq, v_w_ukv, v_w_out_even, v_w_qkv, v_w_out_odd, v_w_gate, v_w_up, v_w_down, v_final_norm):
    given = dict(locals())
    wts = {n: given[n] for n in WEIGHTS}
    x0 = x[0]
    tgt = loss_target[0]
    t, d = x0.shape
    hw = SSD_HEADS * SSD_HEAD_DIM
    gn = SSD_GROUPS * SSD_STATE
    cc = hw + 2 * gn
    qr, kvr = MLA_Q_RANK, MLA_KV_RANK

    def shard(n, layer=None):
        a = wts[n] if layer is None else wts[n][layer:layer + 1]
        return a if n == 'conv_w' else a.astype(BF16)
    g0_names = ['w_in', 'conv_w', 'w_uq', 'w_ukv']
    g1_names = [('w_out_even', None), ('w_gate', 0), ('w_up', 0), ('w_down', 0)]
    g2_names = [('w_qkv', None), ('w_out_odd', None), ('w_gate', 1), ('w_up', 1), ('w_down', 1)]
    hg0 = _xstart([shard(n) for n in g0_names], True, [], "gather0_start")
    hg1 = _xstart([shard(n, l) for n, l in g1_names], True, [hg0[-1]], "gather1_start")
    hg2 = _xstart([shard(n, l) for n, l in g2_names], True, [hg1[-1]], "gather2_start")
    full = lambda n, g: _full_from_gather(n, g)[0]
    g0 = dict(zip(g0_names, _xwait(hg0, True, hg2[-1], "gather0_wait")))
    win = _win_pad(full('w_in', g0['w_in']))
    wuq = _wuq_pad(full('w_uq', g0['w_uq']))
    wukv = full('w_ukv', g0['w_ukv'])
    cw = _padc(full('conv_w', g0['conv_w']).T, 8).T
    o_cq, o_ckv, o_kr, o_dt = hw + cc, hw + cc + qr, hw + cc + qr + kvr, hw + cc + qr + kvr + LANES

    row = lambda v: v.reshape(1, -1)
    narrow = lambda v: _padc(v.reshape(1, -1), LANES)
    dtb, alog = narrow(dt_bias[0]), narrow(a_log[0])
    dsk_x = jnp.repeat(d_skip[0], SSD_HEAD_DIM).reshape(1, hw)
    cos, sin = _rope_tables(t)

    h0, r0 = _rms_fwd(x0, row(mix_norm[0]), "mix_norm_0")
    u = _mm(h0, win, "nn", "in_proj")
    z, xbc, c_q, c_kv = u[:, :hw], u[:, hw:hw + cc], u[:, o_cq:o_ckv], u[:, o_ckv:o_kr]
    kr_raw, dtraw = u[:, o_kr:o_dt], u[:, o_dt:]
    xc = _conv_fwd(xbc, cw, row(conv_b[0]), "conv")
    y_ssd, sprev = _ssd_fwd(dtraw, xc, dtb, alog, dsk_x, "ssd")
    yg, r_g = _gated_fwd(y_ssd, z, row(ssd_norm[0]), "ssd_gate_norm")
    q_lat, r_q = _rms_fwd(c_q, row(q_norm[0]), "q_norm")
    kv_lat, r_kv = _rms_fwd(c_kv, row(kv_norm[0]), "kv_norm")
    qf = _mm(q_lat, wuq, "nn", "q_up")
    kvb = _mm(kv_lat, wukv, "nn", "kv_up", out_dtype=BF16)
    q_r = _rope(qf, cos, sin, 2, False, BF16, "rope_q")
    k_r = _rope(kr_raw, cos, sin, 1, False, BF16, "rope_k")
    o_mla, lse = _mla_fwd(q_r, k_r, kvb, "mla")
    g1 = dict(zip(g1_names, _xwait(hg1, True, lse, "gather1_wait")))
    woe, wg0, wu0, wd0 = [full(k[0], g1[k]) for k in g1_names]
    cat = jnp.concatenate([yg, o_mla.astype(BF16)], axis=1)
    x1 = _mm(cat, woe, "nn", "mix_out_0", add=x0)
    x2, ffn0 = _ffn_fwd(x1, row(ffn_norm[0]), wg0, wu0, wd0, "0")

    g2 = dict(zip(g2_names, _xwait(hg2, True, x2, "gather2_wait")))
    wqkv, woo, wg1, wu1, wd1 = [full(k[0], g2[k]) for k in g2_names]
    h1, r1 = _rms_fwd(x2, row(mix_norm[1]), "mix_norm_1")
    qkv = _mm(h1, wqkv, "nn", "qkv_proj", out_dtype=BF16)
    o_sb, ltot = _sb_fwd(qkv, "sb")
    o_sbb = o_sb.astype(BF16)
    x3 = _mm(o_sbb, woo, "nn", "mix_out_1", add=x2)
    x4, ffn1 = _ffn_fwd(x3, row(ffn_norm[1]), wg1, wu1, wd1, "1")

    loss_part, dx4, dx4b, d_final = _loss_fwd_bwd(x4, row(final_norm), tgt, "loss")
    loss = lax.psum(loss_part, ("x", "y", "c"))

    def grad_shards(pairs):
        return [_shards_from_full(n, [g]) for n, g in pairs]

    dx3, dx3b, d_ffn1, d_wg1, d_wu1, d_wd1 = _ffn_bwd(dx4, dx4b, row(ffn_norm[1]), wg1, wu1, wd1, ffn1, "1")
    do_sb = _mm(dx3b, woo.T, "nn", "sb_dout")
    d_woo = _mm(o_sbb.T, dx3b, "nn", "d_w_out_odd", out_dtype=BF16)
    dq, dk, dv = _sb_bwd(qkv, do_sb, ltot, "sb_bwd")
    dqkv = jnp.concatenate([dq, dk, dv], axis=1).astype(BF16)
    d_wqkv = _mm(h1.T, dqkv, "nn", "d_w_qkv", out_dtype=BF16)
    x2_names = ['w_qkv', 'w_out_odd', 'w_gate', 'w_up', 'w_down']
    hx2 = _xstart(grad_shards(zip(x2_names, [d_wqkv, d_woo, d_wg1, d_wu1, d_wd1])), False, [], "grads2_start")
    dh1 = _mm(dqkv, wqkv.T, "nn", "qkv_dh")
    dx2, dx2b, d_mix1 = _rms_bwd(x2, row(mix_norm[1]) + hx2[-1][0:1, 0:1], r1, dh1, "mix_norm_bwd_1", dres=dx3)

    dx1, dx1b, d_ffn0, d_wg0, d_wu0, d_wd0 = _ffn_bwd(dx2, dx2b, row(ffn_norm[0]), wg0, wu0, wd0, ffn0, "0")
    d_woe = _mm(cat.T, dx1b, "nn", "d_w_out_even", out_dtype=BF16)
    x1_names = ['w_out_even', 'w_gate', 'w_up', 'w_down']
    hx1 = _xstart(grad_shards(zip(x1_names, [d_woe, d_wg0, d_wu0, d_wd0])), False, [], "grads1_start")
    tok1 = hx1[-1][0:1, 0:1]
    dcat = _mm(dx1b, woe.T, "nn", "mix_dcat")
    dy_ssd, dz, d_ssdn = _gated_bwd(y_ssd, z, row(ssd_norm[0]) + tok1, r_g, dcat[:, :hw], "ssd_gate_norm_bwd")
    dxs, db_, dc_, ddtraw, d_dtb, d_alog, d_dsk = _ssd_bwd(dtraw, xc, dtb, alog, dsk_x, sprev, dy_ssd, "ssd_bwd")
    dxbc, d_cw, d_cb = _conv_bwd(xbc, cw, row(conv_b[0]), jnp.concatenate([dxs, db_, dc_], axis=1), "conv_bwd")
    dqm, dkvm, dkr = _mla_bwd(q_r, k_r, kvb, o_mla, dcat[:, hw:], lse + tok1, "mla_bwd")
    dqf = _rope(dqm, cos, sin, 2, True, BF16, "rope_q_bwd")
    dkr_raw = _rope(dkr, cos, sin, 1, True, F32, "rope_k_bwd")
    dkvf = dkvm.astype(BF16)
    d_wuq = _mm(q_lat.T, dqf, "nn", "d_w_uq", out_dtype=BF16)
    d_wukv = _mm(kv_lat.T, dkvf, "nn", "d_w_ukv", out_dtype=BF16)
    dq_lat = _mm(dqf, wuq.T, "nn", "q_up_bwd")
    dkv_lat = _mm(dkvf, wukv.T, "nn", "kv_up_bwd")
    dc_q, _, d_qn = _rms_bwd(c_q, row(q_norm[0]), r_q, dq_lat, "q_norm_bwd")
    dc_kv, _, d_kvn = _rms_bwd(c_kv, row(kv_norm[0]), r_kv, dkv_lat, "kv_norm_bwd")
    du = jnp.concatenate([dz, dxbc, dc_q, dc_kv, dkr_raw, ddtraw], axis=1).astype(BF16)
    d_win = _mm(h0.T, du, "nn", "d_w_in", out_dtype=BF16)
    x0_names = ['w_in', 'conv_w', 'w_uq', 'w_ukv']
    hx0 = _xstart(grad_shards(zip(x0_names, [_win_unpad(d_win), d_cw, _wuq_unpad(d_wuq), d_wukv])), False, [], "grads0_start")
    dh0 = _mm(du, win.T, "nn", "in_proj_bwd")
    grad_x, _, d_mix0 = _rms_bwd(x0, row(mix_norm[0]) + hx0[-1][0:1, 0:1], r0, dh0, "mix_norm_bwd_0", dres=dx1)

    def chip_sums(handle, names, after, tag):
        lands = _xwait(handle, False, after, f"grads{tag}_wait")
        return {n: _sum4(p.reshape(4, -1, p.shape[-1]), f"sum{tag}_{n}") for n, p in zip(names, lands)}
    s2 = chip_sums(hx2, x2_names, grad_x, "2")
    s1 = chip_sums(hx1, x1_names, s2['w_qkv'], "1")
    s0 = chip_sums(hx0, x0_names, s1['w_out_even'], "0")
    by_name = {**s0, 'w_out_even': s1['w_out_even'], 'w_qkv': s2['w_qkv'], 'w_out_odd': s2['w_out_odd']}
    for n in ('w_gate', 'w_up', 'w_down'):
        by_name[n] = jnp.concatenate([s1[n], s2[n]], axis=0)
    psum = [by_name[n] for n in SHARDED]
    sib = _sibling_exchange(psum, "exchange_sibling")

    nhs = SSD_HEADS
    small_g = {'mix_norm': jnp.stack([d_mix0, d_mix1]), 'ffn_norm': jnp.stack([d_ffn0, d_ffn1]), 'conv_b': d_cb[None],
               'dt_bias': d_dtb[None, :nhs], 'a_log': d_alog[None, :nhs], 'd_skip': d_dsk[None, :nhs], 'ssd_norm': d_ssdn[None],
               'q_norm': d_qn[None], 'kv_norm': d_kvn[None], 'final_norm': d_final}
    g_small = _allreduce_small(_pack_small(small_g), "allreduce_small")
    pk = lambda pre: _pack_small({n: given[pre + n] for n in SMALL})
    sg, sd, sm, sv = _adamw(pk(''), g_small, None, pk('m_'), pk('v_'), "adamw_small")

    grads, deltas, new_m, new_v = {}, {}, {}, {}
    for n, ga, gb in zip(SHARDED, psum, sib):
        shp = wts[n].shape
        res = _adamw(_as2d(wts[n]), ga, gb, _as2d(given['m_' + n]), _as2d(given['v_' + n]), f"adamw_{n}")
        grads[n], deltas[n], new_m[n], new_v[n] = [r.reshape(shp) for r in res]
    for dst, src in ((grads, sg), (deltas, sd), (new_m, sm), (new_v, sv)):
        dst.update(_unpack_small(src, wts))

    outs = [loss, grad_x[None]]
    for dct in (grads, deltas, new_m, new_v):
        outs += [dct[n] for n in WEIGHTS]
    return tuple(outs)
```

```python
import functools
import math

import jax
import jax.numpy as jnp
import numpy as np
from jax import lax
from jax.experimental import pallas as pl
from jax.experimental.pallas import tpu as pltpu

F32, BF16 = jnp.float32, jnp.bfloat16

RMS_EPS = 1e-6
SSD_HEADS, SSD_HEAD_DIM, SSD_GROUPS, SSD_STATE, SSD_CONV, SSD_CHUNK = 32, 64, 4, 128, 4, 128
MLA_HEADS, MLA_Q_RANK, MLA_KV_RANK, MLA_NOPE, MLA_ROPE, MLA_V = 16, 512, 512, 128, 64, 128
ROPE_THETA = 10000.0
SB_HEADS, SB_HEAD_DIM = 16, 128
ADAM_LR, ADAM_B1, ADAM_B2, ADAM_EPS, ADAM_WD, ADAM_STEP = 0.001, 0.9, 0.999, 1e-08, 0.01, 10

LANES = 128
VMEM_LIMIT_BYTES = 56 * 1024 * 1024
MM_VMEM_BUDGET = 40 * 1024 * 1024
ATT_BLK = 256
ROW_TILE = 256
NEG = -1e30

MESH_T = pl.DeviceIdType.MESH
WEIGHTS = ['mix_norm', 'ffn_norm', 'w_in', 'conv_w', 'conv_b', 'dt_bias', 'a_log', 'd_skip', 'ssd_norm', 'q_norm',
           'kv_norm', 'w_uq', 'w_ukv', 'w_out_even', 'w_qkv', 'w_out_odd', 'w_gate', 'w_up', 'w_down', 'final_norm']
SHARDED = ['w_in', 'conv_w', 'w_uq', 'w_ukv', 'w_out_even', 'w_qkv', 'w_out_odd', 'w_gate', 'w_up', 'w_down']
COL_SHARDED = ['w_in', 'conv_w', 'w_uq', 'w_ukv', 'w_qkv', 'w_gate', 'w_up']
SMALL = [n for n in WEIGHTS if n not in SHARDED]


def _tile(n, cands):
    for c in cands:
        if n % c == 0:
            return c
    return n


def _params(ngrid):
    return pltpu.CompilerParams(dimension_semantics=("arbitrary",) * ngrid, vmem_limit_bytes=VMEM_LIMIT_BYTES)


def _dot(a, b, mode="nn"):
    dims = {"nn": (((1,), (0,)), ((), ())), "nt": (((1,), (1,)), ((), ())), "tn": (((0,), (0,)), ((), ()))}[mode]
    return lax.dot_general(a, b, dims, preferred_element_type=F32)


def _split(x, parts):
    out, r = [], x
    for _ in range(parts):
        p = r.astype(BF16)
        out.append(p)
        r = r - p.astype(F32)
    return out


def _xdot(x, e, parts=3):
    acc = None
    for p in _split(x, parts):
        t = _dot(p, e)
        acc = t if acc is None else acc + t
    return acc


def _xdot_l(e, x, parts=3):
    acc = None
    for p in _split(x, parts):
        t = _dot(e, p)
        acc = t if acc is None else acc + t
    return acc


def _iota(shape, dim):
    return lax.broadcasted_iota(jnp.int32, shape, dim)


def _softplus(z):
    return jnp.maximum(z, 0.0) + jnp.log(1.0 + jnp.exp(-jnp.abs(z)))


def _sigmoid(z):
    return 1.0 / (1.0 + jnp.exp(-z))


def _acc_rows(ref, first, val):
    @pl.when(first)
    def _():
        ref[...] = jnp.zeros_like(ref)
    ref[...] += jnp.broadcast_to(val, ref.shape)


def _mm(a, b, mode, name, out_dtype=F32, add=None):
    if mode == "nn":
        (m, k), n = a.shape, b.shape[1]
    elif mode == "nt":
        (m, k), n = a.shape, b.shape[0]
    else:
        (k, m), n = a.shape, b.shape[1]
    tn = _tile(n, (1536, 1408, 1280, 1024, 768, 640, 512, 256, 128))
    tk = k if k <= 2048 else _tile(k, (2048, 1536, 1408, 1280, 1024, 512, 256, 128))
    ob = jnp.dtype(out_dtype).itemsize

    def need(tm_):
        per = tm_ * tk * a.dtype.itemsize + tk * tn * b.dtype.itemsize + tm_ * tn * ob + (tm_ * tn * 4 if add is not None else 0)
        return 2 * per + (tm_ * tn * 4 if k > tk else 0)
    tm = m
    for cand in (1024, 512, 256, 128):
        if m % cand == 0:
            tm = cand
            if need(cand) <= MM_VMEM_BUDGET:
                break
    nk = k // tk
    a_spec = {"nn": pl.BlockSpec((tm, tk), lambda i, j, kk: (i, kk)), "nt": pl.BlockSpec((tm, tk), lambda i, j, kk: (i, kk)),
              "tn": pl.BlockSpec((tk, tm), lambda i, j, kk: (kk, i))}[mode]
    b_spec = {"nn": pl.BlockSpec((tk, tn), lambda i, j, kk: (kk, j)), "nt": pl.BlockSpec((tn, tk), lambda i, j, kk: (j, kk)),
              "tn": pl.BlockSpec((tk, tn), lambda i, j, kk: (kk, j))}[mode]
    o_spec = pl.BlockSpec((tm, tn), lambda i, j, kk: (i, j))
    has_add = add is not None

    def body(*refs):
        a_ref, b_ref = refs[0], refs[1]
        add_ref = refs[2] if has_add else None
        o_ref = refs[2 + has_add]
        part = _dot(a_ref[...].astype(BF16), b_ref[...].astype(BF16), mode)
        if nk == 1:
            if has_add:
                part = part + add_ref[...]
            o_ref[...] = part.astype(o_ref.dtype)
            return
        acc_ref = refs[3 + has_add]
        kk = pl.program_id(2)

        @pl.when(kk == 0)
        def _():
            acc_ref[...] = jnp.zeros_like(acc_ref)
        acc_ref[...] += part

        @pl.when(kk == nk - 1)
        def _():
            r = acc_ref[...]
            if has_add:
                r = r + add_ref[...]
            o_ref[...] = r.astype(o_ref.dtype)

    ins = [a, b] + ([add] if has_add else [])
    specs = [a_spec, b_spec] + ([o_spec] if has_add else [])
    return pl.pallas_call(
        body, name=name, grid=(m // tm, n // tn, nk), in_specs=specs, out_specs=o_spec,
        out_shape=jax.ShapeDtypeStruct((m, n), out_dtype),
        scratch_shapes=[pltpu.VMEM((tm, tn), F32)] if nk > 1 else [],
        compiler_params=_params(3))(*ins)


def _rows(t):
    return _tile(t, (ROW_TILE, 128, 64, 32, 16, 8))


def _rms_fwd(x, g, name):
    t, c = x.shape
    tr = _rows(t)

    def body(x_ref, g_ref, h_ref, r_ref):
        xv = x_ref[...]
        r = lax.rsqrt(jnp.mean(xv * xv, axis=-1, keepdims=True) + RMS_EPS)
        h_ref[...] = (xv * r * g_ref[...]).astype(h_ref.dtype)
        r_ref[...] = r

    return pl.pallas_call(
        body, name=name, grid=(t // tr,),
        in_specs=[pl.BlockSpec((tr, c), lambda i: (i, 0)), pl.BlockSpec((1, c), lambda i: (0, 0))],
        out_specs=[pl.BlockSpec((tr, c), lambda i: (i, 0)), pl.BlockSpec((tr, 1), lambda i: (i, 0))],
        out_shape=[jax.ShapeDtypeStruct((t, c), BF16), jax.ShapeDtypeStruct((t, 1), F32)],
        compiler_params=_params(1))(x, g)


def _rms_bwd(x, g, r, dh, name, dres=None):
    t, c = x.shape
    tr = _rows(t)
    has_res = dres is not None

    def body(*refs):
        x_ref, g_ref, r_ref, dh_ref = refs[:4]
        res_ref = refs[4] if has_res else None
        dx_ref, dxb_ref, dg_ref = refs[4 + has_res:]
        rv = r_ref[...]
        xh = x_ref[...] * rv
        dhv = dh_ref[...]
        dxh = dhv * g_ref[...]
        cm = jnp.mean(dxh * xh, axis=-1, keepdims=True)
        dx = (dxh - xh * cm) * rv
        if has_res:
            dx = dx + res_ref[...]
        dx_ref[...] = dx
        dxb_ref[...] = dx.astype(BF16)
        _acc_rows(dg_ref, pl.program_id(0) == 0, jnp.sum(dhv * xh, axis=0, keepdims=True))

    row = pl.BlockSpec((tr, c), lambda i: (i, 0))
    ins = [x, g, r, dh] + ([dres] if has_res else [])
    specs = [row, pl.BlockSpec((1, c), lambda i: (0, 0)), pl.BlockSpec((tr, 1), lambda i: (i, 0)), row] + ([row] if has_res else [])
    dx, dxb, dg = pl.pallas_call(
        body, name=name, grid=(t // tr,), in_specs=specs,
        out_specs=[row, row, pl.BlockSpec((8, c), lambda i: (0, 0))],
        out_shape=[jax.ShapeDtypeStruct((t, c), F32), jax.ShapeDtypeStruct((t, c), BF16), jax.ShapeDtypeStruct((8, c), F32)],
        compiler_params=_params(1))(*ins)
    return dx, dxb, dg[0]


def _gated_fwd(y, z, g, name):
    t, c = y.shape
    tr = _rows(t)

    def body(y_ref, z_ref, g_ref, o_ref, r_ref):
        zv = z_ref[...]
        v = y_ref[...] * zv * _sigmoid(zv)
        r = lax.rsqrt(jnp.mean(v * v, axis=-1, keepdims=True) + RMS_EPS)
        o_ref[...] = (v * r * g_ref[...]).astype(o_ref.dtype)
        r_ref[...] = r

    row = pl.BlockSpec((tr, c), lambda i: (i, 0))
    return pl.pallas_call(
        body, name=name, grid=(t // tr,), in_specs=[row, row, pl.BlockSpec((1, c), lambda i: (0, 0))],
        out_specs=[row, pl.BlockSpec((tr, 1), lambda i: (i, 0))],
        out_shape=[jax.ShapeDtypeStruct((t, c), BF16), jax.ShapeDtypeStruct((t, 1), F32)],
        compiler_params=_params(1))(y, z, g)


def _gated_bwd(y, z, g, r, dout, name):
    t, c = y.shape
    tr = _rows(t)

    def body(y_ref, z_ref, g_ref, r_ref, do_ref, dy_ref, dz_ref, dg_ref):
        yv, zv, rv, dov = y_ref[...], z_ref[...], r_ref[...], do_ref[...]
        s = _sigmoid(zv)
        sz = zv * s
        xh = yv * sz * rv
        dxh = dov * g_ref[...]
        cm = jnp.mean(dxh * xh, axis=-1, keepdims=True)
        dv = (dxh - xh * cm) * rv
        dy_ref[...] = dv * sz
        dz_ref[...] = dv * yv * s * (1.0 + zv * (1.0 - s))
        _acc_rows(dg_ref, pl.program_id(0) == 0, jnp.sum(dov * xh, axis=0, keepdims=True))

    row = pl.BlockSpec((tr, c), lambda i: (i, 0))
    dy, dz, dg = pl.pallas_call(
        body, name=name, grid=(t // tr,),
        in_specs=[row, row, pl.BlockSpec((1, c), lambda i: (0, 0)), pl.BlockSpec((tr, 1), lambda i: (i, 0)), row],
        out_specs=[row, row, pl.BlockSpec((8, c), lambda i: (0, 0))],
        out_shape=[jax.ShapeDtypeStruct((t, c), F32), jax.ShapeDtypeStruct((t, c), F32), jax.ShapeDtypeStruct((8, c), F32)],
        compiler_params=_params(1))(y, z, g, r, dout)
    return dy, dz, dg[0]


def _swiglu_fwd(gate, up, name):
    t, c = gate.shape
    tr, tc = _rows(t), _tile(c, (2816, 1408, 1024, 512, 256, 128))

    def body(g_ref, u_ref, o_ref):
        gv = g_ref[...]
        o_ref[...] = (gv * _sigmoid(gv) * u_ref[...]).astype(o_ref.dtype)

    blk = pl.BlockSpec((tr, tc), lambda i, j: (i, j))
    return pl.pallas_call(body, name=name, grid=(t // tr, c // tc), in_specs=[blk, blk], out_specs=blk,
                          out_shape=jax.ShapeDtypeStruct((t, c), BF16), compiler_params=_params(2))(gate, up)


def _swiglu_bwd(gate, up, dact, name):
    t, c = gate.shape
    tr, tc = _rows(t), _tile(c, (2816, 1408, 1024, 512, 256, 128))

    def body(g_ref, u_ref, d_ref, dg_ref, du_ref):
        gv, dv = g_ref[...], d_ref[...]
        s = _sigmoid(gv)
        dg_ref[...] = (dv * u_ref[...] * s * (1.0 + gv * (1.0 - s))).astype(dg_ref.dtype)
        du_ref[...] = (dv * gv * s).astype(du_ref.dtype)

    blk = pl.BlockSpec((tr, tc), lambda i, j: (i, j))
    return pl.pallas_call(body, name=name, grid=(t // tr, c // tc), in_specs=[blk, blk, blk], out_specs=[blk, blk],
                          out_shape=[jax.ShapeDtypeStruct((t, c), BF16)] * 2, compiler_params=_params(2))(gate, up, dact)


def _loss_fwd_bwd(x, g, tgt, name):
    t, c = x.shape
    tr = _rows(t)

    def body(x_ref, g_ref, t_ref, l_ref, dx_ref, dxb_ref, dg_ref):
        xv, gv = x_ref[...], g_ref[...]
        r = lax.rsqrt(jnp.mean(xv * xv, axis=-1, keepdims=True) + RMS_EPS)
        xh = xv * r
        err = xh * gv - t_ref[...]
        per_row = jnp.mean(err * err, axis=-1, keepdims=True)
        dy = err * (1.0 / c)
        dxh = dy * gv
        cm = jnp.mean(dxh * xh, axis=-1, keepdims=True)
        dx = (dxh - xh * cm) * r
        dx_ref[...] = dx
        dxb_ref[...] = dx.astype(BF16)
        first = pl.program_id(0) == 0
        _acc_rows(dg_ref, first, jnp.sum(dy * xh, axis=0, keepdims=True))
        _acc_rows(l_ref, first, jnp.broadcast_to(0.5 * jnp.sum(per_row, axis=0, keepdims=True), (1, LANES)))

    row = pl.BlockSpec((tr, c), lambda i: (i, 0))
    lo, dx, dxb, dg = pl.pallas_call(
        body, name=name, grid=(t // tr,), in_specs=[row, pl.BlockSpec((1, c), lambda i: (0, 0)), row],
        out_specs=[pl.BlockSpec((8, LANES), lambda i: (0, 0)), row, row, pl.BlockSpec((8, c), lambda i: (0, 0))],
        out_shape=[jax.ShapeDtypeStruct((8, LANES), F32), jax.ShapeDtypeStruct((t, c), F32), jax.ShapeDtypeStruct((t, c), BF16),
                   jax.ShapeDtypeStruct((8, c), F32)],
        compiler_params=_params(1))(x, g, tgt)
    return lo[0, 0], dx, dxb, dg[0]


def _conv_specs(t, c):
    tr = _rows(t)
    tc = _tile(c, (1024, 768, 512, 256, 128))
    h8 = tr // 8
    tile = pl.BlockSpec((tr, tc), lambda j, i: (i, j))
    prev = pl.BlockSpec((8, tc), lambda j, i: (jnp.maximum(i * h8 - 1, 0), j))
    nxt = pl.BlockSpec((8, tc), lambda j, i: (jnp.minimum((i + 1) * h8, t // 8 - 1), j))
    return tr, tc, tile, prev, nxt


def _conv_fwd(xbc, w, b, name):
    t, c = xbc.shape
    tr, tc, tile, prev, _ = _conv_specs(t, c)

    def body(x_ref, p_ref, w_ref, b_ref, o_ref, buf):
        i = pl.program_id(1)
        buf[0:8, :] = jnp.where(i > 0, p_ref[...], 0.0)
        buf[8:, :] = x_ref[...]
        pre = b_ref[...]
        for k in range(SSD_CONV):
            pre = pre + w_ref[k:k + 1, :] * buf[pl.ds(8 - (SSD_CONV - 1) + k, tr), :]
        o_ref[...] = pre * _sigmoid(pre)

    return pl.pallas_call(
        body, name=name, grid=(c // tc, t // tr),
        in_specs=[tile, prev, pl.BlockSpec((8, tc), lambda j, i: (0, j)), pl.BlockSpec((1, tc), lambda j, i: (0, j))],
        out_specs=tile, out_shape=jax.ShapeDtypeStruct((t, c), F32),
        scratch_shapes=[pltpu.VMEM((tr + 8, tc), F32)], compiler_params=_params(2))(xbc, xbc, w, b)


def _conv_bwd(xbc, w, b, dout, name):
    t, c = xbc.shape
    tr, tc, tile, prev, nxt = _conv_specs(t, c)
    nt = t // tr
    kc = SSD_CONV

    def body(x_ref, p_ref, n_ref, w_ref, b_ref, d_ref, dn_ref, dx_ref, dw_ref, db_ref, buf, dbuf):
        i = pl.program_id(1)
        last = i == nt - 1
        buf[0:8, :] = jnp.where(i > 0, p_ref[...], 0.0)
        buf[8:8 + tr, :] = x_ref[...]
        buf[8 + tr:, :] = jnp.where(last, 0.0, n_ref[...])
        pre = b_ref[...]
        for k in range(kc):
            pre = pre + w_ref[k:k + 1, :] * buf[pl.ds(8 - (kc - 1) + k, tr + 8), :]
        s = _sigmoid(pre)
        dsilu = s * (1.0 + pre * (1.0 - s))
        dbuf[0:tr, :] = d_ref[...] * dsilu[0:tr, :]
        dbuf[tr:, :] = jnp.where(last, 0.0, dn_ref[...]) * dsilu[tr:, :]
        dpre = dbuf[0:tr, :]
        dx = jnp.zeros((tr, tc), F32)
        first = i == 0
        for k in range(kc):
            dx = dx + w_ref[k:k + 1, :] * dbuf[pl.ds(kc - 1 - k, tr), :]
        dx_ref[...] = dx

        @pl.when(first)
        def _():
            dw_ref[...] = jnp.zeros_like(dw_ref)
        for k in range(kc):
            dw_ref[k:k + 1, :] += jnp.sum(dpre * buf[pl.ds(8 - (kc - 1) + k, tr), :], axis=0, keepdims=True)
        _acc_rows(db_ref, first, jnp.sum(dpre, axis=0, keepdims=True))

    par = pl.BlockSpec((8, tc), lambda j, i: (0, j))
    dx, dw, db = pl.pallas_call(
        body, name=name, grid=(c // tc, nt),
        in_specs=[tile, prev, nxt, par, pl.BlockSpec((1, tc), lambda j, i: (0, j)), tile, nxt],
        out_specs=[tile, par, par],
        out_shape=[jax.ShapeDtypeStruct((t, c), F32), jax.ShapeDtypeStruct((8, c), F32), jax.ShapeDtypeStruct((8, c), F32)],
        scratch_shapes=[pltpu.VMEM((tr + 16, tc), F32), pltpu.VMEM((tr + 8, tc), F32)],
        compiler_params=_params(2))(xbc, xbc, xbc, w, b, dout, dout)
    return dx, dw[:kc], db[0]


def _ssd_consts():
    h, p, ln = SSD_HEADS, SSD_HEAD_DIM, SSD_CHUNK
    w = h * p
    hrow, jcol = _iota((LANES, w), 0), _iota((LANES, w), 1)
    e = ((jcol >= hrow * p) & (jcol < (hrow + 1) * p)).astype(BF16)
    jrow, hcol = _iota((w, LANES), 0), _iota((w, LANES), 1)
    et = ((jrow >= hcol * p) & (jrow < (hcol + 1) * p)).astype(BF16)
    row, col = _iota((ln, ln), 0), _iota((ln, ln), 1)
    return e, et, row, col


def _ssd_common(dtraw_ref, dtb_ref, alog_ref, e):
    ln = SSD_CHUNK
    raw = dtraw_ref[...] + dtb_ref[...]
    dt = _softplus(raw)
    a = -jnp.exp(alog_ref[...])
    adt = dt * a
    row, col = _iota((ln, ln), 0), _iota((ln, ln), 1)
    cs = _xdot_l((col <= row).astype(BF16), adt)
    cl = jnp.sum(adt, axis=0, keepdims=True)
    ex = _xdot(jnp.concatenate([dt, cs, jnp.broadcast_to(cl, (ln, LANES))], axis=0), e)
    return raw, dt, a, cs, ex[:ln], ex[ln:2 * ln], ex[2 * ln:]


def _head_decay(cs, h, causal):
    ln = SSD_CHUNK
    lane = _iota((1, LANES), 1)
    colv = jnp.sum(jnp.where(lane == h, cs, 0.0), axis=1, keepdims=True)
    cb = jnp.broadcast_to(colv, (ln, ln))
    return jnp.exp(jnp.where(causal, cb - cb.T, NEG))


def _ssd_fwd(dtraw, xc, dtb, alog, dskip_x, name):
    t = dtraw.shape[0]
    h, p, g, n, ln = SSD_HEADS, SSD_HEAD_DIM, SSD_GROUPS, SSD_STATE, SSD_CHUNK
    w, gn, gw, hpg = h * p, g * n, (h // g) * p, h // g
    assert n == ln and gw % LANES == 0 and w % gn == 0
    nc = t // ln

    def body(dtraw_ref, x_ref, b_ref, c_ref, dtb_ref, alog_ref, dsk_ref, y_ref, sp_ref, s_ref):
        @pl.when(pl.program_id(0) == 0)
        def _():
            s_ref[...] = jnp.zeros_like(s_ref)
        e, _, row, col = _ssd_consts()
        causal = col <= row
        _, _, _, cs, dt_x, cs_x, cl_x = _ssd_common(dtraw_ref, dtb_ref, alog_ref, e)
        xv = x_ref[...]
        xd = xv * dt_x
        xdb = xd.astype(BF16)
        sv = s_ref[...]
        sp_ref[0] = sv
        el_x = jnp.exp(cs_x)
        zb = (xd * jnp.exp(cl_x - cs_x)).astype(BF16)
        cd_x = jnp.exp(cl_x)
        dsk = dsk_ref[...]
        half = _iota((1, LANES), 1) >= p
        for gi in range(g):
            gs = slice(gi * gw, (gi + 1) * gw)
            bg = b_ref[:, gi * n:(gi + 1) * n].astype(BF16)
            cg = c_ref[:, gi * n:(gi + 1) * n].astype(BF16)
            gm = _dot(cg, bg, "nt")
            sg = sv[:, gs]
            yoff = _dot(cg, sg.astype(BF16)) * el_x[:, gs]
            s_ref[:, gs] = sg * cd_x[:, gs] + _dot(bg, zb[:, gs], "tn")
            for pp in range(gw // LANES):
                ls = slice(gi * gw + pp * LANES, gi * gw + (pp + 1) * LANES)
                xp = xdb[:, ls]
                yp = yoff[:, pp * LANES:(pp + 1) * LANES] + dsk[:, ls] * xv[:, ls]
                for hh in range(LANES // p):
                    hd = gi * hpg + pp * (LANES // p) + hh
                    wm = (gm * _head_decay(cs, hd, causal)).astype(BF16)
                    yp = yp + _dot(wm, jnp.where(half == (hh == 1), xp, jnp.zeros_like(xp)))
                y_ref[:, ls] = yp

    nar = pl.BlockSpec((ln, LANES), lambda c: (c, 0))
    one = pl.BlockSpec((1, LANES), lambda c: (0, 0))
    return pl.pallas_call(
        body, name=name, grid=(nc,),
        in_specs=[nar, pl.BlockSpec((ln, w), lambda c: (c, 0)), pl.BlockSpec((ln, gn), lambda c: (c, w // gn)),
                  pl.BlockSpec((ln, gn), lambda c: (c, w // gn + 1)), one, one, pl.BlockSpec((1, w), lambda c: (0, 0))],
        out_specs=[pl.BlockSpec((ln, w), lambda c: (c, 0)), pl.BlockSpec((1, n, w), lambda c: (c, 0, 0))],
        out_shape=[jax.ShapeDtypeStruct((t, w), F32), jax.ShapeDtypeStruct((nc, n, w), F32)],
        scratch_shapes=[pltpu.VMEM((n, w), F32)], compiler_params=_params(1))(dtraw, xc, xc, xc, dtb, alog, dskip_x)


def _ssd_bwd(dtraw, xc, dtb, alog, dskip_x, sprev, dy, name):
    t = dtraw.shape[0]
    h, p, g, n, ln = SSD_HEADS, SSD_HEAD_DIM, SSD_GROUPS, SSD_STATE, SSD_CHUNK
    w, gn, gw, hpg = h * p, g * n, (h // g) * p, h // g
    nc = t // ln

    def body(dtraw_ref, x_ref, b_ref, c_ref, dtb_ref, alog_ref, dsk_ref, sp_ref, dy_ref,
             dx_ref, db_ref, dc_ref, ddt_ref, dbias_ref, dalog_ref, ddsk_ref, ds_ref, dxd_ref, qcs_ref):
        first = pl.program_id(0) == 0

        @pl.when(first)
        def _():
            ds_ref[...] = jnp.zeros_like(ds_ref)
        e, et, row, col = _ssd_consts()
        causal = col <= row
        raw, dt, a, cs, dt_x, cs_x, cl_x = _ssd_common(dtraw_ref, dtb_ref, alog_ref, e)
        xv = x_ref[...]
        xd = xv * dt_x
        xdb = xd.astype(BF16)
        sv = sp_ref[0]
        dyv = dy_ref[...]
        dyb = dyv.astype(BF16)
        dsn = ds_ref[...]
        el_x = jnp.exp(cs_x)
        dte_x = jnp.exp(cl_x - cs_x)
        cd_x = jnp.exp(cl_x)
        zf = xd * dte_x
        lane = _iota((1, LANES), 1)
        half = lane >= p
        lastrow = _iota((ln, 1), 0) == ln - 1
        dcs = jnp.zeros((ln, LANES), F32)
        for gi in range(g):
            gs = slice(gi * gw, (gi + 1) * gw)
            ns = slice(gi * n, (gi + 1) * n)
            bg = b_ref[:, ns].astype(BF16)
            cg = c_ref[:, ns].astype(BF16)
            gm = _dot(cg, bg, "nt")
            sgb = sv[:, gs].astype(BF16)
            dsg = dsn[:, gs]
            dsgb = dsg.astype(BF16)
            yoff = _dot(cg, sgb) * el_x[:, gs]
            drb = (el_x[:, gs] * dyv[:, gs]).astype(BF16)
            dcg = _dot(drb, sgb, "nt")
            ds_ref[:, gs] = cd_x[:, gs] * dsg + _dot(cg, drb, "tn")
            dz = _dot(bg, dsgb)
            zg = zf[:, gs]
            dbg = _dot(zg.astype(BF16), dsgb, "nt")
            dzz = dz * zg
            qcl = jnp.sum(dzz + cd_x[:, gs] * dsg * sv[:, gs], axis=0, keepdims=True)
            qcs_ref[:, gs] = dyv[:, gs] * yoff - dzz + jnp.where(lastrow, jnp.broadcast_to(qcl, (ln, gw)), 0.0)
            dgm = jnp.zeros((ln, ln), F32)
            for pp in range(gw // LANES):
                ls = slice(gi * gw + pp * LANES, gi * gw + (pp + 1) * LANES)
                xp = xdb[:, ls]
                dxp = dz[:, pp * LANES:(pp + 1) * LANES] * dte_x[:, ls]
                for hh in range(LANES // p):
                    hd = gi * hpg + pp * (LANES // p) + hh
                    dm = _head_decay(cs, hd, causal)
                    wf = gm * dm
                    dym = jnp.where(half == (hh == 1), dyb[:, ls], jnp.zeros_like(xp))
                    dw = _dot(dym, xp, "nt")
                    dxp = dxp + _dot(wf.astype(BF16), dym, "tn")
                    dgm = dgm + dw * dm
                    mm = dw * wf
                    rc = jnp.sum(mm, axis=1, keepdims=True) - jnp.sum(mm.T, axis=1, keepdims=True)
                    dcs = dcs + rc * (lane == hd).astype(F32)
                dxd_ref[:, ls] = dxp
            dgb = dgm.astype(BF16)
            dc_ref[:, ns] = dcg + _dot(dgb, bg)
            db_ref[:, ns] = dbg + _dot(dgb, cg, "tn")
        dxd = dxd_ref[...]
        dx_ref[...] = dxd * dt_x + dsk_ref[...] * dyv
        red = _xdot(jnp.concatenate([qcs_ref[...], dxd * xv, dyv * xv], axis=0), et)
        dcs = dcs + red[:ln]
        dadt = _xdot_l((row <= col).astype(BF16), dcs)
        ddt = red[ln:2 * ln] + dadt * a
        draw = ddt * _sigmoid(raw)
        ddt_ref[...] = draw
        _acc_rows(dbias_ref, first, jnp.sum(draw, axis=0, keepdims=True))
        _acc_rows(dalog_ref, first, jnp.sum(dadt * dt, axis=0, keepdims=True) * a)
        _acc_rows(ddsk_ref, first, jnp.sum(red[2 * ln:], axis=0, keepdims=True))

    rev = lambda c: nc - 1 - c
    nar = pl.BlockSpec((ln, LANES), lambda c: (rev(c), 0))
    one = pl.BlockSpec((1, LANES), lambda c: (0, 0))
    wide = pl.BlockSpec((ln, w), lambda c: (rev(c), 0))
    bcs = pl.BlockSpec((ln, gn), lambda c: (rev(c), 0))
    acc = pl.BlockSpec((8, LANES), lambda c: (0, 0))
    outs = pl.pallas_call(
        body, name=name, grid=(nc,),
        in_specs=[nar, wide, pl.BlockSpec((ln, gn), lambda c: (rev(c), w // gn)), pl.BlockSpec((ln, gn), lambda c: (rev(c), w // gn + 1)),
                  one, one, pl.BlockSpec((1, w), lambda c: (0, 0)), pl.BlockSpec((1, n, w), lambda c: (rev(c), 0, 0)), wide],
        out_specs=[wide, bcs, bcs, nar, acc, acc, acc],
        out_shape=[jax.ShapeDtypeStruct((t, w), F32), jax.ShapeDtypeStruct((t, gn), F32), jax.ShapeDtypeStruct((t, gn), F32),
                   jax.ShapeDtypeStruct((t, LANES), F32)] + [jax.ShapeDtypeStruct((8, LANES), F32)] * 3,
        scratch_shapes=[pltpu.VMEM((n, w), F32), pltpu.VMEM((ln, w), F32), pltpu.VMEM((ln, w), F32)],
        compiler_params=_params(1))(dtraw, xc, xc, xc, dtb, alog, dskip_x, sprev, dy)
    dx, db, dc, ddt, dbias, dalog, ddsk = outs
    return dx, db, dc, ddt, dbias[0], dalog[0], ddsk[0]


def _rope_tables(t):
    half = MLA_ROPE // 2
    inv_freq = ROPE_THETA ** (-jnp.arange(half, dtype=F32) / half)
    ang = jnp.arange(t, dtype=F32)[:, None] * inv_freq[None, :]
    cos, sin = jnp.cos(ang), jnp.sin(ang)
    pad = LANES - MLA_ROPE
    cos_r = jnp.concatenate([cos, cos, jnp.ones((t, pad), F32)], axis=1)
    sin_r = jnp.concatenate([sin, sin, jnp.zeros((t, pad), F32)], axis=1)
    return cos_r, sin_r


def _rot_matrix():
    half = MLA_ROPE // 2
    i, j = _iota((LANES, LANES), 0), _iota((LANES, LANES), 1)
    neg = (j < half) & (i == j + half)
    pos = (j >= half) & (j < 2 * half) & (i == j - half)
    return (pos.astype(F32) - neg.astype(F32)).astype(BF16)


def _rope(x, cos, sin, every, transpose, out_dtype, name):
    t, w = x.shape
    tr = _rows(t)

    def body(x_ref, c_ref, s_ref, o_ref):
        rot = _rot_matrix()
        cv, sv = c_ref[...], s_ref[...]
        for j in range(w // LANES):
            ls = slice(j * LANES, (j + 1) * LANES)
            xv = x_ref[:, ls]
            if j % every != every - 1:
                o_ref[:, ls] = xv.astype(o_ref.dtype)
            elif transpose:
                o_ref[:, ls] = (xv * cv - _xdot(xv * sv, rot, 2)).astype(o_ref.dtype)
            else:
                o_ref[:, ls] = (xv * cv + _xdot(xv, rot, 2) * sv).astype(o_ref.dtype)

    wide = pl.BlockSpec((tr, w), lambda i: (i, 0))
    tab = pl.BlockSpec((tr, LANES), lambda i: (i, 0))
    return pl.pallas_call(
        body, name=name, grid=(t // tr,), in_specs=[wide, tab, tab], out_specs=wide,
        out_shape=jax.ShapeDtypeStruct((t, w), out_dtype), compiler_params=_params(1))(x, cos, sin)


def _att_masks(blk):
    return _iota((blk, blk), 0), _iota((blk, blk), 1)


def _lanes(j):
    return slice(j * LANES, (j + 1) * LANES)


def _mla_fwd(q, kr, kv, name):
    t = q.shape[0]
    nh, blk = MLA_HEADS, min(ATT_BLK, t)
    scale = (MLA_NOPE + MLA_ROPE) ** -0.5

    def body(q_ref, kv_ref, kr_ref, o_ref, lse_ref):
        i = pl.program_id(1)
        row, col = _att_masks(blk)
        qs = [(q_ref[:, _lanes(2 * hh)], q_ref[:, _lanes(2 * hh + 1)]) for hh in range(2)]

        def scores(kb, hh):
            ks = pl.ds(pl.multiple_of(kb * blk, blk), blk)
            s = (_dot(qs[hh][0], kv_ref[ks, _lanes(2 * hh)], "nt") + _dot(qs[hh][1], kr_ref[ks, :], "nt")) * scale
            return s, kv_ref[ks, _lanes(2 * hh + 1)]
        init = []
        for hh in range(2):
            s, v = scores(i, hh)
            s = jnp.where(col <= row, s, NEG)
            m = jnp.max(s, axis=1, keepdims=True)
            pr = jnp.exp(s - m)
            init += [m, jnp.sum(pr, axis=1, keepdims=True), _dot(pr.astype(BF16), v)]

        def step(kb, carry):
            sv = [scores(kb, hh) for hh in range(2)]
            out, prs = [], []
            for hh in range(2):
                m, l, acc = carry[3 * hh:3 * hh + 3]
                s = sv[hh][0]
                m2 = jnp.maximum(m, jnp.max(s, axis=1, keepdims=True))
                al = jnp.exp(m - m2)
                pr = jnp.exp(s - m2)
                prs.append(pr.astype(BF16))
                out += [m2, al * l + jnp.sum(pr, axis=1, keepdims=True), al * acc]
            for hh in range(2):
                out[3 * hh + 2] = out[3 * hh + 2] + _dot(prs[hh], sv[hh][1])
            return tuple(out)
        res = lax.fori_loop(0, i, step, tuple(init))
        for hh in range(2):
            m, l, acc = res[3 * hh:3 * hh + 3]
            o_ref[:, _lanes(hh)] = acc / l
            lse_ref[hh] = m + jnp.log(l)

    return pl.pallas_call(
        body, name=name, grid=(nh // 2, t // blk),
        in_specs=[pl.BlockSpec((blk, 4 * LANES), lambda h, i: (i, h)), pl.BlockSpec((t, 4 * LANES), lambda h, i: (0, h)),
                  pl.BlockSpec((t, LANES), lambda h, i: (0, 0))],
        out_specs=[pl.BlockSpec((blk, 2 * LANES), lambda h, i: (i, h)), pl.BlockSpec((2, blk, 1), lambda h, i: (h, i, 0))],
        out_shape=[jax.ShapeDtypeStruct((t, nh * LANES), F32), jax.ShapeDtypeStruct((nh, t, 1), F32)],
        compiler_params=_params(2))(q, kv, kr)


def _mla_bwd(q, kr, kv, o, do, lse, name):
    t = q.shape[0]
    nh, blk = MLA_HEADS, min(ATT_BLK, t)
    scale = (MLA_NOPE + MLA_ROPE) ** -0.5

    def body(q_ref, kv_ref, kr_ref, o_ref, do_ref, lse_ref, dq_ref, dkv_ref, dkr_ref):
        hp, i = pl.program_id(0), pl.program_id(1)

        @pl.when(i == 0)
        def _():
            dkv_ref[...] = jnp.zeros_like(dkv_ref)

        @pl.when((i == 0) & (hp == 0))
        def _():
            dkr_ref[...] = jnp.zeros_like(dkr_ref)
        row, col = _att_masks(blk)
        qs = [(q_ref[:, _lanes(2 * hh)], q_ref[:, _lanes(2 * hh + 1)]) for hh in range(2)]
        dobs = [do_ref[:, _lanes(hh)].astype(BF16) for hh in range(2)]
        deltas = [jnp.sum(do_ref[:, _lanes(hh)] * o_ref[:, _lanes(hh)], axis=1, keepdims=True) for hh in range(2)]
        lses = [lse_ref[hh] for hh in range(2)]

        def tile(kb, carry, masked):
            ks = pl.ds(pl.multiple_of(kb * blk, blk), blk)
            krv = kr_ref[ks, :]
            hs = range(2)
            kns = [kv_ref[ks, _lanes(2 * hh)] for hh in hs]
            ss = [_dot(qs[hh][0], kns[hh], "nt") + _dot(qs[hh][1], krv, "nt") for hh in hs]
            dps = [_dot(dobs[hh], kv_ref[ks, _lanes(2 * hh + 1)], "nt") for hh in hs]
            prb, dsb = [], []
            for hh in hs:
                pr = jnp.exp(ss[hh] * scale - lses[hh])
                if masked:
                    pr = jnp.where(col <= row, pr, 0.0)
                prb.append(pr.astype(BF16))
                dsb.append((pr * (dps[hh] - deltas[hh]) * scale).astype(BF16))
            out = []
            for hh in hs:
                dkv_ref[ks, _lanes(2 * hh + 1)] += _dot(prb[hh], dobs[hh], "tn")
                dkv_ref[ks, _lanes(2 * hh)] += _dot(dsb[hh], qs[hh][0], "tn")
                out += [carry[2 * hh] + _dot(dsb[hh], kns[hh]), carry[2 * hh + 1] + _dot(dsb[hh], krv)]
            dkr_ref[ks, :] += _dot(dsb[0], qs[0][1], "tn") + _dot(dsb[1], qs[1][1], "tn")
            return tuple(out)
        zero = jnp.zeros((blk, LANES), F32)
        carry = lax.fori_loop(0, i, lambda kb, c: tile(kb, c, False), (zero,) * 4)
        res = tile(i, carry, True)
        for j in range(4):
            dq_ref[:, _lanes(j)] = res[j]

    qb = lambda w: pl.BlockSpec((blk, w * LANES), lambda h, i: (i, h))
    wide = jax.ShapeDtypeStruct((t, nh * 2 * LANES), F32)
    return pl.pallas_call(
        body, name=name, grid=(nh // 2, t // blk),
        in_specs=[qb(4), pl.BlockSpec((t, 4 * LANES), lambda h, i: (0, h)), pl.BlockSpec((t, LANES), lambda h, i: (0, 0)),
                  qb(2), qb(2), pl.BlockSpec((2, blk, 1), lambda h, i: (h, i, 0))],
        out_specs=[qb(4), pl.BlockSpec((t, 4 * LANES), lambda h, i: (0, h)), pl.BlockSpec((t, LANES), lambda h, i: (0, 0))],
        out_shape=[wide, wide, jax.ShapeDtypeStruct((t, LANES), F32)],
        compiler_params=_params(2))(q, kv, kr, o, do, lse)


def _sb_fwd(qkv, name):
    t = qkv.shape[0]
    nh, blk = SB_HEADS, min(ATT_BLK, t)
    nq = t // blk
    scale = SB_HEAD_DIM ** -0.5

    def body(q_ref, k_ref, v_ref, o_ref, lt_ref):
        i = pl.program_id(1)
        row, col = _att_masks(blk)
        usuf = (row > col).astype(BF16)
        qs = [q_ref[:, _lanes(hh)] for hh in range(2)]

        def tile(kb, carry, masked):
            ks = pl.ds(pl.multiple_of(kb * blk, blk), blk)
            hs = range(2)
            zs = [_dot(qs[hh], k_ref[ks, _lanes(hh)], "nt") for hh in hs]
            lks, lss = [], []
            for hh in hs:
                z = zs[hh] * scale
                lk = -_softplus(z)
                lss.append(lk + z)
                lks.append(jnp.where(col < row, lk, 0.0) if masked else lk)
            pieces = [_split(lks[hh], 2) for hh in hs]
            later = [_dot(pieces[hh][0], usuf) + _dot(pieces[hh][1], usuf) for hh in hs]
            out = []
            for hh in hs:
                wt = jnp.exp(lss[hh] + later[hh] + carry[2 * hh + 1])
                if masked:
                    wt = jnp.where(col < row, wt, 0.0)
                out += [wt.astype(BF16), carry[2 * hh + 1] + jnp.sum(lks[hh], axis=1, keepdims=True)]
            for hh in hs:
                out[2 * hh] = carry[2 * hh] + _dot(out[2 * hh], v_ref[ks, _lanes(hh)])
            return tuple(out)
        za, zr = jnp.zeros((blk, LANES), F32), jnp.zeros((blk, 1), F32)
        carry = tile(i, (za, zr, za, zr), True)
        carry = lax.fori_loop(0, i, lambda j, c: tile(i - 1 - j, c, False), carry)
        for hh in range(2):
            o_ref[:, _lanes(hh)] = carry[2 * hh]
            lt_ref[hh] = carry[2 * hh + 1]

    full = lambda f: pl.BlockSpec((t, 2 * LANES), f)
    return pl.pallas_call(
        body, name=name, grid=(nh // 2, nq),
        in_specs=[pl.BlockSpec((blk, 2 * LANES), lambda h, i: (i, h)), full(lambda h, i: (0, nh // 2 + h)), full(lambda h, i: (0, nh + h))],
        out_specs=[pl.BlockSpec((blk, 2 * LANES), lambda h, i: (i, h)), pl.BlockSpec((2, blk, 1), lambda h, i: (h, i, 0))],
        out_shape=[jax.ShapeDtypeStruct((t, nh * LANES), F32), jax.ShapeDtypeStruct((nh, t, 1), F32)],
        compiler_params=_params(2))(qkv, qkv, qkv)


def _sb_bwd(qkv, do, ltot, name):
    t = qkv.shape[0]
    nh, blk = SB_HEADS, min(ATT_BLK, t)
    nq = t // blk
    scale = SB_HEAD_DIM ** -0.5

    def body(q_ref, k_ref, v_ref, do_ref, lt_ref, dq_ref, dk_ref, dv_ref):
        i = pl.program_id(1)

        @pl.when(i == 0)
        def _():
            dk_ref[...] = jnp.zeros_like(dk_ref)
            dv_ref[...] = jnp.zeros_like(dv_ref)
        row, col = _att_masks(blk)
        uinc = (row <= col).astype(BF16)
        uexc = (row < col).astype(BF16)
        qs = [q_ref[:, _lanes(hh)] for hh in range(2)]
        dobs = [do_ref[:, _lanes(hh)].astype(BF16) for hh in range(2)]
        lts = [lt_ref[hh] for hh in range(2)]

        def tile(kb, carry, masked):
            ks = pl.ds(pl.multiple_of(kb * blk, blk), blk)
            hs = range(2)
            kvs = [k_ref[ks, _lanes(hh)] for hh in hs]
            zs = [_dot(qs[hh], kvs[hh], "nt") for hh in hs]
            dws = [_dot(dobs[hh], v_ref[ks, _lanes(hh)], "nt") for hh in hs]
            lks, lss = [], []
            for hh in hs:
                z = zs[hh] * scale
                lk = -_softplus(z)
                lss.append(lk + z)
                lks.append(jnp.where(col < row, lk, 0.0) if masked else lk)
            pieces = [_split(lks[hh], 2) for hh in hs]
            css = [_dot(pieces[hh][0], uinc) + _dot(pieces[hh][1], uinc) for hh in hs]
            wts, evs = [], []
            for hh in hs:
                wt = jnp.exp(lss[hh] + (lts[hh] - (carry[3 * hh + 1] + css[hh])))
                if masked:
                    wt = jnp.where(col < row, wt, 0.0)
                wts.append(wt.astype(BF16))
                evs.append(dws[hh] * wt)
            epieces = [_split(evs[hh], 2) for hh in hs]
            ecss = [_dot(epieces[hh][0], uexc) + _dot(epieces[hh][1], uexc) for hh in hs]
            for hh in hs:
                dv_ref[ks, _lanes(hh)] += _dot(wts[hh], dobs[hh], "tn")
            dzbs = []
            for hh in hs:
                sig = jnp.exp(lss[hh])
                dz = evs[hh] * (1.0 - sig) - (ecss[hh] + carry[3 * hh + 2]) * sig
                if masked:
                    dz = jnp.where(col < row, dz, 0.0)
                dzbs.append((dz * scale).astype(BF16))
            out = []
            for hh in hs:
                dk_ref[ks, _lanes(hh)] += _dot(dzbs[hh], qs[hh], "tn")
                out += [carry[3 * hh] + _dot(dzbs[hh], kvs[hh]), carry[3 * hh + 1] + jnp.sum(lks[hh], axis=1, keepdims=True),
                        carry[3 * hh + 2] + jnp.sum(evs[hh], axis=1, keepdims=True)]
            return tuple(out)
        za, z1 = jnp.zeros((blk, LANES), F32), jnp.zeros((blk, 1), F32)
        carry = lax.fori_loop(0, i, lambda kb, c: tile(kb, c, False), (za, z1, z1, za, z1, z1))
        res = tile(i, carry, True)
        for hh in range(2):
            dq_ref[:, _lanes(hh)] = res[3 * hh]

    full = lambda f: pl.BlockSpec((t, 2 * LANES), f)
    qb = pl.BlockSpec((blk, 2 * LANES), lambda h, i: (i, h))
    wide = jax.ShapeDtypeStruct((t, nh * LANES), F32)
    return pl.pallas_call(
        body, name=name, grid=(nh // 2, nq),
        in_specs=[qb, full(lambda h, i: (0, nh // 2 + h)), full(lambda h, i: (0, nh + h)), qb,
                  pl.BlockSpec((2, blk, 1), lambda h, i: (h, i, 0))],
        out_specs=[qb, full(lambda h, i: (0, h)), full(lambda h, i: (0, h))],
        out_shape=[wide, wide, wide], compiler_params=_params(2))(qkv, qkv, qkv, do, ltot)


def _as2d(a):
    return a.reshape((-1, a.shape[-1]))


def _sum4(parts, name):
    _, r, c = parts.shape
    tr = _tile(r, (256, 128, 64, 32, 16, 8))

    def body(p_ref, o_ref):
        acc = p_ref[0].astype(F32)
        for j in range(1, 4):
            acc = acc + p_ref[j].astype(F32)
        o_ref[...] = acc

    return pl.pallas_call(body, name=name, grid=(r // tr,), in_specs=[pl.BlockSpec((4, tr, c), lambda i: (0, i, 0))],
                          out_specs=pl.BlockSpec((tr, c), lambda i: (i, 0)), out_shape=jax.ShapeDtypeStruct((r, c), F32),
                          compiler_params=_params(1))(parts)


def _adamw(w, ga, gb, m, v, name):
    r, c = w.shape
    tr = _tile(r, (128, 64, 32, 16, 8))
    c1, c2 = 1.0 - ADAM_B1 ** ADAM_STEP, 1.0 - ADAM_B2 ** ADAM_STEP
    two = gb is not None

    def body(*refs):
        w_ref, ga_ref = refs[0], refs[1]
        gb_ref = refs[2] if two else None
        m_ref, v_ref, g_out, d_out, m_out, v_out = refs[2 + two:]
        gv = ga_ref[...]
        if two:
            gv = gv + gb_ref[...]
        mn = ADAM_B1 * m_ref[...] + (1.0 - ADAM_B1) * gv
        vn = ADAM_B2 * v_ref[...] + (1.0 - ADAM_B2) * (gv * gv)
        g_out[...] = gv
        m_out[...] = mn
        v_out[...] = vn
        d_out[...] = -ADAM_LR * ((mn / c1) / (jnp.sqrt(vn / c2) + ADAM_EPS) + ADAM_WD * w_ref[...])

    blk = pl.BlockSpec((tr, c), lambda i: (i, 0))
    ins = [w, ga] + ([gb] if two else []) + [m, v]
    return pl.pallas_call(body, name=name, grid=(r // tr,), in_specs=[blk] * len(ins), out_specs=[blk] * 4,
                          out_shape=[jax.ShapeDtypeStruct((r, c), F32)] * 4, compiler_params=_params(1))(*ins)


HBM_SPEC = pl.BlockSpec(memory_space=pltpu.HBM)
SEM_SPEC = pl.BlockSpec(memory_space=pltpu.SEMAPHORE)
EFFECT = pltpu.SideEffectType.DATAFLOW_SIDE_EFFECTING


def _chip_copies(ins, lands, send_sems, recv_sems, whole):
    x, y, c = lax.axis_index("x"), lax.axis_index("y"), lax.axis_index("c")
    me = 2 * x + y
    out = []
    for wi in range(len(ins)):
        for k, (px, py) in enumerate([(1 - x, y), (x, 1 - y), (1 - x, 1 - y)]):
            sems = dict(send_sem=send_sems[wi * 3 + k], recv_sem=recv_sems[wi * 3 + k], device_id=(px, py, c), device_id_type=MESH_T)
            peer = 2 * px + py
            sent = pltpu.make_async_remote_copy(src_ref=ins[wi] if whole else ins[wi].at[peer], dst_ref=lands[wi].at[me], **sems)
            got = pltpu.make_async_remote_copy(src_ref=ins[wi] if whole else ins[wi].at[me], dst_ref=lands[wi].at[peer], **sems)
            out.append((sent, got))
    return out


def _xstart(arrs, whole, after, name):
    n, na = len(arrs), len(after)
    me = 2 * lax.axis_index("x") + lax.axis_index("y")
    lands = []
    for a in arrs:
        own = a[None] if whole else lax.dynamic_slice_in_dim(a, me, 1, axis=0)
        empty = lax.empty(((4,) + a.shape) if whole else a.shape, a.dtype)
        lands.append(lax.dynamic_update_slice_in_dim(empty, own, me, axis=0))

    def body(*refs):
        ins, lands_in = refs[:n], refs[n:2 * n]
        outs = refs[2 * n + na:]
        for sent, _ in _chip_copies(ins, lands_in, outs[:3 * n], outs[3 * n:6 * n], whole):
            sent.start()
        outs[8 * n][...] = jnp.zeros((8, LANES), F32)

    hbm = lambda a: pltpu.HBM(a.shape, a.dtype)
    res = pl.pallas_call(
        body, name=name,
        out_shape=[pltpu.SemaphoreType.DMA(())] * (6 * n) + [hbm(a) for a in arrs] + [hbm(a) for a in lands]
        + [jax.ShapeDtypeStruct((8, LANES), F32)],
        in_specs=[HBM_SPEC] * (2 * n) + [pl.BlockSpec(memory_space=pl.ANY)] * na,
        out_specs=[SEM_SPEC] * (6 * n) + [HBM_SPEC] * (2 * n) + [pl.BlockSpec(memory_space=pltpu.VMEM)],
        input_output_aliases={i: 6 * n + i for i in range(2 * n)},
        compiler_params=pltpu.CompilerParams(has_side_effects=EFFECT),
    )(*[pltpu.with_memory_space_constraint(a, pltpu.HBM) for a in list(arrs) + lands], *after)
    return res


def _xwait(handle, whole, after, name):
    n = (len(handle) - 1) // 8
    sems, thru = handle[:6 * n], handle[6 * n:8 * n]

    def body(*refs):
        ins, lands_in = refs[:n], refs[n:2 * n]
        for sent, got in _chip_copies(ins, lands_in, refs[2 * n:5 * n], refs[5 * n:8 * n], whole):
            sent.wait_send()
            got.wait_recv()

    hbm = lambda a: pltpu.HBM(a.shape, a.dtype)
    res = pl.pallas_call(
        body, name=name, out_shape=[hbm(a) for a in thru],
        in_specs=[HBM_SPEC] * (2 * n) + [SEM_SPEC] * (6 * n) + [pl.BlockSpec(memory_space=pl.ANY)],
        out_specs=[HBM_SPEC] * (2 * n), input_output_aliases={i: i for i in range(2 * n)},
        compiler_params=pltpu.CompilerParams(has_side_effects=EFFECT),
    )(*thru, *sems, after)
    return res[n:]


def _sibling_copies(ins, lands, send_sems, recv_sems):
    sib = (lax.axis_index("x"), lax.axis_index("y"), 1 - lax.axis_index("c"))
    return [pltpu.make_async_remote_copy(src_ref=ins[wi], dst_ref=lands[wi], send_sem=send_sems[wi], recv_sem=recv_sems[wi],
                                         device_id=sib, device_id_type=MESH_T) for wi in range(len(ins))]


def _sib_start(arrs, after, name):
    n, na = len(arrs), len(after)
    lands = [lax.empty(a.shape, a.dtype) for a in arrs]

    def body(*refs):
        outs = refs[2 * n + na:]
        for cp in _sibling_copies(refs[:n], refs[n:2 * n], outs[:n], outs[n:2 * n]):
            cp.start()
        outs[4 * n][...] = jnp.zeros((8, LANES), F32)

    hbm = lambda a: pltpu.HBM(a.shape, a.dtype)
    return pl.pallas_call(
        body, name=name,
        out_shape=[pltpu.SemaphoreType.DMA(())] * (2 * n) + [hbm(a) for a in arrs] * 2 + [jax.ShapeDtypeStruct((8, LANES), F32)],
        in_specs=[HBM_SPEC] * (2 * n) + [pl.BlockSpec(memory_space=pl.ANY)] * na,
        out_specs=[SEM_SPEC] * (2 * n) + [HBM_SPEC] * (2 * n) + [pl.BlockSpec(memory_space=pltpu.VMEM)],
        input_output_aliases={i: 2 * n + i for i in range(2 * n)},
        compiler_params=pltpu.CompilerParams(has_side_effects=EFFECT),
    )(*[pltpu.with_memory_space_constraint(a, pltpu.HBM) for a in list(arrs) + lands], *after)


def _sib_wait(handle, after, name):
    n = (len(handle) - 1) // 4
    sems, thru = handle[:2 * n], handle[2 * n:4 * n]

    def body(*refs):
        for cp in _sibling_copies(refs[:n], refs[n:2 * n], refs[2 * n:3 * n], refs[3 * n:4 * n]):
            cp.wait_send()
            cp.wait_recv()

    res = pl.pallas_call(
        body, name=name, out_shape=[pltpu.HBM(a.shape, a.dtype) for a in thru],
        in_specs=[HBM_SPEC] * (2 * n) + [SEM_SPEC] * (2 * n) + [pl.BlockSpec(memory_space=pl.ANY)],
        out_specs=[HBM_SPEC] * (2 * n), input_output_aliases={i: i for i in range(2 * n)},
        compiler_params=pltpu.CompilerParams(has_side_effects=EFFECT),
    )(*thru, *sems, after)
    return res[n:]


def _sibling_exchange(arrs, name):
    n = len(arrs)

    def body(*refs):
        ins, outs = refs[:n], refs[n:2 * n]
        send_sems, recv_sems = refs[2 * n:]
        sib = (lax.axis_index("x"), lax.axis_index("y"), 1 - lax.axis_index("c"))
        cps = [pltpu.make_async_remote_copy(src_ref=ins[wi], dst_ref=outs[wi], send_sem=send_sems.at[wi], recv_sem=recv_sems.at[wi],
                                            device_id=sib, device_id_type=MESH_T) for wi in range(n)]
        for cp in cps:
            cp.start()
        for cp in cps:
            cp.wait_recv()
        for cp in cps:
            cp.wait_send()

    anyspec = pl.BlockSpec(memory_space=pl.ANY)
    return pl.pallas_call(
        body, name=name, in_specs=[anyspec] * n, out_specs=[anyspec] * n,
        out_shape=[jax.ShapeDtypeStruct(a.shape, a.dtype) for a in arrs],
        scratch_shapes=[pltpu.SemaphoreType.DMA((n,)), pltpu.SemaphoreType.DMA((n,))],
        compiler_params=pltpu.CompilerParams(has_side_effects=True))(*arrs)


def _allreduce_small(packed, name):
    r = packed.shape[0]

    def body(in_ref, out_ref, land, send_sems, recv_sems):
        x, y, c = lax.axis_index("x"), lax.axis_index("y"), lax.axis_index("c")
        me = 4 * x + 2 * y + c
        land[me] = in_ref[...]
        rel = [(dx, dy, dc) for dx in (0, 1) for dy in (0, 1) for dc in (0, 1)][1:]
        peers = [((1 - x) if dx else x, (1 - y) if dy else y, (1 - c) if dc else c) for dx, dy, dc in rel]
        sends = []
        for k, peer in enumerate(peers):
            cp = pltpu.make_async_remote_copy(src_ref=in_ref, dst_ref=land.at[me], send_sem=send_sems.at[k], recv_sem=recv_sems.at[k],
                                              device_id=peer, device_id_type=MESH_T)
            cp.start()
            sends.append(cp)
        for k, (px, py, pc) in enumerate(peers):
            pltpu.make_async_remote_copy(src_ref=in_ref, dst_ref=land.at[4 * px + 2 * py + pc], send_sem=send_sems.at[k],
                                         recv_sem=recv_sems.at[k], device_id=(px, py, pc), device_id_type=MESH_T).wait_recv()
        for cp in sends:
            cp.wait_send()
        acc = land[0]
        for j in range(1, 8):
            acc = acc + land[j]
        out_ref[...] = acc

    vm = pl.BlockSpec(memory_space=pltpu.VMEM)
    return pl.pallas_call(
        body, name=name, in_specs=[vm], out_specs=vm, out_shape=jax.ShapeDtypeStruct((r, LANES), F32),
        scratch_shapes=[pltpu.VMEM((8, r, LANES), F32), pltpu.SemaphoreType.DMA((7,)), pltpu.SemaphoreType.DMA((7,))],
        compiler_params=pltpu.CompilerParams(has_side_effects=True))(packed)


def _in_splits():
    w = SSD_HEADS * SSD_HEAD_DIM
    cc = w + 2 * SSD_GROUPS * SSD_STATE
    return [w, cc, SSD_HEADS, MLA_Q_RANK, MLA_KV_RANK, MLA_ROPE]


def _padc(a, n):
    return jnp.pad(a, ((0, 0), (0, n - a.shape[1])))


def _win_pad(wm):
    offs = np.cumsum(_in_splits())[:-1]
    z, xbc, dt, cq, ckv, kr = jnp.split(wm, offs, axis=1)
    return jnp.concatenate([z, xbc, cq, ckv, _padc(kr, LANES), _padc(dt, LANES)], axis=1)


def _win_unpad(g):
    w, cc, nh, qr, kvr, rp = _in_splits()
    offs = np.cumsum([w, cc, qr, kvr, LANES])
    z, xbc, cq, ckv, kr, dt = jnp.split(g, offs, axis=1)
    return jnp.concatenate([z, xbc, dt[:, :nh], cq, ckv, kr[:, :rp]], axis=1)


def _wuq_pad(wm):
    r = wm.shape[0]
    w3 = wm.reshape(r, MLA_HEADS, MLA_NOPE + MLA_ROPE)
    return jnp.pad(w3, ((0, 0), (0, 0), (0, 2 * LANES - MLA_NOPE - MLA_ROPE))).reshape(r, MLA_HEADS * 2 * LANES)


def _wuq_unpad(g):
    r = g.shape[0]
    return g.reshape(r, MLA_HEADS, 2 * LANES)[:, :, :MLA_NOPE + MLA_ROPE].reshape(r, MLA_HEADS * (MLA_NOPE + MLA_ROPE))


def _full_from_gather(name, g):
    col = name in COL_SHARDED
    layers = g.shape[1]
    return [jnp.concatenate([g[j, l] for j in range(4)], axis=1 if col else 0) for l in range(layers)]


def _shards_from_full(name, mats):
    col = name in COL_SHARDED
    per = []
    for mt in mats:
        r, c = mt.shape
        per.append(mt.reshape(r, 4, c // 4).transpose(1, 0, 2) if col else mt.reshape(4, r // 4, c))
    return jnp.stack(per, axis=1).astype(BF16)


def _pack_small(vals, extra=None):
    flat = jnp.concatenate([vals[n].reshape(-1).astype(F32) for n in SMALL] + ([extra.reshape(1)] if extra is not None else []))
    rows = -(-flat.shape[0] // (8 * LANES)) * 8
    return jnp.pad(flat, (0, rows * LANES - flat.shape[0])).reshape(rows, LANES)


def _unpack_small(packed, like):
    flat, out, off = packed.reshape(-1), {}, 0
    for n in SMALL:
        sz = like[n].size
        out[n] = flat[off:off + sz].reshape(like[n].shape)
        off += sz
    return out


def _ffn_fwd(x, norm_g, wg, wu, wd, tag):
    h, r = _rms_fwd(x, norm_g, f"ffn_norm_{tag}")
    gate = _mm(h, wg, "nn", f"ffn_gate_{tag}")
    up = _mm(h, wu, "nn", f"ffn_up_{tag}")
    act = _swiglu_fwd(gate, up, f"swiglu_{tag}")
    out = _mm(act, wd, "nn", f"ffn_down_{tag}", add=x)
    return out, (x, r, h, gate, up, act)


def _ffn_bwd(dout, doutb, norm_g, wg, wu, wd, saved, tag):
    x, r, h, gate, up, act = saved
    dact = _mm(doutb, wd.T, "nn", f"ffn_dact_{tag}")
    dgate, dup = _swiglu_bwd(gate, up, dact, f"swiglu_bwd_{tag}")
    ht = h.T
    d_wd = _mm(act.T, doutb, "nn", f"ffn_dwd_{tag}", out_dtype=BF16)
    d_wg = _mm(ht, dgate, "nn", f"ffn_dwg_{tag}", out_dtype=BF16)
    d_wu = _mm(ht, dup, "nn", f"ffn_dwu_{tag}", out_dtype=BF16)
    dh = _mm(dgate, wg.T, "nn", f"ffn_dh1_{tag}")
    dh = _mm(dup, wu.T, "nn", f"ffn_dh2_{tag}", add=dh)
    dx, dxb, dnorm = _rms_bwd(x, norm_g, r, dh, f"ffn_norm_bwd_{tag}", dres=dout)
    return dx, dxb, dnorm, d_wg, d_wu, d_wd


def kernel(x, mix_norm, ffn_norm, w_in, conv_w, conv_b, dt_bias, a_log, d_skip, ssd_norm, q_norm, kv_norm, w_uq, w_ukv, w_out_even, w_qkv, w_out_odd, w_gate, w_up, w_down, final_norm, loss_target, m_mix_norm, m_ffn_norm, m_w_in, m_conv_w, m_conv_b, m_dt_bias, m_a_log, m_d_skip, m_ssd_norm, m_q_norm, m_kv_norm, m_w_uq, m_w_ukv, m_w_out_even, m_w_qkv, m_w_out_odd, m_w_gate, m_w_up, m_w_down, m_final_norm, v_mix_norm, v_ffn_norm, v_w_in, v_conv_w, v_conv_b, v_dt_bias, v_a_log, v_d_skip, v_ssd_norm, v_q_norm, v_kv_norm, v_w_uq, v_w_ukv, v_w_out_even, v_w_qkv, v_w_out_odd, v_w_gate, v_w_up, v_w_down, v_final_norm):
    given = dict(locals())
    wts = {n: given[n] for n in WEIGHTS}
    x0 = x[0]
    tgt = loss_target[0]
    t, d = x0.shape
    hw = SSD_HEADS * SSD_HEAD_DIM
    gn = SSD_GROUPS * SSD_STATE
    cc = hw + 2 * gn
    qr, kvr = MLA_Q_RANK, MLA_KV_RANK

    def shard(n, layer=None):
        a = wts[n] if layer is None else wts[n][layer:layer + 1]
        return a if n == 'conv_w' else a.astype(BF16)
    g0_names = ['w_in', 'conv_w', 'w_uq', 'w_ukv']
    g1_names = [('w_out_even', None), ('w_gate', 0), ('w_up', 0), ('w_down', 0)]
    g2_names = [('w_qkv', None), ('w_out_odd', None), ('w_gate', 1), ('w_up', 1), ('w_down', 1)]
    hg0 = _xstart([shard(n) for n in g0_names], True, [], "gather0_start")
    hg1 = _xstart([shard(n, l) for n, l in g1_names], True, [hg0[-1]], "gather1_start")
    hg2 = _xstart([shard(n, l) for n, l in g2_names], True, [hg1[-1]], "gather2_start")
    full = lambda n, g: _full_from_gather(n, g)[0]
    g0 = dict(zip(g0_names, _xwait(hg0, True, hg2[-1], "gather0_wait")))
    win = _win_pad(full('w_in', g0['w_in']))
    wuq = _wuq_pad(full('w_uq', g0['w_uq']))
    wukv = full('w_ukv', g0['w_ukv'])
    cw = _padc(full('conv_w', g0['conv_w']).T, 8).T
    o_cq, o_ckv, o_kr, o_dt = hw + cc, hw + cc + qr, hw + cc + qr + kvr, hw + cc + qr + kvr + LANES

    row = lambda v: v.reshape(1, -1)
    narrow = lambda v: _padc(v.reshape(1, -1), LANES)
    dtb, alog = narrow(dt_bias[0]), narrow(a_log[0])
    dsk_x = jnp.repeat(d_skip[0], SSD_HEAD_DIM).reshape(1, hw)
    cos, sin = _rope_tables(t)

    h0, r0 = _rms_fwd(x0, row(mix_norm[0]), "mix_norm_0")
    u = _mm(h0, win, "nn", "in_proj")
    z, xbc, c_q, c_kv = u[:, :hw], u[:, hw:hw + cc], u[:, o_cq:o_ckv], u[:, o_ckv:o_kr]
    kr_raw, dtraw = u[:, o_kr:o_dt], u[:, o_dt:]
    xc = _conv_fwd(xbc, cw, row(conv_b[0]), "conv")
    y_ssd, sprev = _ssd_fwd(dtraw, xc, dtb, alog, dsk_x, "ssd")
    yg, r_g = _gated_fwd(y_ssd, z, row(ssd_norm[0]), "ssd_gate_norm")
    q_lat, r_q = _rms_fwd(c_q, row(q_norm[0]), "q_norm")
    kv_lat, r_kv = _rms_fwd(c_kv, row(kv_norm[0]), "kv_norm")
    qf = _mm(q_lat, wuq, "nn", "q_up")
    kvb = _mm(kv_lat, wukv, "nn", "kv_up", out_dtype=BF16)
    q_r = _rope(qf, cos, sin, 2, False, BF16, "rope_q")
    k_r = _rope(kr_raw, cos, sin, 1, False, BF16, "rope_k")
    o_mla, lse = _mla_fwd(q_r, k_r, kvb, "mla")
    g1 = dict(zip(g1_names, _xwait(hg1, True, lse, "gather1_wait")))
    woe, wg0, wu0, wd0 = [full(k[0], g1[k]) for k in g1_names]
    cat = jnp.concatenate([yg, o_mla.astype(BF16)], axis=1)
    x1 = _mm(cat, woe, "nn", "mix_out_0", add=x0)
    x2, ffn0 = _ffn_fwd(x1, row(ffn_norm[0]), wg0, wu0, wd0, "0")

    g2 = dict(zip(g2_names, _xwait(hg2, True, x2, "gather2_wait")))
    wqkv, woo, wg1, wu1, wd1 = [full(k[0], g2[k]) for k in g2_names]
    h1, r1 = _rms_fwd(x2, row(mix_norm[1]), "mix_norm_1")
    qkv = _mm(h1, wqkv, "nn", "qkv_proj", out_dtype=BF16)
    o_sb, ltot = _sb_fwd(qkv, "sb")
    o_sbb = o_sb.astype(BF16)
    x3 = _mm(o_sbb, woo, "nn", "mix_out_1", add=x2)
    x4, ffn1 = _ffn_fwd(x3, row(ffn_norm[1]), wg1, wu1, wd1, "1")

    loss_part, dx4, dx4b, d_final = _loss_fwd_bwd(x4, row(final_norm), tgt, "loss")

    def grad_shards(pairs):
        return [_shards_from_full(n, [g]) for n, g in pairs]

    dx3, dx3b, d_ffn1, d_wg1, d_wu1, d_wd1 = _ffn_bwd(dx4, dx4b, row(ffn_norm[1]), wg1, wu1, wd1, ffn1, "1")
    do_sb = _mm(dx3b, woo.T, "nn", "sb_dout")
    d_woo = _mm(o_sbb.T, dx3b, "nn", "d_w_out_odd", out_dtype=BF16)
    dq, dk, dv = _sb_bwd(qkv, do_sb, ltot, "sb_bwd")
    dqkv = jnp.concatenate([dq, dk, dv], axis=1).astype(BF16)
    d_wqkv = _mm(h1.T, dqkv, "nn", "d_w_qkv", out_dtype=BF16)
    x2_names = ['w_qkv', 'w_out_odd', 'w_gate', 'w_up', 'w_down']
    hx2 = _xstart(grad_shards(zip(x2_names, [d_wqkv, d_woo, d_wg1, d_wu1, d_wd1])), False, [], "grads2_start")
    dh1 = _mm(dqkv, wqkv.T, "nn", "qkv_dh")
    dx2, dx2b, d_mix1 = _rms_bwd(x2, row(mix_norm[1]) + hx2[-1][0:1, 0:1], r1, dh1, "mix_norm_bwd_1", dres=dx3)

    dx1, dx1b, d_ffn0, d_wg0, d_wu0, d_wd0 = _ffn_bwd(dx2, dx2b, row(ffn_norm[0]), wg0, wu0, wd0, ffn0, "0")
    d_woe = _mm(cat.T, dx1b, "nn", "d_w_out_even", out_dtype=BF16)
    x1_names = ['w_out_even', 'w_gate', 'w_up', 'w_down']
    hx1 = _xstart(grad_shards(zip(x1_names, [d_woe, d_wg0, d_wu0, d_wd0])), False, [], "grads1_start")
    tok1 = hx1[-1][0:1, 0:1]
    dcat = _mm(dx1b, woe.T, "nn", "mix_dcat")
    dy_ssd, dz, d_ssdn = _gated_bwd(y_ssd, z, row(ssd_norm[0]) + tok1, r_g, dcat[:, :hw], "ssd_gate_norm_bwd")
    dxs, db_, dc_, ddtraw, d_dtb, d_alog, d_dsk = _ssd_bwd(dtraw, xc, dtb, alog, dsk_x, sprev, dy_ssd, "ssd_bwd")
    dxbc, d_cw, d_cb = _conv_bwd(xbc, cw, row(conv_b[0]), jnp.concatenate([dxs, db_, dc_], axis=1), "conv_bwd")
    dqm, dkvm, dkr = _mla_bwd(q_r, k_r, kvb, o_mla, dcat[:, hw:], lse + tok1, "mla_bwd")
    dqf = _rope(dqm, cos, sin, 2, True, BF16, "rope_q_bwd")
    dkr_raw = _rope(dkr, cos, sin, 1, True, F32, "rope_k_bwd")
    dkvf = dkvm.astype(BF16)
    d_wuq = _mm(q_lat.T, dqf, "nn", "d_w_uq", out_dtype=BF16)
    d_wukv = _mm(kv_lat.T, dkvf, "nn", "d_w_ukv", out_dtype=BF16)
    dq_lat = _mm(dqf, wuq.T, "nn", "q_up_bwd")
    dkv_lat = _mm(dkvf, wukv.T, "nn", "kv_up_bwd")
    dc_q, _, d_qn = _rms_bwd(c_q, row(q_norm[0]), r_q, dq_lat, "q_norm_bwd")
    dc_kv, _, d_kvn = _rms_bwd(c_kv, row(kv_norm[0]), r_kv, dkv_lat, "kv_norm_bwd")
    du = jnp.concatenate([dz, dxbc, dc_q, dc_kv, dkr_raw, ddtraw], axis=1).astype(BF16)
    d_win = _mm(h0.T, du, "nn", "d_w_in", out_dtype=BF16)
    x0_names = ['w_in', 'conv_w', 'w_uq', 'w_ukv']
    hx0 = _xstart(grad_shards(zip(x0_names, [_win_unpad(d_win), d_cw, _wuq_unpad(d_wuq), d_wukv])), False, [], "grads0_start")
    dh0 = _mm(du, win.T, "nn", "in_proj_bwd")
    grad_x, _, d_mix0 = _rms_bwd(x0, row(mix_norm[0]) + hx0[-1][0:1, 0:1], r0, dh0, "mix_norm_bwd_0", dres=dx1)

    def chip_sums(handle, names, after, tag):
        lands = _xwait(handle, False, after, f"grads{tag}_wait")
        return {n: _sum4(p.reshape(4, -1, p.shape[-1]), f"sum{tag}_{n}") for n, p in zip(names, lands)}
    grads, deltas, new_m, new_v = {}, {}, {}, {}

    def adamw(n, ga, gb):
        res = _adamw(_as2d(wts[n]), ga, gb, _as2d(given['m_' + n]), _as2d(given['v_' + n]), f"adamw_{n}")
        grads[n], deltas[n], new_m[n], new_v[n] = [r.reshape(wts[n].shape) for r in res]
        return res[0]
    s2 = chip_sums(hx2, x2_names, grad_x, "2")
    hs2 = _sib_start([s2[n] for n in x2_names], [], "sibling2_start")
    s1 = chip_sums(hx1, x1_names, hs2[-1], "1")
    hs1 = _sib_start([s1[n] for n in x1_names], [], "sibling1_start")
    r2 = dict(zip(x2_names, _sib_wait(hs2, hs1[-1], "sibling2_wait")))
    r1 = dict(zip(x1_names, _sib_wait(hs1, r2['w_qkv'], "sibling1_wait")))
    last = None
    for n in ('w_qkv', 'w_out_odd'):
        last = adamw(n, s2[n], r2[n])
    last = adamw('w_out_even', s1['w_out_even'], r1['w_out_even'])
    for n in ('w_gate', 'w_up', 'w_down'):
        last = adamw(n, jnp.concatenate([s1[n], s2[n]], axis=0), jnp.concatenate([r1[n], r2[n]], axis=0))
    s0 = chip_sums(hx0, x0_names, last, "0")
    r0_ = _sibling_exchange([s0[n] for n in x0_names], "exchange_sibling0")
    for n, gb in zip(x0_names, r0_):
        adamw(n, s0[n], gb)

    nhs = SSD_HEADS
    small_g = {'mix_norm': jnp.stack([d_mix0, d_mix1]), 'ffn_norm': jnp.stack([d_ffn0, d_ffn1]), 'conv_b': d_cb[None],
               'dt_bias': d_dtb[None, :nhs], 'a_log': d_alog[None, :nhs], 'd_skip': d_dsk[None, :nhs], 'ssd_norm': d_ssdn[None],
               'q_norm': d_qn[None], 'kv_norm': d_kvn[None], 'final_norm': d_final}
    packed, _ = lax.optimization_barrier((_pack_small(small_g, loss_part), r0_[0]))
    g_small = _allreduce_small(packed, "allreduce_small")
    loss = g_small.reshape(-1)[sum(wts[n].size for n in SMALL)]
    pk = lambda pre: _pack_small({n: given[pre + n] for n in SMALL}, jnp.zeros((), F32))
    sg, sd, sm, sv = _adamw(pk(''), g_small, None, pk('m_'), pk('v_'), "adamw_small")
    for dst, src in ((grads, sg), (deltas, sd), (new_m, sm), (new_v, sv)):
        dst.update(_unpack_small(src, wts))

    outs = [loss, grad_x[None]]
    for dct in (grads, deltas, new_m, new_v):
        outs += [dct[n] for n in WEIGHTS]
    return tuple(outs)
```

```python
import functools
import math

import jax
import jax.numpy as jnp
import numpy as np
from jax import lax
from jax.experimental import pallas as pl
from jax.experimental.pallas import tpu as pltpu

F32, BF16 = jnp.float32, jnp.bfloat16

RMS_EPS = 1e-6
SSD_HEADS, SSD_HEAD_DIM, SSD_GROUPS, SSD_STATE, SSD_CONV, SSD_CHUNK = 32, 64, 4, 128, 4, 128
MLA_HEADS, MLA_Q_RANK, MLA_KV_RANK, MLA_NOPE, MLA_ROPE, MLA_V = 16, 512, 512, 128, 64, 128
ROPE_THETA = 10000.0
SB_HEADS, SB_HEAD_DIM = 16, 128
ADAM_LR, ADAM_B1, ADAM_B2, ADAM_EPS, ADAM_WD, ADAM_STEP = 0.001, 0.9, 0.999, 1e-08, 0.01, 10

LANES = 128
VMEM_LIMIT_BYTES = 56 * 1024 * 1024
MM_VMEM_BUDGET = 40 * 1024 * 1024
ATT_BLK = 256
SB_FWD_HEADS = 4
ROW_TILE = 256
NEG = -1e30

MESH_T = pl.DeviceIdType.MESH
WEIGHTS = ['mix_norm', 'ffn_norm', 'w_in', 'conv_w', 'conv_b', 'dt_bias', 'a_log', 'd_skip', 'ssd_norm', 'q_norm',
           'kv_norm', 'w_uq', 'w_ukv', 'w_out_even', 'w_qkv', 'w_out_odd', 'w_gate', 'w_up', 'w_down', 'final_norm']
SHARDED = ['w_in', 'conv_w', 'w_uq', 'w_ukv', 'w_out_even', 'w_qkv', 'w_out_odd', 'w_gate', 'w_up', 'w_down']
COL_SHARDED = ['w_in', 'conv_w', 'w_uq', 'w_ukv', 'w_qkv', 'w_gate', 'w_up']
SMALL = [n for n in WEIGHTS if n not in SHARDED]


def _tile(n, cands):
    for c in cands:
        if n % c == 0:
            return c
    return n


def _params(ngrid):
    return pltpu.CompilerParams(dimension_semantics=("arbitrary",) * ngrid, vmem_limit_bytes=VMEM_LIMIT_BYTES)


def _dot(a, b, mode="nn"):
    dims = {"nn": (((1,), (0,)), ((), ())), "nt": (((1,), (1,)), ((), ())), "tn": (((0,), (0,)), ((), ()))}[mode]
    return lax.dot_general(a, b, dims, preferred_element_type=F32)


def _split(x, parts):
    out, r = [], x
    for _ in range(parts):
        p = r.astype(BF16)
        out.append(p)
        r = r - p.astype(F32)
    return out


def _xdot(x, e, parts=3):
    acc = None
    for p in _split(x, parts):
        t = _dot(p, e)
        acc = t if acc is None else acc + t
    return acc


def _xdot_l(e, x, parts=3):
    acc = None
    for p in _split(x, parts):
        t = _dot(e, p)
        acc = t if acc is None else acc + t
    return acc


def _iota(shape, dim):
    return lax.broadcasted_iota(jnp.int32, shape, dim)


def _softplus(z):
    return jnp.maximum(z, 0.0) + jnp.log(1.0 + jnp.exp(-jnp.abs(z)))


def _sigmoid(z):
    return 1.0 / (1.0 + jnp.exp(-z))


def _acc_rows(ref, first, val):
    @pl.when(first)
    def _():
        ref[...] = jnp.zeros_like(ref)
    ref[...] += jnp.broadcast_to(val, ref.shape)


def _mm(a, b, mode, name, out_dtype=F32, add=None):
    if mode == "nn":
        (m, k), n = a.shape, b.shape[1]
    elif mode == "nt":
        (m, k), n = a.shape, b.shape[0]
    else:
        (k, m), n = a.shape, b.shape[1]
    tn = _tile(n, (1536, 1408, 1280, 1024, 768, 640, 512, 256, 128))
    tk = k if k <= 2048 else _tile(k, (2048, 1536, 1408, 1280, 1024, 512, 256, 128))
    ob = jnp.dtype(out_dtype).itemsize

    def need(tm_):
        per = tm_ * tk * a.dtype.itemsize + tk * tn * b.dtype.itemsize + tm_ * tn * ob + (tm_ * tn * 4 if add is not None else 0)
        return 2 * per + (tm_ * tn * 4 if k > tk else 0)
    tm = m
    for cand in (1408, 1024, 512, 256, 128):
        if m % cand == 0:
            tm = cand
            if need(cand) <= MM_VMEM_BUDGET:
                break
    nk = k // tk
    a_spec = {"nn": pl.BlockSpec((tm, tk), lambda i, j, kk: (i, kk)), "nt": pl.BlockSpec((tm, tk), lambda i, j, kk: (i, kk)),
              "tn": pl.BlockSpec((tk, tm), lambda i, j, kk: (kk, i))}[mode]
    b_spec = {"nn": pl.BlockSpec((tk, tn), lambda i, j, kk: (kk, j)), "nt": pl.BlockSpec((tn, tk), lambda i, j, kk: (j, kk)),
              "tn": pl.BlockSpec((tk, tn), lambda i, j, kk: (kk, j))}[mode]
    o_spec = pl.BlockSpec((tm, tn), lambda i, j, kk: (i, j))
    has_add = add is not None

    def body(*refs):
        a_ref, b_ref = refs[0], refs[1]
        add_ref = refs[2] if has_add else None
        o_ref = refs[2 + has_add]
        part = _dot(a_ref[...].astype(BF16), b_ref[...].astype(BF16), mode)
        if nk == 1:
            if has_add:
                part = part + add_ref[...]
            o_ref[...] = part.astype(o_ref.dtype)
            return
        acc_ref = refs[3 + has_add]
        kk = pl.program_id(2)

        @pl.when(kk == 0)
        def _():
            acc_ref[...] = jnp.zeros_like(acc_ref)
        acc_ref[...] += part

        @pl.when(kk == nk - 1)
        def _():
            r = acc_ref[...]
            if has_add:
                r = r + add_ref[...]
            o_ref[...] = r.astype(o_ref.dtype)

    ins = [a, b] + ([add] if has_add else [])
    specs = [a_spec, b_spec] + ([o_spec] if has_add else [])
    return pl.pallas_call(
        body, name=name, grid=(m // tm, n // tn, nk), in_specs=specs, out_specs=o_spec,
        out_shape=jax.ShapeDtypeStruct((m, n), out_dtype),
        scratch_shapes=[pltpu.VMEM((tm, tn), F32)] if nk > 1 else [],
        compiler_params=_params(3))(*ins)


def _rows(t):
    return _tile(t, (ROW_TILE, 128, 64, 32, 16, 8))


def _rms_fwd(x, g, name):
    t, c = x.shape
    tr = _rows(t)

    def body(x_ref, g_ref, h_ref, r_ref):
        xv = x_ref[...]
        r = lax.rsqrt(jnp.mean(xv * xv, axis=-1, keepdims=True) + RMS_EPS)
        h_ref[...] = (xv * r * g_ref[...]).astype(h_ref.dtype)
        r_ref[...] = r

    return pl.pallas_call(
        body, name=name, grid=(t // tr,),
        in_specs=[pl.BlockSpec((tr, c), lambda i: (i, 0)), pl.BlockSpec((1, c), lambda i: (0, 0))],
        out_specs=[pl.BlockSpec((tr, c), lambda i: (i, 0)), pl.BlockSpec((tr, 1), lambda i: (i, 0))],
        out_shape=[jax.ShapeDtypeStruct((t, c), BF16), jax.ShapeDtypeStruct((t, 1), F32)],
        compiler_params=_params(1))(x, g)


def _rms_bwd(x, g, r, dh, name, dres=None):
    t, c = x.shape
    tr = _rows(t)
    has_res = dres is not None

    def body(*refs):
        x_ref, g_ref, r_ref, dh_ref = refs[:4]
        res_ref = refs[4] if has_res else None
        dx_ref, dxb_ref, dg_ref = refs[4 + has_res:]
        rv = r_ref[...]
        xh = x_ref[...] * rv
        dhv = dh_ref[...]
        dxh = dhv * g_ref[...]
        cm = jnp.mean(dxh * xh, axis=-1, keepdims=True)
        dx = (dxh - xh * cm) * rv
        if has_res:
            dx = dx + res_ref[...]
        dx_ref[...] = dx
        dxb_ref[...] = dx.astype(BF16)
        _acc_rows(dg_ref, pl.program_id(0) == 0, jnp.sum(dhv * xh, axis=0, keepdims=True))

    row = pl.BlockSpec((tr, c), lambda i: (i, 0))
    ins = [x, g, r, dh] + ([dres] if has_res else [])
    specs = [row, pl.BlockSpec((1, c), lambda i: (0, 0)), pl.BlockSpec((tr, 1), lambda i: (i, 0)), row] + ([row] if has_res else [])
    dx, dxb, dg = pl.pallas_call(
        body, name=name, grid=(t // tr,), in_specs=specs,
        out_specs=[row, row, pl.BlockSpec((8, c), lambda i: (0, 0))],
        out_shape=[jax.ShapeDtypeStruct((t, c), F32), jax.ShapeDtypeStruct((t, c), BF16), jax.ShapeDtypeStruct((8, c), F32)],
        compiler_params=_params(1))(*ins)
    return dx, dxb, dg[0]


def _gated_fwd(y, z, g, name):
    t, c = y.shape
    tr = _rows(t)

    def body(y_ref, z_ref, g_ref, o_ref, r_ref):
        zv = z_ref[...]
        v = y_ref[...] * zv * _sigmoid(zv)
        r = lax.rsqrt(jnp.mean(v * v, axis=-1, keepdims=True) + RMS_EPS)
        o_ref[...] = (v * r * g_ref[...]).astype(o_ref.dtype)
        r_ref[...] = r

    row = pl.BlockSpec((tr, c), lambda i: (i, 0))
    return pl.pallas_call(
        body, name=name, grid=(t // tr,), in_specs=[row, row, pl.BlockSpec((1, c), lambda i: (0, 0))],
        out_specs=[row, pl.BlockSpec((tr, 1), lambda i: (i, 0))],
        out_shape=[jax.ShapeDtypeStruct((t, c), BF16), jax.ShapeDtypeStruct((t, 1), F32)],
        compiler_params=_params(1))(y, z, g)


def _gated_bwd(y, z, g, r, dout, name):
    t, c = y.shape
    tr = _rows(t)

    def body(y_ref, z_ref, g_ref, r_ref, do_ref, dy_ref, dz_ref, dg_ref):
        yv, zv, rv, dov = y_ref[...], z_ref[...], r_ref[...], do_ref[...]
        s = _sigmoid(zv)
        sz = zv * s
        xh = yv * sz * rv
        dxh = dov * g_ref[...]
        cm = jnp.mean(dxh * xh, axis=-1, keepdims=True)
        dv = (dxh - xh * cm) * rv
        dy_ref[...] = dv * sz
        dz_ref[...] = dv * yv * s * (1.0 + zv * (1.0 - s))
        _acc_rows(dg_ref, pl.program_id(0) == 0, jnp.sum(dov * xh, axis=0, keepdims=True))

    row = pl.BlockSpec((tr, c), lambda i: (i, 0))
    dy, dz, dg = pl.pallas_call(
        body, name=name, grid=(t // tr,),
        in_specs=[row, row, pl.BlockSpec((1, c), lambda i: (0, 0)), pl.BlockSpec((tr, 1), lambda i: (i, 0)), row],
        out_specs=[row, row, pl.BlockSpec((8, c), lambda i: (0, 0))],
        out_shape=[jax.ShapeDtypeStruct((t, c), F32), jax.ShapeDtypeStruct((t, c), F32), jax.ShapeDtypeStruct((8, c), F32)],
        compiler_params=_params(1))(y, z, g, r, dout)
    return dy, dz, dg[0]


def _swiglu_fwd(gate, up, name):
    t, c = gate.shape
    tr, tc = _rows(t), _tile(c, (2816, 1408, 1024, 512, 256, 128))

    def body(g_ref, u_ref, o_ref):
        gv = g_ref[...].astype(F32)
        o_ref[...] = (gv * _sigmoid(gv) * u_ref[...].astype(F32)).astype(o_ref.dtype)

    blk = pl.BlockSpec((tr, tc), lambda i, j: (i, j))
    return pl.pallas_call(body, name=name, grid=(t // tr, c // tc), in_specs=[blk, blk], out_specs=blk,
                          out_shape=jax.ShapeDtypeStruct((t, c), BF16), compiler_params=_params(2))(gate, up)


def _swiglu_bwd(gate, up, dact, name):
    t, c = gate.shape
    tr, tc = _rows(t), _tile(c, (2816, 1408, 1024, 512, 256, 128))

    def body(g_ref, u_ref, d_ref, dg_ref, du_ref):
        gv, dv = g_ref[...].astype(F32), d_ref[...].astype(F32)
        s = _sigmoid(gv)
        dg_ref[...] = (dv * u_ref[...].astype(F32) * s * (1.0 + gv * (1.0 - s))).astype(dg_ref.dtype)
        du_ref[...] = (dv * gv * s).astype(du_ref.dtype)

    blk = pl.BlockSpec((tr, tc), lambda i, j: (i, j))
    return pl.pallas_call(body, name=name, grid=(t // tr, c // tc), in_specs=[blk, blk, blk], out_specs=[blk, blk],
                          out_shape=[jax.ShapeDtypeStruct((t, c), BF16)] * 2, compiler_params=_params(2))(gate, up, dact)


def _loss_fwd_bwd(x, g, tgt, name):
    t, c = x.shape
    tr = _rows(t)

    def body(x_ref, g_ref, t_ref, l_ref, dx_ref, dxb_ref, dg_ref):
        xv, gv = x_ref[...], g_ref[...]
        r = lax.rsqrt(jnp.mean(xv * xv, axis=-1, keepdims=True) + RMS_EPS)
        xh = xv * r
        err = xh * gv - t_ref[...]
        per_row = jnp.mean(err * err, axis=-1, keepdims=True)
        dy = err * (1.0 / c)
        dxh = dy * gv
        cm = jnp.mean(dxh * xh, axis=-1, keepdims=True)
        dx = (dxh - xh * cm) * r
        dx_ref[...] = dx
        dxb_ref[...] = dx.astype(BF16)
        first = pl.program_id(0) == 0
        _acc_rows(dg_ref, first, jnp.sum(dy * xh, axis=0, keepdims=True))
        _acc_rows(l_ref, first, jnp.broadcast_to(0.5 * jnp.sum(per_row, axis=0, keepdims=True), (1, LANES)))

    row = pl.BlockSpec((tr, c), lambda i: (i, 0))
    lo, dx, dxb, dg = pl.pallas_call(
        body, name=name, grid=(t // tr,), in_specs=[row, pl.BlockSpec((1, c), lambda i: (0, 0)), row],
        out_specs=[pl.BlockSpec((8, LANES), lambda i: (0, 0)), row, row, pl.BlockSpec((8, c), lambda i: (0, 0))],
        out_shape=[jax.ShapeDtypeStruct((8, LANES), F32), jax.ShapeDtypeStruct((t, c), F32), jax.ShapeDtypeStruct((t, c), BF16),
                   jax.ShapeDtypeStruct((8, c), F32)],
        compiler_params=_params(1))(x, g, tgt)
    return lo[0, 0], dx, dxb, dg[0]


def _conv_specs(t, c):
    tr = _rows(t)
    tc = _tile(c, (1024, 768, 512, 256, 128))
    h8 = tr // 8
    tile = pl.BlockSpec((tr, tc), lambda j, i: (i, j))
    prev = pl.BlockSpec((8, tc), lambda j, i: (jnp.maximum(i * h8 - 1, 0), j))
    nxt = pl.BlockSpec((8, tc), lambda j, i: (jnp.minimum((i + 1) * h8, t // 8 - 1), j))
    return tr, tc, tile, prev, nxt


def _conv_fwd(xbc, w, b, name):
    t, c = xbc.shape
    tr, tc, tile, prev, _ = _conv_specs(t, c)

    def body(x_ref, p_ref, w_ref, b_ref, o_ref, buf):
        i = pl.program_id(1)
        buf[0:8, :] = jnp.where(i > 0, p_ref[...], 0.0)
        buf[8:, :] = x_ref[...]
        pre = b_ref[...]
        for k in range(SSD_CONV):
            pre = pre + w_ref[k:k + 1, :] * buf[pl.ds(8 - (SSD_CONV - 1) + k, tr), :]
        o_ref[...] = pre * _sigmoid(pre)

    return pl.pallas_call(
        body, name=name, grid=(c // tc, t // tr),
        in_specs=[tile, prev, pl.BlockSpec((8, tc), lambda j, i: (0, j)), pl.BlockSpec((1, tc), lambda j, i: (0, j))],
        out_specs=tile, out_shape=jax.ShapeDtypeStruct((t, c), F32),
        scratch_shapes=[pltpu.VMEM((tr + 8, tc), F32)], compiler_params=_params(2))(xbc, xbc, w, b)


def _conv_bwd(xbc, w, b, dout, name):
    t, c = xbc.shape
    tr, tc, tile, prev, nxt = _conv_specs(t, c)
    nt = t // tr
    kc = SSD_CONV

    def body(x_ref, p_ref, n_ref, w_ref, b_ref, d_ref, dn_ref, dx_ref, dw_ref, db_ref, buf, dbuf):
        i = pl.program_id(1)
        last = i == nt - 1
        buf[0:8, :] = jnp.where(i > 0, p_ref[...], 0.0)
        buf[8:8 + tr, :] = x_ref[...]
        buf[8 + tr:, :] = jnp.where(last, 0.0, n_ref[...])
        pre = b_ref[...]
        for k in range(kc):
            pre = pre + w_ref[k:k + 1, :] * buf[pl.ds(8 - (kc - 1) + k, tr + 8), :]
        s = _sigmoid(pre)
        dsilu = s * (1.0 + pre * (1.0 - s))
        dbuf[0:tr, :] = d_ref[...] * dsilu[0:tr, :]
        dbuf[tr:, :] = jnp.where(last, 0.0, dn_ref[...]) * dsilu[tr:, :]
        dpre = dbuf[0:tr, :]
        dx = jnp.zeros((tr, tc), F32)
        first = i == 0
        for k in range(kc):
            dx = dx + w_ref[k:k + 1, :] * dbuf[pl.ds(kc - 1 - k, tr), :]
        dx_ref[...] = dx

        @pl.when(first)
        def _():
            dw_ref[...] = jnp.zeros_like(dw_ref)
        for k in range(kc):
            dw_ref[k:k + 1, :] += jnp.sum(dpre * buf[pl.ds(8 - (kc - 1) + k, tr), :], axis=0, keepdims=True)
        _acc_rows(db_ref, first, jnp.sum(dpre, axis=0, keepdims=True))

    par = pl.BlockSpec((8, tc), lambda j, i: (0, j))
    dx, dw, db = pl.pallas_call(
        body, name=name, grid=(c // tc, nt),
        in_specs=[tile, prev, nxt, par, pl.BlockSpec((1, tc), lambda j, i: (0, j)), tile, nxt],
        out_specs=[tile, par, par],
        out_shape=[jax.ShapeDtypeStruct((t, c), F32), jax.ShapeDtypeStruct((8, c), F32), jax.ShapeDtypeStruct((8, c), F32)],
        scratch_shapes=[pltpu.VMEM((tr + 16, tc), F32), pltpu.VMEM((tr + 8, tc), F32)],
        compiler_params=_params(2))(xbc, xbc, xbc, w, b, dout, dout)
    return dx, dw[:kc], db[0]


def _ssd_consts():
    h, p, ln = SSD_HEADS, SSD_HEAD_DIM, SSD_CHUNK
    w = h * p
    hrow, jcol = _iota((LANES, w), 0), _iota((LANES, w), 1)
    e = ((jcol >= hrow * p) & (jcol < (hrow + 1) * p)).astype(BF16)
    jrow, hcol = _iota((w, LANES), 0), _iota((w, LANES), 1)
    et = ((jrow >= hcol * p) & (jrow < (hcol + 1) * p)).astype(BF16)
    row, col = _iota((ln, ln), 0), _iota((ln, ln), 1)
    return e, et, row, col


def _ssd_common(dtraw_ref, dtb_ref, alog_ref, e):
    ln = SSD_CHUNK
    raw = dtraw_ref[...] + dtb_ref[...]
    dt = _softplus(raw)
    a = -jnp.exp(alog_ref[...])
    adt = dt * a
    row, col = _iota((ln, ln), 0), _iota((ln, ln), 1)
    cs = _xdot_l((col <= row).astype(BF16), adt)
    cl = jnp.sum(adt, axis=0, keepdims=True)
    ex = _xdot(jnp.concatenate([dt, cs, jnp.broadcast_to(cl, (ln, LANES))], axis=0), e)
    return raw, dt, a, cs, ex[:ln], ex[ln:2 * ln], ex[2 * ln:]


def _head_decay(cs, h, causal):
    ln = SSD_CHUNK
    lane = _iota((1, LANES), 1)
    colv = jnp.sum(jnp.where(lane == h, cs, 0.0), axis=1, keepdims=True)
    cb = jnp.broadcast_to(colv, (ln, ln))
    return jnp.exp(jnp.where(causal, cb - cb.T, NEG))


def _ssd_fwd(dtraw, xc, dtb, alog, dskip_x, name):
    t = dtraw.shape[0]
    h, p, g, n, ln = SSD_HEADS, SSD_HEAD_DIM, SSD_GROUPS, SSD_STATE, SSD_CHUNK
    w, gn, gw, hpg = h * p, g * n, (h // g) * p, h // g
    assert n == ln and gw % LANES == 0 and w % gn == 0
    nc = t // ln

    def body(dtraw_ref, x_ref, b_ref, c_ref, dtb_ref, alog_ref, dsk_ref, y_ref, sp_ref, s_ref):
        @pl.when(pl.program_id(0) == 0)
        def _():
            s_ref[...] = jnp.zeros_like(s_ref)
        e, _, row, col = _ssd_consts()
        causal = col <= row
        _, _, _, cs, dt_x, cs_x, cl_x = _ssd_common(dtraw_ref, dtb_ref, alog_ref, e)
        xv = x_ref[...]
        xd = xv * dt_x
        xdb = xd.astype(BF16)
        sv = s_ref[...]
        sp_ref[0] = sv
        el_x = jnp.exp(cs_x)
        zb = (xd * jnp.exp(cl_x - cs_x)).astype(BF16)
        cd_x = jnp.exp(cl_x)
        dsk = dsk_ref[...]
        half = _iota((1, LANES), 1) >= p
        for gi in range(g):
            gs = slice(gi * gw, (gi + 1) * gw)
            bg = b_ref[:, gi * n:(gi + 1) * n].astype(BF16)
            cg = c_ref[:, gi * n:(gi + 1) * n].astype(BF16)
            gm = _dot(cg, bg, "nt")
            sg = sv[:, gs]
            yoff = _dot(cg, sg.astype(BF16)) * el_x[:, gs]
            s_ref[:, gs] = sg * cd_x[:, gs] + _dot(bg, zb[:, gs], "tn")
            for pp in range(gw // LANES):
                ls = slice(gi * gw + pp * LANES, gi * gw + (pp + 1) * LANES)
                xp = xdb[:, ls]
                yp = yoff[:, pp * LANES:(pp + 1) * LANES] + dsk[:, ls] * xv[:, ls]
                for hh in range(LANES // p):
                    hd = gi * hpg + pp * (LANES // p) + hh
                    wm = (gm * _head_decay(cs, hd, causal)).astype(BF16)
                    yp = yp + _dot(wm, jnp.where(half == (hh == 1), xp, jnp.zeros_like(xp)))
                y_ref[:, ls] = yp

    nar = pl.BlockSpec((ln, LANES), lambda c: (c, 0))
    one = pl.BlockSpec((1, LANES), lambda c: (0, 0))
    return pl.pallas_call(
        body, name=name, grid=(nc,),
        in_specs=[nar, pl.BlockSpec((ln, w), lambda c: (c, 0)), pl.BlockSpec((ln, gn), lambda c: (c, w // gn)),
                  pl.BlockSpec((ln, gn), lambda c: (c, w // gn + 1)), one, one, pl.BlockSpec((1, w), lambda c: (0, 0))],
        out_specs=[pl.BlockSpec((ln, w), lambda c: (c, 0)), pl.BlockSpec((1, n, w), lambda c: (c, 0, 0))],
        out_shape=[jax.ShapeDtypeStruct((t, w), F32), jax.ShapeDtypeStruct((nc, n, w), F32)],
        scratch_shapes=[pltpu.VMEM((n, w), F32)], compiler_params=_params(1))(dtraw, xc, xc, xc, dtb, alog, dskip_x)


def _ssd_bwd(dtraw, xc, dtb, alog, dskip_x, sprev, dy, name):
    t = dtraw.shape[0]
    h, p, g, n, ln = SSD_HEADS, SSD_HEAD_DIM, SSD_GROUPS, SSD_STATE, SSD_CHUNK
    w, gn, gw, hpg = h * p, g * n, (h // g) * p, h // g
    nc = t // ln

    def body(dtraw_ref, x_ref, b_ref, c_ref, dtb_ref, alog_ref, dsk_ref, sp_ref, dy_ref,
             dx_ref, db_ref, dc_ref, ddt_ref, dbias_ref, dalog_ref, ddsk_ref, ds_ref, dxd_ref, qcs_ref):
        first = pl.program_id(0) == 0

        @pl.when(first)
        def _():
            ds_ref[...] = jnp.zeros_like(ds_ref)
        e, et, row, col = _ssd_consts()
        causal = col <= row
        raw, dt, a, cs, dt_x, cs_x, cl_x = _ssd_common(dtraw_ref, dtb_ref, alog_ref, e)
        xv = x_ref[...]
        xd = xv * dt_x
        xdb = xd.astype(BF16)
        sv = sp_ref[0]
        dyv = dy_ref[...]
        dyb = dyv.astype(BF16)
        dsn = ds_ref[...]
        el_x = jnp.exp(cs_x)
        dte_x = jnp.exp(cl_x - cs_x)
        cd_x = jnp.exp(cl_x)
        zf = xd * dte_x
        lane = _iota((1, LANES), 1)
        half = lane >= p
        lastrow = _iota((ln, 1), 0) == ln - 1
        dcs = jnp.zeros((ln, LANES), F32)
        for gi in range(g):
            gs = slice(gi * gw, (gi + 1) * gw)
            ns = slice(gi * n, (gi + 1) * n)
            bg = b_ref[:, ns].astype(BF16)
            cg = c_ref[:, ns].astype(BF16)
            gm = _dot(cg, bg, "nt")
            sgb = sv[:, gs].astype(BF16)
            dsg = dsn[:, gs]
            dsgb = dsg.astype(BF16)
            yoff = _dot(cg, sgb) * el_x[:, gs]
            drb = (el_x[:, gs] * dyv[:, gs]).astype(BF16)
            dcg = _dot(drb, sgb, "nt")
            ds_ref[:, gs] = cd_x[:, gs] * dsg + _dot(cg, drb, "tn")
            dz = _dot(bg, dsgb)
            zg = zf[:, gs]
            dbg = _dot(zg.astype(BF16), dsgb, "nt")
            dzz = dz * zg
            qcl = jnp.sum(dzz + cd_x[:, gs] * dsg * sv[:, gs], axis=0, keepdims=True)
            qcs_ref[:, gs] = dyv[:, gs] * yoff - dzz + jnp.where(lastrow, jnp.broadcast_to(qcl, (ln, gw)), 0.0)
            dgm = jnp.zeros((ln, ln), F32)
            for pp in range(gw // LANES):
                ls = slice(gi * gw + pp * LANES, gi * gw + (pp + 1) * LANES)
                xp = xdb[:, ls]
                dxp = dz[:, pp * LANES:(pp + 1) * LANES] * dte_x[:, ls]
                for hh in range(LANES // p):
                    hd = gi * hpg + pp * (LANES // p) + hh
                    dm = _head_decay(cs, hd, causal)
                    wf = gm * dm
                    dym = jnp.where(half == (hh == 1), dyb[:, ls], jnp.zeros_like(xp))
                    dw = _dot(dym, xp, "nt")
                    dxp = dxp + _dot(wf.astype(BF16), dym, "tn")
                    dgm = dgm + dw * dm
                    mm = dw * wf
                    rc = jnp.sum(mm, axis=1, keepdims=True) - jnp.sum(mm.T, axis=1, keepdims=True)
                    dcs = dcs + rc * (lane == hd).astype(F32)
                dxd_ref[:, ls] = dxp
            dgb = dgm.astype(BF16)
            dc_ref[:, ns] = dcg + _dot(dgb, bg)
            db_ref[:, ns] = dbg + _dot(dgb, cg, "tn")
        dxd = dxd_ref[...]
        dx_ref[...] = dxd * dt_x + dsk_ref[...] * dyv
        red = _xdot(jnp.concatenate([qcs_ref[...], dxd * xv, dyv * xv], axis=0), et)
        dcs = dcs + red[:ln]
        dadt = _xdot_l((row <= col).astype(BF16), dcs)
        ddt = red[ln:2 * ln] + dadt * a
        draw = ddt * _sigmoid(raw)
        ddt_ref[...] = draw
        _acc_rows(dbias_ref, first, jnp.sum(draw, axis=0, keepdims=True))
        _acc_rows(dalog_ref, first, jnp.sum(dadt * dt, axis=0, keepdims=True) * a)
        _acc_rows(ddsk_ref, first, jnp.sum(red[2 * ln:], axis=0, keepdims=True))

    rev = lambda c: nc - 1 - c
    nar = pl.BlockSpec((ln, LANES), lambda c: (rev(c), 0))
    one = pl.BlockSpec((1, LANES), lambda c: (0, 0))
    wide = pl.BlockSpec((ln, w), lambda c: (rev(c), 0))
    bcs = pl.BlockSpec((ln, gn), lambda c: (rev(c), 0))
    acc = pl.BlockSpec((8, LANES), lambda c: (0, 0))
    outs = pl.pallas_call(
        body, name=name, grid=(nc,),
        in_specs=[nar, wide, pl.BlockSpec((ln, gn), lambda c: (rev(c), w // gn)), pl.BlockSpec((ln, gn), lambda c: (rev(c), w // gn + 1)),
                  one, one, pl.BlockSpec((1, w), lambda c: (0, 0)), pl.BlockSpec((1, n, w), lambda c: (rev(c), 0, 0)), wide],
        out_specs=[wide, bcs, bcs, nar, acc, acc, acc],
        out_shape=[jax.ShapeDtypeStruct((t, w), F32), jax.ShapeDtypeStruct((t, gn), F32), jax.ShapeDtypeStruct((t, gn), F32),
                   jax.ShapeDtypeStruct((t, LANES), F32)] + [jax.ShapeDtypeStruct((8, LANES), F32)] * 3,
        scratch_shapes=[pltpu.VMEM((n, w), F32), pltpu.VMEM((ln, w), F32), pltpu.VMEM((ln, w), F32)],
        compiler_params=_params(1))(dtraw, xc, xc, xc, dtb, alog, dskip_x, sprev, dy)
    dx, db, dc, ddt, dbias, dalog, ddsk = outs
    return dx, db, dc, ddt, dbias[0], dalog[0], ddsk[0]


def _rope_tables(t):
    half = MLA_ROPE // 2
    inv_freq = ROPE_THETA ** (-jnp.arange(half, dtype=F32) / half)
    ang = jnp.arange(t, dtype=F32)[:, None] * inv_freq[None, :]
    cos, sin = jnp.cos(ang), jnp.sin(ang)
    pad = LANES - MLA_ROPE
    cos_r = jnp.concatenate([cos, cos, jnp.ones((t, pad), F32)], axis=1)
    sin_r = jnp.concatenate([sin, sin, jnp.zeros((t, pad), F32)], axis=1)
    return cos_r, sin_r


def _rot_matrix():
    half = MLA_ROPE // 2
    i, j = _iota((LANES, LANES), 0), _iota((LANES, LANES), 1)
    neg = (j < half) & (i == j + half)
    pos = (j >= half) & (j < 2 * half) & (i == j - half)
    return (pos.astype(F32) - neg.astype(F32)).astype(BF16)


def _rope(x, cos, sin, every, transpose, out_dtype, name):
    t, w = x.shape
    tr = _rows(t)

    def body(x_ref, c_ref, s_ref, o_ref):
        rot = _rot_matrix()
        cv, sv = c_ref[...], s_ref[...]
        for j in range(w // LANES):
            ls = slice(j * LANES, (j + 1) * LANES)
            xv = x_ref[:, ls]
            if j % every != every - 1:
                o_ref[:, ls] = xv.astype(o_ref.dtype)
            elif transpose:
                o_ref[:, ls] = (xv * cv - _xdot(xv * sv, rot, 2)).astype(o_ref.dtype)
            else:
                o_ref[:, ls] = (xv * cv + _xdot(xv, rot, 2) * sv).astype(o_ref.dtype)

    wide = pl.BlockSpec((tr, w), lambda i: (i, 0))
    tab = pl.BlockSpec((tr, LANES), lambda i: (i, 0))
    return pl.pallas_call(
        body, name=name, grid=(t // tr,), in_specs=[wide, tab, tab], out_specs=wide,
        out_shape=jax.ShapeDtypeStruct((t, w), out_dtype), compiler_params=_params(1))(x, cos, sin)


def _att_masks(blk):
    return _iota((blk, blk), 0), _iota((blk, blk), 1)


def _lanes(j):
    return slice(j * LANES, (j + 1) * LANES)


def _mla_fwd(q, kr, kv, name):
    t = q.shape[0]
    nh, blk = MLA_HEADS, min(ATT_BLK, t)
    scale = (MLA_NOPE + MLA_ROPE) ** -0.5

    def body(q_ref, kv_ref, kr_ref, o_ref, lse_ref):
        i = pl.program_id(1)
        row, col = _att_masks(blk)
        qs = [q_ref[:, 2 * hh * LANES:(2 * hh + 2) * LANES] for hh in range(2)]

        def scores(kb, hh):
            ks = pl.ds(pl.multiple_of(kb * blk, blk), blk)
            kfull = jnp.concatenate([kv_ref[ks, _lanes(2 * hh)], kr_ref[ks, :]], axis=1)
            return _dot(qs[hh], kfull, "nt") * scale, kv_ref[ks, _lanes(2 * hh + 1)]
        init = []
        for hh in range(2):
            s, v = scores(i, hh)
            s = jnp.where(col <= row, s, NEG)
            m = jnp.max(s, axis=1, keepdims=True)
            pr = jnp.exp(s - m)
            init += [m, jnp.sum(pr, axis=1, keepdims=True), _dot(pr.astype(BF16), v)]

        def step(kb, carry):
            sv = [scores(kb, hh) for hh in range(2)]
            out, prs = [], []
            for hh in range(2):
                m, l, acc = carry[3 * hh:3 * hh + 3]
                s = sv[hh][0]
                m2 = jnp.maximum(m, jnp.max(s, axis=1, keepdims=True))
                al = jnp.exp(m - m2)
                pr = jnp.exp(s - m2)
                prs.append(pr.astype(BF16))
                out += [m2, al * l + jnp.sum(pr, axis=1, keepdims=True), al * acc]
            for hh in range(2):
                out[3 * hh + 2] = out[3 * hh + 2] + _dot(prs[hh], sv[hh][1])
            return tuple(out)
        res = lax.fori_loop(0, i, step, tuple(init))
        for hh in range(2):
            m, l, acc = res[3 * hh:3 * hh + 3]
            o_ref[:, _lanes(hh)] = acc / l
            lse_ref[hh] = m + jnp.log(l)

    return pl.pallas_call(
        body, name=name, grid=(nh // 2, t // blk),
        in_specs=[pl.BlockSpec((blk, 4 * LANES), lambda h, i: (i, h)), pl.BlockSpec((t, 4 * LANES), lambda h, i: (0, h)),
                  pl.BlockSpec((t, LANES), lambda h, i: (0, 0))],
        out_specs=[pl.BlockSpec((blk, 2 * LANES), lambda h, i: (i, h)), pl.BlockSpec((2, blk, 1), lambda h, i: (h, i, 0))],
        out_shape=[jax.ShapeDtypeStruct((t, nh * LANES), F32), jax.ShapeDtypeStruct((nh, t, 1), F32)],
        compiler_params=_params(2))(q, kv, kr)


def _mla_bwd(q, kr, kv, o, do, lse, name):
    t = q.shape[0]
    nh, blk = MLA_HEADS, min(ATT_BLK, t)
    scale = (MLA_NOPE + MLA_ROPE) ** -0.5

    def body(q_ref, kv_ref, kr_ref, o_ref, do_ref, lse_ref, dq_ref, dkv_ref, dkr_ref):
        hp, i = pl.program_id(0), pl.program_id(1)

        @pl.when(i == 0)
        def _():
            dkv_ref[...] = jnp.zeros_like(dkv_ref)

        @pl.when((i == 0) & (hp == 0))
        def _():
            dkr_ref[...] = jnp.zeros_like(dkr_ref)
        row, col = _att_masks(blk)
        qs = [q_ref[:, 2 * hh * LANES:(2 * hh + 2) * LANES] for hh in range(2)]
        dobs = [do_ref[:, _lanes(hh)].astype(BF16) for hh in range(2)]
        deltas = [jnp.sum(do_ref[:, _lanes(hh)] * o_ref[:, _lanes(hh)], axis=1, keepdims=True) for hh in range(2)]
        lses = [lse_ref[hh] for hh in range(2)]

        def tile(kb, carry, masked):
            ks = pl.ds(pl.multiple_of(kb * blk, blk), blk)
            krv = kr_ref[ks, :]
            hs = range(2)
            kfull = [jnp.concatenate([kv_ref[ks, _lanes(2 * hh)], krv], axis=1) for hh in hs]
            ss = [_dot(qs[hh], kfull[hh], "nt") for hh in hs]
            dps = [_dot(dobs[hh], kv_ref[ks, _lanes(2 * hh + 1)], "nt") for hh in hs]
            prb, dsb = [], []
            for hh in hs:
                pr = jnp.exp(ss[hh] * scale - lses[hh])
                if masked:
                    pr = jnp.where(col <= row, pr, 0.0)
                prb.append(pr.astype(BF16))
                dsb.append((pr * (dps[hh] - deltas[hh]) * scale).astype(BF16))
            out, dkr = [], None
            for hh in hs:
                dkv_ref[ks, _lanes(2 * hh + 1)] += _dot(prb[hh], dobs[hh], "tn")
                dk = _dot(dsb[hh], qs[hh], "tn")
                dkv_ref[ks, _lanes(2 * hh)] += dk[:, :LANES]
                dkr = dk[:, LANES:] if dkr is None else dkr + dk[:, LANES:]
                out.append(carry[hh] + _dot(dsb[hh], kfull[hh]))
            dkr_ref[ks, :] += dkr
            return tuple(out)
        zero = jnp.zeros((blk, 2 * LANES), F32)
        carry = lax.fori_loop(0, i, lambda kb, c: tile(kb, c, False), (zero,) * 2)
        res = tile(i, carry, True)
        for hh in range(2):
            dq_ref[:, 2 * hh * LANES:(2 * hh + 2) * LANES] = res[hh]

    qb = lambda w: pl.BlockSpec((blk, w * LANES), lambda h, i: (i, h))
    wide = jax.ShapeDtypeStruct((t, nh * 2 * LANES), F32)
    return pl.pallas_call(
        body, name=name, grid=(nh // 2, t // blk),
        in_specs=[qb(4), pl.BlockSpec((t, 4 * LANES), lambda h, i: (0, h)), pl.BlockSpec((t, LANES), lambda h, i: (0, 0)),
                  qb(2), qb(2), pl.BlockSpec((2, blk, 1), lambda h, i: (h, i, 0))],
        out_specs=[qb(4), pl.BlockSpec((t, 4 * LANES), lambda h, i: (0, h)), pl.BlockSpec((t, LANES), lambda h, i: (0, 0))],
        out_shape=[wide, wide, jax.ShapeDtypeStruct((t, LANES), F32)],
        compiler_params=_params(2))(q, kv, kr, o, do, lse)


def _sb_fwd(qkv, name):
    t = qkv.shape[0]
    nh, blk = SB_HEADS, min(ATT_BLK, t)
    hps = math.gcd(nh, SB_FWD_HEADS)
    nq = t // blk
    scale = SB_HEAD_DIM ** -0.5

    def body(q_ref, k_ref, v_ref, o_ref, lt_ref):
        i = pl.program_id(1)
        row, col = _att_masks(blk)
        usuf = (row > col).astype(BF16)
        qs = [q_ref[:, _lanes(hh)] for hh in range(hps)]

        def tile(kb, carry, masked):
            ks = pl.ds(pl.multiple_of(kb * blk, blk), blk)
            hs = range(hps)
            zs = [_dot(qs[hh], k_ref[ks, _lanes(hh)], "nt") for hh in hs]
            lks, lss = [], []
            for hh in hs:
                z = zs[hh] * scale
                lk = -_softplus(z)
                lss.append(lk + z)
                lks.append(jnp.where(col < row, lk, 0.0) if masked else lk)
            pieces = [_split(lks[hh], 2) for hh in hs]
            later = [_dot(pieces[hh][0], usuf) + _dot(pieces[hh][1], usuf) for hh in hs]
            out = []
            for hh in hs:
                wt = jnp.exp(lss[hh] + later[hh] + carry[2 * hh + 1])
                if masked:
                    wt = jnp.where(col < row, wt, 0.0)
                out += [wt.astype(BF16), carry[2 * hh + 1] + jnp.sum(lks[hh], axis=1, keepdims=True)]
            for hh in hs:
                out[2 * hh] = carry[2 * hh] + _dot(out[2 * hh], v_ref[ks, _lanes(hh)])
            return tuple(out)
        za, zr = jnp.zeros((blk, LANES), F32), jnp.zeros((blk, 1), F32)
        carry = tile(i, (za, zr) * hps, True)
        carry = lax.fori_loop(0, i, lambda j, c: tile(i - 1 - j, c, False), carry)
        for hh in range(hps):
            o_ref[:, _lanes(hh)] = carry[2 * hh]
            lt_ref[hh] = carry[2 * hh + 1]

    ng = nh // hps
    full = lambda f: pl.BlockSpec((t, hps * LANES), f)
    return pl.pallas_call(
        body, name=name, grid=(ng, nq),
        in_specs=[pl.BlockSpec((blk, hps * LANES), lambda h, i: (i, h)), full(lambda h, i: (0, ng + h)), full(lambda h, i: (0, 2 * ng + h))],
        out_specs=[pl.BlockSpec((blk, hps * LANES), lambda h, i: (i, h)), pl.BlockSpec((hps, blk, 1), lambda h, i: (h, i, 0))],
        out_shape=[jax.ShapeDtypeStruct((t, nh * LANES), F32), jax.ShapeDtypeStruct((nh, t, 1), F32)],
        compiler_params=_params(2))(qkv, qkv, qkv)


def _sb_bwd(qkv, do, ltot, name):
    t = qkv.shape[0]
    nh, blk = SB_HEADS, min(ATT_BLK, t)
    nq = t // blk
    scale = SB_HEAD_DIM ** -0.5

    def body(q_ref, k_ref, v_ref, do_ref, lt_ref, dq_ref, dk_ref, dv_ref):
        i = pl.program_id(1)

        @pl.when(i == 0)
        def _():
            dk_ref[...] = jnp.zeros_like(dk_ref)
            dv_ref[...] = jnp.zeros_like(dv_ref)
        row, col = _att_masks(blk)
        uinc = (row <= col).astype(BF16)
        uexc = (row < col).astype(BF16)
        qs = [q_ref[:, _lanes(hh)] for hh in range(2)]
        dobs = [do_ref[:, _lanes(hh)].astype(BF16) for hh in range(2)]
        lts = [lt_ref[hh] for hh in range(2)]

        def tile(kb, carry, masked):
            ks = pl.ds(pl.multiple_of(kb * blk, blk), blk)
            hs = range(2)
            kvs = [k_ref[ks, _lanes(hh)] for hh in hs]
            zs = [_dot(qs[hh], kvs[hh], "nt") for hh in hs]
            dws = [_dot(dobs[hh], v_ref[ks, _lanes(hh)], "nt") for hh in hs]
            lks, lss = [], []
            for hh in hs:
                z = zs[hh] * scale
                lk = -_softplus(z)
                lss.append(lk + z)
                lks.append(jnp.where(col < row, lk, 0.0) if masked else lk)
            pieces = [_split(lks[hh], 2) for hh in hs]
            css = [_dot(pieces[hh][0], uinc) + _dot(pieces[hh][1], uinc) for hh in hs]
            wts, evs = [], []
            for hh in hs:
                wt = jnp.exp(lss[hh] + (lts[hh] - (carry[3 * hh + 1] + css[hh])))
                if masked:
                    wt = jnp.where(col < row, wt, 0.0)
                wts.append(wt.astype(BF16))
                evs.append(dws[hh] * wt)
            epieces = [_split(evs[hh], 2) for hh in hs]
            ecss = [_dot(epieces[hh][0], uexc) + _dot(epieces[hh][1], uexc) for hh in hs]
            for hh in hs:
                dv_ref[ks, _lanes(hh)] += _dot(wts[hh], dobs[hh], "tn")
            dzbs = []
            for hh in hs:
                sig = jnp.exp(lss[hh])
                dz = evs[hh] * (1.0 - sig) - (ecss[hh] + carry[3 * hh + 2]) * sig
                if masked:
                    dz = jnp.where(col < row, dz, 0.0)
                dzbs.append((dz * scale).astype(BF16))
            out = []
            for hh in hs:
                dk_ref[ks, _lanes(hh)] += _dot(dzbs[hh], qs[hh], "tn")
                out += [carry[3 * hh] + _dot(dzbs[hh], kvs[hh]), carry[3 * hh + 1] + jnp.sum(lks[hh], axis=1, keepdims=True),
                        carry[3 * hh + 2] + jnp.sum(evs[hh], axis=1, keepdims=True)]
            return tuple(out)
        za, z1 = jnp.zeros((blk, LANES), F32), jnp.zeros((blk, 1), F32)
        carry = lax.fori_loop(0, i, lambda kb, c: tile(kb, c, False), (za, z1, z1, za, z1, z1))
        res = tile(i, carry, True)
        for hh in range(2):
            dq_ref[:, _lanes(hh)] = res[3 * hh]

    full = lambda f: pl.BlockSpec((t, 2 * LANES), f)
    qb = pl.BlockSpec((blk, 2 * LANES), lambda h, i: (i, h))
    wide = jax.ShapeDtypeStruct((t, nh * LANES), F32)
    return pl.pallas_call(
        body, name=name, grid=(nh // 2, nq),
        in_specs=[qb, full(lambda h, i: (0, nh // 2 + h)), full(lambda h, i: (0, nh + h)), qb,
                  pl.BlockSpec((2, blk, 1), lambda h, i: (h, i, 0))],
        out_specs=[qb, full(lambda h, i: (0, h)), full(lambda h, i: (0, h))],
        out_shape=[wide, wide, wide], compiler_params=_params(2))(qkv, qkv, qkv, do, ltot)


def _as2d(a):
    return a.reshape((-1, a.shape[-1]))


def _sum4(parts, name):
    _, r, c = parts.shape
    tr = _tile(r, (256, 128, 64, 32, 16, 8))

    def body(p_ref, o_ref):
        acc = p_ref[0].astype(F32)
        for j in range(1, 4):
            acc = acc + p_ref[j].astype(F32)
        o_ref[...] = acc

    return pl.pallas_call(body, name=name, grid=(r // tr,), in_specs=[pl.BlockSpec((4, tr, c), lambda i: (0, i, 0))],
                          out_specs=pl.BlockSpec((tr, c), lambda i: (i, 0)), out_shape=jax.ShapeDtypeStruct((r, c), F32),
                          compiler_params=_params(1))(parts)


def _adamw(w, ga, gb, m, v, name):
    r, c = w.shape
    tr = _tile(r, (128, 64, 32, 16, 8))
    c1, c2 = 1.0 - ADAM_B1 ** ADAM_STEP, 1.0 - ADAM_B2 ** ADAM_STEP
    two = gb is not None

    def body(*refs):
        w_ref, ga_ref = refs[0], refs[1]
        gb_ref = refs[2] if two else None
        m_ref, v_ref, g_out, d_out, m_out, v_out = refs[2 + two:]
        gv = ga_ref[...]
        if two:
            gv = gv + gb_ref[...]
        mn = ADAM_B1 * m_ref[...] + (1.0 - ADAM_B1) * gv
        vn = ADAM_B2 * v_ref[...] + (1.0 - ADAM_B2) * (gv * gv)
        g_out[...] = gv
        m_out[...] = mn
        v_out[...] = vn
        d_out[...] = -ADAM_LR * ((mn / c1) / (jnp.sqrt(vn / c2) + ADAM_EPS) + ADAM_WD * w_ref[...])

    blk = pl.BlockSpec((tr, c), lambda i: (i, 0))
    ins = [w, ga] + ([gb] if two else []) + [m, v]
    return pl.pallas_call(body, name=name, grid=(r // tr,), in_specs=[blk] * len(ins), out_specs=[blk] * 4,
                          out_shape=[jax.ShapeDtypeStruct((r, c), F32)] * 4, compiler_params=_params(1))(*ins)


HBM_SPEC = pl.BlockSpec(memory_space=pltpu.HBM)
SEM_SPEC = pl.BlockSpec(memory_space=pltpu.SEMAPHORE)
EFFECT = pltpu.SideEffectType.DATAFLOW_SIDE_EFFECTING


def _chip_copies(ins, lands, send_sems, recv_sems, whole):
    x, y, c = lax.axis_index("x"), lax.axis_index("y"), lax.axis_index("c")
    me = 2 * x + y
    out = []
    for wi in range(len(ins)):
        for k, (px, py) in enumerate([(1 - x, y), (x, 1 - y), (1 - x, 1 - y)]):
            sems = dict(send_sem=send_sems[wi * 3 + k], recv_sem=recv_sems[wi * 3 + k], device_id=(px, py, c), device_id_type=MESH_T)
            peer = 2 * px + py
            sent = pltpu.make_async_remote_copy(src_ref=ins[wi] if whole else ins[wi].at[peer], dst_ref=lands[wi].at[me], **sems)
            got = pltpu.make_async_remote_copy(src_ref=ins[wi] if whole else ins[wi].at[me], dst_ref=lands[wi].at[peer], **sems)
            out.append((sent, got))
    return out


def _xstart(arrs, whole, after, name):
    n, na = len(arrs), len(after)
    me = 2 * lax.axis_index("x") + lax.axis_index("y")
    lands = []
    for a in arrs:
        own = a[None] if whole else lax.dynamic_slice_in_dim(a, me, 1, axis=0)
        empty = lax.empty(((4,) + a.shape) if whole else a.shape, a.dtype)
        lands.append(lax.dynamic_update_slice_in_dim(empty, own, me, axis=0))

    def body(*refs):
        ins, lands_in = refs[:n], refs[n:2 * n]
        outs = refs[2 * n + na:]
        for sent, _ in _chip_copies(ins, lands_in, outs[:3 * n], outs[3 * n:6 * n], whole):
            sent.start()
        outs[8 * n][...] = jnp.zeros((8, LANES), F32)

    hbm = lambda a: pltpu.HBM(a.shape, a.dtype)
    res = pl.pallas_call(
        body, name=name,
        out_shape=[pltpu.SemaphoreType.DMA(())] * (6 * n) + [hbm(a) for a in arrs] + [hbm(a) for a in lands]
        + [jax.ShapeDtypeStruct((8, LANES), F32)],
        in_specs=[HBM_SPEC] * (2 * n) + [pl.BlockSpec(memory_space=pl.ANY)] * na,
        out_specs=[SEM_SPEC] * (6 * n) + [HBM_SPEC] * (2 * n) + [pl.BlockSpec(memory_space=pltpu.VMEM)],
        input_output_aliases={i: 6 * n + i for i in range(2 * n)},
        compiler_params=pltpu.CompilerParams(has_side_effects=EFFECT),
    )(*[pltpu.with_memory_space_constraint(a, pltpu.HBM) for a in list(arrs) + lands], *after)
    return res


def _xwait(handle, whole, after, name):
    n = (len(handle) - 1) // 8
    sems, thru = handle[:6 * n], handle[6 * n:8 * n]

    def body(*refs):
        ins, lands_in = refs[:n], refs[n:2 * n]
        for sent, got in _chip_copies(ins, lands_in, refs[2 * n:5 * n], refs[5 * n:8 * n], whole):
            sent.wait_send()
            got.wait_recv()

    hbm = lambda a: pltpu.HBM(a.shape, a.dtype)
    res = pl.pallas_call(
        body, name=name, out_shape=[hbm(a) for a in thru],
        in_specs=[HBM_SPEC] * (2 * n) + [SEM_SPEC] * (6 * n) + [pl.BlockSpec(memory_space=pl.ANY)],
        out_specs=[HBM_SPEC] * (2 * n), input_output_aliases={i: i for i in range(2 * n)},
        compiler_params=pltpu.CompilerParams(has_side_effects=EFFECT),
    )(*thru, *sems, after)
    return res[n:]


def _sibling_copies(ins, lands, send_sems, recv_sems):
    sib = (lax.axis_index("x"), lax.axis_index("y"), 1 - lax.axis_index("c"))
    return [pltpu.make_async_remote_copy(src_ref=ins[wi], dst_ref=lands[wi], send_sem=send_sems[wi], recv_sem=recv_sems[wi],
                                         device_id=sib, device_id_type=MESH_T) for wi in range(len(ins))]


def _sib_start(arrs, after, name):
    n, na = len(arrs), len(after)
    lands = [lax.empty(a.shape, a.dtype) for a in arrs]

    def body(*refs):
        outs = refs[2 * n + na:]
        for cp in _sibling_copies(refs[:n], refs[n:2 * n], outs[:n], outs[n:2 * n]):
            cp.start()
        outs[4 * n][...] = jnp.zeros((8, LANES), F32)

    hbm = lambda a: pltpu.HBM(a.shape, a.dtype)
    return pl.pallas_call(
        body, name=name,
        out_shape=[pltpu.SemaphoreType.DMA(())] * (2 * n) + [hbm(a) for a in arrs] * 2 + [jax.ShapeDtypeStruct((8, LANES), F32)],
        in_specs=[HBM_SPEC] * (2 * n) + [pl.BlockSpec(memory_space=pl.ANY)] * na,
        out_specs=[SEM_SPEC] * (2 * n) + [HBM_SPEC] * (2 * n) + [pl.BlockSpec(memory_space=pltpu.VMEM)],
        input_output_aliases={i: 2 * n + i for i in range(2 * n)},
        compiler_params=pltpu.CompilerParams(has_side_effects=EFFECT),
    )(*[pltpu.with_memory_space_constraint(a, pltpu.HBM) for a in list(arrs) + lands], *after)


def _sib_wait(handle, after, name):
    n = (len(handle) - 1) // 4
    sems, thru = handle[:2 * n], handle[2 * n:4 * n]

    def body(*refs):
        for cp in _sibling_copies(refs[:n], refs[n:2 * n], refs[2 * n:3 * n], refs[3 * n:4 * n]):
            cp.wait_send()
            cp.wait_recv()

    res = pl.pallas_call(
        body, name=name, out_shape=[pltpu.HBM(a.shape, a.dtype) for a in thru],
        in_specs=[HBM_SPEC] * (2 * n) + [SEM_SPEC] * (2 * n) + [pl.BlockSpec(memory_space=pl.ANY)],
        out_specs=[HBM_SPEC] * (2 * n), input_output_aliases={i: i for i in range(2 * n)},
        compiler_params=pltpu.CompilerParams(has_side_effects=EFFECT),
    )(*thru, *sems, after)
    return res[n:]


def _sibling_exchange(arrs, name):
    n = len(arrs)

    def body(*refs):
        ins, outs = refs[:n], refs[n:2 * n]
        send_sems, recv_sems = refs[2 * n:]
        sib = (lax.axis_index("x"), lax.axis_index("y"), 1 - lax.axis_index("c"))
        cps = [pltpu.make_async_remote_copy(src_ref=ins[wi], dst_ref=outs[wi], send_sem=send_sems.at[wi], recv_sem=recv_sems.at[wi],
                                            device_id=sib, device_id_type=MESH_T) for wi in range(n)]
        for cp in cps:
            cp.start()
        for cp in cps:
            cp.wait_recv()
        for cp in cps:
            cp.wait_send()

    anyspec = pl.BlockSpec(memory_space=pl.ANY)
    return pl.pallas_call(
        body, name=name, in_specs=[anyspec] * n, out_specs=[anyspec] * n,
        out_shape=[jax.ShapeDtypeStruct(a.shape, a.dtype) for a in arrs],
        scratch_shapes=[pltpu.SemaphoreType.DMA((n,)), pltpu.SemaphoreType.DMA((n,))],
        compiler_params=pltpu.CompilerParams(has_side_effects=True))(*arrs)


def _allreduce_small(packed, name):
    r = packed.shape[0]

    def body(in_ref, out_ref, land, send_sems, recv_sems):
        x, y, c = lax.axis_index("x"), lax.axis_index("y"), lax.axis_index("c")
        me = 4 * x + 2 * y + c
        land[me] = in_ref[...]
        rel = [(dx, dy, dc) for dx in (0, 1) for dy in (0, 1) for dc in (0, 1)][1:]
        peers = [((1 - x) if dx else x, (1 - y) if dy else y, (1 - c) if dc else c) for dx, dy, dc in rel]
        sends = []
        for k, peer in enumerate(peers):
            cp = pltpu.make_async_remote_copy(src_ref=in_ref, dst_ref=land.at[me], send_sem=send_sems.at[k], recv_sem=recv_sems.at[k],
                                              device_id=peer, device_id_type=MESH_T)
            cp.start()
            sends.append(cp)
        for k, (px, py, pc) in enumerate(peers):
            pltpu.make_async_remote_copy(src_ref=in_ref, dst_ref=land.at[4 * px + 2 * py + pc], send_sem=send_sems.at[k],
                                         recv_sem=recv_sems.at[k], device_id=(px, py, pc), device_id_type=MESH_T).wait_recv()
        for cp in sends:
            cp.wait_send()
        acc = land[0]
        for j in range(1, 8):
            acc = acc + land[j]
        out_ref[...] = acc

    vm = pl.BlockSpec(memory_space=pltpu.VMEM)
    return pl.pallas_call(
        body, name=name, in_specs=[vm], out_specs=vm, out_shape=jax.ShapeDtypeStruct((r, LANES), F32),
        scratch_shapes=[pltpu.VMEM((8, r, LANES), F32), pltpu.SemaphoreType.DMA((7,)), pltpu.SemaphoreType.DMA((7,))],
        compiler_params=pltpu.CompilerParams(has_side_effects=True))(packed)


def _in_splits():
    w = SSD_HEADS * SSD_HEAD_DIM
    cc = w + 2 * SSD_GROUPS * SSD_STATE
    return [w, cc, SSD_HEADS, MLA_Q_RANK, MLA_KV_RANK, MLA_ROPE]


def _padc(a, n):
    return jnp.pad(a, ((0, 0), (0, n - a.shape[1])))


def _win_pad(wm):
    offs = np.cumsum(_in_splits())[:-1]
    z, xbc, dt, cq, ckv, kr = jnp.split(wm, offs, axis=1)
    return jnp.concatenate([z, xbc, cq, ckv, _padc(kr, LANES), _padc(dt, LANES)], axis=1)


def _win_unpad(g):
    w, cc, nh, qr, kvr, rp = _in_splits()
    offs = np.cumsum([w, cc, qr, kvr, LANES])
    z, xbc, cq, ckv, kr, dt = jnp.split(g, offs, axis=1)
    return jnp.concatenate([z, xbc, dt[:, :nh], cq, ckv, kr[:, :rp]], axis=1)


def _wuq_pad(wm):
    r = wm.shape[0]
    w3 = wm.reshape(r, MLA_HEADS, MLA_NOPE + MLA_ROPE)
    return jnp.pad(w3, ((0, 0), (0, 0), (0, 2 * LANES - MLA_NOPE - MLA_ROPE))).reshape(r, MLA_HEADS * 2 * LANES)


def _wuq_unpad(g):
    r = g.shape[0]
    return g.reshape(r, MLA_HEADS, 2 * LANES)[:, :, :MLA_NOPE + MLA_ROPE].reshape(r, MLA_HEADS * (MLA_NOPE + MLA_ROPE))


def _full_from_gather(name, g):
    col = name in COL_SHARDED
    layers = g.shape[1]
    return [jnp.concatenate([g[j, l] for j in range(4)], axis=1 if col else 0) for l in range(layers)]


def _shards_from_full(name, mats):
    col = name in COL_SHARDED
    per = []
    for mt in mats:
        r, c = mt.shape
        per.append(mt.reshape(r, 4, c // 4).transpose(1, 0, 2) if col else mt.reshape(4, r // 4, c))
    return jnp.stack(per, axis=1).astype(BF16)


def _pack_small(vals, extra=None):
    flat = jnp.concatenate([vals[n].reshape(-1).astype(F32) for n in SMALL] + ([extra.reshape(1)] if extra is not None else []))
    rows = -(-flat.shape[0] // (8 * LANES)) * 8
    return jnp.pad(flat, (0, rows * LANES - flat.shape[0])).reshape(rows, LANES)


def _unpack_small(packed, like):
    flat, out, off = packed.reshape(-1), {}, 0
    for n in SMALL:
        sz = like[n].size
        out[n] = flat[off:off + sz].reshape(like[n].shape)
        off += sz
    return out


def _ffn_fwd(x, norm_g, wg, wu, wd, tag):
    h, r = _rms_fwd(x, norm_g, f"ffn_norm_{tag}")
    gate = _mm(h, wg, "nn", f"ffn_gate_{tag}", out_dtype=BF16)
    up = _mm(h, wu, "nn", f"ffn_up_{tag}", out_dtype=BF16)
    act = _swiglu_fwd(gate, up, f"swiglu_{tag}")
    out = _mm(act, wd, "nn", f"ffn_down_{tag}", add=x)
    return out, (x, r, h, gate, up, act)


def _ffn_bwd(dout, doutb, norm_g, wg, wu, wd, saved, tag):
    x, r, h, gate, up, act = saved
    dact = _mm(doutb, wd.T, "nn", f"ffn_dact_{tag}", out_dtype=BF16)
    dgate, dup = _swiglu_bwd(gate, up, dact, f"swiglu_bwd_{tag}")
    ht = h.T
    d_wd = _mm(act.T, doutb, "nn", f"ffn_dwd_{tag}", out_dtype=BF16)
    d_wg = _mm(ht, dgate, "nn", f"ffn_dwg_{tag}", out_dtype=BF16)
    d_wu = _mm(ht, dup, "nn", f"ffn_dwu_{tag}", out_dtype=BF16)
    dh = _mm(dgate, wg.T, "nn", f"ffn_dh1_{tag}")
    dh = _mm(dup, wu.T, "nn", f"ffn_dh2_{tag}", add=dh)
    dx, dxb, dnorm = _rms_bwd(x, norm_g, r, dh, f"ffn_norm_bwd_{tag}", dres=dout)
    return dx, dxb, dnorm, d_wg, d_wu, d_wd


def kernel(x, mix_norm, ffn_norm, w_in, conv_w, conv_b, dt_bias, a_log, d_skip, ssd_norm, q_norm, kv_norm, w_uq, w_ukv, w_out_even, w_qkv, w_out_odd, w_gate, w_up, w_down, final_norm, loss_target, m_mix_norm, m_ffn_norm, m_w_in, m_conv_w, m_conv_b, m_dt_bias, m_a_log, m_d_skip, m_ssd_norm, m_q_norm, m_kv_norm, m_w_uq, m_w_ukv, m_w_out_even, m_w_qkv, m_w_out_odd, m_w_gate, m_w_up, m_w_down, m_final_norm, v_mix_norm, v_ffn_norm, v_w_in, v_conv_w, v_conv_b, v_dt_bias, v_a_log, v_d_skip, v_ssd_norm, v_q_norm, v_kv_norm, v_w_uq, v_w_ukv, v_w_out_even, v_w_qkv, v_w_out_odd, v_w_gate, v_w_up, v_w_down, v_final_norm):
    given = dict(locals())
    wts = {n: given[n] for n in WEIGHTS}
    x0 = x[0]
    tgt = loss_target[0]
    t, d = x0.shape
    hw = SSD_HEADS * SSD_HEAD_DIM
    gn = SSD_GROUPS * SSD_STATE
    cc = hw + 2 * gn
    qr, kvr = MLA_Q_RANK, MLA_KV_RANK

    def shard(n, layer=None):
        a = wts[n] if layer is None else wts[n][layer:layer + 1]
        return a if n == 'conv_w' else a.astype(BF16)
    g0_names = ['w_in', 'conv_w']
    gq_names = ['w_uq', 'w_ukv']
    g1_names = [('w_out_even', None), ('w_gate', 0), ('w_up', 0), ('w_down', 0)]
    g2_names = [('w_qkv', None), ('w_out_odd', None), ('w_gate', 1), ('w_up', 1), ('w_down', 1)]
    hg0 = _xstart([shard(n) for n in g0_names], True, [], "gather0_start")
    hgq = _xstart([shard(n) for n in gq_names], True, [hg0[-1]], "gatherq_start")
    hg1 = _xstart([shard(n, l) for n, l in g1_names], True, [hgq[-1]], "gather1_start")
    hg2 = _xstart([shard(n, l) for n, l in g2_names], True, [hg1[-1]], "gather2_start")
    full = lambda n, g: _full_from_gather(n, g)[0]
    g0 = dict(zip(g0_names, _xwait(hg0, True, hg2[-1], "gather0_wait")))
    win = _win_pad(full('w_in', g0['w_in']))
    cw = _padc(full('conv_w', g0['conv_w']).T, 8).T
    o_cq, o_ckv, o_kr, o_dt = hw + cc, hw + cc + qr, hw + cc + qr + kvr, hw + cc + qr + kvr + LANES

    row = lambda v: v.reshape(1, -1)
    narrow = lambda v: _padc(v.reshape(1, -1), LANES)
    dtb, alog = narrow(dt_bias[0]), narrow(a_log[0])
    dsk_x = jnp.repeat(d_skip[0], SSD_HEAD_DIM).reshape(1, hw)
    cos, sin = _rope_tables(t)

    h0, r0 = _rms_fwd(x0, row(mix_norm[0]), "mix_norm_0")
    u = _mm(h0, win, "nn", "in_proj")
    z, xbc, c_q, c_kv = u[:, :hw], u[:, hw:hw + cc], u[:, o_cq:o_ckv], u[:, o_ckv:o_kr]
    kr_raw, dtraw = u[:, o_kr:o_dt], u[:, o_dt:]
    xc = _conv_fwd(xbc, cw, row(conv_b[0]), "conv")
    y_ssd, sprev = _ssd_fwd(dtraw, xc, dtb, alog, dsk_x, "ssd")
    yg, r_g = _gated_fwd(y_ssd, z, row(ssd_norm[0]), "ssd_gate_norm")
    q_lat, r_q = _rms_fwd(c_q, row(q_norm[0]), "q_norm")
    kv_lat, r_kv = _rms_fwd(c_kv, row(kv_norm[0]), "kv_norm")
    gq = dict(zip(gq_names, _xwait(hgq, True, r_kv, "gatherq_wait")))
    wuq = _wuq_pad(full('w_uq', gq['w_uq']))
    wukv = full('w_ukv', gq['w_ukv'])
    qf = _mm(q_lat, wuq, "nn", "q_up")
    kvb = _mm(kv_lat, wukv, "nn", "kv_up", out_dtype=BF16)
    q_r = _rope(qf, cos, sin, 2, False, BF16, "rope_q")
    k_r = _rope(kr_raw, cos, sin, 1, False, BF16, "rope_k")
    o_mla, lse = _mla_fwd(q_r, k_r, kvb, "mla")
    g1 = dict(zip(g1_names, _xwait(hg1, True, lse, "gather1_wait")))
    woe, wg0, wu0, wd0 = [full(k[0], g1[k]) for k in g1_names]
    cat = jnp.concatenate([yg, o_mla.astype(BF16)], axis=1)
    x1 = _mm(cat, woe, "nn", "mix_out_0", add=x0)
    x2, ffn0 = _ffn_fwd(x1, row(ffn_norm[0]), wg0, wu0, wd0, "0")

    g2 = dict(zip(g2_names, _xwait(hg2, True, x2, "gather2_wait")))
    wqkv, woo, wg1, wu1, wd1 = [full(k[0], g2[k]) for k in g2_names]
    h1, r1 = _rms_fwd(x2, row(mix_norm[1]), "mix_norm_1")
    qkv = _mm(h1, wqkv, "nn", "qkv_proj", out_dtype=BF16)
    o_sb, ltot = _sb_fwd(qkv, "sb")
    o_sbb = o_sb.astype(BF16)
    x3 = _mm(o_sbb, woo, "nn", "mix_out_1", add=x2)
    x4, ffn1 = _ffn_fwd(x3, row(ffn_norm[1]), wg1, wu1, wd1, "1")

    loss_part, dx4, dx4b, d_final = _loss_fwd_bwd(x4, row(final_norm), tgt, "loss")

    def grad_shards(pairs):
        return [_shards_from_full(n, [g]) for n, g in pairs]

    dx3, dx3b, d_ffn1, d_wg1, d_wu1, d_wd1 = _ffn_bwd(dx4, dx4b, row(ffn_norm[1]), wg1, wu1, wd1, ffn1, "1")
    do_sb = _mm(dx3b, woo.T, "nn", "sb_dout")
    d_woo = _mm(o_sbb.T, dx3b, "nn", "d_w_out_odd", out_dtype=BF16)
    dq, dk, dv = _sb_bwd(qkv, do_sb, ltot, "sb_bwd")
    dqkv = jnp.concatenate([dq, dk, dv], axis=1).astype(BF16)
    d_wqkv = _mm(h1.T, dqkv, "nn", "d_w_qkv", out_dtype=BF16)
    x2_names = ['w_qkv', 'w_out_odd', 'w_gate', 'w_up', 'w_down']
    hx2 = _xstart(grad_shards(zip(x2_names, [d_wqkv, d_woo, d_wg1, d_wu1, d_wd1])), False, [], "grads2_start")
    dh1 = _mm(dqkv, wqkv.T, "nn", "qkv_dh")
    dx2, dx2b, d_mix1 = _rms_bwd(x2, row(mix_norm[1]) + hx2[-1][0:1, 0:1], r1, dh1, "mix_norm_bwd_1", dres=dx3)

    dx1, dx1b, d_ffn0, d_wg0, d_wu0, d_wd0 = _ffn_bwd(dx2, dx2b, row(ffn_norm[0]), wg0, wu0, wd0, ffn0, "0")
    d_woe = _mm(cat.T, dx1b, "nn", "d_w_out_even", out_dtype=BF16)
    x1_names = ['w_out_even', 'w_gate', 'w_up', 'w_down']
    hx1 = _xstart(grad_shards(zip(x1_names, [d_woe, d_wg0, d_wu0, d_wd0])), False, [], "grads1_start")
    tok1 = hx1[-1][0:1, 0:1]
    dcat = _mm(dx1b, woe.T, "nn", "mix_dcat")
    dy_ssd, dz, d_ssdn = _gated_bwd(y_ssd, z, row(ssd_norm[0]) + tok1, r_g, dcat[:, :hw], "ssd_gate_norm_bwd")
    dxs, db_, dc_, ddtraw, d_dtb, d_alog, d_dsk = _ssd_bwd(dtraw, xc, dtb, alog, dsk_x, sprev, dy_ssd, "ssd_bwd")
    dxbc, d_cw, d_cb = _conv_bwd(xbc, cw, row(conv_b[0]), jnp.concatenate([dxs, db_, dc_], axis=1), "conv_bwd")
    dqm, dkvm, dkr = _mla_bwd(q_r, k_r, kvb, o_mla, dcat[:, hw:], lse + tok1, "mla_bwd")
    dqf = _rope(dqm, cos, sin, 2, True, BF16, "rope_q_bwd")
    dkr_raw = _rope(dkr, cos, sin, 1, True, F32, "rope_k_bwd")
    dkvf = dkvm.astype(BF16)
    d_wuq = _mm(q_lat.T, dqf, "nn", "d_w_uq", out_dtype=BF16)
    d_wukv = _mm(kv_lat.T, dkvf, "nn", "d_w_ukv", out_dtype=BF16)
    dq_lat = _mm(dqf, wuq.T, "nn", "q_up_bwd")
    dkv_lat = _mm(dkvf, wukv.T, "nn", "kv_up_bwd")
    dc_q, _, d_qn = _rms_bwd(c_q, row(q_norm[0]), r_q, dq_lat, "q_norm_bwd")
    dc_kv, _, d_kvn = _rms_bwd(c_kv, row(kv_norm[0]), r_kv, dkv_lat, "kv_norm_bwd")
    du = jnp.concatenate([dz, dxbc, dc_q, dc_kv, dkr_raw, ddtraw], axis=1).astype(BF16)
    d_win = _mm(h0.T, du, "nn", "d_w_in", out_dtype=BF16)
    x0_names = ['w_in', 'conv_w', 'w_uq', 'w_ukv']
    hx0 = _xstart(grad_shards(zip(x0_names, [_win_unpad(d_win), d_cw, _wuq_unpad(d_wuq), d_wukv])), False, [], "grads0_start")
    dh0 = _mm(du, win.T, "nn", "in_proj_bwd")
    grad_x, _, d_mix0 = _rms_bwd(x0, row(mix_norm[0]) + hx0[-1][0:1, 0:1], r0, dh0, "mix_norm_bwd_0", dres=dx1)

    def chip_sums(handle, names, after, tag):
        lands = _xwait(handle, False, after, f"grads{tag}_wait")
        return {n: _sum4(p.reshape(4, -1, p.shape[-1]), f"sum{tag}_{n}") for n, p in zip(names, lands)}
    grads, deltas, new_m, new_v = {}, {}, {}, {}

    def adamw(n, ga, gb):
        res = _adamw(_as2d(wts[n]), ga, gb, _as2d(given['m_' + n]), _as2d(given['v_' + n]), f"adamw_{n}")
        grads[n], deltas[n], new_m[n], new_v[n] = [r.reshape(wts[n].shape) for r in res]
        return res[0]
    s2 = chip_sums(hx2, x2_names, grad_x, "2")
    hs2 = _sib_start([s2[n] for n in x2_names], [], "sibling2_start")
    s1 = chip_sums(hx1, x1_names, hs2[-1], "1")
    hs1 = _sib_start([s1[n] for n in x1_names], [], "sibling1_start")
    r2 = dict(zip(x2_names, _sib_wait(hs2, hs1[-1], "sibling2_wait")))
    r1 = dict(zip(x1_names, _sib_wait(hs1, r2['w_qkv'], "sibling1_wait")))
    last = None
    for n in ('w_qkv', 'w_out_odd'):
        last = adamw(n, s2[n], r2[n])
    last = adamw('w_out_even', s1['w_out_even'], r1['w_out_even'])
    for n in ('w_gate', 'w_up', 'w_down'):
        last = adamw(n, jnp.concatenate([s1[n], s2[n]], axis=0), jnp.concatenate([r1[n], r2[n]], axis=0))
    s0 = chip_sums(hx0, x0_names, last, "0")
    r0_ = _sibling_exchange([s0[n] for n in x0_names], "exchange_sibling0")
    for n, gb in zip(x0_names, r0_):
        adamw(n, s0[n], gb)

    nhs = SSD_HEADS
    small_g = {'mix_norm': jnp.stack([d_mix0, d_mix1]), 'ffn_norm': jnp.stack([d_ffn0, d_ffn1]), 'conv_b': d_cb[None],
               'dt_bias': d_dtb[None, :nhs], 'a_log': d_alog[None, :nhs], 'd_skip': d_dsk[None, :nhs], 'ssd_norm': d_ssdn[None],
               'q_norm': d_qn[None], 'kv_norm': d_kvn[None], 'final_norm': d_final}
    packed, _ = lax.optimization_barrier((_pack_small(small_g, loss_part), r0_[0]))
    g_small = _allreduce_small(packed, "allreduce_small")
    loss = g_small.reshape(-1)[sum(wts[n].size for n in SMALL)]
    pk = lambda pre: _pack_small({n: given[pre + n] for n in SMALL}, jnp.zeros((), F32))
    sg, sd, sm, sv = _adamw(pk(''), g_small, None, pk('m_'), pk('v_'), "adamw_small")
    for dst, src in ((grads, sg), (deltas, sd), (new_m, sm), (new_v, sv)):
        dst.update(_unpack_small(src, wts))

    outs = [loss, grad_x[None]]
    for dct in (grads, deltas, new_m, new_v):
        outs += [dct[n] for n in WEIGHTS]
    return tuple(outs)
```

```python
import functools
import math

import jax
import jax.numpy as jnp
import numpy as np
from jax import lax
from jax.experimental import pallas as pl
from jax.experimental.pallas import tpu as pltpu

F32, BF16 = jnp.float32, jnp.bfloat16

RMS_EPS = 1e-6
SSD_HEADS, SSD_HEAD_DIM, SSD_GROUPS, SSD_STATE, SSD_CONV, SSD_CHUNK = 32, 64, 4, 128, 4, 128
MLA_HEADS, MLA_Q_RANK, MLA_KV_RANK, MLA_NOPE, MLA_ROPE, MLA_V = 16, 512, 512, 128, 64, 128
ROPE_THETA = 10000.0
SB_HEADS, SB_HEAD_DIM = 16, 128
ADAM_LR, ADAM_B1, ADAM_B2, ADAM_EPS, ADAM_WD, ADAM_STEP = 0.001, 0.9, 0.999, 1e-08, 0.01, 10

LANES = 128
VMEM_LIMIT_BYTES = 56 * 1024 * 1024
MM_VMEM_BUDGET = 40 * 1024 * 1024
ATT_BLK = 256
SB_FWD_HEADS = 4
SB_BWD_HEADS = 4
MLA_FWD_HEADS = 4
ROW_TILE = 256
NEG = -1e30

MESH_T = pl.DeviceIdType.MESH
WEIGHTS = ['mix_norm', 'ffn_norm', 'w_in', 'conv_w', 'conv_b', 'dt_bias', 'a_log', 'd_skip', 'ssd_norm', 'q_norm',
           'kv_norm', 'w_uq', 'w_ukv', 'w_out_even', 'w_qkv', 'w_out_odd', 'w_gate', 'w_up', 'w_down', 'final_norm']
SHARDED = ['w_in', 'conv_w', 'w_uq', 'w_ukv', 'w_out_even', 'w_qkv', 'w_out_odd', 'w_gate', 'w_up', 'w_down']
COL_SHARDED = ['w_in', 'conv_w', 'w_uq', 'w_ukv', 'w_qkv', 'w_gate', 'w_up']
SMALL = [n for n in WEIGHTS if n not in SHARDED]


def _tile(n, cands):
    for c in cands:
        if n % c == 0:
            return c
    return n


def _params(ngrid):
    return pltpu.CompilerParams(dimension_semantics=("arbitrary",) * ngrid, vmem_limit_bytes=VMEM_LIMIT_BYTES)


def _dot(a, b, mode="nn"):
    dims = {"nn": (((1,), (0,)), ((), ())), "nt": (((1,), (1,)), ((), ())), "tn": (((0,), (0,)), ((), ()))}[mode]
    return lax.dot_general(a, b, dims, preferred_element_type=F32)


def _split(x, parts):
    out, r = [], x
    for _ in range(parts):
        p = r.astype(BF16)
        out.append(p)
        r = r - p.astype(F32)
    return out


def _xdot(x, e, parts=3):
    acc = None
    for p in _split(x, parts):
        t = _dot(p, e)
        acc = t if acc is None else acc + t
    return acc


def _xdot_l(e, x, parts=3):
    acc = None
    for p in _split(x, parts):
        t = _dot(e, p)
        acc = t if acc is None else acc + t
    return acc


def _iota(shape, dim):
    return lax.broadcasted_iota(jnp.int32, shape, dim)


def _softplus(z):
    return jnp.maximum(z, 0.0) + jnp.log(1.0 + jnp.exp(-jnp.abs(z)))


def _sigmoid(z):
    return 1.0 / (1.0 + jnp.exp(-z))


def _acc_rows(ref, first, val):
    @pl.when(first)
    def _():
        ref[...] = jnp.zeros_like(ref)
    ref[...] += jnp.broadcast_to(val, ref.shape)


def _mm(a, b, mode, name, out_dtype=F32, add=None):
    if mode == "nn":
        (m, k), n = a.shape, b.shape[1]
    elif mode == "nt":
        (m, k), n = a.shape, b.shape[0]
    else:
        (k, m), n = a.shape, b.shape[1]
    tn = _tile(n, (1536, 1408, 1280, 1024, 768, 640, 512, 256, 128))
    tk = k if k <= 2048 else _tile(k, (2048, 1536, 1408, 1280, 1024, 512, 256, 128))
    ob = jnp.dtype(out_dtype).itemsize

    def need(tm_):
        per = tm_ * tk * a.dtype.itemsize + tk * tn * b.dtype.itemsize + tm_ * tn * ob + (tm_ * tn * 4 if add is not None else 0)
        return 2 * per + (tm_ * tn * 4 if k > tk else 0)
    tm = m
    for cand in (1408, 1024, 512, 256, 128):
        if m % cand == 0:
            tm = cand
            if need(cand) <= MM_VMEM_BUDGET:
                break
    nk = k // tk
    a_spec = {"nn": pl.BlockSpec((tm, tk), lambda i, j, kk: (i, kk)), "nt": pl.BlockSpec((tm, tk), lambda i, j, kk: (i, kk)),
              "tn": pl.BlockSpec((tk, tm), lambda i, j, kk: (kk, i))}[mode]
    b_spec = {"nn": pl.BlockSpec((tk, tn), lambda i, j, kk: (kk, j)), "nt": pl.BlockSpec((tn, tk), lambda i, j, kk: (j, kk)),
              "tn": pl.BlockSpec((tk, tn), lambda i, j, kk: (kk, j))}[mode]
    o_spec = pl.BlockSpec((tm, tn), lambda i, j, kk: (i, j))
    has_add = add is not None

    def body(*refs):
        a_ref, b_ref = refs[0], refs[1]
        add_ref = refs[2] if has_add else None
        o_ref = refs[2 + has_add]
        part = _dot(a_ref[...].astype(BF16), b_ref[...].astype(BF16), mode)
        if nk == 1:
            if has_add:
                part = part + add_ref[...]
            o_ref[...] = part.astype(o_ref.dtype)
            return
        acc_ref = refs[3 + has_add]
        kk = pl.program_id(2)

        @pl.when(kk == 0)
        def _():
            acc_ref[...] = jnp.zeros_like(acc_ref)
        acc_ref[...] += part

        @pl.when(kk == nk - 1)
        def _():
            r = acc_ref[...]
            if has_add:
                r = r + add_ref[...]
            o_ref[...] = r.astype(o_ref.dtype)

    ins = [a, b] + ([add] if has_add else [])
    specs = [a_spec, b_spec] + ([o_spec] if has_add else [])
    return pl.pallas_call(
        body, name=name, grid=(m // tm, n // tn, nk), in_specs=specs, out_specs=o_spec,
        out_shape=jax.ShapeDtypeStruct((m, n), out_dtype),
        scratch_shapes=[pltpu.VMEM((tm, tn), F32)] if nk > 1 else [],
        compiler_params=_params(3))(*ins)


def _rows(t):
    return _tile(t, (ROW_TILE, 128, 64, 32, 16, 8))


def _rms_fwd(x, g, name):
    t, c = x.shape
    tr = _rows(t)

    def body(x_ref, g_ref, h_ref, r_ref):
        xv = x_ref[...]
        r = lax.rsqrt(jnp.mean(xv * xv, axis=-1, keepdims=True) + RMS_EPS)
        h_ref[...] = (xv * r * g_ref[...]).astype(h_ref.dtype)
        r_ref[...] = r

    return pl.pallas_call(
        body, name=name, grid=(t // tr,),
        in_specs=[pl.BlockSpec((tr, c), lambda i: (i, 0)), pl.BlockSpec((1, c), lambda i: (0, 0))],
        out_specs=[pl.BlockSpec((tr, c), lambda i: (i, 0)), pl.BlockSpec((tr, 1), lambda i: (i, 0))],
        out_shape=[jax.ShapeDtypeStruct((t, c), BF16), jax.ShapeDtypeStruct((t, 1), F32)],
        compiler_params=_params(1))(x, g)


def _rms_bwd(x, g, r, dh, name, dres=None):
    t, c = x.shape
    tr = _rows(t)
    has_res = dres is not None

    def body(*refs):
        x_ref, g_ref, r_ref, dh_ref = refs[:4]
        res_ref = refs[4] if has_res else None
        dx_ref, dxb_ref, dg_ref = refs[4 + has_res:]
        rv = r_ref[...]
        xh = x_ref[...] * rv
        dhv = dh_ref[...]
        dxh = dhv * g_ref[...]
        cm = jnp.mean(dxh * xh, axis=-1, keepdims=True)
        dx = (dxh - xh * cm) * rv
        if has_res:
            dx = dx + res_ref[...]
        dx_ref[...] = dx
        dxb_ref[...] = dx.astype(BF16)
        _acc_rows(dg_ref, pl.program_id(0) == 0, jnp.sum(dhv * xh, axis=0, keepdims=True))

    row = pl.BlockSpec((tr, c), lambda i: (i, 0))
    ins = [x, g, r, dh] + ([dres] if has_res else [])
    specs = [row, pl.BlockSpec((1, c), lambda i: (0, 0)), pl.BlockSpec((tr, 1), lambda i: (i, 0)), row] + ([row] if has_res else [])
    dx, dxb, dg = pl.pallas_call(
        body, name=name, grid=(t // tr,), in_specs=specs,
        out_specs=[row, row, pl.BlockSpec((8, c), lambda i: (0, 0))],
        out_shape=[jax.ShapeDtypeStruct((t, c), F32), jax.ShapeDtypeStruct((t, c), BF16), jax.ShapeDtypeStruct((8, c), F32)],
        compiler_params=_params(1))(*ins)
    return dx, dxb, dg[0]


def _gated_fwd(y, z, g, name):
    t, c = y.shape
    tr = _rows(t)

    def body(y_ref, z_ref, g_ref, o_ref, r_ref):
        zv = z_ref[...]
        v = y_ref[...] * zv * _sigmoid(zv)
        r = lax.rsqrt(jnp.mean(v * v, axis=-1, keepdims=True) + RMS_EPS)
        o_ref[...] = (v * r * g_ref[...]).astype(o_ref.dtype)
        r_ref[...] = r

    row = pl.BlockSpec((tr, c), lambda i: (i, 0))
    return pl.pallas_call(
        body, name=name, grid=(t // tr,), in_specs=[row, row, pl.BlockSpec((1, c), lambda i: (0, 0))],
        out_specs=[row, pl.BlockSpec((tr, 1), lambda i: (i, 0))],
        out_shape=[jax.ShapeDtypeStruct((t, c), BF16), jax.ShapeDtypeStruct((t, 1), F32)],
        compiler_params=_params(1))(y, z, g)


def _gated_bwd(y, z, g, r, dout, name):
    t, c = y.shape
    tr = _rows(t)

    def body(y_ref, z_ref, g_ref, r_ref, do_ref, dy_ref, dz_ref, dg_ref):
        yv, zv, rv, dov = y_ref[...], z_ref[...], r_ref[...], do_ref[...]
        s = _sigmoid(zv)
        sz = zv * s
        xh = yv * sz * rv
        dxh = dov * g_ref[...]
        cm = jnp.mean(dxh * xh, axis=-1, keepdims=True)
        dv = (dxh - xh * cm) * rv
        dy_ref[...] = dv * sz
        dz_ref[...] = dv * yv * s * (1.0 + zv * (1.0 - s))
        _acc_rows(dg_ref, pl.program_id(0) == 0, jnp.sum(dov * xh, axis=0, keepdims=True))

    row = pl.BlockSpec((tr, c), lambda i: (i, 0))
    dy, dz, dg = pl.pallas_call(
        body, name=name, grid=(t // tr,),
        in_specs=[row, row, pl.BlockSpec((1, c), lambda i: (0, 0)), pl.BlockSpec((tr, 1), lambda i: (i, 0)), row],
        out_specs=[row, row, pl.BlockSpec((8, c), lambda i: (0, 0))],
        out_shape=[jax.ShapeDtypeStruct((t, c), F32), jax.ShapeDtypeStruct((t, c), F32), jax.ShapeDtypeStruct((8, c), F32)],
        compiler_params=_params(1))(y, z, g, r, dout)
    return dy, dz, dg[0]


def _swiglu_fwd(gate, up, name):
    t, c = gate.shape
    tr, tc = _rows(t), _tile(c, (2816, 1408, 1024, 512, 256, 128))

    def body(g_ref, u_ref, o_ref):
        gv = g_ref[...].astype(F32)
        o_ref[...] = (gv * _sigmoid(gv) * u_ref[...].astype(F32)).astype(o_ref.dtype)

    blk = pl.BlockSpec((tr, tc), lambda i, j: (i, j))
    return pl.pallas_call(body, name=name, grid=(t // tr, c // tc), in_specs=[blk, blk], out_specs=blk,
                          out_shape=jax.ShapeDtypeStruct((t, c), BF16), compiler_params=_params(2))(gate, up)


def _swiglu_bwd(gate, up, dact, name):
    t, c = gate.shape
    tr, tc = _rows(t), _tile(c, (2816, 1408, 1024, 512, 256, 128))

    def body(g_ref, u_ref, d_ref, dg_ref, du_ref):
        gv, dv = g_ref[...].astype(F32), d_ref[...].astype(F32)
        s = _sigmoid(gv)
        dg_ref[...] = (dv * u_ref[...].astype(F32) * s * (1.0 + gv * (1.0 - s))).astype(dg_ref.dtype)
        du_ref[...] = (dv * gv * s).astype(du_ref.dtype)

    blk = pl.BlockSpec((tr, tc), lambda i, j: (i, j))
    return pl.pallas_call(body, name=name, grid=(t // tr, c // tc), in_specs=[blk, blk, blk], out_specs=[blk, blk],
                          out_shape=[jax.ShapeDtypeStruct((t, c), BF16)] * 2, compiler_params=_params(2))(gate, up, dact)


def _loss_fwd_bwd(x, g, tgt, name):
    t, c = x.shape
    tr = _rows(t)

    def body(x_ref, g_ref, t_ref, l_ref, dx_ref, dxb_ref, dg_ref):
        xv, gv = x_ref[...], g_ref[...]
        r = lax.rsqrt(jnp.mean(xv * xv, axis=-1, keepdims=True) + RMS_EPS)
        xh = xv * r
        err = xh * gv - t_ref[...]
        per_row = jnp.mean(err * err, axis=-1, keepdims=True)
        dy = err * (1.0 / c)
        dxh = dy * gv
        cm = jnp.mean(dxh * xh, axis=-1, keepdims=True)
        dx = (dxh - xh * cm) * r
        dx_ref[...] = dx
        dxb_ref[...] = dx.astype(BF16)
        first = pl.program_id(0) == 0
        _acc_rows(dg_ref, first, jnp.sum(dy * xh, axis=0, keepdims=True))
        _acc_rows(l_ref, first, jnp.broadcast_to(0.5 * jnp.sum(per_row, axis=0, keepdims=True), (1, LANES)))

    row = pl.BlockSpec((tr, c), lambda i: (i, 0))
    lo, dx, dxb, dg = pl.pallas_call(
        body, name=name, grid=(t // tr,), in_specs=[row, pl.BlockSpec((1, c), lambda i: (0, 0)), row],
        out_specs=[pl.BlockSpec((8, LANES), lambda i: (0, 0)), row, row, pl.BlockSpec((8, c), lambda i: (0, 0))],
        out_shape=[jax.ShapeDtypeStruct((8, LANES), F32), jax.ShapeDtypeStruct((t, c), F32), jax.ShapeDtypeStruct((t, c), BF16),
                   jax.ShapeDtypeStruct((8, c), F32)],
        compiler_params=_params(1))(x, g, tgt)
    return lo[0, 0], dx, dxb, dg[0]


def _conv_specs(t, c):
    tr = _rows(t)
    tc = _tile(c, (1024, 768, 512, 256, 128))
    h8 = tr // 8
    tile = pl.BlockSpec((tr, tc), lambda j, i: (i, j))
    prev = pl.BlockSpec((8, tc), lambda j, i: (jnp.maximum(i * h8 - 1, 0), j))
    nxt = pl.BlockSpec((8, tc), lambda j, i: (jnp.minimum((i + 1) * h8, t // 8 - 1), j))
    return tr, tc, tile, prev, nxt


def _conv_fwd(xbc, w, b, name):
    t, c = xbc.shape
    tr, tc, tile, prev, _ = _conv_specs(t, c)

    def body(x_ref, p_ref, w_ref, b_ref, o_ref, buf):
        i = pl.program_id(1)
        buf[0:8, :] = jnp.where(i > 0, p_ref[...], 0.0)
        buf[8:, :] = x_ref[...]
        pre = b_ref[...]
        for k in range(SSD_CONV):
            pre = pre + w_ref[k:k + 1, :] * buf[pl.ds(8 - (SSD_CONV - 1) + k, tr), :]
        o_ref[...] = pre * _sigmoid(pre)

    return pl.pallas_call(
        body, name=name, grid=(c // tc, t // tr),
        in_specs=[tile, prev, pl.BlockSpec((8, tc), lambda j, i: (0, j)), pl.BlockSpec((1, tc), lambda j, i: (0, j))],
        out_specs=tile, out_shape=jax.ShapeDtypeStruct((t, c), F32),
        scratch_shapes=[pltpu.VMEM((tr + 8, tc), F32)], compiler_params=_params(2))(xbc, xbc, w, b)


def _conv_bwd(xbc, w, b, dout, name):
    t, c = xbc.shape
    tr, tc, tile, prev, nxt = _conv_specs(t, c)
    nt = t // tr
    kc = SSD_CONV

    def body(x_ref, p_ref, n_ref, w_ref, b_ref, d_ref, dn_ref, dx_ref, dw_ref, db_ref, buf, dbuf):
        i = pl.program_id(1)
        last = i == nt - 1
        buf[0:8, :] = jnp.where(i > 0, p_ref[...], 0.0)
        buf[8:8 + tr, :] = x_ref[...]
        buf[8 + tr:, :] = jnp.where(last, 0.0, n_ref[...])
        pre = b_ref[...]
        for k in range(kc):
            pre = pre + w_ref[k:k + 1, :] * buf[pl.ds(8 - (kc - 1) + k, tr + 8), :]
        s = _sigmoid(pre)
        dsilu = s * (1.0 + pre * (1.0 - s))
        dbuf[0:tr, :] = d_ref[...] * dsilu[0:tr, :]
        dbuf[tr:, :] = jnp.where(last, 0.0, dn_ref[...]) * dsilu[tr:, :]
        dpre = dbuf[0:tr, :]
        dx = jnp.zeros((tr, tc), F32)
        first = i == 0
        for k in range(kc):
            dx = dx + w_ref[k:k + 1, :] * dbuf[pl.ds(kc - 1 - k, tr), :]
        dx_ref[...] = dx

        @pl.when(first)
        def _():
            dw_ref[...] = jnp.zeros_like(dw_ref)
        for k in range(kc):
            dw_ref[k:k + 1, :] += jnp.sum(dpre * buf[pl.ds(8 - (kc - 1) + k, tr), :], axis=0, keepdims=True)
        _acc_rows(db_ref, first, jnp.sum(dpre, axis=0, keepdims=True))

    par = pl.BlockSpec((8, tc), lambda j, i: (0, j))
    dx, dw, db = pl.pallas_call(
        body, name=name, grid=(c // tc, nt),
        in_specs=[tile, prev, nxt, par, pl.BlockSpec((1, tc), lambda j, i: (0, j)), tile, nxt],
        out_specs=[tile, par, par],
        out_shape=[jax.ShapeDtypeStruct((t, c), F32), jax.ShapeDtypeStruct((8, c), F32), jax.ShapeDtypeStruct((8, c), F32)],
        scratch_shapes=[pltpu.VMEM((tr + 16, tc), F32), pltpu.VMEM((tr + 8, tc), F32)],
        compiler_params=_params(2))(xbc, xbc, xbc, w, b, dout, dout)
    return dx, dw[:kc], db[0]


def _ssd_consts():
    h, p, ln = SSD_HEADS, SSD_HEAD_DIM, SSD_CHUNK
    w = h * p
    hrow, jcol = _iota((LANES, w), 0), _iota((LANES, w), 1)
    e = ((jcol >= hrow * p) & (jcol < (hrow + 1) * p)).astype(BF16)
    jrow, hcol = _iota((w, LANES), 0), _iota((w, LANES), 1)
    et = ((jrow >= hcol * p) & (jrow < (hcol + 1) * p)).astype(BF16)
    row, col = _iota((ln, ln), 0), _iota((ln, ln), 1)
    return e, et, row, col


def _ssd_common(dtraw_ref, dtb_ref, alog_ref, e):
    ln = SSD_CHUNK
    raw = dtraw_ref[...] + dtb_ref[...]
    dt = _softplus(raw)
    a = -jnp.exp(alog_ref[...])
    adt = dt * a
    row, col = _iota((ln, ln), 0), _iota((ln, ln), 1)
    cs = _xdot_l((col <= row).astype(BF16), adt)
    cl = jnp.sum(adt, axis=0, keepdims=True)
    ex = _xdot(jnp.concatenate([dt, cs, jnp.broadcast_to(cl, (ln, LANES))], axis=0), e)
    return raw, dt, a, cs, ex[:ln], ex[ln:2 * ln], ex[2 * ln:]


def _head_decay(cs, h, causal):
    ln = SSD_CHUNK
    lane = _iota((1, LANES), 1)
    colv = jnp.sum(jnp.where(lane == h, cs, 0.0), axis=1, keepdims=True)
    cb = jnp.broadcast_to(colv, (ln, ln))
    return jnp.exp(jnp.where(causal, cb - cb.T, NEG))


def _ssd_fwd(dtraw, xc, dtb, alog, dskip_x, name):
    t = dtraw.shape[0]
    h, p, g, n, ln = SSD_HEADS, SSD_HEAD_DIM, SSD_GROUPS, SSD_STATE, SSD_CHUNK
    w, gn, gw, hpg = h * p, g * n, (h // g) * p, h // g
    assert n == ln and gw % LANES == 0 and w % gn == 0
    nc = t // ln

    def body(dtraw_ref, x_ref, b_ref, c_ref, dtb_ref, alog_ref, dsk_ref, y_ref, sp_ref, s_ref):
        @pl.when(pl.program_id(0) == 0)
        def _():
            s_ref[...] = jnp.zeros_like(s_ref)
        e, _, row, col = _ssd_consts()
        causal = col <= row
        _, _, _, cs, dt_x, cs_x, cl_x = _ssd_common(dtraw_ref, dtb_ref, alog_ref, e)
        xv = x_ref[...]
        xd = xv * dt_x
        xdb = xd.astype(BF16)
        sv = s_ref[...]
        sp_ref[0] = sv
        el_x = jnp.exp(cs_x)
        zb = (xd * jnp.exp(cl_x - cs_x)).astype(BF16)
        cd_x = jnp.exp(cl_x)
        dsk = dsk_ref[...]
        half = _iota((1, LANES), 1) >= p
        for gi in range(g):
            gs = slice(gi * gw, (gi + 1) * gw)
            bg = b_ref[:, gi * n:(gi + 1) * n].astype(BF16)
            cg = c_ref[:, gi * n:(gi + 1) * n].astype(BF16)
            gm = _dot(cg, bg, "nt")
            sg = sv[:, gs]
            yoff = _dot(cg, sg.astype(BF16)) * el_x[:, gs]
            s_ref[:, gs] = sg * cd_x[:, gs] + _dot(bg, zb[:, gs], "tn")
            for pp in range(gw // LANES):
                ls = slice(gi * gw + pp * LANES, gi * gw + (pp + 1) * LANES)
                xp = xdb[:, ls]
                yp = yoff[:, pp * LANES:(pp + 1) * LANES] + dsk[:, ls] * xv[:, ls]
                for hh in range(LANES // p):
                    hd = gi * hpg + pp * (LANES // p) + hh
                    wm = (gm * _head_decay(cs, hd, causal)).astype(BF16)
                    yp = yp + _dot(wm, jnp.where(half == (hh == 1), xp, jnp.zeros_like(xp)))
                y_ref[:, ls] = yp

    nar = pl.BlockSpec((ln, LANES), lambda c: (c, 0))
    one = pl.BlockSpec((1, LANES), lambda c: (0, 0))
    return pl.pallas_call(
        body, name=name, grid=(nc,),
        in_specs=[nar, pl.BlockSpec((ln, w), lambda c: (c, 0)), pl.BlockSpec((ln, gn), lambda c: (c, w // gn)),
                  pl.BlockSpec((ln, gn), lambda c: (c, w // gn + 1)), one, one, pl.BlockSpec((1, w), lambda c: (0, 0))],
        out_specs=[pl.BlockSpec((ln, w), lambda c: (c, 0)), pl.BlockSpec((1, n, w), lambda c: (c, 0, 0))],
        out_shape=[jax.ShapeDtypeStruct((t, w), F32), jax.ShapeDtypeStruct((nc, n, w), F32)],
        scratch_shapes=[pltpu.VMEM((n, w), F32)], compiler_params=_params(1))(dtraw, xc, xc, xc, dtb, alog, dskip_x)


def _ssd_bwd(dtraw, xc, dtb, alog, dskip_x, sprev, dy, name):
    t = dtraw.shape[0]
    h, p, g, n, ln = SSD_HEADS, SSD_HEAD_DIM, SSD_GROUPS, SSD_STATE, SSD_CHUNK
    w, gn, gw, hpg = h * p, g * n, (h // g) * p, h // g
    nc = t // ln

    def body(dtraw_ref, x_ref, b_ref, c_ref, dtb_ref, alog_ref, dsk_ref, sp_ref, dy_ref,
             dx_ref, db_ref, dc_ref, ddt_ref, dbias_ref, dalog_ref, ddsk_ref, ds_ref, dxd_ref, qcs_ref):
        first = pl.program_id(0) == 0

        @pl.when(first)
        def _():
            ds_ref[...] = jnp.zeros_like(ds_ref)
        e, et, row, col = _ssd_consts()
        causal = col <= row
        raw, dt, a, cs, dt_x, cs_x, cl_x = _ssd_common(dtraw_ref, dtb_ref, alog_ref, e)
        xv = x_ref[...]
        xd = xv * dt_x
        xdb = xd.astype(BF16)
        sv = sp_ref[0]
        dyv = dy_ref[...]
        dyb = dyv.astype(BF16)
        dsn = ds_ref[...]
        el_x = jnp.exp(cs_x)
        dte_x = jnp.exp(cl_x - cs_x)
        cd_x = jnp.exp(cl_x)
        zf = xd * dte_x
        lane = _iota((1, LANES), 1)
        half = lane >= p
        lastrow = _iota((ln, 1), 0) == ln - 1
        dcs = jnp.zeros((ln, LANES), F32)
        for gi in range(g):
            gs = slice(gi * gw, (gi + 1) * gw)
            ns = slice(gi * n, (gi + 1) * n)
            bg = b_ref[:, ns].astype(BF16)
            cg = c_ref[:, ns].astype(BF16)
            gm = _dot(cg, bg, "nt")
            sgb = sv[:, gs].astype(BF16)
            dsg = dsn[:, gs]
            dsgb = dsg.astype(BF16)
            yoff = _dot(cg, sgb) * el_x[:, gs]
            drb = (el_x[:, gs] * dyv[:, gs]).astype(BF16)
            dcg = _dot(drb, sgb, "nt")
            ds_ref[:, gs] = cd_x[:, gs] * dsg + _dot(cg, drb, "tn")
            dz = _dot(bg, dsgb)
            zg = zf[:, gs]
            dbg = _dot(zg.astype(BF16), dsgb, "nt")
            dzz = dz * zg
            qcl = jnp.sum(dzz + cd_x[:, gs] * dsg * sv[:, gs], axis=0, keepdims=True)
            qcs_ref[:, gs] = dyv[:, gs] * yoff - dzz + jnp.where(lastrow, jnp.broadcast_to(qcl, (ln, gw)), 0.0)
            dgm = jnp.zeros((ln, ln), F32)
            for pp in range(gw // LANES):
                ls = slice(gi * gw + pp * LANES, gi * gw + (pp + 1) * LANES)
                xp = xdb[:, ls]
                dxp = dz[:, pp * LANES:(pp + 1) * LANES] * dte_x[:, ls]
                for hh in range(LANES // p):
                    hd = gi * hpg + pp * (LANES // p) + hh
                    dm = _head_decay(cs, hd, causal)
                    wf = gm * dm
                    dym = jnp.where(half == (hh == 1), dyb[:, ls], jnp.zeros_like(xp))
                    dw = _dot(dym, xp, "nt")
                    dxp = dxp + _dot(wf.astype(BF16), dym, "tn")
                    dgm = dgm + dw * dm
                    mm = dw * wf
                    rc = jnp.sum(mm, axis=1, keepdims=True) - jnp.sum(mm.T, axis=1, keepdims=True)
                    dcs = dcs + rc * (lane == hd).astype(F32)
                dxd_ref[:, ls] = dxp
            dgb = dgm.astype(BF16)
            dc_ref[:, ns] = dcg + _dot(dgb, bg)
            db_ref[:, ns] = dbg + _dot(dgb, cg, "tn")
        dxd = dxd_ref[...]
        dx_ref[...] = dxd * dt_x + dsk_ref[...] * dyv
        red = _xdot(jnp.concatenate([qcs_ref[...], dxd * xv, dyv * xv], axis=0), et)
        dcs = dcs + red[:ln]
        dadt = _xdot_l((row <= col).astype(BF16), dcs)
        ddt = red[ln:2 * ln] + dadt * a
        draw = ddt * _sigmoid(raw)
        ddt_ref[...] = draw
        _acc_rows(dbias_ref, first, jnp.sum(draw, axis=0, keepdims=True))
        _acc_rows(dalog_ref, first, jnp.sum(dadt * dt, axis=0, keepdims=True) * a)
        _acc_rows(ddsk_ref, first, jnp.sum(red[2 * ln:], axis=0, keepdims=True))

    rev = lambda c: nc - 1 - c
    nar = pl.BlockSpec((ln, LANES), lambda c: (rev(c), 0))
    one = pl.BlockSpec((1, LANES), lambda c: (0, 0))
    wide = pl.BlockSpec((ln, w), lambda c: (rev(c), 0))
    bcs = pl.BlockSpec((ln, gn), lambda c: (rev(c), 0))
    acc = pl.BlockSpec((8, LANES), lambda c: (0, 0))
    outs = pl.pallas_call(
        body, name=name, grid=(nc,),
        in_specs=[nar, wide, pl.BlockSpec((ln, gn), lambda c: (rev(c), w // gn)), pl.BlockSpec((ln, gn), lambda c: (rev(c), w // gn + 1)),
                  one, one, pl.BlockSpec((1, w), lambda c: (0, 0)), pl.BlockSpec((1, n, w), lambda c: (rev(c), 0, 0)), wide],
        out_specs=[wide, bcs, bcs, nar, acc, acc, acc],
        out_shape=[jax.ShapeDtypeStruct((t, w), F32), jax.ShapeDtypeStruct((t, gn), F32), jax.ShapeDtypeStruct((t, gn), F32),
                   jax.ShapeDtypeStruct((t, LANES), F32)] + [jax.ShapeDtypeStruct((8, LANES), F32)] * 3,
        scratch_shapes=[pltpu.VMEM((n, w), F32), pltpu.VMEM((ln, w), F32), pltpu.VMEM((ln, w), F32)],
        compiler_params=_params(1))(dtraw, xc, xc, xc, dtb, alog, dskip_x, sprev, dy)
    dx, db, dc, ddt, dbias, dalog, ddsk = outs
    return dx, db, dc, ddt, dbias[0], dalog[0], ddsk[0]


def _rope_tables(t):
    half = MLA_ROPE // 2
    inv_freq = ROPE_THETA ** (-jnp.arange(half, dtype=F32) / half)
    ang = jnp.arange(t, dtype=F32)[:, None] * inv_freq[None, :]
    cos, sin = jnp.cos(ang), jnp.sin(ang)
    pad = LANES - MLA_ROPE
    cos_r = jnp.concatenate([cos, cos, jnp.ones((t, pad), F32)], axis=1)
    sin_r = jnp.concatenate([sin, sin, jnp.zeros((t, pad), F32)], axis=1)
    return cos_r, sin_r


def _rot_matrix():
    half = MLA_ROPE // 2
    i, j = _iota((LANES, LANES), 0), _iota((LANES, LANES), 1)
    neg = (j < half) & (i == j + half)
    pos = (j >= half) & (j < 2 * half) & (i == j - half)
    return (pos.astype(F32) - neg.astype(F32)).astype(BF16)


def _rope(x, cos, sin, every, transpose, out_dtype, name):
    t, w = x.shape
    tr = _rows(t)

    def body(x_ref, c_ref, s_ref, o_ref):
        rot = _rot_matrix()
        cv, sv = c_ref[...], s_ref[...]
        for j in range(w // LANES):
            ls = slice(j * LANES, (j + 1) * LANES)
            xv = x_ref[:, ls]
            if j % every != every - 1:
                o_ref[:, ls] = xv.astype(o_ref.dtype)
            elif transpose:
                o_ref[:, ls] = (xv * cv - _xdot(xv * sv, rot, 2)).astype(o_ref.dtype)
            else:
                o_ref[:, ls] = (xv * cv + _xdot(xv, rot, 2) * sv).astype(o_ref.dtype)

    wide = pl.BlockSpec((tr, w), lambda i: (i, 0))
    tab = pl.BlockSpec((tr, LANES), lambda i: (i, 0))
    return pl.pallas_call(
        body, name=name, grid=(t // tr,), in_specs=[wide, tab, tab], out_specs=wide,
        out_shape=jax.ShapeDtypeStruct((t, w), out_dtype), compiler_params=_params(1))(x, cos, sin)


def _att_masks(blk):
    return _iota((blk, blk), 0), _iota((blk, blk), 1)


def _lanes(j):
    return slice(j * LANES, (j + 1) * LANES)


def _mla_fwd(q, kr, kv, name):
    t = q.shape[0]
    nh, blk = MLA_HEADS, min(ATT_BLK, t)
    scale = (MLA_NOPE + MLA_ROPE) ** -0.5
    hps = math.gcd(nh, MLA_FWD_HEADS)

    def body(q_ref, kv_ref, kr_ref, o_ref, lse_ref):
        i = pl.program_id(1)
        row, col = _att_masks(blk)
        qs = [q_ref[:, 2 * hh * LANES:(2 * hh + 2) * LANES] for hh in range(hps)]

        def scores(kb, hh):
            ks = pl.ds(pl.multiple_of(kb * blk, blk), blk)
            kfull = jnp.concatenate([kv_ref[ks, _lanes(2 * hh)], kr_ref[ks, :]], axis=1)
            return _dot(qs[hh], kfull, "nt") * scale, kv_ref[ks, _lanes(2 * hh + 1)]
        init = []
        for hh in range(hps):
            s, v = scores(i, hh)
            s = jnp.where(col <= row, s, NEG)
            m = jnp.max(s, axis=1, keepdims=True)
            pr = jnp.exp(s - m)
            init += [m, jnp.sum(pr, axis=1, keepdims=True), _dot(pr.astype(BF16), v)]

        def step(kb, carry):
            sv = [scores(kb, hh) for hh in range(hps)]
            out, prs = [], []
            for hh in range(hps):
                m, l, acc = carry[3 * hh:3 * hh + 3]
                s = sv[hh][0]
                m2 = jnp.maximum(m, jnp.max(s, axis=1, keepdims=True))
                al = jnp.exp(m - m2)
                pr = jnp.exp(s - m2)
                prs.append(pr.astype(BF16))
                out += [m2, al * l + jnp.sum(pr, axis=1, keepdims=True), al * acc]
            for hh in range(hps):
                out[3 * hh + 2] = out[3 * hh + 2] + _dot(prs[hh], sv[hh][1])
            return tuple(out)
        res = lax.fori_loop(0, i, step, tuple(init))
        for hh in range(hps):
            m, l, acc = res[3 * hh:3 * hh + 3]
            o_ref[:, _lanes(hh)] = acc / l
            lse_ref[hh] = m + jnp.log(l)

    return pl.pallas_call(
        body, name=name, grid=(nh // hps, t // blk),
        in_specs=[pl.BlockSpec((blk, 2 * hps * LANES), lambda h, i: (i, h)),
                  pl.BlockSpec((t, 2 * hps * LANES), lambda h, i: (0, h), pipeline_mode=pl.Buffered(1)),
                  pl.BlockSpec((t, LANES), lambda h, i: (0, 0), pipeline_mode=pl.Buffered(1))],
        out_specs=[pl.BlockSpec((blk, hps * LANES), lambda h, i: (i, h)), pl.BlockSpec((hps, blk, 1), lambda h, i: (h, i, 0))],
        out_shape=[jax.ShapeDtypeStruct((t, nh * LANES), F32), jax.ShapeDtypeStruct((nh, t, 1), F32)],
        compiler_params=_params(2))(q, kv, kr)


def _mla_bwd(q, kr, kv, o, do, lse, name):
    t = q.shape[0]
    nh, blk = MLA_HEADS, min(ATT_BLK, t)
    scale = (MLA_NOPE + MLA_ROPE) ** -0.5

    def body(q_ref, kv_ref, kr_ref, o_ref, do_ref, lse_ref, dq_ref, dkv_ref, dkr_ref):
        hp, i = pl.program_id(0), pl.program_id(1)

        @pl.when(i == 0)
        def _():
            dkv_ref[...] = jnp.zeros_like(dkv_ref)

        @pl.when((i == 0) & (hp == 0))
        def _():
            dkr_ref[...] = jnp.zeros_like(dkr_ref)
        row, col = _att_masks(blk)
        qs = [q_ref[:, 2 * hh * LANES:(2 * hh + 2) * LANES] for hh in range(2)]
        dobs = [do_ref[:, _lanes(hh)].astype(BF16) for hh in range(2)]
        deltas = [jnp.sum(do_ref[:, _lanes(hh)] * o_ref[:, _lanes(hh)], axis=1, keepdims=True) for hh in range(2)]
        lses = [lse_ref[hh] for hh in range(2)]

        def tile(kb, carry, masked):
            ks = pl.ds(pl.multiple_of(kb * blk, blk), blk)
            krv = kr_ref[ks, :]
            hs = range(2)
            kfull = [jnp.concatenate([kv_ref[ks, _lanes(2 * hh)], krv], axis=1) for hh in hs]
            ss = [_dot(qs[hh], kfull[hh], "nt") for hh in hs]
            dps = [_dot(dobs[hh], kv_ref[ks, _lanes(2 * hh + 1)], "nt") for hh in hs]
            prb, dsb = [], []
            for hh in hs:
                pr = jnp.exp(ss[hh] * scale - lses[hh])
                if masked:
                    pr = jnp.where(col <= row, pr, 0.0)
                prb.append(pr.astype(BF16))
                dsb.append((pr * (dps[hh] - deltas[hh]) * scale).astype(BF16))
            out, dkr = [], None
            for hh in hs:
                dkv_ref[ks, _lanes(2 * hh + 1)] += _dot(prb[hh], dobs[hh], "tn")
                dk = _dot(dsb[hh], qs[hh], "tn")
                dkv_ref[ks, _lanes(2 * hh)] += dk[:, :LANES]
                dkr = dk[:, LANES:] if dkr is None else dkr + dk[:, LANES:]
                out.append(carry[hh] + _dot(dsb[hh], kfull[hh]))
            dkr_ref[ks, :] += dkr
            return tuple(out)
        zero = jnp.zeros((blk, 2 * LANES), F32)
        carry = lax.fori_loop(0, i, lambda kb, c: tile(kb, c, False), (zero,) * 2)
        res = tile(i, carry, True)
        for hh in range(2):
            dq_ref[:, 2 * hh * LANES:(2 * hh + 2) * LANES] = res[hh]

    qb = lambda w: pl.BlockSpec((blk, w * LANES), lambda h, i: (i, h))
    wide = jax.ShapeDtypeStruct((t, nh * 2 * LANES), F32)
    return pl.pallas_call(
        body, name=name, grid=(nh // 2, t // blk),
        in_specs=[qb(4), pl.BlockSpec((t, 4 * LANES), lambda h, i: (0, h)), pl.BlockSpec((t, LANES), lambda h, i: (0, 0)),
                  qb(2), qb(2), pl.BlockSpec((2, blk, 1), lambda h, i: (h, i, 0))],
        out_specs=[qb(4), pl.BlockSpec((t, 4 * LANES), lambda h, i: (0, h)), pl.BlockSpec((t, LANES), lambda h, i: (0, 0))],
        out_shape=[wide, wide, jax.ShapeDtypeStruct((t, LANES), F32)],
        compiler_params=_params(2))(q, kv, kr, o, do, lse)


def _sb_fwd(qkv, name):
    t = qkv.shape[0]
    nh, blk = SB_HEADS, min(ATT_BLK, t)
    hps = math.gcd(nh, SB_FWD_HEADS)
    nq = t // blk
    scale = SB_HEAD_DIM ** -0.5

    def body(q_ref, k_ref, v_ref, o_ref, lt_ref):
        i = pl.program_id(1)
        row, col = _att_masks(blk)
        usuf = (row > col).astype(BF16)
        qs = [q_ref[:, _lanes(hh)] for hh in range(hps)]

        def tile(kb, carry, masked):
            ks = pl.ds(pl.multiple_of(kb * blk, blk), blk)
            hs = range(hps)
            zs = [_dot(qs[hh], k_ref[ks, _lanes(hh)], "nt") for hh in hs]
            lks, lss = [], []
            for hh in hs:
                z = zs[hh] * scale
                lk = -_softplus(z)
                lss.append(lk + z)
                lks.append(jnp.where(col < row, lk, 0.0) if masked else lk)
            pieces = [_split(lks[hh], 2) for hh in hs]
            later = [_dot(pieces[hh][0], usuf) + _dot(pieces[hh][1], usuf) for hh in hs]
            out = []
            for hh in hs:
                wt = jnp.exp(lss[hh] + later[hh] + carry[2 * hh + 1])
                if masked:
                    wt = jnp.where(col < row, wt, 0.0)
                out += [wt.astype(BF16), carry[2 * hh + 1] + jnp.sum(lks[hh], axis=1, keepdims=True)]
            for hh in hs:
                out[2 * hh] = carry[2 * hh] + _dot(out[2 * hh], v_ref[ks, _lanes(hh)])
            return tuple(out)
        za, zr = jnp.zeros((blk, LANES), F32), jnp.zeros((blk, 1), F32)
        carry = tile(i, (za, zr) * hps, True)
        carry = lax.fori_loop(0, i, lambda j, c: tile(i - 1 - j, c, False), carry)
        for hh in range(hps):
            o_ref[:, _lanes(hh)] = carry[2 * hh]
            lt_ref[hh] = carry[2 * hh + 1]

    ng = nh // hps
    full = lambda f: pl.BlockSpec((t, hps * LANES), f)
    return pl.pallas_call(
        body, name=name, grid=(ng, nq),
        in_specs=[pl.BlockSpec((blk, hps * LANES), lambda h, i: (i, h)), full(lambda h, i: (0, ng + h)), full(lambda h, i: (0, 2 * ng + h))],
        out_specs=[pl.BlockSpec((blk, hps * LANES), lambda h, i: (i, h)), pl.BlockSpec((hps, blk, 1), lambda h, i: (h, i, 0))],
        out_shape=[jax.ShapeDtypeStruct((t, nh * LANES), F32), jax.ShapeDtypeStruct((nh, t, 1), F32)],
        compiler_params=_params(2))(qkv, qkv, qkv)


def _sb_bwd(qkv, do, ltot, name):
    t = qkv.shape[0]
    nh, blk = SB_HEADS, min(ATT_BLK, t)
    hps = math.gcd(nh, SB_BWD_HEADS)
    nq = t // blk
    scale = SB_HEAD_DIM ** -0.5

    def body(q_ref, k_ref, v_ref, do_ref, lt_ref, dq_ref, dk_ref, dv_ref):
        i = pl.program_id(1)

        @pl.when(i == 0)
        def _():
            dk_ref[...] = jnp.zeros_like(dk_ref)
            dv_ref[...] = jnp.zeros_like(dv_ref)
        row, col = _att_masks(blk)
        uinc = (row <= col).astype(BF16)
        uexc = (row < col).astype(BF16)
        qs = [q_ref[:, _lanes(hh)] for hh in range(hps)]
        dobs = [do_ref[:, _lanes(hh)].astype(BF16) for hh in range(hps)]
        lts = [lt_ref[hh] for hh in range(hps)]

        def tile(kb, carry, masked):
            ks = pl.ds(pl.multiple_of(kb * blk, blk), blk)
            hs = range(hps)
            kvs = [k_ref[ks, _lanes(hh)] for hh in hs]
            zs = [_dot(qs[hh], kvs[hh], "nt") for hh in hs]
            dws = [_dot(dobs[hh], v_ref[ks, _lanes(hh)], "nt") for hh in hs]
            lks, lss = [], []
            for hh in hs:
                z = zs[hh] * scale
                lk = -_softplus(z)
                lss.append(lk + z)
                lks.append(jnp.where(col < row, lk, 0.0) if masked else lk)
            pieces = [_split(lks[hh], 2) for hh in hs]
            css = [_dot(pieces[hh][0], uinc) + _dot(pieces[hh][1], uinc) for hh in hs]
            wts, evs = [], []
            for hh in hs:
                wt = jnp.exp(lss[hh] + (lts[hh] - (carry[3 * hh + 1] + css[hh])))
                if masked:
                    wt = jnp.where(col < row, wt, 0.0)
                wts.append(wt.astype(BF16))
                evs.append(dws[hh] * wt)
            epieces = [_split(evs[hh], 2) for hh in hs]
            ecss = [_dot(epieces[hh][0], uexc) + _dot(epieces[hh][1], uexc) for hh in hs]
            for hh in hs:
                dv_ref[ks, _lanes(hh)] += _dot(wts[hh], dobs[hh], "tn")
            dzbs = []
            for hh in hs:
                sig = jnp.exp(lss[hh])
                dz = evs[hh] * (1.0 - sig) - (ecss[hh] + carry[3 * hh + 2]) * sig
                if masked:
                    dz = jnp.where(col < row, dz, 0.0)
                dzbs.append((dz * scale).astype(BF16))
            out = []
            for hh in hs:
                dk_ref[ks, _lanes(hh)] += _dot(dzbs[hh], qs[hh], "tn")
                out += [carry[3 * hh] + _dot(dzbs[hh], kvs[hh]), carry[3 * hh + 1] + jnp.sum(lks[hh], axis=1, keepdims=True),
                        carry[3 * hh + 2] + jnp.sum(evs[hh], axis=1, keepdims=True)]
            return tuple(out)
        za, z1 = jnp.zeros((blk, LANES), F32), jnp.zeros((blk, 1), F32)
        carry = lax.fori_loop(0, i, lambda kb, c: tile(kb, c, False), (za, z1, z1) * hps)
        res = tile(i, carry, True)
        for hh in range(hps):
            dq_ref[:, _lanes(hh)] = res[3 * hh]

    ng = nh // hps
    full = lambda f: pl.BlockSpec((t, hps * LANES), f, pipeline_mode=pl.Buffered(1))
    qb = pl.BlockSpec((blk, hps * LANES), lambda h, i: (i, h))
    wide = jax.ShapeDtypeStruct((t, nh * LANES), F32)
    return pl.pallas_call(
        body, name=name, grid=(ng, nq),
        in_specs=[qb, full(lambda h, i: (0, ng + h)), full(lambda h, i: (0, 2 * ng + h)), qb,
                  pl.BlockSpec((hps, blk, 1), lambda h, i: (h, i, 0))],
        out_specs=[qb, full(lambda h, i: (0, h)), full(lambda h, i: (0, h))],
        out_shape=[wide, wide, wide], compiler_params=_params(2))(qkv, qkv, qkv, do, ltot)


def _as2d(a):
    return a.reshape((-1, a.shape[-1]))


def _sum4(parts, name):
    _, r, c = parts.shape
    tr = _tile(r, (256, 128, 64, 32, 16, 8))

    def body(p_ref, o_ref):
        acc = p_ref[0].astype(F32)
        for j in range(1, 4):
            acc = acc + p_ref[j].astype(F32)
        o_ref[...] = acc

    return pl.pallas_call(body, name=name, grid=(r // tr,), in_specs=[pl.BlockSpec((4, tr, c), lambda i: (0, i, 0))],
                          out_specs=pl.BlockSpec((tr, c), lambda i: (i, 0)), out_shape=jax.ShapeDtypeStruct((r, c), F32),
                          compiler_params=_params(1))(parts)


def _adamw(w, ga, gb, m, v, name):
    r, c = w.shape
    tr = _tile(r, (128, 64, 32, 16, 8))
    c1, c2 = 1.0 - ADAM_B1 ** ADAM_STEP, 1.0 - ADAM_B2 ** ADAM_STEP
    two = gb is not None

    def body(*refs):
        w_ref, ga_ref = refs[0], refs[1]
        gb_ref = refs[2] if two else None
        m_ref, v_ref, g_out, d_out, m_out, v_out = refs[2 + two:]
        gv = ga_ref[...]
        if two:
            gv = gv + gb_ref[...]
        mn = ADAM_B1 * m_ref[...] + (1.0 - ADAM_B1) * gv
        vn = ADAM_B2 * v_ref[...] + (1.0 - ADAM_B2) * (gv * gv)
        g_out[...] = gv
        m_out[...] = mn
        v_out[...] = vn
        d_out[...] = -ADAM_LR * ((mn / c1) / (jnp.sqrt(vn / c2) + ADAM_EPS) + ADAM_WD * w_ref[...])

    blk = pl.BlockSpec((tr, c), lambda i: (i, 0))
    ins = [w, ga] + ([gb] if two else []) + [m, v]
    return pl.pallas_call(body, name=name, grid=(r // tr,), in_specs=[blk] * len(ins), out_specs=[blk] * 4,
                          out_shape=[jax.ShapeDtypeStruct((r, c), F32)] * 4, compiler_params=_params(1))(*ins)


HBM_SPEC = pl.BlockSpec(memory_space=pltpu.HBM)
SEM_SPEC = pl.BlockSpec(memory_space=pltpu.SEMAPHORE)
EFFECT = pltpu.SideEffectType.DATAFLOW_SIDE_EFFECTING


def _chip_copies(ins, lands, send_sems, recv_sems, whole):
    x, y, c = lax.axis_index("x"), lax.axis_index("y"), lax.axis_index("c")
    me = 2 * x + y
    out = []
    for wi in range(len(ins)):
        for k, (px, py) in enumerate([(1 - x, y), (x, 1 - y), (1 - x, 1 - y)]):
            sems = dict(send_sem=send_sems[wi * 3 + k], recv_sem=recv_sems[wi * 3 + k], device_id=(px, py, c), device_id_type=MESH_T)
            peer = 2 * px + py
            sent = pltpu.make_async_remote_copy(src_ref=ins[wi] if whole else ins[wi].at[peer], dst_ref=lands[wi].at[me], **sems)
            got = functools.partial(pltpu.make_async_remote_copy, src_ref=ins[wi] if whole else ins[wi].at[me],
                                    dst_ref=lands[wi].at[peer], **sems)
            out.append((sent, got))
    return out


def _xstart(arrs, whole, after, name):
    n, na = len(arrs), len(after)
    me = 2 * lax.axis_index("x") + lax.axis_index("y")
    lands = []
    for a in arrs:
        own = a[None] if whole else lax.dynamic_slice_in_dim(a, me, 1, axis=0)
        empty = lax.empty(((4,) + a.shape) if whole else a.shape, a.dtype)
        lands.append(lax.dynamic_update_slice_in_dim(empty, own, me, axis=0))

    def body(*refs):
        ins, lands_in = refs[:n], refs[n:2 * n]
        outs = refs[2 * n + na:]
        for sent, _ in _chip_copies(ins, lands_in, outs[:3 * n], outs[3 * n:6 * n], whole):
            sent.start()
        outs[8 * n][...] = jnp.zeros((8, LANES), F32)

    hbm = lambda a: pltpu.HBM(a.shape, a.dtype)
    res = pl.pallas_call(
        body, name=name,
        out_shape=[pltpu.SemaphoreType.DMA(())] * (6 * n) + [hbm(a) for a in arrs] + [hbm(a) for a in lands]
        + [jax.ShapeDtypeStruct((8, LANES), F32)],
        in_specs=[HBM_SPEC] * (2 * n) + [pl.BlockSpec(memory_space=pl.ANY)] * na,
        out_specs=[SEM_SPEC] * (6 * n) + [HBM_SPEC] * (2 * n) + [pl.BlockSpec(memory_space=pltpu.VMEM)],
        input_output_aliases={i: 6 * n + i for i in range(2 * n)},
        compiler_params=pltpu.CompilerParams(has_side_effects=EFFECT),
    )(*[pltpu.with_memory_space_constraint(a, pltpu.HBM) for a in list(arrs) + lands], *after)
    return res


def _xwait(handle, whole, after, name):
    n = (len(handle) - 1) // 8
    sems, thru = handle[:6 * n], handle[6 * n:8 * n]

    def body(*refs):
        ins, lands_in = refs[:n], refs[n:2 * n]
        for sent, got in _chip_copies(ins, lands_in, refs[2 * n:5 * n], refs[5 * n:8 * n], whole):
            sent.wait_send()
            got().wait_recv()

    hbm = lambda a: pltpu.HBM(a.shape, a.dtype)
    res = pl.pallas_call(
        body, name=name, out_shape=[hbm(a) for a in thru],
        in_specs=[HBM_SPEC] * (2 * n) + [SEM_SPEC] * (6 * n) + [pl.BlockSpec(memory_space=pl.ANY)],
        out_specs=[HBM_SPEC] * (2 * n), input_output_aliases={i: i for i in range(2 * n)},
        compiler_params=pltpu.CompilerParams(has_side_effects=EFFECT),
    )(*thru, *sems, after)
    return res[n:]


def _sibling_copies(ins, lands, send_sems, recv_sems):
    sib = (lax.axis_index("x"), lax.axis_index("y"), 1 - lax.axis_index("c"))
    return [pltpu.make_async_remote_copy(src_ref=ins[wi], dst_ref=lands[wi], send_sem=send_sems[wi], recv_sem=recv_sems[wi],
                                         device_id=sib, device_id_type=MESH_T) for wi in range(len(ins))]


def _sib_start(arrs, after, name):
    n, na = len(arrs), len(after)
    lands = [lax.empty(a.shape, a.dtype) for a in arrs]

    def body(*refs):
        outs = refs[2 * n + na:]
        for cp in _sibling_copies(refs[:n], refs[n:2 * n], outs[:n], outs[n:2 * n]):
            cp.start()
        outs[4 * n][...] = jnp.zeros((8, LANES), F32)

    hbm = lambda a: pltpu.HBM(a.shape, a.dtype)
    return pl.pallas_call(
        body, name=name,
        out_shape=[pltpu.SemaphoreType.DMA(())] * (2 * n) + [hbm(a) for a in arrs] * 2 + [jax.ShapeDtypeStruct((8, LANES), F32)],
        in_specs=[HBM_SPEC] * (2 * n) + [pl.BlockSpec(memory_space=pl.ANY)] * na,
        out_specs=[SEM_SPEC] * (2 * n) + [HBM_SPEC] * (2 * n) + [pl.BlockSpec(memory_space=pltpu.VMEM)],
        input_output_aliases={i: 2 * n + i for i in range(2 * n)},
        compiler_params=pltpu.CompilerParams(has_side_effects=EFFECT),
    )(*[pltpu.with_memory_space_constraint(a, pltpu.HBM) for a in list(arrs) + lands], *after)


def _sib_wait(handle, after, name):
    n = (len(handle) - 1) // 4
    sems, thru = handle[:2 * n], handle[2 * n:4 * n]

    def body(*refs):
        for cp in _sibling_copies(refs[:n], refs[n:2 * n], refs[2 * n:3 * n], refs[3 * n:4 * n]):
            cp.wait_send()
            cp.wait_recv()

    res = pl.pallas_call(
        body, name=name, out_shape=[pltpu.HBM(a.shape, a.dtype) for a in thru],
        in_specs=[HBM_SPEC] * (2 * n) + [SEM_SPEC] * (2 * n) + [pl.BlockSpec(memory_space=pl.ANY)],
        out_specs=[HBM_SPEC] * (2 * n), input_output_aliases={i: i for i in range(2 * n)},
        compiler_params=pltpu.CompilerParams(has_side_effects=EFFECT),
    )(*thru, *sems, after)
    return res[n:]


def _sibling_exchange(arrs, name):
    n = len(arrs)

    def body(*refs):
        ins, outs = refs[:n], refs[n:2 * n]
        send_sems, recv_sems = refs[2 * n:]
        sib = (lax.axis_index("x"), lax.axis_index("y"), 1 - lax.axis_index("c"))
        cps = [pltpu.make_async_remote_copy(src_ref=ins[wi], dst_ref=outs[wi], send_sem=send_sems.at[wi], recv_sem=recv_sems.at[wi],
                                            device_id=sib, device_id_type=MESH_T) for wi in range(n)]
        for cp in cps:
            cp.start()
        for cp in cps:
            cp.wait_recv()
        for cp in cps:
            cp.wait_send()

    anyspec = pl.BlockSpec(memory_space=pl.ANY)
    return pl.pallas_call(
        body, name=name, in_specs=[anyspec] * n, out_specs=[anyspec] * n,
        out_shape=[jax.ShapeDtypeStruct(a.shape, a.dtype) for a in arrs],
        scratch_shapes=[pltpu.SemaphoreType.DMA((n,)), pltpu.SemaphoreType.DMA((n,))],
        compiler_params=pltpu.CompilerParams(has_side_effects=True))(*arrs)


def _allreduce_small(packed, name):
    r = packed.shape[0]

    def body(in_ref, out_ref, land, send_sems, recv_sems):
        x, y, c = lax.axis_index("x"), lax.axis_index("y"), lax.axis_index("c")
        me = 4 * x + 2 * y + c
        land[me] = in_ref[...]
        rel = [(dx, dy, dc) for dx in (0, 1) for dy in (0, 1) for dc in (0, 1)][1:]
        peers = [((1 - x) if dx else x, (1 - y) if dy else y, (1 - c) if dc else c) for dx, dy, dc in rel]
        sends = []
        for k, peer in enumerate(peers):
            cp = pltpu.make_async_remote_copy(src_ref=in_ref, dst_ref=land.at[me], send_sem=send_sems.at[k], recv_sem=recv_sems.at[k],
                                              device_id=peer, device_id_type=MESH_T)
            cp.start()
            sends.append(cp)
        for k, (px, py, pc) in enumerate(peers):
            pltpu.make_async_remote_copy(src_ref=in_ref, dst_ref=land.at[4 * px + 2 * py + pc], send_sem=send_sems.at[k],
                                         recv_sem=recv_sems.at[k], device_id=(px, py, pc), device_id_type=MESH_T).wait_recv()
        for cp in sends:
            cp.wait_send()
        acc = land[0]
        for j in range(1, 8):
            acc = acc + land[j]
        out_ref[...] = acc

    vm = pl.BlockSpec(memory_space=pltpu.VMEM)
    return pl.pallas_call(
        body, name=name, in_specs=[vm], out_specs=vm, out_shape=jax.ShapeDtypeStruct((r, LANES), F32),
        scratch_shapes=[pltpu.VMEM((8, r, LANES), F32), pltpu.SemaphoreType.DMA((7,)), pltpu.SemaphoreType.DMA((7,))],
        compiler_params=pltpu.CompilerParams(has_side_effects=True))(packed)


def _in_splits():
    w = SSD_HEADS * SSD_HEAD_DIM
    cc = w + 2 * SSD_GROUPS * SSD_STATE
    return [w, cc, SSD_HEADS, MLA_Q_RANK, MLA_KV_RANK, MLA_ROPE]


def _padc(a, n):
    return jnp.pad(a, ((0, 0), (0, n - a.shape[1])))


def _win_pad(wm):
    offs = np.cumsum(_in_splits())[:-1]
    z, xbc, dt, cq, ckv, kr = jnp.split(wm, offs, axis=1)
    return jnp.concatenate([z, xbc, cq, ckv, _padc(kr, LANES), _padc(dt, LANES)], axis=1)


def _win_unpad(g):
    w, cc, nh, qr, kvr, rp = _in_splits()
    offs = np.cumsum([w, cc, qr, kvr, LANES])
    z, xbc, cq, ckv, kr, dt = jnp.split(g, offs, axis=1)
    return jnp.concatenate([z, xbc, dt[:, :nh], cq, ckv, kr[:, :rp]], axis=1)


def _wuq_pad(wm):
    r = wm.shape[0]
    w3 = wm.reshape(r, MLA_HEADS, MLA_NOPE + MLA_ROPE)
    return jnp.pad(w3, ((0, 0), (0, 0), (0, 2 * LANES - MLA_NOPE - MLA_ROPE))).reshape(r, MLA_HEADS * 2 * LANES)


def _wuq_unpad(g):
    r = g.shape[0]
    return g.reshape(r, MLA_HEADS, 2 * LANES)[:, :, :MLA_NOPE + MLA_ROPE].reshape(r, MLA_HEADS * (MLA_NOPE + MLA_ROPE))


def _full_from_gather(name, g):
    col = name in COL_SHARDED
    layers = g.shape[1]
    return [jnp.concatenate([g[j, l] for j in range(4)], axis=1 if col else 0) for l in range(layers)]


def _shards_from_full(name, mats):
    col = name in COL_SHARDED
    per = []
    for mt in mats:
        r, c = mt.shape
        per.append(mt.reshape(r, 4, c // 4).transpose(1, 0, 2) if col else mt.reshape(4, r // 4, c))
    return jnp.stack(per, axis=1).astype(BF16)


def _pack_small(vals, extra=None):
    flat = jnp.concatenate([vals[n].reshape(-1).astype(F32) for n in SMALL] + ([extra.reshape(1)] if extra is not None else []))
    rows = -(-flat.shape[0] // (8 * LANES)) * 8
    return jnp.pad(flat, (0, rows * LANES - flat.shape[0])).reshape(rows, LANES)


def _unpack_small(packed, like):
    flat, out, off = packed.reshape(-1), {}, 0
    for n in SMALL:
        sz = like[n].size
        out[n] = flat[off:off + sz].reshape(like[n].shape)
        off += sz
    return out


def _ffn_fwd(x, norm_g, wg, wu, wd, tag):
    h, r = _rms_fwd(x, norm_g, f"ffn_norm_{tag}")
    gate = _mm(h, wg, "nn", f"ffn_gate_{tag}", out_dtype=BF16)
    up = _mm(h, wu, "nn", f"ffn_up_{tag}", out_dtype=BF16)
    act = _swiglu_fwd(gate, up, f"swiglu_{tag}")
    out = _mm(act, wd, "nn", f"ffn_down_{tag}", add=x)
    return out, (x, r, h, gate, up, act)


def _ffn_bwd(dout, doutb, norm_g, wg, wu, wd, saved, tag):
    x, r, h, gate, up, act = saved
    dact = _mm(doutb, wd.T, "nn", f"ffn_dact_{tag}", out_dtype=BF16)
    dgate, dup = _swiglu_bwd(gate, up, dact, f"swiglu_bwd_{tag}")
    ht = h.T
    d_wd = _mm(act.T, doutb, "nn", f"ffn_dwd_{tag}", out_dtype=BF16)
    d_wg = _mm(ht, dgate, "nn", f"ffn_dwg_{tag}", out_dtype=BF16)
    d_wu = _mm(ht, dup, "nn", f"ffn_dwu_{tag}", out_dtype=BF16)
    dh = _mm(dgate, wg.T, "nn", f"ffn_dh1_{tag}")
    dh = _mm(dup, wu.T, "nn", f"ffn_dh2_{tag}", add=dh)
    dx, dxb, dnorm = _rms_bwd(x, norm_g, r, dh, f"ffn_norm_bwd_{tag}", dres=dout)
    return dx, dxb, dnorm, d_wg, d_wu, d_wd


def kernel(x, mix_norm, ffn_norm, w_in, conv_w, conv_b, dt_bias, a_log, d_skip, ssd_norm, q_norm, kv_norm, w_uq, w_ukv, w_out_even, w_qkv, w_out_odd, w_gate, w_up, w_down, final_norm, loss_target, m_mix_norm, m_ffn_norm, m_w_in, m_conv_w, m_conv_b, m_dt_bias, m_a_log, m_d_skip, m_ssd_norm, m_q_norm, m_kv_norm, m_w_uq, m_w_ukv, m_w_out_even, m_w_qkv, m_w_out_odd, m_w_gate, m_w_up, m_w_down, m_final_norm, v_mix_norm, v_ffn_norm, v_w_in, v_conv_w, v_conv_b, v_dt_bias, v_a_log, v_d_skip, v_ssd_norm, v_q_norm, v_kv_norm, v_w_uq, v_w_ukv, v_w_out_even, v_w_qkv, v_w_out_odd, v_w_gate, v_w_up, v_w_down, v_final_norm):
    given = dict(locals())
    wts = {n: given[n] for n in WEIGHTS}
    x0 = x[0]
    tgt = loss_target[0]
    t, d = x0.shape
    hw = SSD_HEADS * SSD_HEAD_DIM
    gn = SSD_GROUPS * SSD_STATE
    cc = hw + 2 * gn
    qr, kvr = MLA_Q_RANK, MLA_KV_RANK

    def shard(n, layer=None):
        a = wts[n] if layer is None else wts[n][layer:layer + 1]
        return a if n == 'conv_w' else a.astype(BF16)
    g0_names = ['w_in', 'conv_w']
    gq_names = ['w_uq', 'w_ukv']
    g1_names = [('w_out_even', None), ('w_gate', 0), ('w_up', 0), ('w_down', 0)]
    g2_names = [('w_qkv', None), ('w_out_odd', None), ('w_gate', 1), ('w_up', 1), ('w_down', 1)]
    hg0 = _xstart([shard(n) for n in g0_names], True, [], "gather0_start")
    hgq = _xstart([shard(n) for n in gq_names], True, [hg0[-1]], "gatherq_start")
    hg1 = _xstart([shard(n, l) for n, l in g1_names], True, [hgq[-1]], "gather1_start")
    hg2 = _xstart([shard(n, l) for n, l in g2_names], True, [hg1[-1]], "gather2_start")
    full = lambda n, g: _full_from_gather(n, g)[0]
    g0 = dict(zip(g0_names, _xwait(hg0, True, hg2[-1], "gather0_wait")))
    win = _win_pad(full('w_in', g0['w_in']))
    cw = _padc(full('conv_w', g0['conv_w']).T, 8).T
    o_cq, o_ckv, o_kr, o_dt = hw + cc, hw + cc + qr, hw + cc + qr + kvr, hw + cc + qr + kvr + LANES

    row = lambda v: v.reshape(1, -1)
    narrow = lambda v: _padc(v.reshape(1, -1), LANES)
    dtb, alog = narrow(dt_bias[0]), narrow(a_log[0])
    dsk_x = jnp.repeat(d_skip[0], SSD_HEAD_DIM).reshape(1, hw)
    cos, sin = _rope_tables(t)

    h0, r0 = _rms_fwd(x0, row(mix_norm[0]), "mix_norm_0")
    u = _mm(h0, win, "nn", "in_proj")
    z, xbc, c_q, c_kv = u[:, :hw], u[:, hw:hw + cc], u[:, o_cq:o_ckv], u[:, o_ckv:o_kr]
    kr_raw, dtraw = u[:, o_kr:o_dt], u[:, o_dt:]
    xc = _conv_fwd(xbc, cw, row(conv_b[0]), "conv")
    y_ssd, sprev = _ssd_fwd(dtraw, xc, dtb, alog, dsk_x, "ssd")
    yg, r_g = _gated_fwd(y_ssd, z, row(ssd_norm[0]), "ssd_gate_norm")
    q_lat, r_q = _rms_fwd(c_q, row(q_norm[0]), "q_norm")
    kv_lat, r_kv = _rms_fwd(c_kv, row(kv_norm[0]), "kv_norm")
    gq = dict(zip(gq_names, _xwait(hgq, True, r_kv, "gatherq_wait")))
    wuq = _wuq_pad(full('w_uq', gq['w_uq']))
    wukv = full('w_ukv', gq['w_ukv'])
    qf = _mm(q_lat, wuq, "nn", "q_up")
    kvb = _mm(kv_lat, wukv, "nn", "kv_up", out_dtype=BF16)
    q_r = _rope(qf, cos, sin, 2, False, BF16, "rope_q")
    k_r = _rope(kr_raw, cos, sin, 1, False, BF16, "rope_k")
    o_mla, lse = _mla_fwd(q_r, k_r, kvb, "mla")
    g1 = dict(zip(g1_names, _xwait(hg1, True, lse, "gather1_wait")))
    woe, wg0, wu0, wd0 = [full(k[0], g1[k]) for k in g1_names]
    cat = jnp.concatenate([yg, o_mla.astype(BF16)], axis=1)
    x1 = _mm(cat, woe, "nn", "mix_out_0", add=x0)
    x2, ffn0 = _ffn_fwd(x1, row(ffn_norm[0]), wg0, wu0, wd0, "0")

    g2 = dict(zip(g2_names, _xwait(hg2, True, x2, "gather2_wait")))
    wqkv, woo, wg1, wu1, wd1 = [full(k[0], g2[k]) for k in g2_names]
    h1, r1 = _rms_fwd(x2, row(mix_norm[1]), "mix_norm_1")
    qkv = _mm(h1, wqkv, "nn", "qkv_proj", out_dtype=BF16)
    o_sb, ltot = _sb_fwd(qkv, "sb")
    o_sbb = o_sb.astype(BF16)
    x3 = _mm(o_sbb, woo, "nn", "mix_out_1", add=x2)
    x4, ffn1 = _ffn_fwd(x3, row(ffn_norm[1]), wg1, wu1, wd1, "1")

    loss_part, dx4, dx4b, d_final = _loss_fwd_bwd(x4, row(final_norm), tgt, "loss")

    def grad_shards(pairs):
        return [_shards_from_full(n, [g]) for n, g in pairs]

    dx3, dx3b, d_ffn1, d_wg1, d_wu1, d_wd1 = _ffn_bwd(dx4, dx4b, row(ffn_norm[1]), wg1, wu1, wd1, ffn1, "1")
    do_sb = _mm(dx3b, woo.T, "nn", "sb_dout")
    d_woo = _mm(o_sbb.T, dx3b, "nn", "d_w_out_odd", out_dtype=BF16)
    dq, dk, dv = _sb_bwd(qkv, do_sb, ltot, "sb_bwd")
    dqkv = jnp.concatenate([dq, dk, dv], axis=1).astype(BF16)
    d_wqkv = _mm(h1.T, dqkv, "nn", "d_w_qkv", out_dtype=BF16)
    x2_names = ['w_qkv', 'w_out_odd', 'w_gate', 'w_up', 'w_down']
    hx2 = _xstart(grad_shards(zip(x2_names, [d_wqkv, d_woo, d_wg1, d_wu1, d_wd1])), False, [], "grads2_start")
    dh1 = _mm(dqkv, wqkv.T, "nn", "qkv_dh")
    dx2, dx2b, d_mix1 = _rms_bwd(x2, row(mix_norm[1]) + hx2[-1][0:1, 0:1], r1, dh1, "mix_norm_bwd_1", dres=dx3)

    dx1, dx1b, d_ffn0, d_wg0, d_wu0, d_wd0 = _ffn_bwd(dx2, dx2b, row(ffn_norm[0]), wg0, wu0, wd0, ffn0, "0")
    d_woe = _mm(cat.T, dx1b, "nn", "d_w_out_even", out_dtype=BF16)
    x1_names = ['w_out_even', 'w_gate', 'w_up', 'w_down']
    hx1 = _xstart(grad_shards(zip(x1_names, [d_woe, d_wg0, d_wu0, d_wd0])), False, [], "grads1_start")
    tok1 = hx1[-1][0:1, 0:1]
    dcat = _mm(dx1b, woe.T, "nn", "mix_dcat")
    dy_ssd, dz, d_ssdn = _gated_bwd(y_ssd, z, row(ssd_norm[0]) + tok1, r_g, dcat[:, :hw], "ssd_gate_norm_bwd")
    dxs, db_, dc_, ddtraw, d_dtb, d_alog, d_dsk = _ssd_bwd(dtraw, xc, dtb, alog, dsk_x, sprev, dy_ssd, "ssd_bwd")
    dxbc, d_cw, d_cb = _conv_bwd(xbc, cw, row(conv_b[0]), jnp.concatenate([dxs, db_, dc_], axis=1), "conv_bwd")
    dqm, dkvm, dkr = _mla_bwd(q_r, k_r, kvb, o_mla, dcat[:, hw:], lse + tok1, "mla_bwd")
    dqf = _rope(dqm, cos, sin, 2, True, BF16, "rope_q_bwd")
    dkr_raw = _rope(dkr, cos, sin, 1, True, F32, "rope_k_bwd")
    dkvf = dkvm.astype(BF16)
    d_wuq = _mm(q_lat.T, dqf, "nn", "d_w_uq", out_dtype=BF16)
    d_wukv = _mm(kv_lat.T, dkvf, "nn", "d_w_ukv", out_dtype=BF16)
    dq_lat = _mm(dqf, wuq.T, "nn", "q_up_bwd")
    dkv_lat = _mm(dkvf, wukv.T, "nn", "kv_up_bwd")
    dc_q, _, d_qn = _rms_bwd(c_q, row(q_norm[0]), r_q, dq_lat, "q_norm_bwd")
    dc_kv, _, d_kvn = _rms_bwd(c_kv, row(kv_norm[0]), r_kv, dkv_lat, "kv_norm_bwd")
    du = jnp.concatenate([dz, dxbc, dc_q, dc_kv, dkr_raw, ddtraw], axis=1).astype(BF16)
    d_win = _mm(h0.T, du, "nn", "d_w_in", out_dtype=BF16)
    x0_names = ['w_in', 'conv_w', 'w_uq', 'w_ukv']
    hx0 = _xstart(grad_shards(zip(x0_names, [_win_unpad(d_win), d_cw, _wuq_unpad(d_wuq), d_wukv])), False, [], "grads0_start")
    dh0 = _mm(du, win.T, "nn", "in_proj_bwd")
    grad_x, _, d_mix0 = _rms_bwd(x0, row(mix_norm[0]) + hx0[-1][0:1, 0:1], r0, dh0, "mix_norm_bwd_0", dres=dx1)

    def chip_sums(handle, names, after, tag):
        lands = _xwait(handle, False, after, f"grads{tag}_wait")
        return {n: _sum4(p.reshape(4, -1, p.shape[-1]), f"sum{tag}_{n}") for n, p in zip(names, lands)}
    grads, deltas, new_m, new_v = {}, {}, {}, {}

    def adamw(n, ga, gb):
        res = _adamw(_as2d(wts[n]), ga, gb, _as2d(given['m_' + n]), _as2d(given['v_' + n]), f"adamw_{n}")
        grads[n], deltas[n], new_m[n], new_v[n] = [r.reshape(wts[n].shape) for r in res]
        return res[0]
    s2 = chip_sums(hx2, x2_names, grad_x, "2")
    hs2 = _sib_start([s2[n] for n in x2_names], [], "sibling2_start")
    s1 = chip_sums(hx1, x1_names, hs2[-1], "1")
    hs1 = _sib_start([s1[n] for n in x1_names], [], "sibling1_start")
    r2 = dict(zip(x2_names, _sib_wait(hs2, hs1[-1], "sibling2_wait")))
    r1 = dict(zip(x1_names, _sib_wait(hs1, r2['w_qkv'], "sibling1_wait")))
    last = None
    for n in ('w_qkv', 'w_out_odd'):
        last = adamw(n, s2[n], r2[n])
    last = adamw('w_out_even', s1['w_out_even'], r1['w_out_even'])
    for n in ('w_gate', 'w_up', 'w_down'):
        last = adamw(n, jnp.concatenate([s1[n], s2[n]], axis=0), jnp.concatenate([r1[n], r2[n]], axis=0))
    s0 = chip_sums(hx0, x0_names, last, "0")
    r0_ = _sibling_exchange([s0[n] for n in x0_names], "exchange_sibling0")
    for n, gb in zip(x0_names, r0_):
        adamw(n, s0[n], gb)

    nhs = SSD_HEADS
    small_g = {'mix_norm': jnp.stack([d_mix0, d_mix1]), 'ffn_norm': jnp.stack([d_ffn0, d_ffn1]), 'conv_b': d_cb[None],
               'dt_bias': d_dtb[None, :nhs], 'a_log': d_alog[None, :nhs], 'd_skip': d_dsk[None, :nhs], 'ssd_norm': d_ssdn[None],
               'q_norm': d_qn[None], 'kv_norm': d_kvn[None], 'final_norm': d_final}
    packed, _ = lax.optimization_barrier((_pack_small(small_g, loss_part), r0_[0]))
    g_small = _allreduce_small(packed, "allreduce_small")
    loss = g_small.reshape(-1)[sum(wts[n].size for n in SMALL)]
    pk = lambda pre: _pack_small({n: given[pre + n] for n in SMALL}, jnp.zeros((), F32))
    sg, sd, sm, sv = _adamw(pk(''), g_small, None, pk('m_'), pk('v_'), "adamw_small")
    for dst, src in ((grads, sg), (deltas, sd), (new_m, sm), (new_v, sv)):
        dst.update(_unpack_small(src, wts))

    outs = [loss, grad_x[None]]
    for dct in (grads, deltas, new_m, new_v):
        outs += [dct[n] for n in WEIGHTS]
    return tuple(outs)
```

```python
import functools
import math

import jax
import jax.numpy as jnp
import numpy as np
from jax import lax
from jax.experimental import pallas as pl
from jax.experimental.pallas import tpu as pltpu

F32, BF16 = jnp.float32, jnp.bfloat16

RMS_EPS = 1e-6
SSD_HEADS, SSD_HEAD_DIM, SSD_GROUPS, SSD_STATE, SSD_CONV, SSD_CHUNK = 32, 64, 4, 128, 4, 128
MLA_HEADS, MLA_Q_RANK, MLA_KV_RANK, MLA_NOPE, MLA_ROPE, MLA_V = 16, 512, 512, 128, 64, 128
ROPE_THETA = 10000.0
SB_HEADS, SB_HEAD_DIM = 16, 128
ADAM_LR, ADAM_B1, ADAM_B2, ADAM_EPS, ADAM_WD, ADAM_STEP = 0.001, 0.9, 0.999, 1e-08, 0.01, 10

LANES = 128
VMEM_LIMIT_BYTES = 56 * 1024 * 1024
MM_VMEM_BUDGET = 40 * 1024 * 1024
ATT_BLK = 256
SB_FWD_HEADS = 4
SB_BWD_HEADS = 4
MLA_FWD_HEADS = 4
MLA_BWD_HEADS = 4
ROW_TILE = 256
NEG = -1e30

MESH_T = pl.DeviceIdType.MESH
WEIGHTS = ['mix_norm', 'ffn_norm', 'w_in', 'conv_w', 'conv_b', 'dt_bias', 'a_log', 'd_skip', 'ssd_norm', 'q_norm',
           'kv_norm', 'w_uq', 'w_ukv', 'w_out_even', 'w_qkv', 'w_out_odd', 'w_gate', 'w_up', 'w_down', 'final_norm']
SHARDED = ['w_in', 'conv_w', 'w_uq', 'w_ukv', 'w_out_even', 'w_qkv', 'w_out_odd', 'w_gate', 'w_up', 'w_down']
COL_SHARDED = ['w_in', 'conv_w', 'w_uq', 'w_ukv', 'w_qkv', 'w_gate', 'w_up']
SMALL = [n for n in WEIGHTS if n not in SHARDED]


def _tile(n, cands):
    for c in cands:
        if n % c == 0:
            return c
    return n


def _params(ngrid):
    return pltpu.CompilerParams(dimension_semantics=("arbitrary",) * ngrid, vmem_limit_bytes=VMEM_LIMIT_BYTES)


def _dot(a, b, mode="nn"):
    dims = {"nn": (((1,), (0,)), ((), ())), "nt": (((1,), (1,)), ((), ())), "tn": (((0,), (0,)), ((), ()))}[mode]
    return lax.dot_general(a, b, dims, preferred_element_type=F32)


def _split(x, parts):
    out, r = [], x
    for _ in range(parts):
        p = r.astype(BF16)
        out.append(p)
        r = r - p.astype(F32)
    return out


def _xdot(x, e, parts=3):
    acc = None
    for p in _split(x, parts):
        t = _dot(p, e)
        acc = t if acc is None else acc + t
    return acc


def _xdot_l(e, x, parts=3):
    acc = None
    for p in _split(x, parts):
        t = _dot(e, p)
        acc = t if acc is None else acc + t
    return acc


def _iota(shape, dim):
    return lax.broadcasted_iota(jnp.int32, shape, dim)


def _softplus(z):
    return jnp.maximum(z, 0.0) + jnp.log(1.0 + jnp.exp(-jnp.abs(z)))


def _sigmoid(z):
    return 1.0 / (1.0 + jnp.exp(-z))


def _acc_rows(ref, first, val):
    @pl.when(first)
    def _():
        ref[...] = jnp.zeros_like(ref)
    ref[...] += jnp.broadcast_to(val, ref.shape)


def _mm(a, b, mode, name, out_dtype=F32, add=None, b_shards=False, out_shards=False):
    assert mode == "nn"
    m, k = a.shape
    n = 4 * b.shape[3] if b_shards else b.shape[1]
    ns = n // 4 if (b_shards or out_shards) else n
    tn = _tile(ns, (1536, 1408, 1280, 1024, 768, 640, 512, 256, 128))
    per = ns // tn
    tk = k if k <= 2048 else _tile(k, (2048, 1536, 1408, 1280, 1024, 512, 256, 128))
    ob = jnp.dtype(out_dtype).itemsize

    def need(tm_):
        per = tm_ * tk * a.dtype.itemsize + tk * tn * b.dtype.itemsize + tm_ * tn * ob + (tm_ * tn * 4 if add is not None else 0)
        return 2 * per + (tm_ * tn * 4 if k > tk else 0)
    tm = m
    for cand in (1408, 1024, 512, 256, 128):
        if m % cand == 0:
            tm = cand
            if need(cand) <= MM_VMEM_BUDGET:
                break
    nk = k // tk
    a_spec = pl.BlockSpec((tm, tk), lambda i, j, kk: (i, kk))
    if b_shards:
        b_spec = pl.BlockSpec((None, None, tk, tn), lambda i, j, kk: (j // per, 0, kk, j % per))
    else:
        b_spec = pl.BlockSpec((tk, tn), lambda i, j, kk: (kk, j))
    if out_shards:
        assert add is None
        o_spec = pl.BlockSpec((None, tm, tn), lambda i, j, kk: (j // per, i, j % per))
    else:
        o_spec = pl.BlockSpec((tm, tn), lambda i, j, kk: (i, j))
    has_add = add is not None

    def body(*refs):
        a_ref, b_ref = refs[0], refs[1]
        add_ref = refs[2] if has_add else None
        o_ref = refs[2 + has_add]
        part = _dot(a_ref[...].astype(BF16), b_ref[...].astype(BF16), mode)
        if nk == 1:
            if has_add:
                part = part + add_ref[...]
            o_ref[...] = part.astype(o_ref.dtype)
            return
        acc_ref = refs[3 + has_add]
        kk = pl.program_id(2)

        @pl.when(kk == 0)
        def _():
            acc_ref[...] = jnp.zeros_like(acc_ref)
        acc_ref[...] += part

        @pl.when(kk == nk - 1)
        def _():
            r = acc_ref[...]
            if has_add:
                r = r + add_ref[...]
            o_ref[...] = r.astype(o_ref.dtype)

    ins = [a, b] + ([add] if has_add else [])
    specs = [a_spec, b_spec] + ([o_spec] if has_add else [])
    return pl.pallas_call(
        body, name=name, grid=(m // tm, n // tn, nk), in_specs=specs, out_specs=o_spec,
        out_shape=jax.ShapeDtypeStruct((4, m, ns) if out_shards else (m, n), out_dtype),
        scratch_shapes=[pltpu.VMEM((tm, tn), F32)] if nk > 1 else [],
        compiler_params=_params(3))(*ins)


def _rows(t):
    return _tile(t, (ROW_TILE, 128, 64, 32, 16, 8))


def _rms_fwd(x, g, name):
    t, c = x.shape
    tr = _rows(t)

    def body(x_ref, g_ref, h_ref, r_ref):
        xv = x_ref[...]
        r = lax.rsqrt(jnp.mean(xv * xv, axis=-1, keepdims=True) + RMS_EPS)
        h_ref[...] = (xv * r * g_ref[...]).astype(h_ref.dtype)
        r_ref[...] = r

    return pl.pallas_call(
        body, name=name, grid=(t // tr,),
        in_specs=[pl.BlockSpec((tr, c), lambda i: (i, 0)), pl.BlockSpec((1, c), lambda i: (0, 0))],
        out_specs=[pl.BlockSpec((tr, c), lambda i: (i, 0)), pl.BlockSpec((tr, 1), lambda i: (i, 0))],
        out_shape=[jax.ShapeDtypeStruct((t, c), BF16), jax.ShapeDtypeStruct((t, 1), F32)],
        compiler_params=_params(1))(x, g)


def _rms_bwd(x, g, r, dh, name, dres=None):
    t, c = x.shape
    tr = _rows(t)
    has_res = dres is not None

    def body(*refs):
        x_ref, g_ref, r_ref, dh_ref = refs[:4]
        res_ref = refs[4] if has_res else None
        dx_ref, dxb_ref, dg_ref = refs[4 + has_res:]
        rv = r_ref[...]
        xh = x_ref[...] * rv
        dhv = dh_ref[...]
        dxh = dhv * g_ref[...]
        cm = jnp.mean(dxh * xh, axis=-1, keepdims=True)
        dx = (dxh - xh * cm) * rv
        if has_res:
            dx = dx + res_ref[...]
        dx_ref[...] = dx
        dxb_ref[...] = dx.astype(BF16)
        _acc_rows(dg_ref, pl.program_id(0) == 0, jnp.sum(dhv * xh, axis=0, keepdims=True))

    row = pl.BlockSpec((tr, c), lambda i: (i, 0))
    ins = [x, g, r, dh] + ([dres] if has_res else [])
    specs = [row, pl.BlockSpec((1, c), lambda i: (0, 0)), pl.BlockSpec((tr, 1), lambda i: (i, 0)), row] + ([row] if has_res else [])
    dx, dxb, dg = pl.pallas_call(
        body, name=name, grid=(t // tr,), in_specs=specs,
        out_specs=[row, row, pl.BlockSpec((8, c), lambda i: (0, 0))],
        out_shape=[jax.ShapeDtypeStruct((t, c), F32), jax.ShapeDtypeStruct((t, c), BF16), jax.ShapeDtypeStruct((8, c), F32)],
        compiler_params=_params(1))(*ins)
    return dx, dxb, dg[0]


def _gated_fwd(y, z, g, name):
    t, c = y.shape
    tr = _rows(t)

    def body(y_ref, z_ref, g_ref, o_ref, r_ref):
        zv = z_ref[...]
        v = y_ref[...] * zv * _sigmoid(zv)
        r = lax.rsqrt(jnp.mean(v * v, axis=-1, keepdims=True) + RMS_EPS)
        o_ref[...] = (v * r * g_ref[...]).astype(o_ref.dtype)
        r_ref[...] = r

    row = pl.BlockSpec((tr, c), lambda i: (i, 0))
    return pl.pallas_call(
        body, name=name, grid=(t // tr,), in_specs=[row, row, pl.BlockSpec((1, c), lambda i: (0, 0))],
        out_specs=[row, pl.BlockSpec((tr, 1), lambda i: (i, 0))],
        out_shape=[jax.ShapeDtypeStruct((t, c), BF16), jax.ShapeDtypeStruct((t, 1), F32)],
        compiler_params=_params(1))(y, z, g)


def _gated_bwd(y, z, g, r, dout, name):
    t, c = y.shape
    tr = _rows(t)

    def body(y_ref, z_ref, g_ref, r_ref, do_ref, dy_ref, dz_ref, dg_ref):
        yv, zv, rv, dov = y_ref[...], z_ref[...], r_ref[...], do_ref[...]
        s = _sigmoid(zv)
        sz = zv * s
        xh = yv * sz * rv
        dxh = dov * g_ref[...]
        cm = jnp.mean(dxh * xh, axis=-1, keepdims=True)
        dv = (dxh - xh * cm) * rv
        dy_ref[...] = dv * sz
        dz_ref[...] = dv * yv * s * (1.0 + zv * (1.0 - s))
        _acc_rows(dg_ref, pl.program_id(0) == 0, jnp.sum(dov * xh, axis=0, keepdims=True))

    row = pl.BlockSpec((tr, c), lambda i: (i, 0))
    dy, dz, dg = pl.pallas_call(
        body, name=name, grid=(t // tr,),
        in_specs=[row, row, pl.BlockSpec((1, c), lambda i: (0, 0)), pl.BlockSpec((tr, 1), lambda i: (i, 0)), row],
        out_specs=[row, row, pl.BlockSpec((8, c), lambda i: (0, 0))],
        out_shape=[jax.ShapeDtypeStruct((t, c), F32), jax.ShapeDtypeStruct((t, c), F32), jax.ShapeDtypeStruct((8, c), F32)],
        compiler_params=_params(1))(y, z, g, r, dout)
    return dy, dz, dg[0]


def _swiglu_fwd(gate, up, name):
    t, c = gate.shape
    tr, tc = _rows(t), _tile(c, (2816, 1408, 1024, 512, 256, 128))

    def body(g_ref, u_ref, o_ref):
        gv = g_ref[...].astype(F32)
        o_ref[...] = (gv * _sigmoid(gv) * u_ref[...].astype(F32)).astype(o_ref.dtype)

    blk = pl.BlockSpec((tr, tc), lambda i, j: (i, j))
    return pl.pallas_call(body, name=name, grid=(t // tr, c // tc), in_specs=[blk, blk], out_specs=blk,
                          out_shape=jax.ShapeDtypeStruct((t, c), BF16), compiler_params=_params(2))(gate, up)


def _swiglu_bwd(gate, up, dact, name):
    t, c = gate.shape
    tr, tc = _rows(t), _tile(c, (2816, 1408, 1024, 512, 256, 128))

    def body(g_ref, u_ref, d_ref, dg_ref, du_ref):
        gv, dv = g_ref[...].astype(F32), d_ref[...].astype(F32)
        s = _sigmoid(gv)
        dg_ref[...] = (dv * u_ref[...].astype(F32) * s * (1.0 + gv * (1.0 - s))).astype(dg_ref.dtype)
        du_ref[...] = (dv * gv * s).astype(du_ref.dtype)

    blk = pl.BlockSpec((tr, tc), lambda i, j: (i, j))
    return pl.pallas_call(body, name=name, grid=(t // tr, c // tc), in_specs=[blk, blk, blk], out_specs=[blk, blk],
                          out_shape=[jax.ShapeDtypeStruct((t, c), BF16)] * 2, compiler_params=_params(2))(gate, up, dact)


def _loss_fwd_bwd(x, g, tgt, name):
    t, c = x.shape
    tr = _rows(t)

    def body(x_ref, g_ref, t_ref, l_ref, dx_ref, dxb_ref, dg_ref):
        xv, gv = x_ref[...], g_ref[...]
        r = lax.rsqrt(jnp.mean(xv * xv, axis=-1, keepdims=True) + RMS_EPS)
        xh = xv * r
        err = xh * gv - t_ref[...]
        per_row = jnp.mean(err * err, axis=-1, keepdims=True)
        dy = err * (1.0 / c)
        dxh = dy * gv
        cm = jnp.mean(dxh * xh, axis=-1, keepdims=True)
        dx = (dxh - xh * cm) * r
        dx_ref[...] = dx
        dxb_ref[...] = dx.astype(BF16)
        first = pl.program_id(0) == 0
        _acc_rows(dg_ref, first, jnp.sum(dy * xh, axis=0, keepdims=True))
        _acc_rows(l_ref, first, jnp.broadcast_to(0.5 * jnp.sum(per_row, axis=0, keepdims=True), (1, LANES)))

    row = pl.BlockSpec((tr, c), lambda i: (i, 0))
    lo, dx, dxb, dg = pl.pallas_call(
        body, name=name, grid=(t // tr,), in_specs=[row, pl.BlockSpec((1, c), lambda i: (0, 0)), row],
        out_specs=[pl.BlockSpec((8, LANES), lambda i: (0, 0)), row, row, pl.BlockSpec((8, c), lambda i: (0, 0))],
        out_shape=[jax.ShapeDtypeStruct((8, LANES), F32), jax.ShapeDtypeStruct((t, c), F32), jax.ShapeDtypeStruct((t, c), BF16),
                   jax.ShapeDtypeStruct((8, c), F32)],
        compiler_params=_params(1))(x, g, tgt)
    return lo[0, 0], dx, dxb, dg[0]


def _conv_specs(t, c):
    tr = _rows(t)
    tc = _tile(c, (1024, 768, 512, 256, 128))
    h8 = tr // 8
    tile = pl.BlockSpec((tr, tc), lambda j, i: (i, j))
    prev = pl.BlockSpec((8, tc), lambda j, i: (jnp.maximum(i * h8 - 1, 0), j))
    nxt = pl.BlockSpec((8, tc), lambda j, i: (jnp.minimum((i + 1) * h8, t // 8 - 1), j))
    return tr, tc, tile, prev, nxt


def _conv_fwd(xbc, w, b, name):
    t, c = xbc.shape
    tr, tc, tile, prev, _ = _conv_specs(t, c)

    def body(x_ref, p_ref, w_ref, b_ref, o_ref, buf):
        i = pl.program_id(1)
        buf[0:8, :] = jnp.where(i > 0, p_ref[...], 0.0)
        buf[8:, :] = x_ref[...]
        pre = b_ref[...]
        for k in range(SSD_CONV):
            pre = pre + w_ref[k:k + 1, :] * buf[pl.ds(8 - (SSD_CONV - 1) + k, tr), :]
        o_ref[...] = pre * _sigmoid(pre)

    return pl.pallas_call(
        body, name=name, grid=(c // tc, t // tr),
        in_specs=[tile, prev, pl.BlockSpec((8, tc), lambda j, i: (0, j)), pl.BlockSpec((1, tc), lambda j, i: (0, j))],
        out_specs=tile, out_shape=jax.ShapeDtypeStruct((t, c), F32),
        scratch_shapes=[pltpu.VMEM((tr + 8, tc), F32)], compiler_params=_params(2))(xbc, xbc, w, b)


def _conv_bwd(xbc, w, b, dout, name):
    t, c = xbc.shape
    tr, tc, tile, prev, nxt = _conv_specs(t, c)
    nt = t // tr
    kc = SSD_CONV

    def body(x_ref, p_ref, n_ref, w_ref, b_ref, d_ref, dn_ref, dx_ref, dw_ref, db_ref, buf, dbuf):
        i = pl.program_id(1)
        last = i == nt - 1
        buf[0:8, :] = jnp.where(i > 0, p_ref[...], 0.0)
        buf[8:8 + tr, :] = x_ref[...]
        buf[8 + tr:, :] = jnp.where(last, 0.0, n_ref[...])
        pre = b_ref[...]
        for k in range(kc):
            pre = pre + w_ref[k:k + 1, :] * buf[pl.ds(8 - (kc - 1) + k, tr + 8), :]
        s = _sigmoid(pre)
        dsilu = s * (1.0 + pre * (1.0 - s))
        dbuf[0:tr, :] = d_ref[...] * dsilu[0:tr, :]
        dbuf[tr:, :] = jnp.where(last, 0.0, dn_ref[...]) * dsilu[tr:, :]
        dpre = dbuf[0:tr, :]
        dx = jnp.zeros((tr, tc), F32)
        first = i == 0
        for k in range(kc):
            dx = dx + w_ref[k:k + 1, :] * dbuf[pl.ds(kc - 1 - k, tr), :]
        dx_ref[...] = dx

        @pl.when(first)
        def _():
            dw_ref[...] = jnp.zeros_like(dw_ref)
        for k in range(kc):
            dw_ref[k:k + 1, :] += jnp.sum(dpre * buf[pl.ds(8 - (kc - 1) + k, tr), :], axis=0, keepdims=True)
        _acc_rows(db_ref, first, jnp.sum(dpre, axis=0, keepdims=True))

    par = pl.BlockSpec((8, tc), lambda j, i: (0, j))
    dx, dw, db = pl.pallas_call(
        body, name=name, grid=(c // tc, nt),
        in_specs=[tile, prev, nxt, par, pl.BlockSpec((1, tc), lambda j, i: (0, j)), tile, nxt],
        out_specs=[tile, par, par],
        out_shape=[jax.ShapeDtypeStruct((t, c), F32), jax.ShapeDtypeStruct((8, c), F32), jax.ShapeDtypeStruct((8, c), F32)],
        scratch_shapes=[pltpu.VMEM((tr + 16, tc), F32), pltpu.VMEM((tr + 8, tc), F32)],
        compiler_params=_params(2))(xbc, xbc, xbc, w, b, dout, dout)
    return dx, dw[:kc], db[0]


def _ssd_consts():
    h, p, ln = SSD_HEADS, SSD_HEAD_DIM, SSD_CHUNK
    w = h * p
    hrow, jcol = _iota((LANES, w), 0), _iota((LANES, w), 1)
    e = ((jcol >= hrow * p) & (jcol < (hrow + 1) * p)).astype(BF16)
    jrow, hcol = _iota((w, LANES), 0), _iota((w, LANES), 1)
    et = ((jrow >= hcol * p) & (jrow < (hcol + 1) * p)).astype(BF16)
    row, col = _iota((ln, ln), 0), _iota((ln, ln), 1)
    return e, et, row, col


def _ssd_common(dtraw_ref, dtb_ref, alog_ref, e):
    ln = SSD_CHUNK
    raw = dtraw_ref[...] + dtb_ref[...]
    dt = _softplus(raw)
    a = -jnp.exp(alog_ref[...])
    adt = dt * a
    row, col = _iota((ln, ln), 0), _iota((ln, ln), 1)
    cs = _xdot_l((col <= row).astype(BF16), adt)
    cl = jnp.sum(adt, axis=0, keepdims=True)
    ex = _xdot(jnp.concatenate([dt, cs, jnp.broadcast_to(cl, (ln, LANES))], axis=0), e)
    return raw, dt, a, cs, ex[:ln], ex[ln:2 * ln], ex[2 * ln:]


def _head_decay(cs, h, causal):
    ln = SSD_CHUNK
    lane = _iota((1, LANES), 1)
    colv = jnp.sum(jnp.where(lane == h, cs, 0.0), axis=1, keepdims=True)
    cb = jnp.broadcast_to(colv, (ln, ln))
    return jnp.exp(jnp.where(causal, cb - cb.T, NEG))


def _ssd_fwd(dtraw, xc, dtb, alog, dskip_x, name):
    t = dtraw.shape[0]
    h, p, g, n, ln = SSD_HEADS, SSD_HEAD_DIM, SSD_GROUPS, SSD_STATE, SSD_CHUNK
    w, gn, gw, hpg = h * p, g * n, (h // g) * p, h // g
    assert n == ln and gw % LANES == 0 and w % gn == 0
    nc = t // ln

    def body(dtraw_ref, x_ref, b_ref, c_ref, dtb_ref, alog_ref, dsk_ref, y_ref, sp_ref, s_ref):
        @pl.when(pl.program_id(0) == 0)
        def _():
            s_ref[...] = jnp.zeros_like(s_ref)
        e, _, row, col = _ssd_consts()
        causal = col <= row
        _, _, _, cs, dt_x, cs_x, cl_x = _ssd_common(dtraw_ref, dtb_ref, alog_ref, e)
        xv = x_ref[...]
        xd = xv * dt_x
        xdb = xd.astype(BF16)
        sv = s_ref[...]
        sp_ref[0] = sv
        el_x = jnp.exp(cs_x)
        zb = (xd * jnp.exp(cl_x - cs_x)).astype(BF16)
        cd_x = jnp.exp(cl_x)
        dsk = dsk_ref[...]
        half = _iota((1, LANES), 1) >= p
        for gi in range(g):
            gs = slice(gi * gw, (gi + 1) * gw)
            bg = b_ref[:, gi * n:(gi + 1) * n].astype(BF16)
            cg = c_ref[:, gi * n:(gi + 1) * n].astype(BF16)
            gm = _dot(cg, bg, "nt")
            sg = sv[:, gs]
            yoff = _dot(cg, sg.astype(BF16)) * el_x[:, gs]
            s_ref[:, gs] = sg * cd_x[:, gs] + _dot(bg, zb[:, gs], "tn")
            for pp in range(gw // LANES):
                ls = slice(gi * gw + pp * LANES, gi * gw + (pp + 1) * LANES)
                xp = xdb[:, ls]
                yp = yoff[:, pp * LANES:(pp + 1) * LANES] + dsk[:, ls] * xv[:, ls]
                for hh in range(LANES // p):
                    hd = gi * hpg + pp * (LANES // p) + hh
                    wm = (gm * _head_decay(cs, hd, causal)).astype(BF16)
                    yp = yp + _dot(wm, jnp.where(half == (hh == 1), xp, jnp.zeros_like(xp)))
                y_ref[:, ls] = yp

    nar = pl.BlockSpec((ln, LANES), lambda c: (c, 0))
    one = pl.BlockSpec((1, LANES), lambda c: (0, 0))
    return pl.pallas_call(
        body, name=name, grid=(nc,),
        in_specs=[nar, pl.BlockSpec((ln, w), lambda c: (c, 0)), pl.BlockSpec((ln, gn), lambda c: (c, w // gn)),
                  pl.BlockSpec((ln, gn), lambda c: (c, w // gn + 1)), one, one, pl.BlockSpec((1, w), lambda c: (0, 0))],
        out_specs=[pl.BlockSpec((ln, w), lambda c: (c, 0)), pl.BlockSpec((1, n, w), lambda c: (c, 0, 0))],
        out_shape=[jax.ShapeDtypeStruct((t, w), F32), jax.ShapeDtypeStruct((nc, n, w), F32)],
        scratch_shapes=[pltpu.VMEM((n, w), F32)], compiler_params=_params(1))(dtraw, xc, xc, xc, dtb, alog, dskip_x)


def _ssd_bwd(dtraw, xc, dtb, alog, dskip_x, sprev, dy, name):
    t = dtraw.shape[0]
    h, p, g, n, ln = SSD_HEADS, SSD_HEAD_DIM, SSD_GROUPS, SSD_STATE, SSD_CHUNK
    w, gn, gw, hpg = h * p, g * n, (h // g) * p, h // g
    nc = t // ln

    def body(dtraw_ref, x_ref, b_ref, c_ref, dtb_ref, alog_ref, dsk_ref, sp_ref, dy_ref,
             dx_ref, db_ref, dc_ref, ddt_ref, dbias_ref, dalog_ref, ddsk_ref, ds_ref, dxd_ref, qcs_ref):
        first = pl.program_id(0) == 0

        @pl.when(first)
        def _():
            ds_ref[...] = jnp.zeros_like(ds_ref)
        e, et, row, col = _ssd_consts()
        causal = col <= row
        raw, dt, a, cs, dt_x, cs_x, cl_x = _ssd_common(dtraw_ref, dtb_ref, alog_ref, e)
        xv = x_ref[...]
        xd = xv * dt_x
        xdb = xd.astype(BF16)
        sv = sp_ref[0]
        dyv = dy_ref[...]
        dyb = dyv.astype(BF16)
        dsn = ds_ref[...]
        el_x = jnp.exp(cs_x)
        dte_x = jnp.exp(cl_x - cs_x)
        cd_x = jnp.exp(cl_x)
        zf = xd * dte_x
        lane = _iota((1, LANES), 1)
        half = lane >= p
        lastrow = _iota((ln, 1), 0) == ln - 1
        dcs = jnp.zeros((ln, LANES), F32)
        for gi in range(g):
            gs = slice(gi * gw, (gi + 1) * gw)
            ns = slice(gi * n, (gi + 1) * n)
            bg = b_ref[:, ns].astype(BF16)
            cg = c_ref[:, ns].astype(BF16)
            gm = _dot(cg, bg, "nt")
            sgb = sv[:, gs].astype(BF16)
            dsg = dsn[:, gs]
            dsgb = dsg.astype(BF16)
            yoff = _dot(cg, sgb) * el_x[:, gs]
            drb = (el_x[:, gs] * dyv[:, gs]).astype(BF16)
            dcg = _dot(drb, sgb, "nt")
            ds_ref[:, gs] = cd_x[:, gs] * dsg + _dot(cg, drb, "tn")
            dz = _dot(bg, dsgb)
            zg = zf[:, gs]
            dbg = _dot(zg.astype(BF16), dsgb, "nt")
            dzz = dz * zg
            qcl = jnp.sum(dzz + cd_x[:, gs] * dsg * sv[:, gs], axis=0, keepdims=True)
            qcs_ref[:, gs] = dyv[:, gs] * yoff - dzz + jnp.where(lastrow, jnp.broadcast_to(qcl, (ln, gw)), 0.0)
            dgm = jnp.zeros((ln, ln), F32)
            for pp in range(gw // LANES):
                ls = slice(gi * gw + pp * LANES, gi * gw + (pp + 1) * LANES)
                xp = xdb[:, ls]
                dxp = dz[:, pp * LANES:(pp + 1) * LANES] * dte_x[:, ls]
                for hh in range(LANES // p):
                    hd = gi * hpg + pp * (LANES // p) + hh
                    dm = _head_decay(cs, hd, causal)
                    wf = gm * dm
                    dym = jnp.where(half == (hh == 1), dyb[:, ls], jnp.zeros_like(xp))
                    dw = _dot(dym, xp, "nt")
                    dxp = dxp + _dot(wf.astype(BF16), dym, "tn")
                    dgm = dgm + dw * dm
                    mm = dw * wf
                    rc = jnp.sum(mm, axis=1, keepdims=True) - jnp.sum(mm.T, axis=1, keepdims=True)
                    dcs = dcs + rc * (lane == hd).astype(F32)
                dxd_ref[:, ls] = dxp
            dgb = dgm.astype(BF16)
            dc_ref[:, ns] = dcg + _dot(dgb, bg)
            db_ref[:, ns] = dbg + _dot(dgb, cg, "tn")
        dxd = dxd_ref[...]
        dx_ref[...] = dxd * dt_x + dsk_ref[...] * dyv
        red = _xdot(jnp.concatenate([qcs_ref[...], dxd * xv, dyv * xv], axis=0), et)
        dcs = dcs + red[:ln]
        dadt = _xdot_l((row <= col).astype(BF16), dcs)
        ddt = red[ln:2 * ln] + dadt * a
        draw = ddt * _sigmoid(raw)
        ddt_ref[...] = draw
        _acc_rows(dbias_ref, first, jnp.sum(draw, axis=0, keepdims=True))
        _acc_rows(dalog_ref, first, jnp.sum(dadt * dt, axis=0, keepdims=True) * a)
        _acc_rows(ddsk_ref, first, jnp.sum(red[2 * ln:], axis=0, keepdims=True))

    rev = lambda c: nc - 1 - c
    nar = pl.BlockSpec((ln, LANES), lambda c: (rev(c), 0))
    one = pl.BlockSpec((1, LANES), lambda c: (0, 0))
    wide = pl.BlockSpec((ln, w), lambda c: (rev(c), 0))
    bcs = pl.BlockSpec((ln, gn), lambda c: (rev(c), 0))
    acc = pl.BlockSpec((8, LANES), lambda c: (0, 0))
    outs = pl.pallas_call(
        body, name=name, grid=(nc,),
        in_specs=[nar, wide, pl.BlockSpec((ln, gn), lambda c: (rev(c), w // gn)), pl.BlockSpec((ln, gn), lambda c: (rev(c), w // gn + 1)),
                  one, one, pl.BlockSpec((1, w), lambda c: (0, 0)), pl.BlockSpec((1, n, w), lambda c: (rev(c), 0, 0)), wide],
        out_specs=[wide, bcs, bcs, nar, acc, acc, acc],
        out_shape=[jax.ShapeDtypeStruct((t, w), F32), jax.ShapeDtypeStruct((t, gn), F32), jax.ShapeDtypeStruct((t, gn), F32),
                   jax.ShapeDtypeStruct((t, LANES), F32)] + [jax.ShapeDtypeStruct((8, LANES), F32)] * 3,
        scratch_shapes=[pltpu.VMEM((n, w), F32), pltpu.VMEM((ln, w), F32), pltpu.VMEM((ln, w), F32)],
        compiler_params=_params(1))(dtraw, xc, xc, xc, dtb, alog, dskip_x, sprev, dy)
    dx, db, dc, ddt, dbias, dalog, ddsk = outs
    return dx, db, dc, ddt, dbias[0], dalog[0], ddsk[0]


def _rope_tables(t):
    half = MLA_ROPE // 2
    inv_freq = ROPE_THETA ** (-jnp.arange(half, dtype=F32) / half)
    ang = jnp.arange(t, dtype=F32)[:, None] * inv_freq[None, :]
    cos, sin = jnp.cos(ang), jnp.sin(ang)
    pad = LANES - MLA_ROPE
    cos_r = jnp.concatenate([cos, cos, jnp.ones((t, pad), F32)], axis=1)
    sin_r = jnp.concatenate([sin, sin, jnp.zeros((t, pad), F32)], axis=1)
    return cos_r, sin_r


def _rot_matrix():
    half = MLA_ROPE // 2
    i, j = _iota((LANES, LANES), 0), _iota((LANES, LANES), 1)
    neg = (j < half) & (i == j + half)
    pos = (j >= half) & (j < 2 * half) & (i == j - half)
    return (pos.astype(F32) - neg.astype(F32)).astype(BF16)


def _rope(x, cos, sin, every, transpose, out_dtype, name):
    t, w = x.shape
    tr = _rows(t)

    def body(x_ref, c_ref, s_ref, o_ref):
        rot = _rot_matrix()
        cv, sv = c_ref[...], s_ref[...]
        for j in range(w // LANES):
            ls = slice(j * LANES, (j + 1) * LANES)
            xv = x_ref[:, ls]
            if j % every != every - 1:
                o_ref[:, ls] = xv.astype(o_ref.dtype)
            elif transpose:
                o_ref[:, ls] = (xv * cv - _xdot(xv * sv, rot, 2)).astype(o_ref.dtype)
            else:
                o_ref[:, ls] = (xv * cv + _xdot(xv, rot, 2) * sv).astype(o_ref.dtype)

    wide = pl.BlockSpec((tr, w), lambda i: (i, 0))
    tab = pl.BlockSpec((tr, LANES), lambda i: (i, 0))
    return pl.pallas_call(
        body, name=name, grid=(t // tr,), in_specs=[wide, tab, tab], out_specs=wide,
        out_shape=jax.ShapeDtypeStruct((t, w), out_dtype), compiler_params=_params(1))(x, cos, sin)


def _att_masks(blk):
    return _iota((blk, blk), 0), _iota((blk, blk), 1)


def _lanes(j):
    return slice(j * LANES, (j + 1) * LANES)


def _mla_fwd(q, kr, kv, name):
    t = q.shape[0]
    nh, blk = MLA_HEADS, min(ATT_BLK, t)
    scale = (MLA_NOPE + MLA_ROPE) ** -0.5
    hps = math.gcd(nh, MLA_FWD_HEADS)

    def body(q_ref, kv_ref, kr_ref, o_ref, lse_ref):
        i = pl.program_id(1)
        row, col = _att_masks(blk)
        qs = [q_ref[:, 2 * hh * LANES:(2 * hh + 2) * LANES] for hh in range(hps)]

        def scores(kb, hh):
            ks = pl.ds(pl.multiple_of(kb * blk, blk), blk)
            kfull = jnp.concatenate([kv_ref[ks, _lanes(2 * hh)], kr_ref[ks, :]], axis=1)
            return _dot(qs[hh], kfull, "nt") * scale, kv_ref[ks, _lanes(2 * hh + 1)]
        init = []
        for hh in range(hps):
            s, v = scores(i, hh)
            s = jnp.where(col <= row, s, NEG)
            m = jnp.max(s, axis=1, keepdims=True)
            pr = jnp.exp(s - m)
            init += [m, jnp.sum(pr, axis=1, keepdims=True), _dot(pr.astype(BF16), v)]

        def step(kb, carry):
            sv = [scores(kb, hh) for hh in range(hps)]
            out, prs = [], []
            for hh in range(hps):
                m, l, acc = carry[3 * hh:3 * hh + 3]
                s = sv[hh][0]
                m2 = jnp.maximum(m, jnp.max(s, axis=1, keepdims=True))
                al = jnp.exp(m - m2)
                pr = jnp.exp(s - m2)
                prs.append(pr.astype(BF16))
                out += [m2, al * l + jnp.sum(pr, axis=1, keepdims=True), al * acc]
            for hh in range(hps):
                out[3 * hh + 2] = out[3 * hh + 2] + _dot(prs[hh], sv[hh][1])
            return tuple(out)
        res = lax.fori_loop(0, i, step, tuple(init))
        for hh in range(hps):
            m, l, acc = res[3 * hh:3 * hh + 3]
            o_ref[:, _lanes(hh)] = acc / l
            lse_ref[hh] = m + jnp.log(l)

    return pl.pallas_call(
        body, name=name, grid=(nh // hps, t // blk),
        in_specs=[pl.BlockSpec((blk, 2 * hps * LANES), lambda h, i: (i, h)),
                  pl.BlockSpec((t, 2 * hps * LANES), lambda h, i: (0, h), pipeline_mode=pl.Buffered(1)),
                  pl.BlockSpec((t, LANES), lambda h, i: (0, 0), pipeline_mode=pl.Buffered(1))],
        out_specs=[pl.BlockSpec((blk, hps * LANES), lambda h, i: (i, h)), pl.BlockSpec((hps, blk, 1), lambda h, i: (h, i, 0))],
        out_shape=[jax.ShapeDtypeStruct((t, nh * LANES), F32), jax.ShapeDtypeStruct((nh, t, 1), F32)],
        compiler_params=_params(2))(q, kv, kr)


def _mla_bwd(q, kr, kv, o, do, lse, name):
    t = q.shape[0]
    nh, blk = MLA_HEADS, min(ATT_BLK, t)
    hps = math.gcd(nh, MLA_BWD_HEADS)
    scale = (MLA_NOPE + MLA_ROPE) ** -0.5

    def body(q_ref, kv_ref, kr_ref, o_ref, do_ref, lse_ref, dq_ref, dkv_ref, dkr_ref):
        hp, i = pl.program_id(0), pl.program_id(1)

        @pl.when(i == 0)
        def _():
            dkv_ref[...] = jnp.zeros_like(dkv_ref)

        @pl.when((i == 0) & (hp == 0))
        def _():
            dkr_ref[...] = jnp.zeros_like(dkr_ref)
        row, col = _att_masks(blk)
        qs = [q_ref[:, 2 * hh * LANES:(2 * hh + 2) * LANES] for hh in range(hps)]
        dobs = [do_ref[:, _lanes(hh)].astype(BF16) for hh in range(hps)]
        deltas = [jnp.sum(do_ref[:, _lanes(hh)] * o_ref[:, _lanes(hh)], axis=1, keepdims=True) for hh in range(hps)]
        lses = [lse_ref[hh] for hh in range(hps)]

        def tile(kb, carry, masked):
            ks = pl.ds(pl.multiple_of(kb * blk, blk), blk)
            krv = kr_ref[ks, :]
            hs = range(hps)
            kfull = [jnp.concatenate([kv_ref[ks, _lanes(2 * hh)], krv], axis=1) for hh in hs]
            ss = [_dot(qs[hh], kfull[hh], "nt") for hh in hs]
            dps = [_dot(dobs[hh], kv_ref[ks, _lanes(2 * hh + 1)], "nt") for hh in hs]
            prb, dsb = [], []
            for hh in hs:
                pr = jnp.exp(ss[hh] * scale - lses[hh])
                if masked:
                    pr = jnp.where(col <= row, pr, 0.0)
                prb.append(pr.astype(BF16))
                dsb.append((pr * (dps[hh] - deltas[hh]) * scale).astype(BF16))
            out, dkr = [], None
            for hh in hs:
                dkv_ref[ks, _lanes(2 * hh + 1)] += _dot(prb[hh], dobs[hh], "tn")
                dk = _dot(dsb[hh], qs[hh], "tn")
                dkv_ref[ks, _lanes(2 * hh)] += dk[:, :LANES]
                dkr = dk[:, LANES:] if dkr is None else dkr + dk[:, LANES:]
                out.append(carry[hh] + _dot(dsb[hh], kfull[hh]))
            dkr_ref[ks, :] += dkr
            return tuple(out)
        zero = jnp.zeros((blk, 2 * LANES), F32)
        carry = lax.fori_loop(0, i, lambda kb, c: tile(kb, c, False), (zero,) * hps)
        res = tile(i, carry, True)
        for hh in range(hps):
            dq_ref[:, 2 * hh * LANES:(2 * hh + 2) * LANES] = res[hh]

    qb = lambda w: pl.BlockSpec((blk, w * LANES), lambda h, i: (i, h))
    seq = lambda w, f: pl.BlockSpec((t, w * LANES), f, pipeline_mode=pl.Buffered(1))
    wide = jax.ShapeDtypeStruct((t, nh * 2 * LANES), F32)
    return pl.pallas_call(
        body, name=name, grid=(nh // hps, t // blk),
        in_specs=[qb(2 * hps), seq(2 * hps, lambda h, i: (0, h)), seq(1, lambda h, i: (0, 0)),
                  qb(hps), qb(hps), pl.BlockSpec((hps, blk, 1), lambda h, i: (h, i, 0))],
        out_specs=[qb(2 * hps), seq(2 * hps, lambda h, i: (0, h)), seq(1, lambda h, i: (0, 0))],
        out_shape=[wide, wide, jax.ShapeDtypeStruct((t, LANES), F32)],
        compiler_params=_params(2))(q, kv, kr, o, do, lse)


def _sb_fwd(qkv, name):
    t = qkv.shape[0]
    nh, blk = SB_HEADS, min(ATT_BLK, t)
    hps = math.gcd(nh, SB_FWD_HEADS)
    nq = t // blk
    scale = SB_HEAD_DIM ** -0.5

    def body(q_ref, k_ref, v_ref, o_ref, lt_ref):
        i = pl.program_id(1)
        row, col = _att_masks(blk)
        usuf = (row > col).astype(BF16)
        qs = [q_ref[:, _lanes(hh)] for hh in range(hps)]

        def tile(kb, carry, masked):
            ks = pl.ds(pl.multiple_of(kb * blk, blk), blk)
            hs = range(hps)
            zs = [_dot(qs[hh], k_ref[ks, _lanes(hh)], "nt") for hh in hs]
            lks, lss = [], []
            for hh in hs:
                z = zs[hh] * scale
                lk = -_softplus(z)
                lss.append(lk + z)
                lks.append(jnp.where(col < row, lk, 0.0) if masked else lk)
            pieces = [_split(lks[hh], 2) for hh in hs]
            later = [_dot(pieces[hh][0], usuf) + _dot(pieces[hh][1], usuf) for hh in hs]
            out = []
            for hh in hs:
                wt = jnp.exp(lss[hh] + later[hh] + carry[2 * hh + 1])
                if masked:
                    wt = jnp.where(col < row, wt, 0.0)
                out += [wt.astype(BF16), carry[2 * hh + 1] + jnp.sum(lks[hh], axis=1, keepdims=True)]
            for hh in hs:
                out[2 * hh] = carry[2 * hh] + _dot(out[2 * hh], v_ref[ks, _lanes(hh)])
            return tuple(out)
        za, zr = jnp.zeros((blk, LANES), F32), jnp.zeros((blk, 1), F32)
        carry = tile(i, (za, zr) * hps, True)
        carry = lax.fori_loop(0, i, lambda j, c: tile(i - 1 - j, c, False), carry)
        for hh in range(hps):
            o_ref[:, _lanes(hh)] = carry[2 * hh]
            lt_ref[hh] = carry[2 * hh + 1]

    ng = nh // hps
    full = lambda f: pl.BlockSpec((t, hps * LANES), f)
    return pl.pallas_call(
        body, name=name, grid=(ng, nq),
        in_specs=[pl.BlockSpec((blk, hps * LANES), lambda h, i: (i, h)), full(lambda h, i: (0, ng + h)), full(lambda h, i: (0, 2 * ng + h))],
        out_specs=[pl.BlockSpec((blk, hps * LANES), lambda h, i: (i, h)), pl.BlockSpec((hps, blk, 1), lambda h, i: (h, i, 0))],
        out_shape=[jax.ShapeDtypeStruct((t, nh * LANES), F32), jax.ShapeDtypeStruct((nh, t, 1), F32)],
        compiler_params=_params(2))(qkv, qkv, qkv)


def _sb_bwd(qkv, do, ltot, name):
    t = qkv.shape[0]
    nh, blk = SB_HEADS, min(ATT_BLK, t)
    hps = math.gcd(nh, SB_BWD_HEADS)
    nq = t // blk
    scale = SB_HEAD_DIM ** -0.5

    def body(q_ref, k_ref, v_ref, do_ref, lt_ref, dq_ref, dk_ref, dv_ref):
        i = pl.program_id(1)

        @pl.when(i == 0)
        def _():
            dk_ref[...] = jnp.zeros_like(dk_ref)
            dv_ref[...] = jnp.zeros_like(dv_ref)
        row, col = _att_masks(blk)
        uinc = (row <= col).astype(BF16)
        uexc = (row < col).astype(BF16)
        qs = [q_ref[:, _lanes(hh)] for hh in range(hps)]
        dobs = [do_ref[:, _lanes(hh)].astype(BF16) for hh in range(hps)]
        lts = [lt_ref[hh] for hh in range(hps)]

        def tile(kb, carry, masked):
            ks = pl.ds(pl.multiple_of(kb * blk, blk), blk)
            hs = range(hps)
            kvs = [k_ref[ks, _lanes(hh)] for hh in hs]
            zs = [_dot(qs[hh], kvs[hh], "nt") for hh in hs]
            dws = [_dot(dobs[hh], v_ref[ks, _lanes(hh)], "nt") for hh in hs]
            lks, lss = [], []
            for hh in hs:
                z = zs[hh] * scale
                lk = -_softplus(z)
                lss.append(lk + z)
                lks.append(jnp.where(col < row, lk, 0.0) if masked else lk)
            pieces = [_split(lks[hh], 2) for hh in hs]
            css = [_dot(pieces[hh][0], uinc) + _dot(pieces[hh][1], uinc) for hh in hs]
            wts, evs = [], []
            for hh in hs:
                wt = jnp.exp(lss[hh] + (lts[hh] - (carry[3 * hh + 1] + css[hh])))
                if masked:
                    wt = jnp.where(col < row, wt, 0.0)
                wts.append(wt.astype(BF16))
                evs.append(dws[hh] * wt)
            epieces = [_split(evs[hh], 2) for hh in hs]
            ecss = [_dot(epieces[hh][0], uexc) + _dot(epieces[hh][1], uexc) for hh in hs]
            for hh in hs:
                dv_ref[ks, _lanes(hh)] += _dot(wts[hh], dobs[hh], "tn")
            dzbs = []
            for hh in hs:
                sig = jnp.exp(lss[hh])
                dz = evs[hh] * (1.0 - sig) - (ecss[hh] + carry[3 * hh + 2]) * sig
                if masked:
                    dz = jnp.where(col < row, dz, 0.0)
                dzbs.append((dz * scale).astype(BF16))
            out = []
            for hh in hs:
                dk_ref[ks, _lanes(hh)] += _dot(dzbs[hh], qs[hh], "tn")
                out += [carry[3 * hh] + _dot(dzbs[hh], kvs[hh]), carry[3 * hh + 1] + jnp.sum(lks[hh], axis=1, keepdims=True),
                        carry[3 * hh + 2] + jnp.sum(evs[hh], axis=1, keepdims=True)]
            return tuple(out)
        za, z1 = jnp.zeros((blk, LANES), F32), jnp.zeros((blk, 1), F32)
        carry = lax.fori_loop(0, i, lambda kb, c: tile(kb, c, False), (za, z1, z1) * hps)
        res = tile(i, carry, True)
        for hh in range(hps):
            dq_ref[:, _lanes(hh)] = res[3 * hh]

    ng = nh // hps
    full = lambda f: pl.BlockSpec((t, hps * LANES), f, pipeline_mode=pl.Buffered(1))
    qb = pl.BlockSpec((blk, hps * LANES), lambda h, i: (i, h))
    wide = jax.ShapeDtypeStruct((t, nh * LANES), F32)
    return pl.pallas_call(
        body, name=name, grid=(ng, nq),
        in_specs=[qb, full(lambda h, i: (0, ng + h)), full(lambda h, i: (0, 2 * ng + h)), qb,
                  pl.BlockSpec((hps, blk, 1), lambda h, i: (h, i, 0))],
        out_specs=[qb, full(lambda h, i: (0, h)), full(lambda h, i: (0, h))],
        out_shape=[wide, wide, wide], compiler_params=_params(2))(qkv, qkv, qkv, do, ltot)


def _as2d(a):
    return a.reshape((-1, a.shape[-1]))


def _sum4(parts, name):
    _, r, c = parts.shape
    tr = _tile(r, (256, 128, 64, 32, 16, 8))

    def body(p_ref, o_ref):
        acc = p_ref[0].astype(F32)
        for j in range(1, 4):
            acc = acc + p_ref[j].astype(F32)
        o_ref[...] = acc

    return pl.pallas_call(body, name=name, grid=(r // tr,), in_specs=[pl.BlockSpec((4, tr, c), lambda i: (0, i, 0))],
                          out_specs=pl.BlockSpec((tr, c), lambda i: (i, 0)), out_shape=jax.ShapeDtypeStruct((r, c), F32),
                          compiler_params=_params(1))(parts)


def _adamw(w, ga, gb, m, v, name):
    r, c = w.shape
    tr = _tile(r, (128, 64, 32, 16, 8))
    c1, c2 = 1.0 - ADAM_B1 ** ADAM_STEP, 1.0 - ADAM_B2 ** ADAM_STEP
    two = gb is not None

    def body(*refs):
        w_ref, ga_ref = refs[0], refs[1]
        gb_ref = refs[2] if two else None
        m_ref, v_ref, g_out, d_out, m_out, v_out = refs[2 + two:]
        gv = ga_ref[...]
        if two:
            gv = gv + gb_ref[...]
        mn = ADAM_B1 * m_ref[...] + (1.0 - ADAM_B1) * gv
        vn = ADAM_B2 * v_ref[...] + (1.0 - ADAM_B2) * (gv * gv)
        g_out[...] = gv
        m_out[...] = mn
        v_out[...] = vn
        d_out[...] = -ADAM_LR * ((mn / c1) / (jnp.sqrt(vn / c2) + ADAM_EPS) + ADAM_WD * w_ref[...])

    blk = pl.BlockSpec((tr, c), lambda i: (i, 0))
    ins = [w, ga] + ([gb] if two else []) + [m, v]
    return pl.pallas_call(body, name=name, grid=(r // tr,), in_specs=[blk] * len(ins), out_specs=[blk] * 4,
                          out_shape=[jax.ShapeDtypeStruct((r, c), F32)] * 4, compiler_params=_params(1))(*ins)


HBM_SPEC = pl.BlockSpec(memory_space=pltpu.HBM)
SEM_SPEC = pl.BlockSpec(memory_space=pltpu.SEMAPHORE)
EFFECT = pltpu.SideEffectType.DATAFLOW_SIDE_EFFECTING


def _chip_copies(ins, lands, send_sems, recv_sems, whole):
    x, y, c = lax.axis_index("x"), lax.axis_index("y"), lax.axis_index("c")
    me = 2 * x + y
    out = []
    for wi in range(len(ins)):
        for k, (px, py) in enumerate([(1 - x, y), (x, 1 - y), (1 - x, 1 - y)]):
            sems = dict(send_sem=send_sems[wi * 3 + k], recv_sem=recv_sems[wi * 3 + k], device_id=(px, py, c), device_id_type=MESH_T)
            peer = 2 * px + py
            sent = pltpu.make_async_remote_copy(src_ref=ins[wi] if whole else ins[wi].at[peer], dst_ref=lands[wi].at[me], **sems)
            got = functools.partial(pltpu.make_async_remote_copy, src_ref=ins[wi] if whole else ins[wi].at[me],
                                    dst_ref=lands[wi].at[peer], **sems)
            out.append((sent, got))
    return out


def _xstart(arrs, whole, after, name):
    n, na = len(arrs), len(after)
    me = 2 * lax.axis_index("x") + lax.axis_index("y")
    lands = []
    for a in arrs:
        own = a[None] if whole else lax.dynamic_slice_in_dim(a, me, 1, axis=0)
        empty = lax.empty(((4,) + a.shape) if whole else a.shape, a.dtype)
        lands.append(lax.dynamic_update_slice_in_dim(empty, own, me, axis=0))

    def body(*refs):
        ins, lands_in = refs[:n], refs[n:2 * n]
        outs = refs[2 * n + na:]
        for sent, _ in _chip_copies(ins, lands_in, outs[:3 * n], outs[3 * n:6 * n], whole):
            sent.start()
        outs[8 * n][...] = jnp.zeros((8, LANES), F32)

    hbm = lambda a: pltpu.HBM(a.shape, a.dtype)
    res = pl.pallas_call(
        body, name=name,
        out_shape=[pltpu.SemaphoreType.DMA(())] * (6 * n) + [hbm(a) for a in arrs] + [hbm(a) for a in lands]
        + [jax.ShapeDtypeStruct((8, LANES), F32)],
        in_specs=[HBM_SPEC] * (2 * n) + [pl.BlockSpec(memory_space=pl.ANY)] * na,
        out_specs=[SEM_SPEC] * (6 * n) + [HBM_SPEC] * (2 * n) + [pl.BlockSpec(memory_space=pltpu.VMEM)],
        input_output_aliases={i: 6 * n + i for i in range(2 * n)},
        compiler_params=pltpu.CompilerParams(has_side_effects=EFFECT),
    )(*[pltpu.with_memory_space_constraint(a, pltpu.HBM) for a in list(arrs) + lands], *after)
    return res


def _xwait(handle, whole, after, name):
    n = (len(handle) - 1) // 8
    sems, thru = handle[:6 * n], handle[6 * n:8 * n]

    def body(*refs):
        ins, lands_in = refs[:n], refs[n:2 * n]
        for sent, got in _chip_copies(ins, lands_in, refs[2 * n:5 * n], refs[5 * n:8 * n], whole):
            sent.wait_send()
            got().wait_recv()

    hbm = lambda a: pltpu.HBM(a.shape, a.dtype)
    res = pl.pallas_call(
        body, name=name, out_shape=[hbm(a) for a in thru],
        in_specs=[HBM_SPEC] * (2 * n) + [SEM_SPEC] * (6 * n) + [pl.BlockSpec(memory_space=pl.ANY)],
        out_specs=[HBM_SPEC] * (2 * n), input_output_aliases={i: i for i in range(2 * n)},
        compiler_params=pltpu.CompilerParams(has_side_effects=EFFECT),
    )(*thru, *sems, after)
    return res[n:]


def _sibling_copies(ins, lands, send_sems, recv_sems):
    sib = (lax.axis_index("x"), lax.axis_index("y"), 1 - lax.axis_index("c"))
    return [pltpu.make_async_remote_copy(src_ref=ins[wi], dst_ref=lands[wi], send_sem=send_sems[wi], recv_sem=recv_sems[wi],
                                         device_id=sib, device_id_type=MESH_T) for wi in range(len(ins))]


def _sib_start(arrs, after, name):
    n, na = len(arrs), len(after)
    lands = [lax.empty(a.shape, a.dtype) for a in arrs]

    def body(*refs):
        outs = refs[2 * n + na:]
        for cp in _sibling_copies(refs[:n], refs[n:2 * n], outs[:n], outs[n:2 * n]):
            cp.start()
        outs[4 * n][...] = jnp.zeros((8, LANES), F32)

    hbm = lambda a: pltpu.HBM(a.shape, a.dtype)
    return pl.pallas_call(
        body, name=name,
        out_shape=[pltpu.SemaphoreType.DMA(())] * (2 * n) + [hbm(a) for a in arrs] * 2 + [jax.ShapeDtypeStruct((8, LANES), F32)],
        in_specs=[HBM_SPEC] * (2 * n) + [pl.BlockSpec(memory_space=pl.ANY)] * na,
        out_specs=[SEM_SPEC] * (2 * n) + [HBM_SPEC] * (2 * n) + [pl.BlockSpec(memory_space=pltpu.VMEM)],
        input_output_aliases={i: 2 * n + i for i in range(2 * n)},
        compiler_params=pltpu.CompilerParams(has_side_effects=EFFECT),
    )(*[pltpu.with_memory_space_constraint(a, pltpu.HBM) for a in list(arrs) + lands], *after)


def _sib_wait(handle, after, name):
    n = (len(handle) - 1) // 4
    sems, thru = handle[:2 * n], handle[2 * n:4 * n]

    def body(*refs):
        for cp in _sibling_copies(refs[:n], refs[n:2 * n], refs[2 * n:3 * n], refs[3 * n:4 * n]):
            cp.wait_send()
            cp.wait_recv()

    res = pl.pallas_call(
        body, name=name, out_shape=[pltpu.HBM(a.shape, a.dtype) for a in thru],
        in_specs=[HBM_SPEC] * (2 * n) + [SEM_SPEC] * (2 * n) + [pl.BlockSpec(memory_space=pl.ANY)],
        out_specs=[HBM_SPEC] * (2 * n), input_output_aliases={i: i for i in range(2 * n)},
        compiler_params=pltpu.CompilerParams(has_side_effects=EFFECT),
    )(*thru, *sems, after)
    return res[n:]


def _sibling_exchange(arrs, name):
    n = len(arrs)

    def body(*refs):
        ins, outs = refs[:n], refs[n:2 * n]
        send_sems, recv_sems = refs[2 * n:]
        sib = (lax.axis_index("x"), lax.axis_index("y"), 1 - lax.axis_index("c"))
        cps = [pltpu.make_async_remote_copy(src_ref=ins[wi], dst_ref=outs[wi], send_sem=send_sems.at[wi], recv_sem=recv_sems.at[wi],
                                            device_id=sib, device_id_type=MESH_T) for wi in range(n)]
        for cp in cps:
            cp.start()
        for cp in cps:
            cp.wait_recv()
        for cp in cps:
            cp.wait_send()

    anyspec = pl.BlockSpec(memory_space=pl.ANY)
    return pl.pallas_call(
        body, name=name, in_specs=[anyspec] * n, out_specs=[anyspec] * n,
        out_shape=[jax.ShapeDtypeStruct(a.shape, a.dtype) for a in arrs],
        scratch_shapes=[pltpu.SemaphoreType.DMA((n,)), pltpu.SemaphoreType.DMA((n,))],
        compiler_params=pltpu.CompilerParams(has_side_effects=True))(*arrs)


def _allreduce_small(packed, name):
    r = packed.shape[0]

    def body(in_ref, out_ref, land, send_sems, recv_sems):
        x, y, c = lax.axis_index("x"), lax.axis_index("y"), lax.axis_index("c")
        me = 4 * x + 2 * y + c
        land[me] = in_ref[...]
        rel = [(dx, dy, dc) for dx in (0, 1) for dy in (0, 1) for dc in (0, 1)][1:]
        peers = [((1 - x) if dx else x, (1 - y) if dy else y, (1 - c) if dc else c) for dx, dy, dc in rel]
        sends = []
        for k, peer in enumerate(peers):
            cp = pltpu.make_async_remote_copy(src_ref=in_ref, dst_ref=land.at[me], send_sem=send_sems.at[k], recv_sem=recv_sems.at[k],
                                              device_id=peer, device_id_type=MESH_T)
            cp.start()
            sends.append(cp)
        for k, (px, py, pc) in enumerate(peers):
            pltpu.make_async_remote_copy(src_ref=in_ref, dst_ref=land.at[4 * px + 2 * py + pc], send_sem=send_sems.at[k],
                                         recv_sem=recv_sems.at[k], device_id=(px, py, pc), device_id_type=MESH_T).wait_recv()
        for cp in sends:
            cp.wait_send()
        acc = land[0]
        for j in range(1, 8):
            acc = acc + land[j]
        out_ref[...] = acc

    vm = pl.BlockSpec(memory_space=pltpu.VMEM)
    return pl.pallas_call(
        body, name=name, in_specs=[vm], out_specs=vm, out_shape=jax.ShapeDtypeStruct((r, LANES), F32),
        scratch_shapes=[pltpu.VMEM((8, r, LANES), F32), pltpu.SemaphoreType.DMA((7,)), pltpu.SemaphoreType.DMA((7,))],
        compiler_params=pltpu.CompilerParams(has_side_effects=True))(packed)


def _in_splits():
    w = SSD_HEADS * SSD_HEAD_DIM
    cc = w + 2 * SSD_GROUPS * SSD_STATE
    return [w, cc, SSD_HEADS, MLA_Q_RANK, MLA_KV_RANK, MLA_ROPE]


def _padc(a, n):
    return jnp.pad(a, ((0, 0), (0, n - a.shape[1])))


def _win_pad(wm):
    offs = np.cumsum(_in_splits())[:-1]
    z, xbc, dt, cq, ckv, kr = jnp.split(wm, offs, axis=1)
    return jnp.concatenate([z, xbc, cq, ckv, _padc(kr, LANES), _padc(dt, LANES)], axis=1)


def _win_unpad(g):
    w, cc, nh, qr, kvr, rp = _in_splits()
    offs = np.cumsum([w, cc, qr, kvr, LANES])
    z, xbc, cq, ckv, kr, dt = jnp.split(g, offs, axis=1)
    return jnp.concatenate([z, xbc, dt[:, :nh], cq, ckv, kr[:, :rp]], axis=1)


def _wuq_pad(wm):
    r = wm.shape[0]
    w3 = wm.reshape(r, MLA_HEADS, MLA_NOPE + MLA_ROPE)
    return jnp.pad(w3, ((0, 0), (0, 0), (0, 2 * LANES - MLA_NOPE - MLA_ROPE))).reshape(r, MLA_HEADS * 2 * LANES)


def _wuq_unpad(g):
    r = g.shape[0]
    return g.reshape(r, MLA_HEADS, 2 * LANES)[:, :, :MLA_NOPE + MLA_ROPE].reshape(r, MLA_HEADS * (MLA_NOPE + MLA_ROPE))


def _full_from_gather(name, g):
    layers = g.shape[1]
    if name not in COL_SHARDED:
        return [g[:, l].reshape(-1, g.shape[-1]) for l in range(layers)]
    return [jnp.concatenate([g[j, l] for j in range(4)], axis=1) for l in range(layers)]


def _gathered_t(g):
    return g.transpose(0, 1, 3, 2).reshape(-1, g.shape[2])


def _shards_from_full(name, mats):
    col = name in COL_SHARDED
    per = []
    for mt in mats:
        r, c = mt.shape
        per.append(mt.reshape(r, 4, c // 4).transpose(1, 0, 2) if col else mt.reshape(4, r // 4, c))
    return jnp.stack(per, axis=1).astype(BF16)


def _pack_small(vals, extra=None):
    flat = jnp.concatenate([vals[n].reshape(-1).astype(F32) for n in SMALL] + ([extra.reshape(1)] if extra is not None else []))
    rows = -(-flat.shape[0] // (8 * LANES)) * 8
    return jnp.pad(flat, (0, rows * LANES - flat.shape[0])).reshape(rows, LANES)


def _unpack_small(packed, like):
    flat, out, off = packed.reshape(-1), {}, 0
    for n in SMALL:
        sz = like[n].size
        out[n] = flat[off:off + sz].reshape(like[n].shape)
        off += sz
    return out


def _ffn_fwd(x, norm_g, wg, wu, wd, tag):
    h, r = _rms_fwd(x, norm_g, f"ffn_norm_{tag}")
    gate = _mm(h, wg, "nn", f"ffn_gate_{tag}", out_dtype=BF16, b_shards=True)
    up = _mm(h, wu, "nn", f"ffn_up_{tag}", out_dtype=BF16, b_shards=True)
    act = _swiglu_fwd(gate, up, f"swiglu_{tag}")
    out = _mm(act, wd, "nn", f"ffn_down_{tag}", add=x)
    return out, (x, r, h, gate, up, act)


def _ffn_bwd(dout, doutb, norm_g, wg, wu, wd, saved, tag):
    x, r, h, gate, up, act = saved
    dact = _mm(doutb, wd.T, "nn", f"ffn_dact_{tag}", out_dtype=BF16)
    dgate, dup = _swiglu_bwd(gate, up, dact, f"swiglu_bwd_{tag}")
    ht = h.T
    d_wd = _mm(act.T, doutb, "nn", f"ffn_dwd_{tag}", out_dtype=BF16)
    d_wg = _mm(ht, dgate, "nn", f"ffn_dwg_{tag}", out_dtype=BF16, out_shards=True)
    d_wu = _mm(ht, dup, "nn", f"ffn_dwu_{tag}", out_dtype=BF16, out_shards=True)
    dh = _mm(dgate, _gathered_t(wg), "nn", f"ffn_dh1_{tag}")
    dh = _mm(dup, _gathered_t(wu), "nn", f"ffn_dh2_{tag}", add=dh)
    dx, dxb, dnorm = _rms_bwd(x, norm_g, r, dh, f"ffn_norm_bwd_{tag}", dres=dout)
    return dx, dxb, dnorm, d_wg, d_wu, d_wd


def kernel(x, mix_norm, ffn_norm, w_in, conv_w, conv_b, dt_bias, a_log, d_skip, ssd_norm, q_norm, kv_norm, w_uq, w_ukv, w_out_even, w_qkv, w_out_odd, w_gate, w_up, w_down, final_norm, loss_target, m_mix_norm, m_ffn_norm, m_w_in, m_conv_w, m_conv_b, m_dt_bias, m_a_log, m_d_skip, m_ssd_norm, m_q_norm, m_kv_norm, m_w_uq, m_w_ukv, m_w_out_even, m_w_qkv, m_w_out_odd, m_w_gate, m_w_up, m_w_down, m_final_norm, v_mix_norm, v_ffn_norm, v_w_in, v_conv_w, v_conv_b, v_dt_bias, v_a_log, v_d_skip, v_ssd_norm, v_q_norm, v_kv_norm, v_w_uq, v_w_ukv, v_w_out_even, v_w_qkv, v_w_out_odd, v_w_gate, v_w_up, v_w_down, v_final_norm):
    given = dict(locals())
    wts = {n: given[n] for n in WEIGHTS}
    x0 = x[0]
    tgt = loss_target[0]
    t, d = x0.shape
    hw = SSD_HEADS * SSD_HEAD_DIM
    gn = SSD_GROUPS * SSD_STATE
    cc = hw + 2 * gn
    qr, kvr = MLA_Q_RANK, MLA_KV_RANK

    def shard(n, layer=None):
        a = wts[n] if layer is None else wts[n][layer:layer + 1]
        return a if n == 'conv_w' else a.astype(BF16)
    g0_names = ['w_in', 'conv_w']
    gq_names = ['w_uq', 'w_ukv']
    g1_names = [('w_out_even', None), ('w_gate', 0), ('w_up', 0), ('w_down', 0)]
    g2_names = [('w_qkv', None), ('w_out_odd', None), ('w_gate', 1), ('w_up', 1), ('w_down', 1)]
    hg0 = _xstart([shard(n) for n in g0_names], True, [], "gather0_start")
    hgq = _xstart([shard(n) for n in gq_names], True, [hg0[-1]], "gatherq_start")
    hg1 = _xstart([shard(n, l) for n, l in g1_names], True, [hgq[-1]], "gather1_start")
    hg2 = _xstart([shard(n, l) for n, l in g2_names], True, [hg1[-1]], "gather2_start")
    full = lambda n, g: _full_from_gather(n, g)[0]
    g0 = dict(zip(g0_names, _xwait(hg0, True, hg2[-1], "gather0_wait")))
    win = _win_pad(full('w_in', g0['w_in']))
    cw = _padc(full('conv_w', g0['conv_w']).T, 8).T
    o_cq, o_ckv, o_kr, o_dt = hw + cc, hw + cc + qr, hw + cc + qr + kvr, hw + cc + qr + kvr + LANES

    row = lambda v: v.reshape(1, -1)
    narrow = lambda v: _padc(v.reshape(1, -1), LANES)
    dtb, alog = narrow(dt_bias[0]), narrow(a_log[0])
    dsk_x = jnp.repeat(d_skip[0], SSD_HEAD_DIM).reshape(1, hw)
    cos, sin = _rope_tables(t)

    h0, r0 = _rms_fwd(x0, row(mix_norm[0]), "mix_norm_0")
    u = _mm(h0, win, "nn", "in_proj")
    z, xbc, c_q, c_kv = u[:, :hw], u[:, hw:hw + cc], u[:, o_cq:o_ckv], u[:, o_ckv:o_kr]
    kr_raw, dtraw = u[:, o_kr:o_dt], u[:, o_dt:]
    xc = _conv_fwd(xbc, cw, row(conv_b[0]), "conv")
    y_ssd, sprev = _ssd_fwd(dtraw, xc, dtb, alog, dsk_x, "ssd")
    yg, r_g = _gated_fwd(y_ssd, z, row(ssd_norm[0]), "ssd_gate_norm")
    q_lat, r_q = _rms_fwd(c_q, row(q_norm[0]), "q_norm")
    kv_lat, r_kv = _rms_fwd(c_kv, row(kv_norm[0]), "kv_norm")
    gq = dict(zip(gq_names, _xwait(hgq, True, r_kv, "gatherq_wait")))
    wuq = _wuq_pad(full('w_uq', gq['w_uq']))
    wukv = gq['w_ukv']
    qf = _mm(q_lat, wuq, "nn", "q_up")
    kvb = _mm(kv_lat, wukv, "nn", "kv_up", out_dtype=BF16, b_shards=True)
    q_r = _rope(qf, cos, sin, 2, False, BF16, "rope_q")
    k_r = _rope(kr_raw, cos, sin, 1, False, BF16, "rope_k")
    o_mla, lse = _mla_fwd(q_r, k_r, kvb, "mla")
    g1 = dict(zip(g1_names, _xwait(hg1, True, lse, "gather1_wait")))
    woe, wd0 = full('w_out_even', g1[g1_names[0]]), full('w_down', g1[g1_names[3]])
    wg0, wu0 = g1[g1_names[1]], g1[g1_names[2]]
    cat = jnp.concatenate([yg, o_mla.astype(BF16)], axis=1)
    x1 = _mm(cat, woe, "nn", "mix_out_0", add=x0)
    x2, ffn0 = _ffn_fwd(x1, row(ffn_norm[0]), wg0, wu0, wd0, "0")

    g2 = dict(zip(g2_names, _xwait(hg2, True, x2, "gather2_wait")))
    woo, wd1 = full('w_out_odd', g2[g2_names[1]]), full('w_down', g2[g2_names[4]])
    wqkv, wg1, wu1 = g2[g2_names[0]], g2[g2_names[2]], g2[g2_names[3]]
    h1, r1 = _rms_fwd(x2, row(mix_norm[1]), "mix_norm_1")
    qkv = _mm(h1, wqkv, "nn", "qkv_proj", out_dtype=BF16, b_shards=True)
    o_sb, ltot = _sb_fwd(qkv, "sb")
    o_sbb = o_sb.astype(BF16)
    x3 = _mm(o_sbb, woo, "nn", "mix_out_1", add=x2)
    x4, ffn1 = _ffn_fwd(x3, row(ffn_norm[1]), wg1, wu1, wd1, "1")

    loss_part, dx4, dx4b, d_final = _loss_fwd_bwd(x4, row(final_norm), tgt, "loss")

    def grad_shards(pairs):
        return [g[:, None] if g.ndim == 3 else _shards_from_full(n, [g]) for n, g in pairs]

    dx3, dx3b, d_ffn1, d_wg1, d_wu1, d_wd1 = _ffn_bwd(dx4, dx4b, row(ffn_norm[1]), wg1, wu1, wd1, ffn1, "1")
    do_sb = _mm(dx3b, woo.T, "nn", "sb_dout")
    d_woo = _mm(o_sbb.T, dx3b, "nn", "d_w_out_odd", out_dtype=BF16)
    dq, dk, dv = _sb_bwd(qkv, do_sb, ltot, "sb_bwd")
    dqkv = jnp.concatenate([dq, dk, dv], axis=1).astype(BF16)
    d_wqkv = _mm(h1.T, dqkv, "nn", "d_w_qkv", out_dtype=BF16, out_shards=True)
    x2_names = ['w_qkv', 'w_out_odd', 'w_gate', 'w_up', 'w_down']
    hx2 = _xstart(grad_shards(zip(x2_names, [d_wqkv, d_woo, d_wg1, d_wu1, d_wd1])), False, [], "grads2_start")
    dh1 = _mm(dqkv, _gathered_t(wqkv), "nn", "qkv_dh")
    dx2, dx2b, d_mix1 = _rms_bwd(x2, row(mix_norm[1]) + hx2[-1][0:1, 0:1], r1, dh1, "mix_norm_bwd_1", dres=dx3)

    dx1, dx1b, d_ffn0, d_wg0, d_wu0, d_wd0 = _ffn_bwd(dx2, dx2b, row(ffn_norm[0]), wg0, wu0, wd0, ffn0, "0")
    d_woe = _mm(cat.T, dx1b, "nn", "d_w_out_even", out_dtype=BF16)
    x1_names = ['w_out_even', 'w_gate', 'w_up', 'w_down']
    hx1 = _xstart(grad_shards(zip(x1_names, [d_woe, d_wg0, d_wu0, d_wd0])), False, [], "grads1_start")
    tok1 = hx1[-1][0:1, 0:1]
    dcat = _mm(dx1b, woe.T, "nn", "mix_dcat")
    dy_ssd, dz, d_ssdn = _gated_bwd(y_ssd, z, row(ssd_norm[0]) + tok1, r_g, dcat[:, :hw], "ssd_gate_norm_bwd")
    dxs, db_, dc_, ddtraw, d_dtb, d_alog, d_dsk = _ssd_bwd(dtraw, xc, dtb, alog, dsk_x, sprev, dy_ssd, "ssd_bwd")
    dxbc, d_cw, d_cb = _conv_bwd(xbc, cw, row(conv_b[0]), jnp.concatenate([dxs, db_, dc_], axis=1), "conv_bwd")
    dqm, dkvm, dkr = _mla_bwd(q_r, k_r, kvb, o_mla, dcat[:, hw:], lse + tok1, "mla_bwd")
    dqf = _rope(dqm, cos, sin, 2, True, BF16, "rope_q_bwd")
    dkr_raw = _rope(dkr, cos, sin, 1, True, F32, "rope_k_bwd")
    dkvf = dkvm.astype(BF16)
    d_wuq = _mm(q_lat.T, dqf, "nn", "d_w_uq", out_dtype=BF16)
    d_wukv = _mm(kv_lat.T, dkvf, "nn", "d_w_ukv", out_dtype=BF16, out_shards=True)
    dq_lat = _mm(dqf, wuq.T, "nn", "q_up_bwd")
    dkv_lat = _mm(dkvf, _gathered_t(wukv), "nn", "kv_up_bwd")
    dc_q, _, d_qn = _rms_bwd(c_q, row(q_norm[0]), r_q, dq_lat, "q_norm_bwd")
    dc_kv, _, d_kvn = _rms_bwd(c_kv, row(kv_norm[0]), r_kv, dkv_lat, "kv_norm_bwd")
    du = jnp.concatenate([dz, dxbc, dc_q, dc_kv, dkr_raw, ddtraw], axis=1).astype(BF16)
    d_win = _mm(h0.T, du, "nn", "d_w_in", out_dtype=BF16)
    x0_names = ['w_in', 'conv_w', 'w_uq', 'w_ukv']
    hx0 = _xstart(grad_shards(zip(x0_names, [_win_unpad(d_win), d_cw, _wuq_unpad(d_wuq), d_wukv])), False, [], "grads0_start")
    dh0 = _mm(du, win.T, "nn", "in_proj_bwd")
    grad_x, _, d_mix0 = _rms_bwd(x0, row(mix_norm[0]) + hx0[-1][0:1, 0:1], r0, dh0, "mix_norm_bwd_0", dres=dx1)

    def chip_sums(handle, names, after, tag):
        lands = _xwait(handle, False, after, f"grads{tag}_wait")
        return {n: _sum4(p.reshape(4, -1, p.shape[-1]), f"sum{tag}_{n}") for n, p in zip(names, lands)}
    grads, deltas, new_m, new_v = {}, {}, {}, {}

    def adamw(n, ga, gb):
        res = _adamw(_as2d(wts[n]), ga, gb, _as2d(given['m_' + n]), _as2d(given['v_' + n]), f"adamw_{n}")
        grads[n], deltas[n], new_m[n], new_v[n] = [r.reshape(wts[n].shape) for r in res]
        return res[0]
    s2 = chip_sums(hx2, x2_names, grad_x, "2")
    hs2 = _sib_start([s2[n] for n in x2_names], [], "sibling2_start")
    s1 = chip_sums(hx1, x1_names, hs2[-1], "1")
    hs1 = _sib_start([s1[n] for n in x1_names], [], "sibling1_start")
    r2 = dict(zip(x2_names, _sib_wait(hs2, hs1[-1], "sibling2_wait")))
    r1 = dict(zip(x1_names, _sib_wait(hs1, r2['w_qkv'], "sibling1_wait")))
    last = None
    for n in ('w_qkv', 'w_out_odd'):
        last = adamw(n, s2[n], r2[n])
    last = adamw('w_out_even', s1['w_out_even'], r1['w_out_even'])
    for n in ('w_gate', 'w_up', 'w_down'):
        last = adamw(n, jnp.concatenate([s1[n], s2[n]], axis=0), jnp.concatenate([r1[n], r2[n]], axis=0))
    s0 = chip_sums(hx0, x0_names, last, "0")
    r0_ = _sibling_exchange([s0[n] for n in x0_names], "exchange_sibling0")
    for n, gb in zip(x0_names, r0_):
        adamw(n, s0[n], gb)

    nhs = SSD_HEADS
    small_g = {'mix_norm': jnp.stack([d_mix0, d_mix1]), 'ffn_norm': jnp.stack([d_ffn0, d_ffn1]), 'conv_b': d_cb[None],
               'dt_bias': d_dtb[None, :nhs], 'a_log': d_alog[None, :nhs], 'd_skip': d_dsk[None, :nhs], 'ssd_norm': d_ssdn[None],
               'q_norm': d_qn[None], 'kv_norm': d_kvn[None], 'final_norm': d_final}
    packed, _ = lax.optimization_barrier((_pack_small(small_g, loss_part), r0_[0]))
    g_small = _allreduce_small(packed, "allreduce_small")
    loss = g_small.reshape(-1)[sum(wts[n].size for n in SMALL)]
    pk = lambda pre: _pack_small({n: given[pre + n] for n in SMALL}, jnp.zeros((), F32))
    sg, sd, sm, sv = _adamw(pk(''), g_small, None, pk('m_'), pk('v_'), "adamw_small")
    for dst, src in ((grads, sg), (deltas, sd), (new_m, sm), (new_v, sv)):
        dst.update(_unpack_small(src, wts))

    outs = [loss, grad_x[None]]
    for dct in (grads, deltas, new_m, new_v):
        outs += [dct[n] for n in WEIGHTS]
    return tuple(outs)
```

```python
import functools
import math

import jax
import jax.numpy as jnp
import numpy as np
from jax import lax
from jax.experimental import pallas as pl
from jax.experimental.pallas import tpu as pltpu

F32, BF16 = jnp.float32, jnp.bfloat16

RMS_EPS = 1e-6
SSD_HEADS, SSD_HEAD_DIM, SSD_GROUPS, SSD_STATE, SSD_CONV, SSD_CHUNK = 32, 64, 4, 128, 4, 128
MLA_HEADS, MLA_Q_RANK, MLA_KV_RANK, MLA_NOPE, MLA_ROPE, MLA_V = 16, 512, 512, 128, 64, 128
ROPE_THETA = 10000.0
SB_HEADS, SB_HEAD_DIM = 16, 128
ADAM_LR, ADAM_B1, ADAM_B2, ADAM_EPS, ADAM_WD, ADAM_STEP = 0.001, 0.9, 0.999, 1e-08, 0.01, 10

LANES = 128
VMEM_LIMIT_BYTES = 56 * 1024 * 1024
MM_VMEM_BUDGET = 40 * 1024 * 1024
ATT_BLK = 256
SB_FWD_HEADS = 4
SB_BWD_HEADS = 4
MLA_FWD_HEADS = 4
MLA_BWD_HEADS = 4
ROW_TILE = 256
NEG = -1e30

MESH_T = pl.DeviceIdType.MESH
WEIGHTS = ['mix_norm', 'ffn_norm', 'w_in', 'conv_w', 'conv_b', 'dt_bias', 'a_log', 'd_skip', 'ssd_norm', 'q_norm',
           'kv_norm', 'w_uq', 'w_ukv', 'w_out_even', 'w_qkv', 'w_out_odd', 'w_gate', 'w_up', 'w_down', 'final_norm']
SHARDED = ['w_in', 'conv_w', 'w_uq', 'w_ukv', 'w_out_even', 'w_qkv', 'w_out_odd', 'w_gate', 'w_up', 'w_down']
COL_SHARDED = ['w_in', 'conv_w', 'w_uq', 'w_ukv', 'w_qkv', 'w_gate', 'w_up']
SMALL = [n for n in WEIGHTS if n not in SHARDED]


def _tile(n, cands):
    for c in cands:
        if n % c == 0:
            return c
    return n


def _params(ngrid):
    return pltpu.CompilerParams(dimension_semantics=("arbitrary",) * ngrid, vmem_limit_bytes=VMEM_LIMIT_BYTES)


def _dot(a, b, mode="nn"):
    dims = {"nn": (((1,), (0,)), ((), ())), "nt": (((1,), (1,)), ((), ())), "tn": (((0,), (0,)), ((), ()))}[mode]
    return lax.dot_general(a, b, dims, preferred_element_type=F32)


def _split(x, parts):
    out, r = [], x
    for _ in range(parts):
        p = r.astype(BF16)
        out.append(p)
        r = r - p.astype(F32)
    return out


def _xdot(x, e, parts=3):
    acc = None
    for p in _split(x, parts):
        t = _dot(p, e)
        acc = t if acc is None else acc + t
    return acc


def _xdot_l(e, x, parts=3):
    acc = None
    for p in _split(x, parts):
        t = _dot(e, p)
        acc = t if acc is None else acc + t
    return acc


def _iota(shape, dim):
    return lax.broadcasted_iota(jnp.int32, shape, dim)


def _softplus(z):
    return jnp.maximum(z, 0.0) + jnp.log(1.0 + jnp.exp(-jnp.abs(z)))


def _sigmoid(z):
    return 1.0 / (1.0 + jnp.exp(-z))


def _acc_rows(ref, first, val):
    @pl.when(first)
    def _():
        ref[...] = jnp.zeros_like(ref)
    ref[...] += jnp.broadcast_to(val, ref.shape)


def _mm(a, b, mode, name, out_dtype=F32, add=None, b_shards=False, out_shards=False):
    assert mode == "nn"
    m, k = a.shape
    n = 4 * b.shape[3] if b_shards else b.shape[1]
    ns = n // 4 if (b_shards or out_shards) else n
    tn = _tile(ns, (1536, 1408, 1280, 1024, 768, 640, 512, 256, 128))
    per = ns // tn
    tk = k if k <= 2048 else _tile(k, (2048, 1536, 1408, 1280, 1024, 512, 256, 128))
    ob = jnp.dtype(out_dtype).itemsize

    def need(tm_):
        per = tm_ * tk * a.dtype.itemsize + tk * tn * b.dtype.itemsize + tm_ * tn * ob + (tm_ * tn * 4 if add is not None else 0)
        return 2 * per + (tm_ * tn * 4 if k > tk else 0)
    tm = m
    for cand in (1408, 1024, 512, 256, 128):
        if m % cand == 0:
            tm = cand
            if need(cand) <= MM_VMEM_BUDGET:
                break
    nk = k // tk
    a_spec = pl.BlockSpec((tm, tk), lambda i, j, kk: (i, kk))
    if b_shards:
        b_spec = pl.BlockSpec((None, None, tk, tn), lambda i, j, kk: (j // per, 0, kk, j % per))
    else:
        b_spec = pl.BlockSpec((tk, tn), lambda i, j, kk: (kk, j))
    if out_shards:
        assert add is None
        o_spec = pl.BlockSpec((None, tm, tn), lambda i, j, kk: (j // per, i, j % per))
    else:
        o_spec = pl.BlockSpec((tm, tn), lambda i, j, kk: (i, j))
    has_add = add is not None

    def body(*refs):
        a_ref, b_ref = refs[0], refs[1]
        add_ref = refs[2] if has_add else None
        o_ref = refs[2 + has_add]
        part = _dot(a_ref[...].astype(BF16), b_ref[...].astype(BF16), mode)
        if nk == 1:
            if has_add:
                part = part + add_ref[...]
            o_ref[...] = part.astype(o_ref.dtype)
            return
        acc_ref = refs[3 + has_add]
        kk = pl.program_id(2)

        @pl.when(kk == 0)
        def _():
            acc_ref[...] = jnp.zeros_like(acc_ref)
        acc_ref[...] += part

        @pl.when(kk == nk - 1)
        def _():
            r = acc_ref[...]
            if has_add:
                r = r + add_ref[...]
            o_ref[...] = r.astype(o_ref.dtype)

    ins = [a, b] + ([add] if has_add else [])
    specs = [a_spec, b_spec] + ([o_spec] if has_add else [])
    return pl.pallas_call(
        body, name=name, grid=(m // tm, n // tn, nk), in_specs=specs, out_specs=o_spec,
        out_shape=jax.ShapeDtypeStruct((4, m, ns) if out_shards else (m, n), out_dtype),
        scratch_shapes=[pltpu.VMEM((tm, tn), F32)] if nk > 1 else [],
        compiler_params=_params(3))(*ins)


def _rows(t):
    return _tile(t, (ROW_TILE, 128, 64, 32, 16, 8))


def _rms_fwd(x, g, name):
    t, c = x.shape
    tr = _rows(t)

    def body(x_ref, g_ref, h_ref, r_ref):
        xv = x_ref[...]
        r = lax.rsqrt(jnp.mean(xv * xv, axis=-1, keepdims=True) + RMS_EPS)
        h_ref[...] = (xv * r * g_ref[...]).astype(h_ref.dtype)
        r_ref[...] = r

    return pl.pallas_call(
        body, name=name, grid=(t // tr,),
        in_specs=[pl.BlockSpec((tr, c), lambda i: (i, 0)), pl.BlockSpec((1, c), lambda i: (0, 0))],
        out_specs=[pl.BlockSpec((tr, c), lambda i: (i, 0)), pl.BlockSpec((tr, 1), lambda i: (i, 0))],
        out_shape=[jax.ShapeDtypeStruct((t, c), BF16), jax.ShapeDtypeStruct((t, 1), F32)],
        compiler_params=_params(1))(x, g)


def _rms_bwd(x, g, r, dh, name, dres=None):
    t, c = x.shape
    tr = _rows(t)
    has_res = dres is not None

    def body(*refs):
        x_ref, g_ref, r_ref, dh_ref = refs[:4]
        res_ref = refs[4] if has_res else None
        dx_ref, dxb_ref, dg_ref = refs[4 + has_res:]
        rv = r_ref[...]
        xh = x_ref[...] * rv
        dhv = dh_ref[...]
        dxh = dhv * g_ref[...]
        cm = jnp.mean(dxh * xh, axis=-1, keepdims=True)
        dx = (dxh - xh * cm) * rv
        if has_res:
            dx = dx + res_ref[...]
        dx_ref[...] = dx
        dxb_ref[...] = dx.astype(BF16)
        _acc_rows(dg_ref, pl.program_id(0) == 0, jnp.sum(dhv * xh, axis=0, keepdims=True))

    row = pl.BlockSpec((tr, c), lambda i: (i, 0))
    ins = [x, g, r, dh] + ([dres] if has_res else [])
    specs = [row, pl.BlockSpec((1, c), lambda i: (0, 0)), pl.BlockSpec((tr, 1), lambda i: (i, 0)), row] + ([row] if has_res else [])
    dx, dxb, dg = pl.pallas_call(
        body, name=name, grid=(t // tr,), in_specs=specs,
        out_specs=[row, row, pl.BlockSpec((8, c), lambda i: (0, 0))],
        out_shape=[jax.ShapeDtypeStruct((t, c), F32), jax.ShapeDtypeStruct((t, c), BF16), jax.ShapeDtypeStruct((8, c), F32)],
        compiler_params=_params(1))(*ins)
    return dx, dxb, dg[0]


def _gated_fwd(y, z, g, name):
    t, c = y.shape
    tr = _rows(t)

    def body(y_ref, z_ref, g_ref, o_ref, r_ref):
        zv = z_ref[...]
        v = y_ref[...] * zv * _sigmoid(zv)
        r = lax.rsqrt(jnp.mean(v * v, axis=-1, keepdims=True) + RMS_EPS)
        o_ref[...] = (v * r * g_ref[...]).astype(o_ref.dtype)
        r_ref[...] = r

    row = pl.BlockSpec((tr, c), lambda i: (i, 0))
    return pl.pallas_call(
        body, name=name, grid=(t // tr,), in_specs=[row, row, pl.BlockSpec((1, c), lambda i: (0, 0))],
        out_specs=[row, pl.BlockSpec((tr, 1), lambda i: (i, 0))],
        out_shape=[jax.ShapeDtypeStruct((t, c), BF16), jax.ShapeDtypeStruct((t, 1), F32)],
        compiler_params=_params(1))(y, z, g)


def _gated_bwd(y, z, g, r, dout, name):
    t, c = y.shape
    tr = _rows(t)

    def body(y_ref, z_ref, g_ref, r_ref, do_ref, dy_ref, dz_ref, dg_ref):
        yv, zv, rv, dov = y_ref[...], z_ref[...], r_ref[...], do_ref[...]
        s = _sigmoid(zv)
        sz = zv * s
        xh = yv * sz * rv
        dxh = dov * g_ref[...]
        cm = jnp.mean(dxh * xh, axis=-1, keepdims=True)
        dv = (dxh - xh * cm) * rv
        dy_ref[...] = dv * sz
        dz_ref[...] = dv * yv * s * (1.0 + zv * (1.0 - s))
        _acc_rows(dg_ref, pl.program_id(0) == 0, jnp.sum(dov * xh, axis=0, keepdims=True))

    row = pl.BlockSpec((tr, c), lambda i: (i, 0))
    dy, dz, dg = pl.pallas_call(
        body, name=name, grid=(t // tr,),
        in_specs=[row, row, pl.BlockSpec((1, c), lambda i: (0, 0)), pl.BlockSpec((tr, 1), lambda i: (i, 0)), row],
        out_specs=[row, row, pl.BlockSpec((8, c), lambda i: (0, 0))],
        out_shape=[jax.ShapeDtypeStruct((t, c), F32), jax.ShapeDtypeStruct((t, c), F32), jax.ShapeDtypeStruct((8, c), F32)],
        compiler_params=_params(1))(y, z, g, r, dout)
    return dy, dz, dg[0]


def _swiglu_fwd(gate, up, name):
    t, c = gate.shape
    tr, tc = _rows(t), _tile(c, (2816, 1408, 1024, 512, 256, 128))

    def body(g_ref, u_ref, o_ref):
        gv = g_ref[...].astype(F32)
        o_ref[...] = (gv * _sigmoid(gv) * u_ref[...].astype(F32)).astype(o_ref.dtype)

    blk = pl.BlockSpec((tr, tc), lambda i, j: (i, j))
    return pl.pallas_call(body, name=name, grid=(t // tr, c // tc), in_specs=[blk, blk], out_specs=blk,
                          out_shape=jax.ShapeDtypeStruct((t, c), BF16), compiler_params=_params(2))(gate, up)


def _swiglu_bwd(gate, up, dact, name):
    t, c = gate.shape
    tr, tc = _rows(t), _tile(c, (2816, 1408, 1024, 512, 256, 128))

    def body(g_ref, u_ref, d_ref, dg_ref, du_ref):
        gv, dv = g_ref[...].astype(F32), d_ref[...].astype(F32)
        s = _sigmoid(gv)
        dg_ref[...] = (dv * u_ref[...].astype(F32) * s * (1.0 + gv * (1.0 - s))).astype(dg_ref.dtype)
        du_ref[...] = (dv * gv * s).astype(du_ref.dtype)

    blk = pl.BlockSpec((tr, tc), lambda i, j: (i, j))
    return pl.pallas_call(body, name=name, grid=(t // tr, c // tc), in_specs=[blk, blk, blk], out_specs=[blk, blk],
                          out_shape=[jax.ShapeDtypeStruct((t, c), BF16)] * 2, compiler_params=_params(2))(gate, up, dact)


def _loss_fwd_bwd(x, g, tgt, name):
    t, c = x.shape
    tr = _rows(t)

    def body(x_ref, g_ref, t_ref, l_ref, dx_ref, dxb_ref, dg_ref):
        xv, gv = x_ref[...], g_ref[...]
        r = lax.rsqrt(jnp.mean(xv * xv, axis=-1, keepdims=True) + RMS_EPS)
        xh = xv * r
        err = xh * gv - t_ref[...]
        per_row = jnp.mean(err * err, axis=-1, keepdims=True)
        dy = err * (1.0 / c)
        dxh = dy * gv
        cm = jnp.mean(dxh * xh, axis=-1, keepdims=True)
        dx = (dxh - xh * cm) * r
        dx_ref[...] = dx
        dxb_ref[...] = dx.astype(BF16)
        first = pl.program_id(0) == 0
        _acc_rows(dg_ref, first, jnp.sum(dy * xh, axis=0, keepdims=True))
        _acc_rows(l_ref, first, jnp.broadcast_to(0.5 * jnp.sum(per_row, axis=0, keepdims=True), (1, LANES)))

    row = pl.BlockSpec((tr, c), lambda i: (i, 0))
    lo, dx, dxb, dg = pl.pallas_call(
        body, name=name, grid=(t // tr,), in_specs=[row, pl.BlockSpec((1, c), lambda i: (0, 0)), row],
        out_specs=[pl.BlockSpec((8, LANES), lambda i: (0, 0)), row, row, pl.BlockSpec((8, c), lambda i: (0, 0))],
        out_shape=[jax.ShapeDtypeStruct((8, LANES), F32), jax.ShapeDtypeStruct((t, c), F32), jax.ShapeDtypeStruct((t, c), BF16),
                   jax.ShapeDtypeStruct((8, c), F32)],
        compiler_params=_params(1))(x, g, tgt)
    return lo[0, 0], dx, dxb, dg[0]


def _conv_specs(t, c):
    tr = _rows(t)
    tc = _tile(c, (1024, 768, 512, 256, 128))
    h8 = tr // 8
    tile = pl.BlockSpec((tr, tc), lambda j, i: (i, j))
    prev = pl.BlockSpec((8, tc), lambda j, i: (jnp.maximum(i * h8 - 1, 0), j))
    nxt = pl.BlockSpec((8, tc), lambda j, i: (jnp.minimum((i + 1) * h8, t // 8 - 1), j))
    return tr, tc, tile, prev, nxt


def _conv_fwd(xbc, w, b, name):
    t, c = xbc.shape
    tr, tc, tile, prev, _ = _conv_specs(t, c)

    def body(x_ref, p_ref, w_ref, b_ref, o_ref, buf):
        i = pl.program_id(1)
        buf[0:8, :] = jnp.where(i > 0, p_ref[...], 0.0)
        buf[8:, :] = x_ref[...]
        pre = b_ref[...]
        for k in range(SSD_CONV):
            pre = pre + w_ref[k:k + 1, :] * buf[pl.ds(8 - (SSD_CONV - 1) + k, tr), :]
        o_ref[...] = pre * _sigmoid(pre)

    return pl.pallas_call(
        body, name=name, grid=(c // tc, t // tr),
        in_specs=[tile, prev, pl.BlockSpec((8, tc), lambda j, i: (0, j)), pl.BlockSpec((1, tc), lambda j, i: (0, j))],
        out_specs=tile, out_shape=jax.ShapeDtypeStruct((t, c), F32),
        scratch_shapes=[pltpu.VMEM((tr + 8, tc), F32)], compiler_params=_params(2))(xbc, xbc, w, b)


def _conv_bwd(xbc, w, b, dout, name):
    t, c = xbc.shape
    tr, tc, tile, prev, nxt = _conv_specs(t, c)
    nt = t // tr
    kc = SSD_CONV

    def body(x_ref, p_ref, n_ref, w_ref, b_ref, d_ref, dn_ref, dx_ref, dw_ref, db_ref, buf, dbuf):
        i = pl.program_id(1)
        last = i == nt - 1
        buf[0:8, :] = jnp.where(i > 0, p_ref[...], 0.0)
        buf[8:8 + tr, :] = x_ref[...]
        buf[8 + tr:, :] = jnp.where(last, 0.0, n_ref[...])
        pre = b_ref[...]
        for k in range(kc):
            pre = pre + w_ref[k:k + 1, :] * buf[pl.ds(8 - (kc - 1) + k, tr + 8), :]
        s = _sigmoid(pre)
        dsilu = s * (1.0 + pre * (1.0 - s))
        dbuf[0:tr, :] = d_ref[...] * dsilu[0:tr, :]
        dbuf[tr:, :] = jnp.where(last, 0.0, dn_ref[...]) * dsilu[tr:, :]
        dpre = dbuf[0:tr, :]
        dx = jnp.zeros((tr, tc), F32)
        first = i == 0
        for k in range(kc):
            dx = dx + w_ref[k:k + 1, :] * dbuf[pl.ds(kc - 1 - k, tr), :]
        dx_ref[...] = dx

        @pl.when(first)
        def _():
            dw_ref[...] = jnp.zeros_like(dw_ref)
        for k in range(kc):
            dw_ref[k:k + 1, :] += jnp.sum(dpre * buf[pl.ds(8 - (kc - 1) + k, tr), :], axis=0, keepdims=True)
        _acc_rows(db_ref, first, jnp.sum(dpre, axis=0, keepdims=True))

    par = pl.BlockSpec((8, tc), lambda j, i: (0, j))
    dx, dw, db = pl.pallas_call(
        body, name=name, grid=(c // tc, nt),
        in_specs=[tile, prev, nxt, par, pl.BlockSpec((1, tc), lambda j, i: (0, j)), tile, nxt],
        out_specs=[tile, par, par],
        out_shape=[jax.ShapeDtypeStruct((t, c), F32), jax.ShapeDtypeStruct((8, c), F32), jax.ShapeDtypeStruct((8, c), F32)],
        scratch_shapes=[pltpu.VMEM((tr + 16, tc), F32), pltpu.VMEM((tr + 8, tc), F32)],
        compiler_params=_params(2))(xbc, xbc, xbc, w, b, dout, dout)
    return dx, dw[:kc], db[0]


def _ssd_consts():
    h, p, ln = SSD_HEADS, SSD_HEAD_DIM, SSD_CHUNK
    w = h * p
    hrow, jcol = _iota((LANES, w), 0), _iota((LANES, w), 1)
    e = ((jcol >= hrow * p) & (jcol < (hrow + 1) * p)).astype(BF16)
    jrow, hcol = _iota((w, LANES), 0), _iota((w, LANES), 1)
    et = ((jrow >= hcol * p) & (jrow < (hcol + 1) * p)).astype(BF16)
    row, col = _iota((ln, ln), 0), _iota((ln, ln), 1)
    return e, et, row, col


def _ssd_common(dtraw_ref, dtb_ref, alog_ref, e):
    ln = SSD_CHUNK
    raw = dtraw_ref[...] + dtb_ref[...]
    dt = _softplus(raw)
    a = -jnp.exp(alog_ref[...])
    adt = dt * a
    row, col = _iota((ln, ln), 0), _iota((ln, ln), 1)
    cs = _xdot_l((col <= row).astype(BF16), adt)
    cl = jnp.sum(adt, axis=0, keepdims=True)
    ex = _xdot(jnp.concatenate([dt, cs, jnp.broadcast_to(cl, (ln, LANES))], axis=0), e)
    return raw, dt, a, cs, ex[:ln], ex[ln:2 * ln], ex[2 * ln:]


def _head_decay(cs, h, causal):
    ln = SSD_CHUNK
    lane = _iota((1, LANES), 1)
    colv = jnp.sum(jnp.where(lane == h, cs, 0.0), axis=1, keepdims=True)
    cb = jnp.broadcast_to(colv, (ln, ln))
    return jnp.exp(jnp.where(causal, cb - cb.T, NEG))


def _ssd_fwd(dtraw, xc, dtb, alog, dskip_x, name):
    t = dtraw.shape[0]
    h, p, g, n, ln = SSD_HEADS, SSD_HEAD_DIM, SSD_GROUPS, SSD_STATE, SSD_CHUNK
    w, gn, gw, hpg = h * p, g * n, (h // g) * p, h // g
    assert n == ln and gw % LANES == 0 and w % gn == 0
    nc = t // ln

    def body(dtraw_ref, x_ref, b_ref, c_ref, dtb_ref, alog_ref, dsk_ref, y_ref, sp_ref, s_ref):
        @pl.when(pl.program_id(0) == 0)
        def _():
            s_ref[...] = jnp.zeros_like(s_ref)
        e, _, row, col = _ssd_consts()
        causal = col <= row
        _, _, _, cs, dt_x, cs_x, cl_x = _ssd_common(dtraw_ref, dtb_ref, alog_ref, e)
        xv = x_ref[...]
        xd = xv * dt_x
        xdb = xd.astype(BF16)
        sv = s_ref[...]
        sp_ref[0] = sv
        el_x = jnp.exp(cs_x)
        zb = (xd * jnp.exp(cl_x - cs_x)).astype(BF16)
        cd_x = jnp.exp(cl_x)
        dsk = dsk_ref[...]
        half = _iota((1, LANES), 1) >= p
        for gi in range(g):
            gs = slice(gi * gw, (gi + 1) * gw)
            bg = b_ref[:, gi * n:(gi + 1) * n].astype(BF16)
            cg = c_ref[:, gi * n:(gi + 1) * n].astype(BF16)
            gm = _dot(cg, bg, "nt")
            sg = sv[:, gs]
            yoff = _dot(cg, sg.astype(BF16)) * el_x[:, gs]
            s_ref[:, gs] = sg * cd_x[:, gs] + _dot(bg, zb[:, gs], "tn")
            for pp in range(gw // LANES):
                ls = slice(gi * gw + pp * LANES, gi * gw + (pp + 1) * LANES)
                xp = xdb[:, ls]
                yp = yoff[:, pp * LANES:(pp + 1) * LANES] + dsk[:, ls] * xv[:, ls]
                for hh in range(LANES // p):
                    hd = gi * hpg + pp * (LANES // p) + hh
                    wm = (gm * _head_decay(cs, hd, causal)).astype(BF16)
                    yp = yp + _dot(wm, jnp.where(half == (hh == 1), xp, jnp.zeros_like(xp)))
                y_ref[:, ls] = yp

    nar = pl.BlockSpec((ln, LANES), lambda c: (c, 0))
    one = pl.BlockSpec((1, LANES), lambda c: (0, 0))
    return pl.pallas_call(
        body, name=name, grid=(nc,),
        in_specs=[nar, pl.BlockSpec((ln, w), lambda c: (c, 0)), pl.BlockSpec((ln, gn), lambda c: (c, w // gn)),
                  pl.BlockSpec((ln, gn), lambda c: (c, w // gn + 1)), one, one, pl.BlockSpec((1, w), lambda c: (0, 0))],
        out_specs=[pl.BlockSpec((ln, w), lambda c: (c, 0)), pl.BlockSpec((1, n, w), lambda c: (c, 0, 0))],
        out_shape=[jax.ShapeDtypeStruct((t, w), F32), jax.ShapeDtypeStruct((nc, n, w), F32)],
        scratch_shapes=[pltpu.VMEM((n, w), F32)], compiler_params=_params(1))(dtraw, xc, xc, xc, dtb, alog, dskip_x)


def _ssd_bwd(dtraw, xc, dtb, alog, dskip_x, sprev, dy, name):
    t = dtraw.shape[0]
    h, p, g, n, ln = SSD_HEADS, SSD_HEAD_DIM, SSD_GROUPS, SSD_STATE, SSD_CHUNK
    w, gn, gw, hpg = h * p, g * n, (h // g) * p, h // g
    nc = t // ln

    def body(dtraw_ref, x_ref, b_ref, c_ref, dtb_ref, alog_ref, dsk_ref, sp_ref, dy_ref,
             dx_ref, db_ref, dc_ref, ddt_ref, dbias_ref, dalog_ref, ddsk_ref, ds_ref, dxd_ref, qcs_ref):
        first = pl.program_id(0) == 0

        @pl.when(first)
        def _():
            ds_ref[...] = jnp.zeros_like(ds_ref)
        e, et, row, col = _ssd_consts()
        causal = col <= row
        raw, dt, a, cs, dt_x, cs_x, cl_x = _ssd_common(dtraw_ref, dtb_ref, alog_ref, e)
        xv = x_ref[...]
        xd = xv * dt_x
        xdb = xd.astype(BF16)
        sv = sp_ref[0]
        dyv = dy_ref[...]
        dyb = dyv.astype(BF16)
        dsn = ds_ref[...]
        el_x = jnp.exp(cs_x)
        dte_x = jnp.exp(cl_x - cs_x)
        cd_x = jnp.exp(cl_x)
        zf = xd * dte_x
        lane = _iota((1, LANES), 1)
        half = lane >= p
        lastrow = _iota((ln, 1), 0) == ln - 1
        dcs = jnp.zeros((ln, LANES), F32)
        for gi in range(g):
            gs = slice(gi * gw, (gi + 1) * gw)
            ns = slice(gi * n, (gi + 1) * n)
            bg = b_ref[:, ns].astype(BF16)
            cg = c_ref[:, ns].astype(BF16)
            gm = _dot(cg, bg, "nt")
            sgb = sv[:, gs].astype(BF16)
            dsg = dsn[:, gs]
            dsgb = dsg.astype(BF16)
            yoff = _dot(cg, sgb) * el_x[:, gs]
            drb = (el_x[:, gs] * dyv[:, gs]).astype(BF16)
            dcg = _dot(drb, sgb, "nt")
            ds_ref[:, gs] = cd_x[:, gs] * dsg + _dot(cg, drb, "tn")
            dz = _dot(bg, dsgb)
            zg = zf[:, gs]
            dbg = _dot(zg.astype(BF16), dsgb, "nt")
            dzz = dz * zg
            qcl = jnp.sum(dzz + cd_x[:, gs] * dsg * sv[:, gs], axis=0, keepdims=True)
            qcs_ref[:, gs] = dyv[:, gs] * yoff - dzz + jnp.where(lastrow, jnp.broadcast_to(qcl, (ln, gw)), 0.0)
            dgm = jnp.zeros((ln, ln), F32)
            for pp in range(gw // LANES):
                ls = slice(gi * gw + pp * LANES, gi * gw + (pp + 1) * LANES)
                xp = xdb[:, ls]
                dxp = dz[:, pp * LANES:(pp + 1) * LANES] * dte_x[:, ls]
                for hh in range(LANES // p):
                    hd = gi * hpg + pp * (LANES // p) + hh
                    dm = _head_decay(cs, hd, causal)
                    wf = gm * dm
                    dym = jnp.where(half == (hh == 1), dyb[:, ls], jnp.zeros_like(xp))
                    dw = _dot(dym, xp, "nt")
                    dxp = dxp + _dot(wf.astype(BF16), dym, "tn")
                    dgm = dgm + dw * dm
                    mm = dw * wf
                    rc = jnp.sum(mm, axis=1, keepdims=True) - jnp.sum(mm.T, axis=1, keepdims=True)
                    dcs = dcs + rc * (lane == hd).astype(F32)
                dxd_ref[:, ls] = dxp
            dgb = dgm.astype(BF16)
            dc_ref[:, ns] = dcg + _dot(dgb, bg)
            db_ref[:, ns] = dbg + _dot(dgb, cg, "tn")
        dxd = dxd_ref[...]
        dx_ref[...] = dxd * dt_x + dsk_ref[...] * dyv
        red = _xdot(jnp.concatenate([qcs_ref[...], dxd * xv, dyv * xv], axis=0), et)
        dcs = dcs + red[:ln]
        dadt = _xdot_l((row <= col).astype(BF16), dcs)
        ddt = red[ln:2 * ln] + dadt * a
        draw = ddt * _sigmoid(raw)
        ddt_ref[...] = draw
        _acc_rows(dbias_ref, first, jnp.sum(draw, axis=0, keepdims=True))
        _acc_rows(dalog_ref, first, jnp.sum(dadt * dt, axis=0, keepdims=True) * a)
        _acc_rows(ddsk_ref, first, jnp.sum(red[2 * ln:], axis=0, keepdims=True))

    rev = lambda c: nc - 1 - c
    nar = pl.BlockSpec((ln, LANES), lambda c: (rev(c), 0))
    one = pl.BlockSpec((1, LANES), lambda c: (0, 0))
    wide = pl.BlockSpec((ln, w), lambda c: (rev(c), 0))
    bcs = pl.BlockSpec((ln, gn), lambda c: (rev(c), 0))
    acc = pl.BlockSpec((8, LANES), lambda c: (0, 0))
    outs = pl.pallas_call(
        body, name=name, grid=(nc,),
        in_specs=[nar, wide, pl.BlockSpec((ln, gn), lambda c: (rev(c), w // gn)), pl.BlockSpec((ln, gn), lambda c: (rev(c), w // gn + 1)),
                  one, one, pl.BlockSpec((1, w), lambda c: (0, 0)), pl.BlockSpec((1, n, w), lambda c: (rev(c), 0, 0)), wide],
        out_specs=[wide, bcs, bcs, nar, acc, acc, acc],
        out_shape=[jax.ShapeDtypeStruct((t, w), F32), jax.ShapeDtypeStruct((t, gn), F32), jax.ShapeDtypeStruct((t, gn), F32),
                   jax.ShapeDtypeStruct((t, LANES), F32)] + [jax.ShapeDtypeStruct((8, LANES), F32)] * 3,
        scratch_shapes=[pltpu.VMEM((n, w), F32), pltpu.VMEM((ln, w), F32), pltpu.VMEM((ln, w), F32)],
        compiler_params=_params(1))(dtraw, xc, xc, xc, dtb, alog, dskip_x, sprev, dy)
    dx, db, dc, ddt, dbias, dalog, ddsk = outs
    return dx, db, dc, ddt, dbias[0], dalog[0], ddsk[0]


def _rope_tables(t):
    half = MLA_ROPE // 2
    inv_freq = ROPE_THETA ** (-jnp.arange(half, dtype=F32) / half)
    ang = jnp.arange(t, dtype=F32)[:, None] * inv_freq[None, :]
    cos, sin = jnp.cos(ang), jnp.sin(ang)
    pad = LANES - MLA_ROPE
    cos_r = jnp.concatenate([cos, cos, jnp.ones((t, pad), F32)], axis=1)
    sin_r = jnp.concatenate([sin, sin, jnp.zeros((t, pad), F32)], axis=1)
    return cos_r, sin_r


def _rot_matrix():
    half = MLA_ROPE // 2
    i, j = _iota((LANES, LANES), 0), _iota((LANES, LANES), 1)
    neg = (j < half) & (i == j + half)
    pos = (j >= half) & (j < 2 * half) & (i == j - half)
    return (pos.astype(F32) - neg.astype(F32)).astype(BF16)


def _rope(x, cos, sin, every, transpose, out_dtype, name):
    t, w = x.shape
    tr = _rows(t)

    def body(x_ref, c_ref, s_ref, o_ref):
        rot = _rot_matrix()
        cv, sv = c_ref[...], s_ref[...]
        for j in range(w // LANES):
            ls = slice(j * LANES, (j + 1) * LANES)
            xv = x_ref[:, ls]
            if j % every != every - 1:
                o_ref[:, ls] = xv.astype(o_ref.dtype)
            elif transpose:
                o_ref[:, ls] = (xv * cv - _xdot(xv * sv, rot, 2)).astype(o_ref.dtype)
            else:
                o_ref[:, ls] = (xv * cv + _xdot(xv, rot, 2) * sv).astype(o_ref.dtype)

    wide = pl.BlockSpec((tr, w), lambda i: (i, 0))
    tab = pl.BlockSpec((tr, LANES), lambda i: (i, 0))
    return pl.pallas_call(
        body, name=name, grid=(t // tr,), in_specs=[wide, tab, tab], out_specs=wide,
        out_shape=jax.ShapeDtypeStruct((t, w), out_dtype), compiler_params=_params(1))(x, cos, sin)


def _att_masks(blk):
    return _iota((blk, blk), 0), _iota((blk, blk), 1)


def _lanes(j):
    return slice(j * LANES, (j + 1) * LANES)


def _mla_fwd(q, kr, kv, name):
    t = q.shape[0]
    nh, blk = MLA_HEADS, min(ATT_BLK, t)
    scale = (MLA_NOPE + MLA_ROPE) ** -0.5
    hps = math.gcd(nh, MLA_FWD_HEADS)

    def body(q_ref, kv_ref, kr_ref, o_ref, lse_ref):
        i = pl.program_id(1)
        row, col = _att_masks(blk)
        qs = [q_ref[:, 2 * hh * LANES:(2 * hh + 2) * LANES] for hh in range(hps)]

        def scores(kb, hh):
            ks = pl.ds(pl.multiple_of(kb * blk, blk), blk)
            kfull = jnp.concatenate([kv_ref[ks, _lanes(2 * hh)], kr_ref[ks, :]], axis=1)
            return _dot(qs[hh], kfull, "nt") * scale, kv_ref[ks, _lanes(2 * hh + 1)]
        init = []
        for hh in range(hps):
            s, v = scores(i, hh)
            s = jnp.where(col <= row, s, NEG)
            m = jnp.max(s, axis=1, keepdims=True)
            pr = jnp.exp(s - m)
            init += [m, jnp.sum(pr, axis=1, keepdims=True), _dot(pr.astype(BF16), v)]

        def step(kb, carry):
            sv = [scores(kb, hh) for hh in range(hps)]
            out, prs = [], []
            for hh in range(hps):
                m, l, acc = carry[3 * hh:3 * hh + 3]
                s = sv[hh][0]
                m2 = jnp.maximum(m, jnp.max(s, axis=1, keepdims=True))
                al = jnp.exp(m - m2)
                pr = jnp.exp(s - m2)
                prs.append(pr.astype(BF16))
                out += [m2, al * l + jnp.sum(pr, axis=1, keepdims=True), al * acc]
            for hh in range(hps):
                out[3 * hh + 2] = out[3 * hh + 2] + _dot(prs[hh], sv[hh][1])
            return tuple(out)
        res = lax.fori_loop(0, i, step, tuple(init))
        for hh in range(hps):
            m, l, acc = res[3 * hh:3 * hh + 3]
            o_ref[:, _lanes(hh)] = acc / l
            lse_ref[hh] = m + jnp.log(l)

    return pl.pallas_call(
        body, name=name, grid=(nh // hps, t // blk),
        in_specs=[pl.BlockSpec((blk, 2 * hps * LANES), lambda h, i: (i, h)),
                  pl.BlockSpec((t, 2 * hps * LANES), lambda h, i: (0, h), pipeline_mode=pl.Buffered(1)),
                  pl.BlockSpec((t, LANES), lambda h, i: (0, 0), pipeline_mode=pl.Buffered(1))],
        out_specs=[pl.BlockSpec((blk, hps * LANES), lambda h, i: (i, h)), pl.BlockSpec((hps, blk, 1), lambda h, i: (h, i, 0))],
        out_shape=[jax.ShapeDtypeStruct((t, nh * LANES), F32), jax.ShapeDtypeStruct((nh, t, 1), F32)],
        compiler_params=_params(2))(q, kv, kr)


def _mla_bwd(q, kr, kv, o, do, lse, name):
    t = q.shape[0]
    nh, blk = MLA_HEADS, min(ATT_BLK, t)
    hps = math.gcd(nh, MLA_BWD_HEADS)
    scale = (MLA_NOPE + MLA_ROPE) ** -0.5

    def body(q_ref, kv_ref, kr_ref, o_ref, do_ref, lse_ref, dq_ref, dkv_ref, dkr_ref):
        hp, i = pl.program_id(0), pl.program_id(1)

        @pl.when(i == 0)
        def _():
            dkv_ref[...] = jnp.zeros_like(dkv_ref)

        @pl.when((i == 0) & (hp == 0))
        def _():
            dkr_ref[...] = jnp.zeros_like(dkr_ref)
        row, col = _att_masks(blk)
        qs = [q_ref[:, 2 * hh * LANES:(2 * hh + 2) * LANES] for hh in range(hps)]
        dobs = [do_ref[:, _lanes(hh)].astype(BF16) for hh in range(hps)]
        deltas = [jnp.sum(do_ref[:, _lanes(hh)] * o_ref[:, _lanes(hh)], axis=1, keepdims=True) for hh in range(hps)]
        lses = [lse_ref[hh] for hh in range(hps)]

        def tile(kb, carry, masked):
            ks = pl.ds(pl.multiple_of(kb * blk, blk), blk)
            krv = kr_ref[ks, :]
            hs = range(hps)
            kfull = [jnp.concatenate([kv_ref[ks, _lanes(2 * hh)], krv], axis=1) for hh in hs]
            ss = [_dot(qs[hh], kfull[hh], "nt") for hh in hs]
            dps = [_dot(dobs[hh], kv_ref[ks, _lanes(2 * hh + 1)], "nt") for hh in hs]
            prb, dsb = [], []
            for hh in hs:
                pr = jnp.exp(ss[hh] * scale - lses[hh])
                if masked:
                    pr = jnp.where(col <= row, pr, 0.0)
                prb.append(pr.astype(BF16))
                dsb.append((pr * (dps[hh] - deltas[hh]) * scale).astype(BF16))
            out, dkr = [], None
            for hh in hs:
                dkv_ref[ks, _lanes(2 * hh + 1)] += _dot(prb[hh], dobs[hh], "tn")
                dk = _dot(dsb[hh], qs[hh], "tn")
                dkv_ref[ks, _lanes(2 * hh)] += dk[:, :LANES]
                dkr = dk[:, LANES:] if dkr is None else dkr + dk[:, LANES:]
                out.append(carry[hh] + _dot(dsb[hh], kfull[hh]))
            dkr_ref[ks, :] += dkr
            return tuple(out)
        zero = jnp.zeros((blk, 2 * LANES), F32)
        carry = lax.fori_loop(0, i, lambda kb, c: tile(kb, c, False), (zero,) * hps)
        res = tile(i, carry, True)
        for hh in range(hps):
            dq_ref[:, 2 * hh * LANES:(2 * hh + 2) * LANES] = res[hh]

    qb = lambda w: pl.BlockSpec((blk, w * LANES), lambda h, i: (i, h))
    seq = lambda w, f: pl.BlockSpec((t, w * LANES), f, pipeline_mode=pl.Buffered(1))
    wide = jax.ShapeDtypeStruct((t, nh * 2 * LANES), F32)
    return pl.pallas_call(
        body, name=name, grid=(nh // hps, t // blk),
        in_specs=[qb(2 * hps), seq(2 * hps, lambda h, i: (0, h)), seq(1, lambda h, i: (0, 0)),
                  qb(hps), qb(hps), pl.BlockSpec((hps, blk, 1), lambda h, i: (h, i, 0))],
        out_specs=[qb(2 * hps), seq(2 * hps, lambda h, i: (0, h)), seq(1, lambda h, i: (0, 0))],
        out_shape=[wide, wide, jax.ShapeDtypeStruct((t, LANES), F32)],
        compiler_params=_params(2))(q, kv, kr, o, do, lse)


def _sb_fwd(qkv, name):
    t = qkv.shape[0]
    nh, blk = SB_HEADS, min(ATT_BLK, t)
    hps = math.gcd(nh, SB_FWD_HEADS)
    nq = t // blk
    scale = SB_HEAD_DIM ** -0.5

    def body(q_ref, k_ref, v_ref, o_ref, lt_ref):
        i = pl.program_id(1)
        row, col = _att_masks(blk)
        usuf = (row > col).astype(BF16)
        qs = [q_ref[:, _lanes(hh)] for hh in range(hps)]

        def tile(kb, carry, masked):
            ks = pl.ds(pl.multiple_of(kb * blk, blk), blk)
            hs = range(hps)
            zs = [_dot(qs[hh], k_ref[ks, _lanes(hh)], "nt") for hh in hs]
            lks, lss = [], []
            for hh in hs:
                z = zs[hh] * scale
                lk = -_softplus(z)
                lss.append(lk + z)
                lks.append(jnp.where(col < row, lk, 0.0) if masked else lk)
            pieces = [_split(lks[hh], 2) for hh in hs]
            later = [_dot(pieces[hh][0], usuf) + _dot(pieces[hh][1], usuf) for hh in hs]
            out = []
            for hh in hs:
                wt = jnp.exp(lss[hh] + later[hh] + carry[2 * hh + 1])
                if masked:
                    wt = jnp.where(col < row, wt, 0.0)
                out += [wt.astype(BF16), carry[2 * hh + 1] + jnp.sum(lks[hh], axis=1, keepdims=True)]
            for hh in hs:
                out[2 * hh] = carry[2 * hh] + _dot(out[2 * hh], v_ref[ks, _lanes(hh)])
            return tuple(out)
        za, zr = jnp.zeros((blk, LANES), F32), jnp.zeros((blk, 1), F32)
        carry = tile(i, (za, zr) * hps, True)
        carry = lax.fori_loop(0, i, lambda j, c: tile(i - 1 - j, c, False), carry)
        for hh in range(hps):
            o_ref[:, _lanes(hh)] = carry[2 * hh]
            lt_ref[hh] = carry[2 * hh + 1]

    ng = nh // hps
    full = lambda f: pl.BlockSpec((t, hps * LANES), f)
    return pl.pallas_call(
        body, name=name, grid=(ng, nq),
        in_specs=[pl.BlockSpec((blk, hps * LANES), lambda h, i: (i, h)), full(lambda h, i: (0, ng + h)), full(lambda h, i: (0, 2 * ng + h))],
        out_specs=[pl.BlockSpec((blk, hps * LANES), lambda h, i: (i, h)), pl.BlockSpec((hps, blk, 1), lambda h, i: (h, i, 0))],
        out_shape=[jax.ShapeDtypeStruct((t, nh * LANES), F32), jax.ShapeDtypeStruct((nh, t, 1), F32)],
        compiler_params=_params(2))(qkv, qkv, qkv)


def _sb_bwd(qkv, do, ltot, name):
    t = qkv.shape[0]
    nh, blk = SB_HEADS, min(ATT_BLK, t)
    hps = math.gcd(nh, SB_BWD_HEADS)
    nq = t // blk
    scale = SB_HEAD_DIM ** -0.5

    def body(q_ref, k_ref, v_ref, do_ref, lt_ref, dq_ref, dk_ref, dv_ref):
        i = pl.program_id(1)

        @pl.when(i == 0)
        def _():
            dk_ref[...] = jnp.zeros_like(dk_ref)
            dv_ref[...] = jnp.zeros_like(dv_ref)
        row, col = _att_masks(blk)
        uinc = (row <= col).astype(BF16)
        uexc = (row < col).astype(BF16)
        qs = [q_ref[:, _lanes(hh)] for hh in range(hps)]
        dobs = [do_ref[:, _lanes(hh)].astype(BF16) for hh in range(hps)]
        lts = [lt_ref[hh] for hh in range(hps)]

        def tile(kb, carry, masked):
            ks = pl.ds(pl.multiple_of(kb * blk, blk), blk)
            hs = range(hps)
            kvs = [k_ref[ks, _lanes(hh)] for hh in hs]
            zs = [_dot(qs[hh], kvs[hh], "nt") for hh in hs]
            dws = [_dot(dobs[hh], v_ref[ks, _lanes(hh)], "nt") for hh in hs]
            lks, lss = [], []
            for hh in hs:
                z = zs[hh] * scale
                lk = -_softplus(z)
                lss.append(lk + z)
                lks.append(jnp.where(col < row, lk, 0.0) if masked else lk)
            pieces = [_split(lks[hh], 2) for hh in hs]
            css = [_dot(pieces[hh][0], uinc) + _dot(pieces[hh][1], uinc) for hh in hs]
            wts, evs = [], []
            for hh in hs:
                wt = jnp.exp(lss[hh] + (lts[hh] - (carry[3 * hh + 1] + css[hh])))
                if masked:
                    wt = jnp.where(col < row, wt, 0.0)
                wts.append(wt.astype(BF16))
                evs.append(dws[hh] * wt)
            epieces = [_split(evs[hh], 2) for hh in hs]
            ecss = [_dot(epieces[hh][0], uexc) + _dot(epieces[hh][1], uexc) for hh in hs]
            for hh in hs:
                dv_ref[ks, _lanes(hh)] += _dot(wts[hh], dobs[hh], "tn")
            dzbs = []
            for hh in hs:
                sig = jnp.exp(lss[hh])
                dz = evs[hh] * (1.0 - sig) - (ecss[hh] + carry[3 * hh + 2]) * sig
                if masked:
                    dz = jnp.where(col < row, dz, 0.0)
                dzbs.append((dz * scale).astype(BF16))
            out = []
            for hh in hs:
                dk_ref[ks, _lanes(hh)] += _dot(dzbs[hh], qs[hh], "tn")
                out += [carry[3 * hh] + _dot(dzbs[hh], kvs[hh]), carry[3 * hh + 1] + jnp.sum(lks[hh], axis=1, keepdims=True),
                        carry[3 * hh + 2] + jnp.sum(evs[hh], axis=1, keepdims=True)]
            return tuple(out)
        za, z1 = jnp.zeros((blk, LANES), F32), jnp.zeros((blk, 1), F32)
        carry = lax.fori_loop(0, i, lambda kb, c: tile(kb, c, False), (za, z1, z1) * hps)
        res = tile(i, carry, True)
        for hh in range(hps):
            dq_ref[:, _lanes(hh)] = res[3 * hh]

    ng = nh // hps
    full = lambda f: pl.BlockSpec((t, hps * LANES), f, pipeline_mode=pl.Buffered(1))
    qb = pl.BlockSpec((blk, hps * LANES), lambda h, i: (i, h))
    wide = jax.ShapeDtypeStruct((t, nh * LANES), F32)
    return pl.pallas_call(
        body, name=name, grid=(ng, nq),
        in_specs=[qb, full(lambda h, i: (0, ng + h)), full(lambda h, i: (0, 2 * ng + h)), qb,
                  pl.BlockSpec((hps, blk, 1), lambda h, i: (h, i, 0))],
        out_specs=[qb, full(lambda h, i: (0, h)), full(lambda h, i: (0, h))],
        out_shape=[wide, wide, wide], compiler_params=_params(2))(qkv, qkv, qkv, do, ltot)


def _as2d(a):
    return a.reshape((-1, a.shape[-1]))


def _sum4(parts, name):
    _, r, c = parts.shape
    tr = _tile(r, (256, 128, 64, 32, 16, 8))

    def body(p_ref, o_ref):
        acc = p_ref[0].astype(F32)
        for j in range(1, 4):
            acc = acc + p_ref[j].astype(F32)
        o_ref[...] = acc

    return pl.pallas_call(body, name=name, grid=(r // tr,), in_specs=[pl.BlockSpec((4, tr, c), lambda i: (0, i, 0))],
                          out_specs=pl.BlockSpec((tr, c), lambda i: (i, 0)), out_shape=jax.ShapeDtypeStruct((r, c), F32),
                          compiler_params=_params(1))(parts)


def _adamw(w, gas, gbs, m, v, name):
    r, c = w.shape
    npc = len(gas)
    tr = _tile(r // npc, (128, 64, 32, 16, 8))
    nb = r // npc // tr
    c1, c2 = 1.0 - ADAM_B1 ** ADAM_STEP, 1.0 - ADAM_B2 ** ADAM_STEP
    two = gbs is not None
    ng = npc * (2 if two else 1)

    def body(*refs):
        w_ref, g_refs = refs[0], refs[1:1 + ng]
        m_ref, v_ref, g_out, d_out, m_out, v_out = refs[1 + ng:]
        pieces = [g_refs[p][...] + g_refs[npc + p][...] if two else g_refs[p][...] for p in range(npc)]
        gv = pieces[0]
        for p in range(1, npc):
            gv = jnp.where(pl.program_id(0) >= p * nb, pieces[p], gv)
        mn = ADAM_B1 * m_ref[...] + (1.0 - ADAM_B1) * gv
        vn = ADAM_B2 * v_ref[...] + (1.0 - ADAM_B2) * (gv * gv)
        g_out[...] = gv
        m_out[...] = mn
        v_out[...] = vn
        d_out[...] = -ADAM_LR * ((mn / c1) / (jnp.sqrt(vn / c2) + ADAM_EPS) + ADAM_WD * w_ref[...])

    blk = pl.BlockSpec((tr, c), lambda i: (i, 0))
    piece = lambda p: pl.BlockSpec((tr, c), lambda i: (jnp.clip(i - p * nb, 0, nb - 1), 0))
    gspecs = [piece(p) for p in range(npc)] * (2 if two else 1)
    ins = [w] + list(gas) + (list(gbs) if two else []) + [m, v]
    return pl.pallas_call(body, name=name, grid=(r // tr,), in_specs=[blk] + gspecs + [blk, blk], out_specs=[blk] * 4,
                          out_shape=[jax.ShapeDtypeStruct((r, c), F32)] * 4, compiler_params=_params(1))(*ins)


HBM_SPEC = pl.BlockSpec(memory_space=pltpu.HBM)
SEM_SPEC = pl.BlockSpec(memory_space=pltpu.SEMAPHORE)
EFFECT = pltpu.SideEffectType.DATAFLOW_SIDE_EFFECTING


def _chip_copies(ins, lands, send_sems, recv_sems, whole):
    x, y, c = lax.axis_index("x"), lax.axis_index("y"), lax.axis_index("c")
    me = 2 * x + y
    out = []
    for wi in range(len(ins)):
        for k, (px, py) in enumerate([(1 - x, y), (x, 1 - y), (1 - x, 1 - y)]):
            sems = dict(send_sem=send_sems[wi * 3 + k], recv_sem=recv_sems[wi * 3 + k], device_id=(px, py, c), device_id_type=MESH_T)
            peer = 2 * px + py
            sent = pltpu.make_async_remote_copy(src_ref=ins[wi] if whole else ins[wi].at[peer], dst_ref=lands[wi].at[me], **sems)
            got = functools.partial(pltpu.make_async_remote_copy, src_ref=ins[wi] if whole else ins[wi].at[me],
                                    dst_ref=lands[wi].at[peer], **sems)
            out.append((sent, got))
    return out


def _xstart(arrs, whole, after, name):
    n, na = len(arrs), len(after)
    me = 2 * lax.axis_index("x") + lax.axis_index("y")
    lands = []
    for a in arrs:
        own = a[None] if whole else lax.dynamic_slice_in_dim(a, me, 1, axis=0)
        empty = lax.empty(((4,) + a.shape) if whole else a.shape, a.dtype)
        lands.append(lax.dynamic_update_slice_in_dim(empty, own, me, axis=0))

    def body(*refs):
        ins, lands_in = refs[:n], refs[n:2 * n]
        outs = refs[2 * n + na:]
        for sent, _ in _chip_copies(ins, lands_in, outs[:3 * n], outs[3 * n:6 * n], whole):
            sent.start()
        outs[8 * n][...] = jnp.zeros((8, LANES), F32)

    hbm = lambda a: pltpu.HBM(a.shape, a.dtype)
    res = pl.pallas_call(
        body, name=name,
        out_shape=[pltpu.SemaphoreType.DMA(())] * (6 * n) + [hbm(a) for a in arrs] + [hbm(a) for a in lands]
        + [jax.ShapeDtypeStruct((8, LANES), F32)],
        in_specs=[HBM_SPEC] * (2 * n) + [pl.BlockSpec(memory_space=pl.ANY)] * na,
        out_specs=[SEM_SPEC] * (6 * n) + [HBM_SPEC] * (2 * n) + [pl.BlockSpec(memory_space=pltpu.VMEM)],
        input_output_aliases={i: 6 * n + i for i in range(2 * n)},
        compiler_params=pltpu.CompilerParams(has_side_effects=EFFECT),
    )(*[pltpu.with_memory_space_constraint(a, pltpu.HBM) for a in list(arrs) + lands], *after)
    return res


def _xwait(handle, whole, after, name):
    n = (len(handle) - 1) // 8
    sems, thru = handle[:6 * n], handle[6 * n:8 * n]

    def body(*refs):
        ins, lands_in = refs[:n], refs[n:2 * n]
        for sent, got in _chip_copies(ins, lands_in, refs[2 * n:5 * n], refs[5 * n:8 * n], whole):
            sent.wait_send()
            got().wait_recv()

    hbm = lambda a: pltpu.HBM(a.shape, a.dtype)
    res = pl.pallas_call(
        body, name=name, out_shape=[hbm(a) for a in thru],
        in_specs=[HBM_SPEC] * (2 * n) + [SEM_SPEC] * (6 * n) + [pl.BlockSpec(memory_space=pl.ANY)],
        out_specs=[HBM_SPEC] * (2 * n), input_output_aliases={i: i for i in range(2 * n)},
        compiler_params=pltpu.CompilerParams(has_side_effects=EFFECT),
    )(*thru, *sems, after)
    return res[n:]


def _sibling_copies(ins, lands, send_sems, recv_sems):
    sib = (lax.axis_index("x"), lax.axis_index("y"), 1 - lax.axis_index("c"))
    return [pltpu.make_async_remote_copy(src_ref=ins[wi], dst_ref=lands[wi], send_sem=send_sems[wi], recv_sem=recv_sems[wi],
                                         device_id=sib, device_id_type=MESH_T) for wi in range(len(ins))]


def _sib_start(arrs, after, name):
    n, na = len(arrs), len(after)
    lands = [lax.empty(a.shape, a.dtype) for a in arrs]

    def body(*refs):
        outs = refs[2 * n + na:]
        for cp in _sibling_copies(refs[:n], refs[n:2 * n], outs[:n], outs[n:2 * n]):
            cp.start()
        outs[4 * n][...] = jnp.zeros((8, LANES), F32)

    hbm = lambda a: pltpu.HBM(a.shape, a.dtype)
    return pl.pallas_call(
        body, name=name,
        out_shape=[pltpu.SemaphoreType.DMA(())] * (2 * n) + [hbm(a) for a in arrs] * 2 + [jax.ShapeDtypeStruct((8, LANES), F32)],
        in_specs=[HBM_SPEC] * (2 * n) + [pl.BlockSpec(memory_space=pl.ANY)] * na,
        out_specs=[SEM_SPEC] * (2 * n) + [HBM_SPEC] * (2 * n) + [pl.BlockSpec(memory_space=pltpu.VMEM)],
        input_output_aliases={i: 2 * n + i for i in range(2 * n)},
        compiler_params=pltpu.CompilerParams(has_side_effects=EFFECT),
    )(*[pltpu.with_memory_space_constraint(a, pltpu.HBM) for a in list(arrs) + lands], *after)


def _sib_wait(handle, after, name):
    n = (len(handle) - 1) // 4
    sems, thru = handle[:2 * n], handle[2 * n:4 * n]

    def body(*refs):
        for cp in _sibling_copies(refs[:n], refs[n:2 * n], refs[2 * n:3 * n], refs[3 * n:4 * n]):
            cp.wait_send()
            cp.wait_recv()

    res = pl.pallas_call(
        body, name=name, out_shape=[pltpu.HBM(a.shape, a.dtype) for a in thru],
        in_specs=[HBM_SPEC] * (2 * n) + [SEM_SPEC] * (2 * n) + [pl.BlockSpec(memory_space=pl.ANY)],
        out_specs=[HBM_SPEC] * (2 * n), input_output_aliases={i: i for i in range(2 * n)},
        compiler_params=pltpu.CompilerParams(has_side_effects=EFFECT),
    )(*thru, *sems, after)
    return res[n:]


def _sibling_exchange(arrs, name):
    n = len(arrs)

    def body(*refs):
        ins, outs = refs[:n], refs[n:2 * n]
        send_sems, recv_sems = refs[2 * n:]
        sib = (lax.axis_index("x"), lax.axis_index("y"), 1 - lax.axis_index("c"))
        cps = [pltpu.make_async_remote_copy(src_ref=ins[wi], dst_ref=outs[wi], send_sem=send_sems.at[wi], recv_sem=recv_sems.at[wi],
                                            device_id=sib, device_id_type=MESH_T) for wi in range(n)]
        for cp in cps:
            cp.start()
        for cp in cps:
            cp.wait_recv()
        for cp in cps:
            cp.wait_send()

    anyspec = pl.BlockSpec(memory_space=pl.ANY)
    return pl.pallas_call(
        body, name=name, in_specs=[anyspec] * n, out_specs=[anyspec] * n,
        out_shape=[jax.ShapeDtypeStruct(a.shape, a.dtype) for a in arrs],
        scratch_shapes=[pltpu.SemaphoreType.DMA((n,)), pltpu.SemaphoreType.DMA((n,))],
        compiler_params=pltpu.CompilerParams(has_side_effects=True))(*arrs)


def _allreduce_small(packed, name):
    r = packed.shape[0]

    def body(in_ref, out_ref, land, send_sems, recv_sems):
        x, y, c = lax.axis_index("x"), lax.axis_index("y"), lax.axis_index("c")
        me = 4 * x + 2 * y + c
        land[me] = in_ref[...]
        rel = [(dx, dy, dc) for dx in (0, 1) for dy in (0, 1) for dc in (0, 1)][1:]
        peers = [((1 - x) if dx else x, (1 - y) if dy else y, (1 - c) if dc else c) for dx, dy, dc in rel]
        sends = []
        for k, peer in enumerate(peers):
            cp = pltpu.make_async_remote_copy(src_ref=in_ref, dst_ref=land.at[me], send_sem=send_sems.at[k], recv_sem=recv_sems.at[k],
                                              device_id=peer, device_id_type=MESH_T)
            cp.start()
            sends.append(cp)
        for k, (px, py, pc) in enumerate(peers):
            pltpu.make_async_remote_copy(src_ref=in_ref, dst_ref=land.at[4 * px + 2 * py + pc], send_sem=send_sems.at[k],
                                         recv_sem=recv_sems.at[k], device_id=(px, py, pc), device_id_type=MESH_T).wait_recv()
        for cp in sends:
            cp.wait_send()
        acc = land[0]
        for j in range(1, 8):
            acc = acc + land[j]
        out_ref[...] = acc

    vm = pl.BlockSpec(memory_space=pltpu.VMEM)
    return pl.pallas_call(
        body, name=name, in_specs=[vm], out_specs=vm, out_shape=jax.ShapeDtypeStruct((r, LANES), F32),
        scratch_shapes=[pltpu.VMEM((8, r, LANES), F32), pltpu.SemaphoreType.DMA((7,)), pltpu.SemaphoreType.DMA((7,))],
        compiler_params=pltpu.CompilerParams(has_side_effects=True))(packed)


def _in_splits():
    w = SSD_HEADS * SSD_HEAD_DIM
    cc = w + 2 * SSD_GROUPS * SSD_STATE
    return [w, cc, SSD_HEADS, MLA_Q_RANK, MLA_KV_RANK, MLA_ROPE]


def _padc(a, n):
    return jnp.pad(a, ((0, 0), (0, n - a.shape[1])))


def _win_pad(wm):
    offs = np.cumsum(_in_splits())[:-1]
    z, xbc, dt, cq, ckv, kr = jnp.split(wm, offs, axis=1)
    return jnp.concatenate([z, xbc, cq, ckv, _padc(kr, LANES), _padc(dt, LANES)], axis=1)


def _win_unpad(g):
    w, cc, nh, qr, kvr, rp = _in_splits()
    offs = np.cumsum([w, cc, qr, kvr, LANES])
    z, xbc, cq, ckv, kr, dt = jnp.split(g, offs, axis=1)
    return jnp.concatenate([z, xbc, dt[:, :nh], cq, ckv, kr[:, :rp]], axis=1)


def _wuq_pad(wm):
    r = wm.shape[0]
    w3 = wm.reshape(r, MLA_HEADS, MLA_NOPE + MLA_ROPE)
    return jnp.pad(w3, ((0, 0), (0, 0), (0, 2 * LANES - MLA_NOPE - MLA_ROPE))).reshape(r, MLA_HEADS * 2 * LANES)


def _wuq_unpad(g):
    r = g.shape[0]
    return g.reshape(r, MLA_HEADS, 2 * LANES)[:, :, :MLA_NOPE + MLA_ROPE].reshape(r, MLA_HEADS * (MLA_NOPE + MLA_ROPE))


def _full_from_gather(name, g):
    layers = g.shape[1]
    if name not in COL_SHARDED:
        return [g[:, l].reshape(-1, g.shape[-1]) for l in range(layers)]
    return [jnp.concatenate([g[j, l] for j in range(4)], axis=1) for l in range(layers)]


def _gathered_t(g):
    return g.transpose(0, 1, 3, 2).reshape(-1, g.shape[2])


def _shards_from_full(name, mats):
    col = name in COL_SHARDED
    per = []
    for mt in mats:
        r, c = mt.shape
        per.append(mt.reshape(r, 4, c // 4).transpose(1, 0, 2) if col else mt.reshape(4, r // 4, c))
    return jnp.stack(per, axis=1).astype(BF16)


def _pack_small(vals, extra=None):
    flat = jnp.concatenate([vals[n].reshape(-1).astype(F32) for n in SMALL] + ([extra.reshape(1)] if extra is not None else []))
    rows = -(-flat.shape[0] // (8 * LANES)) * 8
    return jnp.pad(flat, (0, rows * LANES - flat.shape[0])).reshape(rows, LANES)


def _unpack_small(packed, like):
    flat, out, off = packed.reshape(-1), {}, 0
    for n in SMALL:
        sz = like[n].size
        out[n] = flat[off:off + sz].reshape(like[n].shape)
        off += sz
    return out


def _ffn_fwd(x, norm_g, wg, wu, wd, tag):
    h, r = _rms_fwd(x, norm_g, f"ffn_norm_{tag}")
    gate = _mm(h, wg, "nn", f"ffn_gate_{tag}", out_dtype=BF16, b_shards=True)
    up = _mm(h, wu, "nn", f"ffn_up_{tag}", out_dtype=BF16, b_shards=True)
    act = _swiglu_fwd(gate, up, f"swiglu_{tag}")
    out = _mm(act, wd, "nn", f"ffn_down_{tag}", add=x)
    return out, (x, r, h, gate, up, act)


def _ffn_bwd(dout, doutb, norm_g, wg, wu, wd, saved, tag):
    x, r, h, gate, up, act = saved
    dact = _mm(doutb, wd.T, "nn", f"ffn_dact_{tag}", out_dtype=BF16)
    dgate, dup = _swiglu_bwd(gate, up, dact, f"swiglu_bwd_{tag}")
    ht = h.T
    d_wd = _mm(act.T, doutb, "nn", f"ffn_dwd_{tag}", out_dtype=BF16)
    d_wg = _mm(ht, dgate, "nn", f"ffn_dwg_{tag}", out_dtype=BF16, out_shards=True)
    d_wu = _mm(ht, dup, "nn", f"ffn_dwu_{tag}", out_dtype=BF16, out_shards=True)
    dh = _mm(dgate, _gathered_t(wg), "nn", f"ffn_dh1_{tag}")
    dh = _mm(dup, _gathered_t(wu), "nn", f"ffn_dh2_{tag}", add=dh)
    dx, dxb, dnorm = _rms_bwd(x, norm_g, r, dh, f"ffn_norm_bwd_{tag}", dres=dout)
    return dx, dxb, dnorm, d_wg, d_wu, d_wd


def kernel(x, mix_norm, ffn_norm, w_in, conv_w, conv_b, dt_bias, a_log, d_skip, ssd_norm, q_norm, kv_norm, w_uq, w_ukv, w_out_even, w_qkv, w_out_odd, w_gate, w_up, w_down, final_norm, loss_target, m_mix_norm, m_ffn_norm, m_w_in, m_conv_w, m_conv_b, m_dt_bias, m_a_log, m_d_skip, m_ssd_norm, m_q_norm, m_kv_norm, m_w_uq, m_w_ukv, m_w_out_even, m_w_qkv, m_w_out_odd, m_w_gate, m_w_up, m_w_down, m_final_norm, v_mix_norm, v_ffn_norm, v_w_in, v_conv_w, v_conv_b, v_dt_bias, v_a_log, v_d_skip, v_ssd_norm, v_q_norm, v_kv_norm, v_w_uq, v_w_ukv, v_w_out_even, v_w_qkv, v_w_out_odd, v_w_gate, v_w_up, v_w_down, v_final_norm):
    given = dict(locals())
    wts = {n: given[n] for n in WEIGHTS}
    x0 = x[0]
    tgt = loss_target[0]
    t, d = x0.shape
    hw = SSD_HEADS * SSD_HEAD_DIM
    gn = SSD_GROUPS * SSD_STATE
    cc = hw + 2 * gn
    qr, kvr = MLA_Q_RANK, MLA_KV_RANK

    def shard(n, layer=None):
        a = wts[n] if layer is None else wts[n][layer:layer + 1]
        return a if n == 'conv_w' else a.astype(BF16)
    g0_names = ['w_in', 'conv_w']
    gq_names = ['w_uq', 'w_ukv']
    g1_names = [('w_out_even', None), ('w_gate', 0), ('w_up', 0), ('w_down', 0)]
    g2_names = [('w_qkv', None), ('w_out_odd', None), ('w_gate', 1), ('w_up', 1), ('w_down', 1)]
    hg0 = _xstart([shard(n) for n in g0_names], True, [], "gather0_start")
    hgq = _xstart([shard(n) for n in gq_names], True, [hg0[-1]], "gatherq_start")
    hg1 = _xstart([shard(n, l) for n, l in g1_names], True, [hgq[-1]], "gather1_start")
    hg2 = _xstart([shard(n, l) for n, l in g2_names], True, [hg1[-1]], "gather2_start")
    full = lambda n, g: _full_from_gather(n, g)[0]
    g0 = dict(zip(g0_names, _xwait(hg0, True, hg2[-1], "gather0_wait")))
    win = _win_pad(full('w_in', g0['w_in']))
    cw = _padc(full('conv_w', g0['conv_w']).T, 8).T
    o_cq, o_ckv, o_kr, o_dt = hw + cc, hw + cc + qr, hw + cc + qr + kvr, hw + cc + qr + kvr + LANES

    row = lambda v: v.reshape(1, -1)
    narrow = lambda v: _padc(v.reshape(1, -1), LANES)
    dtb, alog = narrow(dt_bias[0]), narrow(a_log[0])
    dsk_x = jnp.repeat(d_skip[0], SSD_HEAD_DIM).reshape(1, hw)
    cos, sin = _rope_tables(t)

    h0, r0 = _rms_fwd(x0, row(mix_norm[0]), "mix_norm_0")
    u = _mm(h0, win, "nn", "in_proj")
    z, xbc, c_q, c_kv = u[:, :hw], u[:, hw:hw + cc], u[:, o_cq:o_ckv], u[:, o_ckv:o_kr]
    kr_raw, dtraw = u[:, o_kr:o_dt], u[:, o_dt:]
    xc = _conv_fwd(xbc, cw, row(conv_b[0]), "conv")
    y_ssd, sprev = _ssd_fwd(dtraw, xc, dtb, alog, dsk_x, "ssd")
    yg, r_g = _gated_fwd(y_ssd, z, row(ssd_norm[0]), "ssd_gate_norm")
    q_lat, r_q = _rms_fwd(c_q, row(q_norm[0]), "q_norm")
    kv_lat, r_kv = _rms_fwd(c_kv, row(kv_norm[0]), "kv_norm")
    gq = dict(zip(gq_names, _xwait(hgq, True, r_kv, "gatherq_wait")))
    wuq = _wuq_pad(full('w_uq', gq['w_uq']))
    wukv = gq['w_ukv']
    qf = _mm(q_lat, wuq, "nn", "q_up")
    kvb = _mm(kv_lat, wukv, "nn", "kv_up", out_dtype=BF16, b_shards=True)
    q_r = _rope(qf, cos, sin, 2, False, BF16, "rope_q")
    k_r = _rope(kr_raw, cos, sin, 1, False, BF16, "rope_k")
    o_mla, lse = _mla_fwd(q_r, k_r, kvb, "mla")
    g1 = dict(zip(g1_names, _xwait(hg1, True, lse, "gather1_wait")))
    woe, wd0 = full('w_out_even', g1[g1_names[0]]), full('w_down', g1[g1_names[3]])
    wg0, wu0 = g1[g1_names[1]], g1[g1_names[2]]
    cat = jnp.concatenate([yg, o_mla.astype(BF16)], axis=1)
    x1 = _mm(cat, woe, "nn", "mix_out_0", add=x0)
    x2, ffn0 = _ffn_fwd(x1, row(ffn_norm[0]), wg0, wu0, wd0, "0")

    g2 = dict(zip(g2_names, _xwait(hg2, True, x2, "gather2_wait")))
    woo, wd1 = full('w_out_odd', g2[g2_names[1]]), full('w_down', g2[g2_names[4]])
    wqkv, wg1, wu1 = g2[g2_names[0]], g2[g2_names[2]], g2[g2_names[3]]
    h1, r1 = _rms_fwd(x2, row(mix_norm[1]), "mix_norm_1")
    qkv = _mm(h1, wqkv, "nn", "qkv_proj", out_dtype=BF16, b_shards=True)
    o_sb, ltot = _sb_fwd(qkv, "sb")
    o_sbb = o_sb.astype(BF16)
    x3 = _mm(o_sbb, woo, "nn", "mix_out_1", add=x2)
    x4, ffn1 = _ffn_fwd(x3, row(ffn_norm[1]), wg1, wu1, wd1, "1")

    loss_part, dx4, dx4b, d_final = _loss_fwd_bwd(x4, row(final_norm), tgt, "loss")

    def grad_shards(pairs):
        return [g[:, None] if g.ndim == 3 else _shards_from_full(n, [g]) for n, g in pairs]

    dx3, dx3b, d_ffn1, d_wg1, d_wu1, d_wd1 = _ffn_bwd(dx4, dx4b, row(ffn_norm[1]), wg1, wu1, wd1, ffn1, "1")
    do_sb = _mm(dx3b, woo.T, "nn", "sb_dout")
    d_woo = _mm(o_sbb.T, dx3b, "nn", "d_w_out_odd", out_dtype=BF16)
    dq, dk, dv = _sb_bwd(qkv, do_sb, ltot, "sb_bwd")
    dqkv = jnp.concatenate([dq, dk, dv], axis=1).astype(BF16)
    d_wqkv = _mm(h1.T, dqkv, "nn", "d_w_qkv", out_dtype=BF16, out_shards=True)
    x2_names = ['w_qkv', 'w_out_odd', 'w_gate', 'w_up', 'w_down']
    hx2 = _xstart(grad_shards(zip(x2_names, [d_wqkv, d_woo, d_wg1, d_wu1, d_wd1])), False, [], "grads2_start")
    dh1 = _mm(dqkv, _gathered_t(wqkv), "nn", "qkv_dh")
    dx2, dx2b, d_mix1 = _rms_bwd(x2, row(mix_norm[1]) + hx2[-1][0:1, 0:1], r1, dh1, "mix_norm_bwd_1", dres=dx3)

    dx1, dx1b, d_ffn0, d_wg0, d_wu0, d_wd0 = _ffn_bwd(dx2, dx2b, row(ffn_norm[0]), wg0, wu0, wd0, ffn0, "0")
    d_woe = _mm(cat.T, dx1b, "nn", "d_w_out_even", out_dtype=BF16)
    x1_names = ['w_out_even', 'w_gate', 'w_up', 'w_down']
    hx1 = _xstart(grad_shards(zip(x1_names, [d_woe, d_wg0, d_wu0, d_wd0])), False, [], "grads1_start")
    tok1 = hx1[-1][0:1, 0:1]
    dcat = _mm(dx1b, woe.T, "nn", "mix_dcat")
    dy_ssd, dz, d_ssdn = _gated_bwd(y_ssd, z, row(ssd_norm[0]) + tok1, r_g, dcat[:, :hw], "ssd_gate_norm_bwd")
    dxs, db_, dc_, ddtraw, d_dtb, d_alog, d_dsk = _ssd_bwd(dtraw, xc, dtb, alog, dsk_x, sprev, dy_ssd, "ssd_bwd")
    dxbc, d_cw, d_cb = _conv_bwd(xbc, cw, row(conv_b[0]), jnp.concatenate([dxs, db_, dc_], axis=1), "conv_bwd")
    dqm, dkvm, dkr = _mla_bwd(q_r, k_r, kvb, o_mla, dcat[:, hw:], lse + tok1, "mla_bwd")
    dqf = _rope(dqm, cos, sin, 2, True, BF16, "rope_q_bwd")
    dkr_raw = _rope(dkr, cos, sin, 1, True, F32, "rope_k_bwd")
    dkvf = dkvm.astype(BF16)
    d_wuq = _mm(q_lat.T, dqf, "nn", "d_w_uq", out_dtype=BF16)
    d_wukv = _mm(kv_lat.T, dkvf, "nn", "d_w_ukv", out_dtype=BF16, out_shards=True)
    dq_lat = _mm(dqf, wuq.T, "nn", "q_up_bwd")
    dkv_lat = _mm(dkvf, _gathered_t(wukv), "nn", "kv_up_bwd")
    dc_q, _, d_qn = _rms_bwd(c_q, row(q_norm[0]), r_q, dq_lat, "q_norm_bwd")
    dc_kv, _, d_kvn = _rms_bwd(c_kv, row(kv_norm[0]), r_kv, dkv_lat, "kv_norm_bwd")
    du = jnp.concatenate([dz, dxbc, dc_q, dc_kv, dkr_raw, ddtraw], axis=1).astype(BF16)
    d_win = _mm(h0.T, du, "nn", "d_w_in", out_dtype=BF16)
    x0_names = ['w_in', 'conv_w', 'w_uq', 'w_ukv']
    hx0 = _xstart(grad_shards(zip(x0_names, [_win_unpad(d_win), d_cw, _wuq_unpad(d_wuq), d_wukv])), False, [], "grads0_start")
    dh0 = _mm(du, win.T, "nn", "in_proj_bwd")
    grad_x, _, d_mix0 = _rms_bwd(x0, row(mix_norm[0]) + hx0[-1][0:1, 0:1], r0, dh0, "mix_norm_bwd_0", dres=dx1)

    def chip_sums(handle, names, after, tag):
        lands = _xwait(handle, False, after, f"grads{tag}_wait")
        return {n: _sum4(p.reshape(4, -1, p.shape[-1]), f"sum{tag}_{n}") for n, p in zip(names, lands)}
    grads, deltas, new_m, new_v = {}, {}, {}, {}

    def adamw(n, ga, gb):
        res = _adamw(_as2d(wts[n]), ga, gb, _as2d(given['m_' + n]), _as2d(given['v_' + n]), f"adamw_{n}")
        grads[n], deltas[n], new_m[n], new_v[n] = [r.reshape(wts[n].shape) for r in res]
        return res[0]
    s2 = chip_sums(hx2, x2_names, grad_x, "2")
    hs2 = _sib_start([s2[n] for n in x2_names], [], "sibling2_start")
    s1 = chip_sums(hx1, x1_names, hs2[-1], "1")
    hs1 = _sib_start([s1[n] for n in x1_names], [], "sibling1_start")
    r2 = dict(zip(x2_names, _sib_wait(hs2, hs1[-1], "sibling2_wait")))
    r1 = dict(zip(x1_names, _sib_wait(hs1, r2['w_qkv'], "sibling1_wait")))
    last = None
    for n in ('w_qkv', 'w_out_odd'):
        last = adamw(n, [s2[n]], [r2[n]])
    last = adamw('w_out_even', [s1['w_out_even']], [r1['w_out_even']])
    for n in ('w_gate', 'w_up', 'w_down'):
        last = adamw(n, [s1[n], s2[n]], [r1[n], r2[n]])
    s0 = chip_sums(hx0, x0_names, last, "0")
    r0_ = _sibling_exchange([s0[n] for n in x0_names], "exchange_sibling0")
    for n, gb in zip(x0_names, r0_):
        adamw(n, [s0[n]], [gb])

    nhs = SSD_HEADS
    small_g = {'mix_norm': jnp.stack([d_mix0, d_mix1]), 'ffn_norm': jnp.stack([d_ffn0, d_ffn1]), 'conv_b': d_cb[None],
               'dt_bias': d_dtb[None, :nhs], 'a_log': d_alog[None, :nhs], 'd_skip': d_dsk[None, :nhs], 'ssd_norm': d_ssdn[None],
               'q_norm': d_qn[None], 'kv_norm': d_kvn[None], 'final_norm': d_final}
    packed, _ = lax.optimization_barrier((_pack_small(small_g, loss_part), r0_[0]))
    g_small = _allreduce_small(packed, "allreduce_small")
    loss = g_small.reshape(-1)[sum(wts[n].size for n in SMALL)]
    pk = lambda pre: _pack_small({n: given[pre + n] for n in SMALL}, jnp.zeros((), F32))
    sg, sd, sm, sv = _adamw(pk(''), [g_small], None, pk('m_'), pk('v_'), "adamw_small")
    for dst, src in ((grads, sg), (deltas, sd), (new_m, sm), (new_v, sv)):
        dst.update(_unpack_small(src, wts))

    outs = [loss, grad_x[None]]
    for dct in (grads, deltas, new_m, new_v):
        outs += [dct[n] for n in WEIGHTS]
    return tuple(outs)
```

```python
import functools
import math

import jax
import jax.numpy as jnp
import numpy as np
from jax import lax
from jax.experimental import pallas as pl
from jax.experimental.pallas import tpu as pltpu

F32, BF16 = jnp.float32, jnp.bfloat16

RMS_EPS = 1e-6
SSD_HEADS, SSD_HEAD_DIM, SSD_GROUPS, SSD_STATE, SSD_CONV, SSD_CHUNK = 32, 64, 4, 128, 4, 128
MLA_HEADS, MLA_Q_RANK, MLA_KV_RANK, MLA_NOPE, MLA_ROPE, MLA_V = 16, 512, 512, 128, 64, 128
ROPE_THETA = 10000.0
SB_HEADS, SB_HEAD_DIM = 16, 128
ADAM_LR, ADAM_B1, ADAM_B2, ADAM_EPS, ADAM_WD, ADAM_STEP = 0.001, 0.9, 0.999, 1e-08, 0.01, 10

LANES = 128
VMEM_LIMIT_BYTES = 56 * 1024 * 1024
MM_VMEM_BUDGET = 40 * 1024 * 1024
ATT_BLK = 256
SB_FWD_HEADS = 4
SB_BWD_HEADS = 4
MLA_FWD_HEADS = 4
MLA_BWD_HEADS = 4
ROW_TILE = 256
NEG = -1e30

MESH_T = pl.DeviceIdType.MESH
WEIGHTS = ['mix_norm', 'ffn_norm', 'w_in', 'conv_w', 'conv_b', 'dt_bias', 'a_log', 'd_skip', 'ssd_norm', 'q_norm',
           'kv_norm', 'w_uq', 'w_ukv', 'w_out_even', 'w_qkv', 'w_out_odd', 'w_gate', 'w_up', 'w_down', 'final_norm']
SHARDED = ['w_in', 'conv_w', 'w_uq', 'w_ukv', 'w_out_even', 'w_qkv', 'w_out_odd', 'w_gate', 'w_up', 'w_down']
COL_SHARDED = ['w_in', 'conv_w', 'w_uq', 'w_ukv', 'w_qkv', 'w_gate', 'w_up']
SMALL = [n for n in WEIGHTS if n not in SHARDED]


def _tile(n, cands):
    for c in cands:
        if n % c == 0:
            return c
    return n


def _params(ngrid):
    return pltpu.CompilerParams(dimension_semantics=("arbitrary",) * ngrid, vmem_limit_bytes=VMEM_LIMIT_BYTES)


def _dot(a, b, mode="nn"):
    dims = {"nn": (((1,), (0,)), ((), ())), "nt": (((1,), (1,)), ((), ())), "tn": (((0,), (0,)), ((), ()))}[mode]
    return lax.dot_general(a, b, dims, preferred_element_type=F32)


def _split(x, parts):
    out, r = [], x
    for _ in range(parts):
        p = r.astype(BF16)
        out.append(p)
        r = r - p.astype(F32)
    return out


def _xdot(x, e, parts=3):
    acc = None
    for p in _split(x, parts):
        t = _dot(p, e)
        acc = t if acc is None else acc + t
    return acc


def _xdot_l(e, x, parts=3):
    acc = None
    for p in _split(x, parts):
        t = _dot(e, p)
        acc = t if acc is None else acc + t
    return acc


def _iota(shape, dim):
    return lax.broadcasted_iota(jnp.int32, shape, dim)


def _softplus(z):
    return jnp.maximum(z, 0.0) + jnp.log(1.0 + jnp.exp(-jnp.abs(z)))


def _sigmoid(z):
    return 1.0 / (1.0 + jnp.exp(-z))


def _acc_rows(ref, first, val):
    @pl.when(first)
    def _():
        ref[...] = jnp.zeros_like(ref)
    ref[...] += jnp.broadcast_to(val, ref.shape)


def _mm(a, b, mode, name, out_dtype=F32, add=None, b_shards=False, out_shards=False):
    assert mode == "nn"
    m, k = a.shape
    n = 4 * b.shape[3] if b_shards else b.shape[1]
    ns = n // 4 if (b_shards or out_shards) else n
    tn = _tile(ns, (1536, 1408, 1280, 1024, 768, 640, 512, 256, 128))
    per = ns // tn
    tk = k if k <= 2048 else _tile(k, (2048, 1536, 1408, 1280, 1024, 512, 256, 128))
    ob = jnp.dtype(out_dtype).itemsize

    def need(tm_):
        per = tm_ * tk * a.dtype.itemsize + tk * tn * b.dtype.itemsize + tm_ * tn * ob + (tm_ * tn * 4 if add is not None else 0)
        return 2 * per + (tm_ * tn * 4 if k > tk else 0)
    tm = m
    for cand in (1408, 1024, 512, 256, 128):
        if m % cand == 0:
            tm = cand
            if need(cand) <= MM_VMEM_BUDGET:
                break
    nk = k // tk
    a_spec = pl.BlockSpec((tm, tk), lambda i, j, kk: (i, kk))
    if b_shards:
        b_spec = pl.BlockSpec((None, None, tk, tn), lambda i, j, kk: (j // per, 0, kk, j % per))
    else:
        b_spec = pl.BlockSpec((tk, tn), lambda i, j, kk: (kk, j))
    if out_shards:
        assert add is None
        o_spec = pl.BlockSpec((None, tm, tn), lambda i, j, kk: (j // per, i, j % per))
    else:
        o_spec = pl.BlockSpec((tm, tn), lambda i, j, kk: (i, j))
    has_add = add is not None

    def body(*refs):
        a_ref, b_ref = refs[0], refs[1]
        add_ref = refs[2] if has_add else None
        o_ref = refs[2 + has_add]
        part = _dot(a_ref[...].astype(BF16), b_ref[...].astype(BF16), mode)
        if nk == 1:
            if has_add:
                part = part + add_ref[...]
            o_ref[...] = part.astype(o_ref.dtype)
            return
        acc_ref = refs[3 + has_add]
        kk = pl.program_id(2)

        @pl.when(kk == 0)
        def _():
            acc_ref[...] = jnp.zeros_like(acc_ref)
        acc_ref[...] += part

        @pl.when(kk == nk - 1)
        def _():
            r = acc_ref[...]
            if has_add:
                r = r + add_ref[...]
            o_ref[...] = r.astype(o_ref.dtype)

    ins = [a, b] + ([add] if has_add else [])
    specs = [a_spec, b_spec] + ([o_spec] if has_add else [])
    return pl.pallas_call(
        body, name=name, grid=(m // tm, n // tn, nk), in_specs=specs, out_specs=o_spec,
        out_shape=jax.ShapeDtypeStruct((4, m, ns) if out_shards else (m, n), out_dtype),
        scratch_shapes=[pltpu.VMEM((tm, tn), F32)] if nk > 1 else [],
        compiler_params=_params(3))(*ins)


def _rows(t):
    return _tile(t, (ROW_TILE, 128, 64, 32, 16, 8))


def _rms_fwd(x, g, name):
    t, c = x.shape
    tr = _rows(t)

    def body(x_ref, g_ref, h_ref, r_ref):
        xv = x_ref[...]
        r = lax.rsqrt(jnp.mean(xv * xv, axis=-1, keepdims=True) + RMS_EPS)
        h_ref[...] = (xv * r * g_ref[...]).astype(h_ref.dtype)
        r_ref[...] = r

    return pl.pallas_call(
        body, name=name, grid=(t // tr,),
        in_specs=[pl.BlockSpec((tr, c), lambda i: (i, 0)), pl.BlockSpec((1, c), lambda i: (0, 0))],
        out_specs=[pl.BlockSpec((tr, c), lambda i: (i, 0)), pl.BlockSpec((tr, 1), lambda i: (i, 0))],
        out_shape=[jax.ShapeDtypeStruct((t, c), BF16), jax.ShapeDtypeStruct((t, 1), F32)],
        compiler_params=_params(1))(x, g)


def _rms_bwd(x, g, r, dh, name, dres=None):
    t, c = x.shape
    tr = _rows(t)
    has_res = dres is not None

    def body(*refs):
        x_ref, g_ref, r_ref, dh_ref = refs[:4]
        res_ref = refs[4] if has_res else None
        dx_ref, dxb_ref, dg_ref = refs[4 + has_res:]
        rv = r_ref[...]
        xh = x_ref[...] * rv
        dhv = dh_ref[...]
        dxh = dhv * g_ref[...]
        cm = jnp.mean(dxh * xh, axis=-1, keepdims=True)
        dx = (dxh - xh * cm) * rv
        if has_res:
            dx = dx + res_ref[...]
        dx_ref[...] = dx
        dxb_ref[...] = dx.astype(BF16)
        _acc_rows(dg_ref, pl.program_id(0) == 0, jnp.sum(dhv * xh, axis=0, keepdims=True))

    row = pl.BlockSpec((tr, c), lambda i: (i, 0))
    ins = [x, g, r, dh] + ([dres] if has_res else [])
    specs = [row, pl.BlockSpec((1, c), lambda i: (0, 0)), pl.BlockSpec((tr, 1), lambda i: (i, 0)), row] + ([row] if has_res else [])
    dx, dxb, dg = pl.pallas_call(
        body, name=name, grid=(t // tr,), in_specs=specs,
        out_specs=[row, row, pl.BlockSpec((8, c), lambda i: (0, 0))],
        out_shape=[jax.ShapeDtypeStruct((t, c), F32), jax.ShapeDtypeStruct((t, c), BF16), jax.ShapeDtypeStruct((8, c), F32)],
        compiler_params=_params(1))(*ins)
    return dx, dxb, dg[0]


def _gated_fwd(y, z, g, name):
    t, c = y.shape
    tr = _rows(t)

    def body(y_ref, z_ref, g_ref, o_ref, r_ref):
        zv = z_ref[...]
        v = y_ref[...] * zv * _sigmoid(zv)
        r = lax.rsqrt(jnp.mean(v * v, axis=-1, keepdims=True) + RMS_EPS)
        o_ref[...] = (v * r * g_ref[...]).astype(o_ref.dtype)
        r_ref[...] = r

    row = pl.BlockSpec((tr, c), lambda i: (i, 0))
    return pl.pallas_call(
        body, name=name, grid=(t // tr,), in_specs=[row, row, pl.BlockSpec((1, c), lambda i: (0, 0))],
        out_specs=[row, pl.BlockSpec((tr, 1), lambda i: (i, 0))],
        out_shape=[jax.ShapeDtypeStruct((t, c), BF16), jax.ShapeDtypeStruct((t, 1), F32)],
        compiler_params=_params(1))(y, z, g)


def _gated_bwd(y, z, g, r, dout, name):
    t, c = y.shape
    tr = _rows(t)

    def body(y_ref, z_ref, g_ref, r_ref, do_ref, dy_ref, dz_ref, dg_ref):
        yv, zv, rv, dov = y_ref[...], z_ref[...], r_ref[...], do_ref[...]
        s = _sigmoid(zv)
        sz = zv * s
        xh = yv * sz * rv
        dxh = dov * g_ref[...]
        cm = jnp.mean(dxh * xh, axis=-1, keepdims=True)
        dv = (dxh - xh * cm) * rv
        dy_ref[...] = dv * sz
        dz_ref[...] = dv * yv * s * (1.0 + zv * (1.0 - s))
        _acc_rows(dg_ref, pl.program_id(0) == 0, jnp.sum(dov * xh, axis=0, keepdims=True))

    row = pl.BlockSpec((tr, c), lambda i: (i, 0))
    dy, dz, dg = pl.pallas_call(
        body, name=name, grid=(t // tr,),
        in_specs=[row, row, pl.BlockSpec((1, c), lambda i: (0, 0)), pl.BlockSpec((tr, 1), lambda i: (i, 0)), row],
        out_specs=[row, row, pl.BlockSpec((8, c), lambda i: (0, 0))],
        out_shape=[jax.ShapeDtypeStruct((t, c), F32), jax.ShapeDtypeStruct((t, c), F32), jax.ShapeDtypeStruct((8, c), F32)],
        compiler_params=_params(1))(y, z, g, r, dout)
    return dy, dz, dg[0]


def _swiglu_fwd(gate, up, name):
    t, c = gate.shape
    tr, tc = _rows(t), _tile(c, (2816, 1408, 1024, 512, 256, 128))

    def body(g_ref, u_ref, o_ref):
        gv = g_ref[...].astype(F32)
        o_ref[...] = (gv * _sigmoid(gv) * u_ref[...].astype(F32)).astype(o_ref.dtype)

    blk = pl.BlockSpec((tr, tc), lambda i, j: (i, j))
    return pl.pallas_call(body, name=name, grid=(t // tr, c // tc), in_specs=[blk, blk], out_specs=blk,
                          out_shape=jax.ShapeDtypeStruct((t, c), BF16), compiler_params=_params(2))(gate, up)


def _swiglu_bwd(gate, up, dact, name):
    t, c = gate.shape
    tr, tc = _rows(t), _tile(c, (2816, 1408, 1024, 512, 256, 128))

    def body(g_ref, u_ref, d_ref, dg_ref, du_ref):
        gv, dv = g_ref[...].astype(F32), d_ref[...].astype(F32)
        s = _sigmoid(gv)
        dg_ref[...] = (dv * u_ref[...].astype(F32) * s * (1.0 + gv * (1.0 - s))).astype(dg_ref.dtype)
        du_ref[...] = (dv * gv * s).astype(du_ref.dtype)

    blk = pl.BlockSpec((tr, tc), lambda i, j: (i, j))
    return pl.pallas_call(body, name=name, grid=(t // tr, c // tc), in_specs=[blk, blk, blk], out_specs=[blk, blk],
                          out_shape=[jax.ShapeDtypeStruct((t, c), BF16)] * 2, compiler_params=_params(2))(gate, up, dact)


def _loss_fwd_bwd(x, g, tgt, name):
    t, c = x.shape
    tr = _rows(t)

    def body(x_ref, g_ref, t_ref, l_ref, dx_ref, dxb_ref, dg_ref):
        xv, gv = x_ref[...], g_ref[...]
        r = lax.rsqrt(jnp.mean(xv * xv, axis=-1, keepdims=True) + RMS_EPS)
        xh = xv * r
        err = xh * gv - t_ref[...]
        per_row = jnp.mean(err * err, axis=-1, keepdims=True)
        dy = err * (1.0 / c)
        dxh = dy * gv
        cm = jnp.mean(dxh * xh, axis=-1, keepdims=True)
        dx = (dxh - xh * cm) * r
        dx_ref[...] = dx
        dxb_ref[...] = dx.astype(BF16)
        first = pl.program_id(0) == 0
        _acc_rows(dg_ref, first, jnp.sum(dy * xh, axis=0, keepdims=True))
        _acc_rows(l_ref, first, jnp.broadcast_to(0.5 * jnp.sum(per_row, axis=0, keepdims=True), (1, LANES)))

    row = pl.BlockSpec((tr, c), lambda i: (i, 0))
    lo, dx, dxb, dg = pl.pallas_call(
        body, name=name, grid=(t // tr,), in_specs=[row, pl.BlockSpec((1, c), lambda i: (0, 0)), row],
        out_specs=[pl.BlockSpec((8, LANES), lambda i: (0, 0)), row, row, pl.BlockSpec((8, c), lambda i: (0, 0))],
        out_shape=[jax.ShapeDtypeStruct((8, LANES), F32), jax.ShapeDtypeStruct((t, c), F32), jax.ShapeDtypeStruct((t, c), BF16),
                   jax.ShapeDtypeStruct((8, c), F32)],
        compiler_params=_params(1))(x, g, tgt)
    return lo[0, 0], dx, dxb, dg[0]


def _conv_specs(t, c):
    tr = _rows(t)
    tc = _tile(c, (1024, 768, 512, 256, 128))
    h8 = tr // 8
    tile = pl.BlockSpec((tr, tc), lambda j, i: (i, j))
    prev = pl.BlockSpec((8, tc), lambda j, i: (jnp.maximum(i * h8 - 1, 0), j))
    nxt = pl.BlockSpec((8, tc), lambda j, i: (jnp.minimum((i + 1) * h8, t // 8 - 1), j))
    return tr, tc, tile, prev, nxt


def _conv_fwd(xbc, w, b, name):
    t, c = xbc.shape
    tr, tc, tile, prev, _ = _conv_specs(t, c)

    def body(x_ref, p_ref, w_ref, b_ref, o_ref, buf):
        i = pl.program_id(1)
        buf[0:8, :] = jnp.where(i > 0, p_ref[...], 0.0)
        buf[8:, :] = x_ref[...]
        pre = b_ref[...]
        for k in range(SSD_CONV):
            pre = pre + w_ref[k:k + 1, :] * buf[pl.ds(8 - (SSD_CONV - 1) + k, tr), :]
        o_ref[...] = pre * _sigmoid(pre)

    return pl.pallas_call(
        body, name=name, grid=(c // tc, t // tr),
        in_specs=[tile, prev, pl.BlockSpec((8, tc), lambda j, i: (0, j)), pl.BlockSpec((1, tc), lambda j, i: (0, j))],
        out_specs=tile, out_shape=jax.ShapeDtypeStruct((t, c), F32),
        scratch_shapes=[pltpu.VMEM((tr + 8, tc), F32)], compiler_params=_params(2))(xbc, xbc, w, b)


def _conv_bwd(xbc, w, b, dout, name):
    t, c = xbc.shape
    tr, tc, tile, prev, nxt = _conv_specs(t, c)
    nt = t // tr
    kc = SSD_CONV

    def body(x_ref, p_ref, n_ref, w_ref, b_ref, d_ref, dn_ref, dx_ref, dw_ref, db_ref, buf, dbuf):
        i = pl.program_id(1)
        last = i == nt - 1
        buf[0:8, :] = jnp.where(i > 0, p_ref[...], 0.0)
        buf[8:8 + tr, :] = x_ref[...]
        buf[8 + tr:, :] = jnp.where(last, 0.0, n_ref[...])
        pre = b_ref[...]
        for k in range(kc):
            pre = pre + w_ref[k:k + 1, :] * buf[pl.ds(8 - (kc - 1) + k, tr + 8), :]
        s = _sigmoid(pre)
        dsilu = s * (1.0 + pre * (1.0 - s))
        dbuf[0:tr, :] = d_ref[...] * dsilu[0:tr, :]
        dbuf[tr:, :] = jnp.where(last, 0.0, dn_ref[...]) * dsilu[tr:, :]
        dpre = dbuf[0:tr, :]
        dx = jnp.zeros((tr, tc), F32)
        first = i == 0
        for k in range(kc):
            dx = dx + w_ref[k:k + 1, :] * dbuf[pl.ds(kc - 1 - k, tr), :]
        dx_ref[...] = dx

        @pl.when(first)
        def _():
            dw_ref[...] = jnp.zeros_like(dw_ref)
        for k in range(kc):
            dw_ref[k:k + 1, :] += jnp.sum(dpre * buf[pl.ds(8 - (kc - 1) + k, tr), :], axis=0, keepdims=True)
        _acc_rows(db_ref, first, jnp.sum(dpre, axis=0, keepdims=True))

    par = pl.BlockSpec((8, tc), lambda j, i: (0, j))
    dx, dw, db = pl.pallas_call(
        body, name=name, grid=(c // tc, nt),
        in_specs=[tile, prev, nxt, par, pl.BlockSpec((1, tc), lambda j, i: (0, j)), tile, nxt],
        out_specs=[tile, par, par],
        out_shape=[jax.ShapeDtypeStruct((t, c), F32), jax.ShapeDtypeStruct((8, c), F32), jax.ShapeDtypeStruct((8, c), F32)],
        scratch_shapes=[pltpu.VMEM((tr + 16, tc), F32), pltpu.VMEM((tr + 8, tc), F32)],
        compiler_params=_params(2))(xbc, xbc, xbc, w, b, dout, dout)
    return dx, dw[:kc], db[0]


def _ssd_consts():
    h, p, ln = SSD_HEADS, SSD_HEAD_DIM, SSD_CHUNK
    w = h * p
    hrow, jcol = _iota((LANES, w), 0), _iota((LANES, w), 1)
    e = ((jcol >= hrow * p) & (jcol < (hrow + 1) * p)).astype(BF16)
    jrow, hcol = _iota((w, LANES), 0), _iota((w, LANES), 1)
    et = ((jrow >= hcol * p) & (jrow < (hcol + 1) * p)).astype(BF16)
    row, col = _iota((ln, ln), 0), _iota((ln, ln), 1)
    return e, et, row, col


def _ssd_common(dtraw_ref, dtb_ref, alog_ref, e):
    ln = SSD_CHUNK
    raw = dtraw_ref[...] + dtb_ref[...]
    dt = _softplus(raw)
    a = -jnp.exp(alog_ref[...])
    adt = dt * a
    row, col = _iota((ln, ln), 0), _iota((ln, ln), 1)
    cs = _xdot_l((col <= row).astype(BF16), adt)
    cl = jnp.sum(adt, axis=0, keepdims=True)
    ex = _xdot(jnp.concatenate([dt, cs, jnp.broadcast_to(cl, (ln, LANES))], axis=0), e)
    return raw, dt, a, cs, ex[:ln], ex[ln:2 * ln], ex[2 * ln:]


def _head_decay(cs, h, causal):
    ln = SSD_CHUNK
    lane = _iota((1, LANES), 1)
    colv = jnp.sum(jnp.where(lane == h, cs, 0.0), axis=1, keepdims=True)
    cb = jnp.broadcast_to(colv, (ln, ln))
    return jnp.exp(jnp.where(causal, cb - cb.T, NEG))


def _ssd_fwd(dtraw, xc, dtb, alog, dskip_x, name):
    t = dtraw.shape[0]
    h, p, g, n, ln = SSD_HEADS, SSD_HEAD_DIM, SSD_GROUPS, SSD_STATE, SSD_CHUNK
    w, gn, gw, hpg = h * p, g * n, (h // g) * p, h // g
    assert n == ln and gw % LANES == 0 and w % gn == 0
    nc = t // ln

    def body(dtraw_ref, x_ref, b_ref, c_ref, dtb_ref, alog_ref, dsk_ref, y_ref, sp_ref, s_ref):
        @pl.when(pl.program_id(0) == 0)
        def _():
            s_ref[...] = jnp.zeros_like(s_ref)
        e, _, row, col = _ssd_consts()
        causal = col <= row
        _, _, _, cs, dt_x, cs_x, cl_x = _ssd_common(dtraw_ref, dtb_ref, alog_ref, e)
        xv = x_ref[...]
        xd = xv * dt_x
        xdb = xd.astype(BF16)
        sv = s_ref[...]
        sp_ref[0] = sv
        el_x = jnp.exp(cs_x)
        zb = (xd * jnp.exp(cl_x - cs_x)).astype(BF16)
        cd_x = jnp.exp(cl_x)
        dsk = dsk_ref[...]
        half = _iota((1, LANES), 1) >= p
        for gi in range(g):
            gs = slice(gi * gw, (gi + 1) * gw)
            bg = b_ref[:, gi * n:(gi + 1) * n].astype(BF16)
            cg = c_ref[:, gi * n:(gi + 1) * n].astype(BF16)
            gm = _dot(cg, bg, "nt")
            sg = sv[:, gs]
            yoff = _dot(cg, sg.astype(BF16)) * el_x[:, gs]
            s_ref[:, gs] = sg * cd_x[:, gs] + _dot(bg, zb[:, gs], "tn")
            for pp in range(gw // LANES):
                ls = slice(gi * gw + pp * LANES, gi * gw + (pp + 1) * LANES)
                xp = xdb[:, ls]
                yp = yoff[:, pp * LANES:(pp + 1) * LANES] + dsk[:, ls] * xv[:, ls]
                for hh in range(LANES // p):
                    hd = gi * hpg + pp * (LANES // p) + hh
                    wm = (gm * _head_decay(cs, hd, causal)).astype(BF16)
                    yp = yp + _dot(wm, jnp.where(half == (hh == 1), xp, jnp.zeros_like(xp)))
                y_ref[:, ls] = yp

    nar = pl.BlockSpec((ln, LANES), lambda c: (c, 0))
    one = pl.BlockSpec((1, LANES), lambda c: (0, 0))
    return pl.pallas_call(
        body, name=name, grid=(nc,),
        in_specs=[nar, pl.BlockSpec((ln, w), lambda c: (c, 0)), pl.BlockSpec((ln, gn), lambda c: (c, w // gn)),
                  pl.BlockSpec((ln, gn), lambda c: (c, w // gn + 1)), one, one, pl.BlockSpec((1, w), lambda c: (0, 0))],
        out_specs=[pl.BlockSpec((ln, w), lambda c: (c, 0)), pl.BlockSpec((1, n, w), lambda c: (c, 0, 0))],
        out_shape=[jax.ShapeDtypeStruct((t, w), F32), jax.ShapeDtypeStruct((nc, n, w), F32)],
        scratch_shapes=[pltpu.VMEM((n, w), F32)], compiler_params=_params(1))(dtraw, xc, xc, xc, dtb, alog, dskip_x)


def _ssd_bwd(dtraw, xc, dtb, alog, dskip_x, sprev, dy, name):
    t = dtraw.shape[0]
    h, p, g, n, ln = SSD_HEADS, SSD_HEAD_DIM, SSD_GROUPS, SSD_STATE, SSD_CHUNK
    w, gn, gw, hpg = h * p, g * n, (h // g) * p, h // g
    nc = t // ln

    def body(dtraw_ref, x_ref, b_ref, c_ref, dtb_ref, alog_ref, dsk_ref, sp_ref, dy_ref,
             dxc_ref, ddt_ref, dbias_ref, dalog_ref, ddsk_ref, ds_ref, dxd_ref, qcs_ref):
        first = pl.program_id(0) == 0

        @pl.when(first)
        def _():
            ds_ref[...] = jnp.zeros_like(ds_ref)
        e, et, row, col = _ssd_consts()
        causal = col <= row
        raw, dt, a, cs, dt_x, cs_x, cl_x = _ssd_common(dtraw_ref, dtb_ref, alog_ref, e)
        xv = x_ref[...]
        xd = xv * dt_x
        xdb = xd.astype(BF16)
        sv = sp_ref[0]
        dyv = dy_ref[...]
        dyb = dyv.astype(BF16)
        dsn = ds_ref[...]
        el_x = jnp.exp(cs_x)
        dte_x = jnp.exp(cl_x - cs_x)
        cd_x = jnp.exp(cl_x)
        zf = xd * dte_x
        lane = _iota((1, LANES), 1)
        half = lane >= p
        lastrow = _iota((ln, 1), 0) == ln - 1
        dcs = jnp.zeros((ln, LANES), F32)
        for gi in range(g):
            gs = slice(gi * gw, (gi + 1) * gw)
            ns = slice(gi * n, (gi + 1) * n)
            bg = b_ref[:, ns].astype(BF16)
            cg = c_ref[:, ns].astype(BF16)
            gm = _dot(cg, bg, "nt")
            sgb = sv[:, gs].astype(BF16)
            dsg = dsn[:, gs]
            dsgb = dsg.astype(BF16)
            yoff = _dot(cg, sgb) * el_x[:, gs]
            drb = (el_x[:, gs] * dyv[:, gs]).astype(BF16)
            dcg = _dot(drb, sgb, "nt")
            ds_ref[:, gs] = cd_x[:, gs] * dsg + _dot(cg, drb, "tn")
            dz = _dot(bg, dsgb)
            zg = zf[:, gs]
            dbg = _dot(zg.astype(BF16), dsgb, "nt")
            dzz = dz * zg
            qcl = jnp.sum(dzz + cd_x[:, gs] * dsg * sv[:, gs], axis=0, keepdims=True)
            qcs_ref[:, gs] = dyv[:, gs] * yoff - dzz + jnp.where(lastrow, jnp.broadcast_to(qcl, (ln, gw)), 0.0)
            dgm = jnp.zeros((ln, ln), F32)
            for pp in range(gw // LANES):
                ls = slice(gi * gw + pp * LANES, gi * gw + (pp + 1) * LANES)
                xp = xdb[:, ls]
                dxp = dz[:, pp * LANES:(pp + 1) * LANES] * dte_x[:, ls]
                for hh in range(LANES // p):
                    hd = gi * hpg + pp * (LANES // p) + hh
                    dm = _head_decay(cs, hd, causal)
                    wf = gm * dm
                    dym = jnp.where(half == (hh == 1), dyb[:, ls], jnp.zeros_like(xp))
                    dw = _dot(dym, xp, "nt")
                    dxp = dxp + _dot(wf.astype(BF16), dym, "tn")
                    dgm = dgm + dw * dm
                    mm = dw * wf
                    rc = jnp.sum(mm, axis=1, keepdims=True) - jnp.sum(mm.T, axis=1, keepdims=True)
                    dcs = dcs + rc * (lane == hd).astype(F32)
                dxd_ref[:, ls] = dxp
            dgb = dgm.astype(BF16)
            dxc_ref[:, w + gn + gi * n:w + gn + (gi + 1) * n] = dcg + _dot(dgb, bg)
            dxc_ref[:, w + gi * n:w + (gi + 1) * n] = dbg + _dot(dgb, cg, "tn")
        dxd = dxd_ref[...]
        dxc_ref[:, :w] = dxd * dt_x + dsk_ref[...] * dyv
        red = _xdot(jnp.concatenate([qcs_ref[...], dxd * xv, dyv * xv], axis=0), et)
        dcs = dcs + red[:ln]
        dadt = _xdot_l((row <= col).astype(BF16), dcs)
        ddt = red[ln:2 * ln] + dadt * a
        draw = ddt * _sigmoid(raw)
        ddt_ref[...] = draw
        _acc_rows(dbias_ref, first, jnp.sum(draw, axis=0, keepdims=True))
        _acc_rows(dalog_ref, first, jnp.sum(dadt * dt, axis=0, keepdims=True) * a)
        _acc_rows(ddsk_ref, first, jnp.sum(red[2 * ln:], axis=0, keepdims=True))

    rev = lambda c: nc - 1 - c
    nar = pl.BlockSpec((ln, LANES), lambda c: (rev(c), 0))
    one = pl.BlockSpec((1, LANES), lambda c: (0, 0))
    wide = pl.BlockSpec((ln, w), lambda c: (rev(c), 0))
    acc = pl.BlockSpec((8, LANES), lambda c: (0, 0))
    outs = pl.pallas_call(
        body, name=name, grid=(nc,),
        in_specs=[nar, wide, pl.BlockSpec((ln, gn), lambda c: (rev(c), w // gn)), pl.BlockSpec((ln, gn), lambda c: (rev(c), w // gn + 1)),
                  one, one, pl.BlockSpec((1, w), lambda c: (0, 0)), pl.BlockSpec((1, n, w), lambda c: (rev(c), 0, 0)), wide],
        out_specs=[pl.BlockSpec((ln, w + 2 * gn), lambda c: (rev(c), 0)), nar, acc, acc, acc],
        out_shape=[jax.ShapeDtypeStruct((t, w + 2 * gn), F32), jax.ShapeDtypeStruct((t, LANES), F32)]
        + [jax.ShapeDtypeStruct((8, LANES), F32)] * 3,
        scratch_shapes=[pltpu.VMEM((n, w), F32), pltpu.VMEM((ln, w), F32), pltpu.VMEM((ln, w), F32)],
        compiler_params=_params(1))(dtraw, xc, xc, xc, dtb, alog, dskip_x, sprev, dy)
    dxc, ddt, dbias, dalog, ddsk = outs
    return dxc, ddt, dbias[0], dalog[0], ddsk[0]


def _rope_tables(t):
    half = MLA_ROPE // 2
    inv_freq = ROPE_THETA ** (-jnp.arange(half, dtype=F32) / half)
    ang = jnp.arange(t, dtype=F32)[:, None] * inv_freq[None, :]
    cos, sin = jnp.cos(ang), jnp.sin(ang)
    pad = LANES - MLA_ROPE
    cos_r = jnp.concatenate([cos, cos, jnp.ones((t, pad), F32)], axis=1)
    sin_r = jnp.concatenate([sin, sin, jnp.zeros((t, pad), F32)], axis=1)
    return cos_r, sin_r


def _rot_matrix():
    half = MLA_ROPE // 2
    i, j = _iota((LANES, LANES), 0), _iota((LANES, LANES), 1)
    neg = (j < half) & (i == j + half)
    pos = (j >= half) & (j < 2 * half) & (i == j - half)
    return (pos.astype(F32) - neg.astype(F32)).astype(BF16)


def _rope(x, cos, sin, every, transpose, out_dtype, name):
    t, w = x.shape
    tr = _rows(t)

    def body(x_ref, c_ref, s_ref, o_ref):
        rot = _rot_matrix()
        cv, sv = c_ref[...], s_ref[...]
        for j in range(w // LANES):
            ls = slice(j * LANES, (j + 1) * LANES)
            xv = x_ref[:, ls]
            if j % every != every - 1:
                o_ref[:, ls] = xv.astype(o_ref.dtype)
            elif transpose:
                o_ref[:, ls] = (xv * cv - _xdot(xv * sv, rot, 2)).astype(o_ref.dtype)
            else:
                o_ref[:, ls] = (xv * cv + _xdot(xv, rot, 2) * sv).astype(o_ref.dtype)

    wide = pl.BlockSpec((tr, w), lambda i: (i, 0))
    tab = pl.BlockSpec((tr, LANES), lambda i: (i, 0))
    return pl.pallas_call(
        body, name=name, grid=(t // tr,), in_specs=[wide, tab, tab], out_specs=wide,
        out_shape=jax.ShapeDtypeStruct((t, w), out_dtype), compiler_params=_params(1))(x, cos, sin)


def _att_masks(blk):
    return _iota((blk, blk), 0), _iota((blk, blk), 1)


def _lanes(j):
    return slice(j * LANES, (j + 1) * LANES)


def _mla_fwd(q, kr, kv, name):
    t = q.shape[0]
    nh, blk = MLA_HEADS, min(ATT_BLK, t)
    scale = (MLA_NOPE + MLA_ROPE) ** -0.5
    hps = math.gcd(nh, MLA_FWD_HEADS)

    def body(q_ref, kv_ref, kr_ref, o_ref, lse_ref):
        i = pl.program_id(1)
        row, col = _att_masks(blk)
        qs = [q_ref[:, 2 * hh * LANES:(2 * hh + 2) * LANES] for hh in range(hps)]

        def scores(kb, hh):
            ks = pl.ds(pl.multiple_of(kb * blk, blk), blk)
            kfull = jnp.concatenate([kv_ref[ks, _lanes(2 * hh)], kr_ref[ks, :]], axis=1)
            return _dot(qs[hh], kfull, "nt") * scale, kv_ref[ks, _lanes(2 * hh + 1)]
        init = []
        for hh in range(hps):
            s, v = scores(i, hh)
            s = jnp.where(col <= row, s, NEG)
            m = jnp.max(s, axis=1, keepdims=True)
            pr = jnp.exp(s - m)
            init += [m, jnp.sum(pr, axis=1, keepdims=True), _dot(pr.astype(BF16), v)]

        def step(kb, carry):
            sv = [scores(kb, hh) for hh in range(hps)]
            out, prs = [], []
            for hh in range(hps):
                m, l, acc = carry[3 * hh:3 * hh + 3]
                s = sv[hh][0]
                m2 = jnp.maximum(m, jnp.max(s, axis=1, keepdims=True))
                al = jnp.exp(m - m2)
                pr = jnp.exp(s - m2)
                prs.append(pr.astype(BF16))
                out += [m2, al * l + jnp.sum(pr, axis=1, keepdims=True), al * acc]
            for hh in range(hps):
                out[3 * hh + 2] = out[3 * hh + 2] + _dot(prs[hh], sv[hh][1])
            return tuple(out)
        res = lax.fori_loop(0, i, step, tuple(init))
        for hh in range(hps):
            m, l, acc = res[3 * hh:3 * hh + 3]
            o_ref[:, _lanes(hh)] = acc / l
            lse_ref[hh] = m + jnp.log(l)

    return pl.pallas_call(
        body, name=name, grid=(nh // hps, t // blk),
        in_specs=[pl.BlockSpec((blk, 2 * hps * LANES), lambda h, i: (i, h)),
                  pl.BlockSpec((t, 2 * hps * LANES), lambda h, i: (0, h), pipeline_mode=pl.Buffered(1)),
                  pl.BlockSpec((t, LANES), lambda h, i: (0, 0), pipeline_mode=pl.Buffered(1))],
        out_specs=[pl.BlockSpec((blk, hps * LANES), lambda h, i: (i, h)), pl.BlockSpec((hps, blk, 1), lambda h, i: (h, i, 0))],
        out_shape=[jax.ShapeDtypeStruct((t, nh * LANES), F32), jax.ShapeDtypeStruct((nh, t, 1), F32)],
        compiler_params=_params(2))(q, kv, kr)


def _mla_bwd(q, kr, kv, o, do, lse, name):
    t = q.shape[0]
    nh, blk = MLA_HEADS, min(ATT_BLK, t)
    hps = math.gcd(nh, MLA_BWD_HEADS)
    scale = (MLA_NOPE + MLA_ROPE) ** -0.5

    def body(q_ref, kv_ref, kr_ref, o_ref, do_ref, lse_ref, dq_ref, dkv_ref, dkr_ref):
        hp, i = pl.program_id(0), pl.program_id(1)

        @pl.when(i == 0)
        def _():
            dkv_ref[...] = jnp.zeros_like(dkv_ref)

        @pl.when((i == 0) & (hp == 0))
        def _():
            dkr_ref[...] = jnp.zeros_like(dkr_ref)
        row, col = _att_masks(blk)
        qs = [q_ref[:, 2 * hh * LANES:(2 * hh + 2) * LANES] for hh in range(hps)]
        dobs = [do_ref[:, _lanes(hh)].astype(BF16) for hh in range(hps)]
        deltas = [jnp.sum(do_ref[:, _lanes(hh)] * o_ref[:, _lanes(hh)], axis=1, keepdims=True) for hh in range(hps)]
        lses = [lse_ref[hh] for hh in range(hps)]

        def tile(kb, carry, masked):
            ks = pl.ds(pl.multiple_of(kb * blk, blk), blk)
            krv = kr_ref[ks, :]
            hs = range(hps)
            kfull = [jnp.concatenate([kv_ref[ks, _lanes(2 * hh)], krv], axis=1) for hh in hs]
            ss = [_dot(qs[hh], kfull[hh], "nt") for hh in hs]
            dps = [_dot(dobs[hh], kv_ref[ks, _lanes(2 * hh + 1)], "nt") for hh in hs]
            prb, dsb = [], []
            for hh in hs:
                pr = jnp.exp(ss[hh] * scale - lses[hh])
                if masked:
                    pr = jnp.where(col <= row, pr, 0.0)
                prb.append(pr.astype(BF16))
                dsb.append((pr * (dps[hh] - deltas[hh]) * scale).astype(BF16))
            out, dkr = [], None
            for hh in hs:
                dkv_ref[ks, _lanes(2 * hh + 1)] += _dot(prb[hh], dobs[hh], "tn")
                dk = _dot(dsb[hh], qs[hh], "tn")
                dkv_ref[ks, _lanes(2 * hh)] += dk[:, :LANES]
                dkr = dk[:, LANES:] if dkr is None else dkr + dk[:, LANES:]
                out.append(carry[hh] + _dot(dsb[hh], kfull[hh]))
            dkr_ref[ks, :] += dkr
            return tuple(out)
        zero = jnp.zeros((blk, 2 * LANES), F32)
        carry = lax.fori_loop(0, i, lambda kb, c: tile(kb, c, False), (zero,) * hps)
        res = tile(i, carry, True)
        for hh in range(hps):
            dq_ref[:, 2 * hh * LANES:(2 * hh + 2) * LANES] = res[hh]

    qb = lambda w: pl.BlockSpec((blk, w * LANES), lambda h, i: (i, h))
    seq = lambda w, f: pl.BlockSpec((t, w * LANES), f, pipeline_mode=pl.Buffered(1))
    wide = jax.ShapeDtypeStruct((t, nh * 2 * LANES), F32)
    return pl.pallas_call(
        body, name=name, grid=(nh // hps, t // blk),
        in_specs=[qb(2 * hps), seq(2 * hps, lambda h, i: (0, h)), seq(1, lambda h, i: (0, 0)),
                  qb(hps), qb(hps), pl.BlockSpec((hps, blk, 1), lambda h, i: (h, i, 0))],
        out_specs=[qb(2 * hps), seq(2 * hps, lambda h, i: (0, h)), seq(1, lambda h, i: (0, 0))],
        out_shape=[wide, wide, jax.ShapeDtypeStruct((t, LANES), F32)],
        compiler_params=_params(2))(q, kv, kr, o, do, lse)


def _sb_fwd(qkv, name):
    t = qkv.shape[0]
    nh, blk = SB_HEADS, min(ATT_BLK, t)
    hps = math.gcd(nh, SB_FWD_HEADS)
    nq = t // blk
    scale = SB_HEAD_DIM ** -0.5

    def body(q_ref, k_ref, v_ref, o_ref, lt_ref):
        i = pl.program_id(1)
        row, col = _att_masks(blk)
        usuf = (row > col).astype(BF16)
        qs = [q_ref[:, _lanes(hh)] for hh in range(hps)]

        def tile(kb, carry, masked):
            ks = pl.ds(pl.multiple_of(kb * blk, blk), blk)
            hs = range(hps)
            zs = [_dot(qs[hh], k_ref[ks, _lanes(hh)], "nt") for hh in hs]
            lks, lss = [], []
            for hh in hs:
                z = zs[hh] * scale
                lk = -_softplus(z)
                lss.append(lk + z)
                lks.append(jnp.where(col < row, lk, 0.0) if masked else lk)
            pieces = [_split(lks[hh], 2) for hh in hs]
            later = [_dot(pieces[hh][0], usuf) + _dot(pieces[hh][1], usuf) for hh in hs]
            out = []
            for hh in hs:
                wt = jnp.exp(lss[hh] + later[hh] + carry[2 * hh + 1])
                if masked:
                    wt = jnp.where(col < row, wt, 0.0)
                out += [wt.astype(BF16), carry[2 * hh + 1] + jnp.sum(lks[hh], axis=1, keepdims=True)]
            for hh in hs:
                out[2 * hh] = carry[2 * hh] + _dot(out[2 * hh], v_ref[ks, _lanes(hh)])
            return tuple(out)
        za, zr = jnp.zeros((blk, LANES), F32), jnp.zeros((blk, 1), F32)
        carry = tile(i, (za, zr) * hps, True)
        carry = lax.fori_loop(0, i, lambda j, c: tile(i - 1 - j, c, False), carry)
        for hh in range(hps):
            o_ref[:, _lanes(hh)] = carry[2 * hh]
            lt_ref[hh] = carry[2 * hh + 1]

    ng = nh // hps
    full = lambda f: pl.BlockSpec((t, hps * LANES), f)
    return pl.pallas_call(
        body, name=name, grid=(ng, nq),
        in_specs=[pl.BlockSpec((blk, hps * LANES), lambda h, i: (i, h)), full(lambda h, i: (0, ng + h)), full(lambda h, i: (0, 2 * ng + h))],
        out_specs=[pl.BlockSpec((blk, hps * LANES), lambda h, i: (i, h)), pl.BlockSpec((hps, blk, 1), lambda h, i: (h, i, 0))],
        out_shape=[jax.ShapeDtypeStruct((t, nh * LANES), F32), jax.ShapeDtypeStruct((nh, t, 1), F32)],
        compiler_params=_params(2))(qkv, qkv, qkv)


def _sb_bwd(qkv, do, ltot, name):
    t = qkv.shape[0]
    nh, blk = SB_HEADS, min(ATT_BLK, t)
    hps = math.gcd(nh, SB_BWD_HEADS)
    nq = t // blk
    scale = SB_HEAD_DIM ** -0.5

    def body(q_ref, k_ref, v_ref, do_ref, lt_ref, dq_ref, dk_ref, dv_ref):
        i = pl.program_id(1)

        @pl.when(i == 0)
        def _():
            dk_ref[...] = jnp.zeros_like(dk_ref)
            dv_ref[...] = jnp.zeros_like(dv_ref)
        row, col = _att_masks(blk)
        uinc = (row <= col).astype(BF16)
        uexc = (row < col).astype(BF16)
        qs = [q_ref[:, _lanes(hh)] for hh in range(hps)]
        dobs = [do_ref[:, _lanes(hh)].astype(BF16) for hh in range(hps)]
        lts = [lt_ref[hh] for hh in range(hps)]

        def tile(kb, carry, masked):
            ks = pl.ds(pl.multiple_of(kb * blk, blk), blk)
            hs = range(hps)
            kvs = [k_ref[ks, _lanes(hh)] for hh in hs]
            zs = [_dot(qs[hh], kvs[hh], "nt") for hh in hs]
            dws = [_dot(dobs[hh], v_ref[ks, _lanes(hh)], "nt") for hh in hs]
            lks, lss = [], []
            for hh in hs:
                z = zs[hh] * scale
                lk = -_softplus(z)
                lss.append(lk + z)
                lks.append(jnp.where(col < row, lk, 0.0) if masked else lk)
            pieces = [_split(lks[hh], 2) for hh in hs]
            css = [_dot(pieces[hh][0], uinc) + _dot(pieces[hh][1], uinc) for hh in hs]
            wts, evs = [], []
            for hh in hs:
                wt = jnp.exp(lss[hh] + (lts[hh] - (carry[3 * hh + 1] + css[hh])))
                if masked:
                    wt = jnp.where(col < row, wt, 0.0)
                wts.append(wt.astype(BF16))
                evs.append(dws[hh] * wt)
            epieces = [_split(evs[hh], 2) for hh in hs]
            ecss = [_dot(epieces[hh][0], uexc) + _dot(epieces[hh][1], uexc) for hh in hs]
            for hh in hs:
                dv_ref[ks, _lanes(hh)] += _dot(wts[hh], dobs[hh], "tn")
            dzbs = []
            for hh in hs:
                sig = jnp.exp(lss[hh])
                dz = evs[hh] * (1.0 - sig) - (ecss[hh] + carry[3 * hh + 2]) * sig
                if masked:
                    dz = jnp.where(col < row, dz, 0.0)
                dzbs.append((dz * scale).astype(BF16))
            out = []
            for hh in hs:
                dk_ref[ks, _lanes(hh)] += _dot(dzbs[hh], qs[hh], "tn")
                out += [carry[3 * hh] + _dot(dzbs[hh], kvs[hh]), carry[3 * hh + 1] + jnp.sum(lks[hh], axis=1, keepdims=True),
                        carry[3 * hh + 2] + jnp.sum(evs[hh], axis=1, keepdims=True)]
            return tuple(out)
        za, z1 = jnp.zeros((blk, LANES), F32), jnp.zeros((blk, 1), F32)
        carry = lax.fori_loop(0, i, lambda kb, c: tile(kb, c, False), (za, z1, z1) * hps)
        res = tile(i, carry, True)
        for hh in range(hps):
            dq_ref[:, _lanes(hh)] = res[3 * hh]

    ng = nh // hps
    full = lambda f: pl.BlockSpec((t, hps * LANES), f, pipeline_mode=pl.Buffered(1))
    qb = pl.BlockSpec((blk, hps * LANES), lambda h, i: (i, h))
    wide = jax.ShapeDtypeStruct((t, nh * LANES), F32)
    return pl.pallas_call(
        body, name=name, grid=(ng, nq),
        in_specs=[qb, full(lambda h, i: (0, ng + h)), full(lambda h, i: (0, 2 * ng + h)), qb,
                  pl.BlockSpec((hps, blk, 1), lambda h, i: (h, i, 0))],
        out_specs=[qb, full(lambda h, i: (0, h)), full(lambda h, i: (0, h))],
        out_shape=[wide, wide, wide], compiler_params=_params(2))(qkv, qkv, qkv, do, ltot)


def _as2d(a):
    return a.reshape((-1, a.shape[-1]))


def _sum4(parts, name):
    _, r, c = parts.shape
    tr = _tile(r, (256, 128, 64, 32, 16, 8))

    def body(p_ref, o_ref):
        acc = p_ref[0].astype(F32)
        for j in range(1, 4):
            acc = acc + p_ref[j].astype(F32)
        o_ref[...] = acc

    return pl.pallas_call(body, name=name, grid=(r // tr,), in_specs=[pl.BlockSpec((4, tr, c), lambda i: (0, i, 0))],
                          out_specs=pl.BlockSpec((tr, c), lambda i: (i, 0)), out_shape=jax.ShapeDtypeStruct((r, c), F32),
                          compiler_params=_params(1))(parts)


def _adamw(w, gas, gbs, m, v, name):
    r, c = w.shape
    npc = len(gas)
    tr = _tile(r // npc, (128, 64, 32, 16, 8))
    nb = r // npc // tr
    c1, c2 = 1.0 - ADAM_B1 ** ADAM_STEP, 1.0 - ADAM_B2 ** ADAM_STEP
    two = gbs is not None
    ng = npc * (2 if two else 1)

    def body(*refs):
        w_ref, g_refs = refs[0], refs[1:1 + ng]
        m_ref, v_ref, g_out, d_out, m_out, v_out = refs[1 + ng:]
        pieces = [g_refs[p][...] + g_refs[npc + p][...] if two else g_refs[p][...] for p in range(npc)]
        gv = pieces[0]
        for p in range(1, npc):
            gv = jnp.where(pl.program_id(0) >= p * nb, pieces[p], gv)
        mn = ADAM_B1 * m_ref[...] + (1.0 - ADAM_B1) * gv
        vn = ADAM_B2 * v_ref[...] + (1.0 - ADAM_B2) * (gv * gv)
        g_out[...] = gv
        m_out[...] = mn
        v_out[...] = vn
        d_out[...] = -ADAM_LR * ((mn / c1) / (jnp.sqrt(vn / c2) + ADAM_EPS) + ADAM_WD * w_ref[...])

    blk = pl.BlockSpec((tr, c), lambda i: (i, 0))
    piece = lambda p: pl.BlockSpec((tr, c), lambda i: (jnp.clip(i - p * nb, 0, nb - 1), 0))
    gspecs = [piece(p) for p in range(npc)] * (2 if two else 1)
    ins = [w] + list(gas) + (list(gbs) if two else []) + [m, v]
    return pl.pallas_call(body, name=name, grid=(r // tr,), in_specs=[blk] + gspecs + [blk, blk], out_specs=[blk] * 4,
                          out_shape=[jax.ShapeDtypeStruct((r, c), F32)] * 4, compiler_params=_params(1))(*ins)


HBM_SPEC = pl.BlockSpec(memory_space=pltpu.HBM)
SEM_SPEC = pl.BlockSpec(memory_space=pltpu.SEMAPHORE)
EFFECT = pltpu.SideEffectType.DATAFLOW_SIDE_EFFECTING


def _chip_copies(ins, lands, send_sems, recv_sems, whole):
    x, y, c = lax.axis_index("x"), lax.axis_index("y"), lax.axis_index("c")
    me = 2 * x + y
    out = []
    for wi in range(len(ins)):
        for k, (px, py) in enumerate([(1 - x, y), (x, 1 - y), (1 - x, 1 - y)]):
            sems = dict(send_sem=send_sems[wi * 3 + k], recv_sem=recv_sems[wi * 3 + k], device_id=(px, py, c), device_id_type=MESH_T)
            peer = 2 * px + py
            sent = pltpu.make_async_remote_copy(src_ref=ins[wi] if whole else ins[wi].at[peer], dst_ref=lands[wi].at[me], **sems)
            got = functools.partial(pltpu.make_async_remote_copy, src_ref=ins[wi] if whole else ins[wi].at[me],
                                    dst_ref=lands[wi].at[peer], **sems)
            out.append((sent, got))
    return out


def _xstart(arrs, whole, after, name):
    n, na = len(arrs), len(after)
    me = 2 * lax.axis_index("x") + lax.axis_index("y")
    lands = []
    for a in arrs:
        own = a[None] if whole else lax.dynamic_slice_in_dim(a, me, 1, axis=0)
        empty = lax.empty(((4,) + a.shape) if whole else a.shape, a.dtype)
        lands.append(lax.dynamic_update_slice_in_dim(empty, own, me, axis=0))

    def body(*refs):
        ins, lands_in = refs[:n], refs[n:2 * n]
        outs = refs[2 * n + na:]
        for sent, _ in _chip_copies(ins, lands_in, outs[:3 * n], outs[3 * n:6 * n], whole):
            sent.start()
        outs[8 * n][...] = jnp.zeros((8, LANES), F32)

    hbm = lambda a: pltpu.HBM(a.shape, a.dtype)
    res = pl.pallas_call(
        body, name=name,
        out_shape=[pltpu.SemaphoreType.DMA(())] * (6 * n) + [hbm(a) for a in arrs] + [hbm(a) for a in lands]
        + [jax.ShapeDtypeStruct((8, LANES), F32)],
        in_specs=[HBM_SPEC] * (2 * n) + [pl.BlockSpec(memory_space=pl.ANY)] * na,
        out_specs=[SEM_SPEC] * (6 * n) + [HBM_SPEC] * (2 * n) + [pl.BlockSpec(memory_space=pltpu.VMEM)],
        input_output_aliases={i: 6 * n + i for i in range(2 * n)},
        compiler_params=pltpu.CompilerParams(has_side_effects=EFFECT),
    )(*[pltpu.with_memory_space_constraint(a, pltpu.HBM) for a in list(arrs) + lands], *after)
    return res


def _xwait(handle, whole, after, name):
    n = (len(handle) - 1) // 8
    sems, thru = handle[:6 * n], handle[6 * n:8 * n]

    def body(*refs):
        ins, lands_in = refs[:n], refs[n:2 * n]
        for sent, got in _chip_copies(ins, lands_in, refs[2 * n:5 * n], refs[5 * n:8 * n], whole):
            sent.wait_send()
            got().wait_recv()

    hbm = lambda a: pltpu.HBM(a.shape, a.dtype)
    res = pl.pallas_call(
        body, name=name, out_shape=[hbm(a) for a in thru],
        in_specs=[HBM_SPEC] * (2 * n) + [SEM_SPEC] * (6 * n) + [pl.BlockSpec(memory_space=pl.ANY)],
        out_specs=[HBM_SPEC] * (2 * n), input_output_aliases={i: i for i in range(2 * n)},
        compiler_params=pltpu.CompilerParams(has_side_effects=EFFECT),
    )(*thru, *sems, after)
    return res[n:]


def _sibling_copies(ins, lands, send_sems, recv_sems):
    sib = (lax.axis_index("x"), lax.axis_index("y"), 1 - lax.axis_index("c"))
    return [pltpu.make_async_remote_copy(src_ref=ins[wi], dst_ref=lands[wi], send_sem=send_sems[wi], recv_sem=recv_sems[wi],
                                         device_id=sib, device_id_type=MESH_T) for wi in range(len(ins))]


def _sib_start(arrs, after, name):
    n, na = len(arrs), len(after)
    lands = [lax.empty(a.shape, a.dtype) for a in arrs]

    def body(*refs):
        outs = refs[2 * n + na:]
        for cp in _sibling_copies(refs[:n], refs[n:2 * n], outs[:n], outs[n:2 * n]):
            cp.start()
        outs[4 * n][...] = jnp.zeros((8, LANES), F32)

    hbm = lambda a: pltpu.HBM(a.shape, a.dtype)
    return pl.pallas_call(
        body, name=name,
        out_shape=[pltpu.SemaphoreType.DMA(())] * (2 * n) + [hbm(a) for a in arrs] * 2 + [jax.ShapeDtypeStruct((8, LANES), F32)],
        in_specs=[HBM_SPEC] * (2 * n) + [pl.BlockSpec(memory_space=pl.ANY)] * na,
        out_specs=[SEM_SPEC] * (2 * n) + [HBM_SPEC] * (2 * n) + [pl.BlockSpec(memory_space=pltpu.VMEM)],
        input_output_aliases={i: 2 * n + i for i in range(2 * n)},
        compiler_params=pltpu.CompilerParams(has_side_effects=EFFECT),
    )(*[pltpu.with_memory_space_constraint(a, pltpu.HBM) for a in list(arrs) + lands], *after)


def _sib_wait(handle, after, name):
    n = (len(handle) - 1) // 4
    sems, thru = handle[:2 * n], handle[2 * n:4 * n]

    def body(*refs):
        for cp in _sibling_copies(refs[:n], refs[n:2 * n], refs[2 * n:3 * n], refs[3 * n:4 * n]):
            cp.wait_send()
            cp.wait_recv()

    res = pl.pallas_call(
        body, name=name, out_shape=[pltpu.HBM(a.shape, a.dtype) for a in thru],
        in_specs=[HBM_SPEC] * (2 * n) + [SEM_SPEC] * (2 * n) + [pl.BlockSpec(memory_space=pl.ANY)],
        out_specs=[HBM_SPEC] * (2 * n), input_output_aliases={i: i for i in range(2 * n)},
        compiler_params=pltpu.CompilerParams(has_side_effects=EFFECT),
    )(*thru, *sems, after)
    return res[n:]


def _sibling_exchange(arrs, name):
    n = len(arrs)

    def body(*refs):
        ins, outs = refs[:n], refs[n:2 * n]
        send_sems, recv_sems = refs[2 * n:]
        sib = (lax.axis_index("x"), lax.axis_index("y"), 1 - lax.axis_index("c"))
        cps = [pltpu.make_async_remote_copy(src_ref=ins[wi], dst_ref=outs[wi], send_sem=send_sems.at[wi], recv_sem=recv_sems.at[wi],
                                            device_id=sib, device_id_type=MESH_T) for wi in range(n)]
        for cp in cps:
            cp.start()
        for cp in cps:
            cp.wait_recv()
        for cp in cps:
            cp.wait_send()

    anyspec = pl.BlockSpec(memory_space=pl.ANY)
    return pl.pallas_call(
        body, name=name, in_specs=[anyspec] * n, out_specs=[anyspec] * n,
        out_shape=[jax.ShapeDtypeStruct(a.shape, a.dtype) for a in arrs],
        scratch_shapes=[pltpu.SemaphoreType.DMA((n,)), pltpu.SemaphoreType.DMA((n,))],
        compiler_params=pltpu.CompilerParams(has_side_effects=True))(*arrs)


def _allreduce_small(packed, name):
    r = packed.shape[0]

    def body(in_ref, out_ref, land, send_sems, recv_sems):
        x, y, c = lax.axis_index("x"), lax.axis_index("y"), lax.axis_index("c")
        me = 4 * x + 2 * y + c
        land[me] = in_ref[...]
        rel = [(dx, dy, dc) for dx in (0, 1) for dy in (0, 1) for dc in (0, 1)][1:]
        peers = [((1 - x) if dx else x, (1 - y) if dy else y, (1 - c) if dc else c) for dx, dy, dc in rel]
        sends = []
        for k, peer in enumerate(peers):
            cp = pltpu.make_async_remote_copy(src_ref=in_ref, dst_ref=land.at[me], send_sem=send_sems.at[k], recv_sem=recv_sems.at[k],
                                              device_id=peer, device_id_type=MESH_T)
            cp.start()
            sends.append(cp)
        for k, (px, py, pc) in enumerate(peers):
            pltpu.make_async_remote_copy(src_ref=in_ref, dst_ref=land.at[4 * px + 2 * py + pc], send_sem=send_sems.at[k],
                                         recv_sem=recv_sems.at[k], device_id=(px, py, pc), device_id_type=MESH_T).wait_recv()
        for cp in sends:
            cp.wait_send()
        acc = land[0]
        for j in range(1, 8):
            acc = acc + land[j]
        out_ref[...] = acc

    vm = pl.BlockSpec(memory_space=pltpu.VMEM)
    return pl.pallas_call(
        body, name=name, in_specs=[vm], out_specs=vm, out_shape=jax.ShapeDtypeStruct((r, LANES), F32),
        scratch_shapes=[pltpu.VMEM((8, r, LANES), F32), pltpu.SemaphoreType.DMA((7,)), pltpu.SemaphoreType.DMA((7,))],
        compiler_params=pltpu.CompilerParams(has_side_effects=True))(packed)


def _in_splits():
    w = SSD_HEADS * SSD_HEAD_DIM
    cc = w + 2 * SSD_GROUPS * SSD_STATE
    return [w, cc, SSD_HEADS, MLA_Q_RANK, MLA_KV_RANK, MLA_ROPE]


def _padc(a, n):
    return jnp.pad(a, ((0, 0), (0, n - a.shape[1])))


def _win_pad(wm):
    offs = np.cumsum(_in_splits())[:-1]
    z, xbc, dt, cq, ckv, kr = jnp.split(wm, offs, axis=1)
    return jnp.concatenate([z, xbc, cq, ckv, _padc(kr, LANES), _padc(dt, LANES)], axis=1)


def _win_unpad(g):
    w, cc, nh, qr, kvr, rp = _in_splits()
    offs = np.cumsum([w, cc, qr, kvr, LANES])
    z, xbc, cq, ckv, kr, dt = jnp.split(g, offs, axis=1)
    return jnp.concatenate([z, xbc, dt[:, :nh], cq, ckv, kr[:, :rp]], axis=1)


def _wuq_pad(wm):
    r = wm.shape[0]
    w3 = wm.reshape(r, MLA_HEADS, MLA_NOPE + MLA_ROPE)
    return jnp.pad(w3, ((0, 0), (0, 0), (0, 2 * LANES - MLA_NOPE - MLA_ROPE))).reshape(r, MLA_HEADS * 2 * LANES)


def _wuq_unpad(g):
    r = g.shape[0]
    return g.reshape(r, MLA_HEADS, 2 * LANES)[:, :, :MLA_NOPE + MLA_ROPE].reshape(r, MLA_HEADS * (MLA_NOPE + MLA_ROPE))


def _full_from_gather(name, g):
    layers = g.shape[1]
    if name not in COL_SHARDED:
        return [g[:, l].reshape(-1, g.shape[-1]) for l in range(layers)]
    return [jnp.concatenate([g[j, l] for j in range(4)], axis=1) for l in range(layers)]


def _gathered_t(g):
    return g.transpose(0, 1, 3, 2).reshape(-1, g.shape[2])


def _shards_from_full(name, mats):
    col = name in COL_SHARDED
    per = []
    for mt in mats:
        r, c = mt.shape
        per.append(mt.reshape(r, 4, c // 4).transpose(1, 0, 2) if col else mt.reshape(4, r // 4, c))
    return jnp.stack(per, axis=1).astype(BF16)


def _pack_small(vals, extra=None):
    flat = jnp.concatenate([vals[n].reshape(-1).astype(F32) for n in SMALL] + ([extra.reshape(1)] if extra is not None else []))
    rows = -(-flat.shape[0] // (8 * LANES)) * 8
    return jnp.pad(flat, (0, rows * LANES - flat.shape[0])).reshape(rows, LANES)


def _unpack_small(packed, like):
    flat, out, off = packed.reshape(-1), {}, 0
    for n in SMALL:
        sz = like[n].size
        out[n] = flat[off:off + sz].reshape(like[n].shape)
        off += sz
    return out


def _ffn_fwd(x, norm_g, wg, wu, wd, tag):
    h, r = _rms_fwd(x, norm_g, f"ffn_norm_{tag}")
    gate = _mm(h, wg, "nn", f"ffn_gate_{tag}", out_dtype=BF16, b_shards=True)
    up = _mm(h, wu, "nn", f"ffn_up_{tag}", out_dtype=BF16, b_shards=True)
    act = _swiglu_fwd(gate, up, f"swiglu_{tag}")
    out = _mm(act, wd, "nn", f"ffn_down_{tag}", add=x)
    return out, (x, r, h, gate, up, act)


def _ffn_bwd(dout, doutb, norm_g, wg, wu, wd, saved, tag):
    x, r, h, gate, up, act = saved
    dact = _mm(doutb, wd.T, "nn", f"ffn_dact_{tag}", out_dtype=BF16)
    dgate, dup = _swiglu_bwd(gate, up, dact, f"swiglu_bwd_{tag}")
    ht = h.T
    d_wd = _mm(act.T, doutb, "nn", f"ffn_dwd_{tag}", out_dtype=BF16)
    d_wg = _mm(ht, dgate, "nn", f"ffn_dwg_{tag}", out_dtype=BF16, out_shards=True)
    d_wu = _mm(ht, dup, "nn", f"ffn_dwu_{tag}", out_dtype=BF16, out_shards=True)
    dh = _mm(dgate, _gathered_t(wg), "nn", f"ffn_dh1_{tag}")
    dh = _mm(dup, _gathered_t(wu), "nn", f"ffn_dh2_{tag}", add=dh)
    dx, dxb, dnorm = _rms_bwd(x, norm_g, r, dh, f"ffn_norm_bwd_{tag}", dres=dout)
    return dx, dxb, dnorm, d_wg, d_wu, d_wd


def kernel(x, mix_norm, ffn_norm, w_in, conv_w, conv_b, dt_bias, a_log, d_skip, ssd_norm, q_norm, kv_norm, w_uq, w_ukv, w_out_even, w_qkv, w_out_odd, w_gate, w_up, w_down, final_norm, loss_target, m_mix_norm, m_ffn_norm, m_w_in, m_conv_w, m_conv_b, m_dt_bias, m_a_log, m_d_skip, m_ssd_norm, m_q_norm, m_kv_norm, m_w_uq, m_w_ukv, m_w_out_even, m_w_qkv, m_w_out_odd, m_w_gate, m_w_up, m_w_down, m_final_norm, v_mix_norm, v_ffn_norm, v_w_in, v_conv_w, v_conv_b, v_dt_bias, v_a_log, v_d_skip, v_ssd_norm, v_q_norm, v_kv_norm, v_w_uq, v_w_ukv, v_w_out_even, v_w_qkv, v_w_out_odd, v_w_gate, v_w_up, v_w_down, v_final_norm):
    given = dict(locals())
    wts = {n: given[n] for n in WEIGHTS}
    x0 = x[0]
    tgt = loss_target[0]
    t, d = x0.shape
    hw = SSD_HEADS * SSD_HEAD_DIM
    gn = SSD_GROUPS * SSD_STATE
    cc = hw + 2 * gn
    qr, kvr = MLA_Q_RANK, MLA_KV_RANK

    def shard(n, layer=None):
        a = wts[n] if layer is None else wts[n][layer:layer + 1]
        return a if n == 'conv_w' else a.astype(BF16)
    g0_names = ['w_in', 'conv_w']
    gq_names = ['w_uq', 'w_ukv']
    g1_names = [('w_out_even', None), ('w_gate', 0), ('w_up', 0), ('w_down', 0)]
    g2_names = [('w_qkv', None), ('w_out_odd', None), ('w_gate', 1), ('w_up', 1), ('w_down', 1)]
    hg0 = _xstart([shard(n) for n in g0_names], True, [], "gather0_start")
    hgq = _xstart([shard(n) for n in gq_names], True, [hg0[-1]], "gatherq_start")
    hg1 = _xstart([shard(n, l) for n, l in g1_names], True, [hgq[-1]], "gather1_start")
    hg2 = _xstart([shard(n, l) for n, l in g2_names], True, [hg1[-1]], "gather2_start")
    full = lambda n, g: _full_from_gather(n, g)[0]
    g0 = dict(zip(g0_names, _xwait(hg0, True, hg2[-1], "gather0_wait")))
    win = _win_pad(full('w_in', g0['w_in']))
    cw = _padc(full('conv_w', g0['conv_w']).T, 8).T
    o_cq, o_ckv, o_kr, o_dt = hw + cc, hw + cc + qr, hw + cc + qr + kvr, hw + cc + qr + kvr + LANES

    row = lambda v: v.reshape(1, -1)
    narrow = lambda v: _padc(v.reshape(1, -1), LANES)
    dtb, alog = narrow(dt_bias[0]), narrow(a_log[0])
    dsk_x = jnp.repeat(d_skip[0], SSD_HEAD_DIM).reshape(1, hw)
    cos, sin = _rope_tables(t)

    h0, r0 = _rms_fwd(x0, row(mix_norm[0]), "mix_norm_0")
    u = _mm(h0, win, "nn", "in_proj")
    z, xbc, c_q, c_kv = u[:, :hw], u[:, hw:hw + cc], u[:, o_cq:o_ckv], u[:, o_ckv:o_kr]
    kr_raw, dtraw = u[:, o_kr:o_dt], u[:, o_dt:]
    xc = _conv_fwd(xbc, cw, row(conv_b[0]), "conv")
    y_ssd, sprev = _ssd_fwd(dtraw, xc, dtb, alog, dsk_x, "ssd")
    yg, r_g = _gated_fwd(y_ssd, z, row(ssd_norm[0]), "ssd_gate_norm")
    q_lat, r_q = _rms_fwd(c_q, row(q_norm[0]), "q_norm")
    kv_lat, r_kv = _rms_fwd(c_kv, row(kv_norm[0]), "kv_norm")
    gq = dict(zip(gq_names, _xwait(hgq, True, r_kv, "gatherq_wait")))
    wuq = _wuq_pad(full('w_uq', gq['w_uq']))
    wukv = gq['w_ukv']
    qf = _mm(q_lat, wuq, "nn", "q_up")
    kvb = _mm(kv_lat, wukv, "nn", "kv_up", out_dtype=BF16, b_shards=True)
    q_r = _rope(qf, cos, sin, 2, False, BF16, "rope_q")
    k_r = _rope(kr_raw, cos, sin, 1, False, BF16, "rope_k")
    o_mla, lse = _mla_fwd(q_r, k_r, kvb, "mla")
    g1 = dict(zip(g1_names, _xwait(hg1, True, lse, "gather1_wait")))
    woe, wd0 = full('w_out_even', g1[g1_names[0]]), full('w_down', g1[g1_names[3]])
    wg0, wu0 = g1[g1_names[1]], g1[g1_names[2]]
    cat = jnp.concatenate([yg, o_mla.astype(BF16)], axis=1)
    x1 = _mm(cat, woe, "nn", "mix_out_0", add=x0)
    x2, ffn0 = _ffn_fwd(x1, row(ffn_norm[0]), wg0, wu0, wd0, "0")

    g2 = dict(zip(g2_names, _xwait(hg2, True, x2, "gather2_wait")))
    woo, wd1 = full('w_out_odd', g2[g2_names[1]]), full('w_down', g2[g2_names[4]])
    wqkv, wg1, wu1 = g2[g2_names[0]], g2[g2_names[2]], g2[g2_names[3]]
    h1, r1 = _rms_fwd(x2, row(mix_norm[1]), "mix_norm_1")
    qkv = _mm(h1, wqkv, "nn", "qkv_proj", out_dtype=BF16, b_shards=True)
    o_sb, ltot = _sb_fwd(qkv, "sb")
    o_sbb = o_sb.astype(BF16)
    x3 = _mm(o_sbb, woo, "nn", "mix_out_1", add=x2)
    x4, ffn1 = _ffn_fwd(x3, row(ffn_norm[1]), wg1, wu1, wd1, "1")

    loss_part, dx4, dx4b, d_final = _loss_fwd_bwd(x4, row(final_norm), tgt, "loss")

    def grad_shards(pairs):
        return [g[:, None] if g.ndim == 3 else _shards_from_full(n, [g]) for n, g in pairs]

    dx3, dx3b, d_ffn1, d_wg1, d_wu1, d_wd1 = _ffn_bwd(dx4, dx4b, row(ffn_norm[1]), wg1, wu1, wd1, ffn1, "1")
    do_sb = _mm(dx3b, woo.T, "nn", "sb_dout")
    d_woo = _mm(o_sbb.T, dx3b, "nn", "d_w_out_odd", out_dtype=BF16)
    dq, dk, dv = _sb_bwd(qkv, do_sb, ltot, "sb_bwd")
    dqkv = jnp.concatenate([dq, dk, dv], axis=1).astype(BF16)
    d_wqkv = _mm(h1.T, dqkv, "nn", "d_w_qkv", out_dtype=BF16, out_shards=True)
    x2_names = ['w_qkv', 'w_out_odd', 'w_gate', 'w_up', 'w_down']
    hx2 = _xstart(grad_shards(zip(x2_names, [d_wqkv, d_woo, d_wg1, d_wu1, d_wd1])), False, [], "grads2_start")
    dh1 = _mm(dqkv, _gathered_t(wqkv), "nn", "qkv_dh")
    dx2, dx2b, d_mix1 = _rms_bwd(x2, row(mix_norm[1]) + hx2[-1][0:1, 0:1], r1, dh1, "mix_norm_bwd_1", dres=dx3)

    dx1, dx1b, d_ffn0, d_wg0, d_wu0, d_wd0 = _ffn_bwd(dx2, dx2b, row(ffn_norm[0]), wg0, wu0, wd0, ffn0, "0")
    d_woe = _mm(cat.T, dx1b, "nn", "d_w_out_even", out_dtype=BF16)
    x1_names = ['w_out_even', 'w_gate', 'w_up', 'w_down']
    hx1 = _xstart(grad_shards(zip(x1_names, [d_woe, d_wg0, d_wu0, d_wd0])), False, [], "grads1_start")
    tok1 = hx1[-1][0:1, 0:1]
    dcat = _mm(dx1b, woe.T, "nn", "mix_dcat")
    dy_ssd, dz, d_ssdn = _gated_bwd(y_ssd, z, row(ssd_norm[0]) + tok1, r_g, dcat[:, :hw], "ssd_gate_norm_bwd")
    dxc, ddtraw, d_dtb, d_alog, d_dsk = _ssd_bwd(dtraw, xc, dtb, alog, dsk_x, sprev, dy_ssd, "ssd_bwd")
    dxbc, d_cw, d_cb = _conv_bwd(xbc, cw, row(conv_b[0]), dxc, "conv_bwd")
    dqm, dkvm, dkr = _mla_bwd(q_r, k_r, kvb, o_mla, dcat[:, hw:] + tok1, lse, "mla_bwd")
    dqf = _rope(dqm, cos, sin, 2, True, BF16, "rope_q_bwd")
    dkr_raw = _rope(dkr, cos, sin, 1, True, F32, "rope_k_bwd")
    dkvf = dkvm.astype(BF16)
    d_wuq = _mm(q_lat.T, dqf, "nn", "d_w_uq", out_dtype=BF16)
    d_wukv = _mm(kv_lat.T, dkvf, "nn", "d_w_ukv", out_dtype=BF16, out_shards=True)
    dq_lat = _mm(dqf, wuq.T, "nn", "q_up_bwd")
    dkv_lat = _mm(dkvf, _gathered_t(wukv), "nn", "kv_up_bwd")
    dc_q, _, d_qn = _rms_bwd(c_q, row(q_norm[0]), r_q, dq_lat, "q_norm_bwd")
    dc_kv, _, d_kvn = _rms_bwd(c_kv, row(kv_norm[0]), r_kv, dkv_lat, "kv_norm_bwd")
    du = jnp.concatenate([dz, dxbc, dc_q, dc_kv, dkr_raw, ddtraw], axis=1).astype(BF16)
    d_win = _mm(h0.T, du, "nn", "d_w_in", out_dtype=BF16)
    x0_names = ['w_in', 'conv_w', 'w_uq', 'w_ukv']
    hx0 = _xstart(grad_shards(zip(x0_names, [_win_unpad(d_win), d_cw, _wuq_unpad(d_wuq), d_wukv])), False, [], "grads0_start")
    dh0 = _mm(du, win.T, "nn", "in_proj_bwd")
    grad_x, _, d_mix0 = _rms_bwd(x0, row(mix_norm[0]) + hx0[-1][0:1, 0:1], r0, dh0, "mix_norm_bwd_0", dres=dx1)

    def chip_sums(handle, names, after, tag):
        lands = _xwait(handle, False, after, f"grads{tag}_wait")
        return {n: _sum4(p.reshape(4, -1, p.shape[-1]), f"sum{tag}_{n}") for n, p in zip(names, lands)}
    grads, deltas, new_m, new_v = {}, {}, {}, {}

    def adamw(n, ga, gb):
        res = _adamw(_as2d(wts[n]), ga, gb, _as2d(given['m_' + n]), _as2d(given['v_' + n]), f"adamw_{n}")
        grads[n], deltas[n], new_m[n], new_v[n] = [r.reshape(wts[n].shape) for r in res]
        return res[0]
    s2 = chip_sums(hx2, x2_names, grad_x, "2")
    hs2 = _sib_start([s2[n] for n in x2_names], [], "sibling2_start")
    s1 = chip_sums(hx1, x1_names, hs2[-1], "1")
    hs1 = _sib_start([s1[n] for n in x1_names], [], "sibling1_start")
    r2 = dict(zip(x2_names, _sib_wait(hs2, hs1[-1], "sibling2_wait")))
    r1 = dict(zip(x1_names, _sib_wait(hs1, r2['w_qkv'], "sibling1_wait")))
    last = None
    for n in ('w_qkv', 'w_out_odd'):
        last = adamw(n, [s2[n]], [r2[n]])
    last = adamw('w_out_even', [s1['w_out_even']], [r1['w_out_even']])
    for n in ('w_gate', 'w_up', 'w_down'):
        last = adamw(n, [s1[n], s2[n]], [r1[n], r2[n]])
    s0 = chip_sums(hx0, x0_names, last, "0")
    r0_ = _sibling_exchange([s0[n] for n in x0_names], "exchange_sibling0")
    for n, gb in zip(x0_names, r0_):
        adamw(n, [s0[n]], [gb])

    nhs = SSD_HEADS
    small_g = {'mix_norm': jnp.stack([d_mix0, d_mix1]), 'ffn_norm': jnp.stack([d_ffn0, d_ffn1]), 'conv_b': d_cb[None],
               'dt_bias': d_dtb[None, :nhs], 'a_log': d_alog[None, :nhs], 'd_skip': d_dsk[None, :nhs], 'ssd_norm': d_ssdn[None],
               'q_norm': d_qn[None], 'kv_norm': d_kvn[None], 'final_norm': d_final}
    packed, _ = lax.optimization_barrier((_pack_small(small_g, loss_part), r0_[0]))
    g_small = _allreduce_small(packed, "allreduce_small")
    loss = g_small.reshape(-1)[sum(wts[n].size for n in SMALL)]
    pk = lambda pre: _pack_small({n: given[pre + n] for n in SMALL}, jnp.zeros((), F32))
    sg, sd, sm, sv = _adamw(pk(''), [g_small], None, pk('m_'), pk('v_'), "adamw_small")
    for dst, src in ((grads, sg), (deltas, sd), (new_m, sm), (new_v, sv)):
        dst.update(_unpack_small(src, wts))

    outs = [loss, grad_x[None]]
    for dct in (grads, deltas, new_m, new_v):
        outs += [dct[n] for n in WEIGHTS]
    return tuple(outs)
```

```python
import functools
import math

import jax
import jax.numpy as jnp
import numpy as np
from jax import lax
from jax.experimental import pallas as pl
from jax.experimental.pallas import tpu as pltpu

F32, BF16 = jnp.float32, jnp.bfloat16

RMS_EPS = 1e-6
SSD_HEADS, SSD_HEAD_DIM, SSD_GROUPS, SSD_STATE, SSD_CONV, SSD_CHUNK = 32, 64, 4, 128, 4, 128
MLA_HEADS, MLA_Q_RANK, MLA_KV_RANK, MLA_NOPE, MLA_ROPE, MLA_V = 16, 512, 512, 128, 64, 128
ROPE_THETA = 10000.0
SB_HEADS, SB_HEAD_DIM = 16, 128
ADAM_LR, ADAM_B1, ADAM_B2, ADAM_EPS, ADAM_WD, ADAM_STEP = 0.001, 0.9, 0.999, 1e-08, 0.01, 10

LANES = 128
VMEM_LIMIT_BYTES = 56 * 1024 * 1024
MM_VMEM_BUDGET = 40 * 1024 * 1024
ATT_BLK = 256
SB_FWD_HEADS = 4
SB_BWD_HEADS = 4
MLA_FWD_HEADS = 4
MLA_BWD_HEADS = 4
ROW_TILE = 256
NEG = -1e30

MESH_T = pl.DeviceIdType.MESH
WEIGHTS = ['mix_norm', 'ffn_norm', 'w_in', 'conv_w', 'conv_b', 'dt_bias', 'a_log', 'd_skip', 'ssd_norm', 'q_norm',
           'kv_norm', 'w_uq', 'w_ukv', 'w_out_even', 'w_qkv', 'w_out_odd', 'w_gate', 'w_up', 'w_down', 'final_norm']
SHARDED = ['w_in', 'conv_w', 'w_uq', 'w_ukv', 'w_out_even', 'w_qkv', 'w_out_odd', 'w_gate', 'w_up', 'w_down']
COL_SHARDED = ['w_in', 'conv_w', 'w_uq', 'w_ukv', 'w_qkv', 'w_gate', 'w_up']
SMALL = [n for n in WEIGHTS if n not in SHARDED]


def _tile(n, cands):
    for c in cands:
        if n % c == 0:
            return c
    return n


def _params(ngrid):
    return pltpu.CompilerParams(dimension_semantics=("arbitrary",) * ngrid, vmem_limit_bytes=VMEM_LIMIT_BYTES)


def _dot(a, b, mode="nn"):
    dims = {"nn": (((1,), (0,)), ((), ())), "nt": (((1,), (1,)), ((), ())), "tn": (((0,), (0,)), ((), ()))}[mode]
    return lax.dot_general(a, b, dims, preferred_element_type=F32)


def _split(x, parts):
    out, r = [], x
    for _ in range(parts):
        p = r.astype(BF16)
        out.append(p)
        r = r - p.astype(F32)
    return out


def _xdot(x, e, parts=3):
    acc = None
    for p in _split(x, parts):
        t = _dot(p, e)
        acc = t if acc is None else acc + t
    return acc


def _xdot_l(e, x, parts=3):
    acc = None
    for p in _split(x, parts):
        t = _dot(e, p)
        acc = t if acc is None else acc + t
    return acc


def _iota(shape, dim):
    return lax.broadcasted_iota(jnp.int32, shape, dim)


def _softplus(z):
    return jnp.maximum(z, 0.0) + jnp.log(1.0 + jnp.exp(-jnp.abs(z)))


def _sigmoid(z):
    return 1.0 / (1.0 + jnp.exp(-z))


def _acc_rows(ref, first, val):
    @pl.when(first)
    def _():
        ref[...] = jnp.zeros_like(ref)
    ref[...] += jnp.broadcast_to(val, ref.shape)


def _mm(a, b, mode, name, out_dtype=F32, add=None, b_shards=False, out_shards=False):
    assert mode == "nn"
    m, k = a.shape
    n = 4 * b.shape[3] if b_shards else b.shape[1]
    ns = n // 4 if (b_shards or out_shards) else n
    tn = _tile(ns, (1536, 1408, 1280, 1024, 768, 640, 512, 256, 128))
    per = ns // tn
    tk = k if k <= 2048 else _tile(k, (2048, 1536, 1408, 1280, 1024, 512, 256, 128))
    ob = jnp.dtype(out_dtype).itemsize

    def need(tm_):
        per = tm_ * tk * a.dtype.itemsize + tk * tn * b.dtype.itemsize + tm_ * tn * ob + (tm_ * tn * 4 if add is not None else 0)
        return 2 * per + (tm_ * tn * 4 if k > tk else 0)
    tm = m
    for cand in (1408, 1024, 512, 256, 128):
        if m % cand == 0:
            tm = cand
            if need(cand) <= MM_VMEM_BUDGET:
                break
    nk = k // tk
    a_spec = pl.BlockSpec((tm, tk), lambda i, j, kk: (i, kk))
    if b_shards:
        b_spec = pl.BlockSpec((None, None, tk, tn), lambda i, j, kk: (j // per, 0, kk, j % per))
    else:
        b_spec = pl.BlockSpec((tk, tn), lambda i, j, kk: (kk, j))
    if out_shards:
        assert add is None
        o_spec = pl.BlockSpec((None, tm, tn), lambda i, j, kk: (j // per, i, j % per))
    else:
        o_spec = pl.BlockSpec((tm, tn), lambda i, j, kk: (i, j))
    has_add = add is not None

    def body(*refs):
        a_ref, b_ref = refs[0], refs[1]
        add_ref = refs[2] if has_add else None
        o_ref = refs[2 + has_add]
        part = _dot(a_ref[...].astype(BF16), b_ref[...].astype(BF16), mode)
        if nk == 1:
            if has_add:
                part = part + add_ref[...]
            o_ref[...] = part.astype(o_ref.dtype)
            return
        acc_ref = refs[3 + has_add]
        kk = pl.program_id(2)

        @pl.when(kk == 0)
        def _():
            acc_ref[...] = jnp.zeros_like(acc_ref)
        acc_ref[...] += part

        @pl.when(kk == nk - 1)
        def _():
            r = acc_ref[...]
            if has_add:
                r = r + add_ref[...]
            o_ref[...] = r.astype(o_ref.dtype)

    ins = [a, b] + ([add] if has_add else [])
    specs = [a_spec, b_spec] + ([o_spec] if has_add else [])
    return pl.pallas_call(
        body, name=name, grid=(m // tm, n // tn, nk), in_specs=specs, out_specs=o_spec,
        out_shape=jax.ShapeDtypeStruct((4, m, ns) if out_shards else (m, n), out_dtype),
        scratch_shapes=[pltpu.VMEM((tm, tn), F32)] if nk > 1 else [],
        compiler_params=_params(3))(*ins)


def _rows(t):
    return _tile(t, (ROW_TILE, 128, 64, 32, 16, 8))


def _rms_fwd(x, g, name):
    t, c = x.shape
    tr = _rows(t)

    def body(x_ref, g_ref, h_ref, r_ref):
        xv = x_ref[...]
        r = lax.rsqrt(jnp.mean(xv * xv, axis=-1, keepdims=True) + RMS_EPS)
        h_ref[...] = (xv * r * g_ref[...]).astype(h_ref.dtype)
        r_ref[...] = r

    return pl.pallas_call(
        body, name=name, grid=(t // tr,),
        in_specs=[pl.BlockSpec((tr, c), lambda i: (i, 0)), pl.BlockSpec((1, c), lambda i: (0, 0))],
        out_specs=[pl.BlockSpec((tr, c), lambda i: (i, 0)), pl.BlockSpec((tr, 1), lambda i: (i, 0))],
        out_shape=[jax.ShapeDtypeStruct((t, c), BF16), jax.ShapeDtypeStruct((t, 1), F32)],
        compiler_params=_params(1))(x, g)


def _rms_bwd(x, g, r, dh, name, dres=None):
    t, c = x.shape
    tr = _rows(t)
    has_res = dres is not None

    def body(*refs):
        x_ref, g_ref, r_ref, dh_ref = refs[:4]
        res_ref = refs[4] if has_res else None
        dx_ref, dxb_ref, dg_ref = refs[4 + has_res:]
        rv = r_ref[...]
        xh = x_ref[...] * rv
        dhv = dh_ref[...]
        dxh = dhv * g_ref[...]
        cm = jnp.mean(dxh * xh, axis=-1, keepdims=True)
        dx = (dxh - xh * cm) * rv
        if has_res:
            dx = dx + res_ref[...]
        dx_ref[...] = dx
        dxb_ref[...] = dx.astype(BF16)
        _acc_rows(dg_ref, pl.program_id(0) == 0, jnp.sum(dhv * xh, axis=0, keepdims=True))

    row = pl.BlockSpec((tr, c), lambda i: (i, 0))
    ins = [x, g, r, dh] + ([dres] if has_res else [])
    specs = [row, pl.BlockSpec((1, c), lambda i: (0, 0)), pl.BlockSpec((tr, 1), lambda i: (i, 0)), row] + ([row] if has_res else [])
    dx, dxb, dg = pl.pallas_call(
        body, name=name, grid=(t // tr,), in_specs=specs,
        out_specs=[row, row, pl.BlockSpec((8, c), lambda i: (0, 0))],
        out_shape=[jax.ShapeDtypeStruct((t, c), F32), jax.ShapeDtypeStruct((t, c), BF16), jax.ShapeDtypeStruct((8, c), F32)],
        compiler_params=_params(1))(*ins)
    return dx, dxb, dg[0]


def _gated_fwd(y, z, g, name):
    t, c = y.shape
    tr = _rows(t)

    def body(y_ref, z_ref, g_ref, o_ref, r_ref):
        zv = z_ref[...]
        v = y_ref[...] * zv * _sigmoid(zv)
        r = lax.rsqrt(jnp.mean(v * v, axis=-1, keepdims=True) + RMS_EPS)
        o_ref[...] = (v * r * g_ref[...]).astype(o_ref.dtype)
        r_ref[...] = r

    row = pl.BlockSpec((tr, c), lambda i: (i, 0))
    return pl.pallas_call(
        body, name=name, grid=(t // tr,), in_specs=[row, row, pl.BlockSpec((1, c), lambda i: (0, 0))],
        out_specs=[row, pl.BlockSpec((tr, 1), lambda i: (i, 0))],
        out_shape=[jax.ShapeDtypeStruct((t, c), BF16), jax.ShapeDtypeStruct((t, 1), F32)],
        compiler_params=_params(1))(y, z, g)


def _gated_bwd(y, z, g, r, dout, name):
    t, c = y.shape
    tr = _rows(t)

    def body(y_ref, z_ref, g_ref, r_ref, do_ref, dy_ref, dz_ref, dg_ref):
        yv, zv, rv, dov = y_ref[...], z_ref[...], r_ref[...], do_ref[...]
        s = _sigmoid(zv)
        sz = zv * s
        xh = yv * sz * rv
        dxh = dov * g_ref[...]
        cm = jnp.mean(dxh * xh, axis=-1, keepdims=True)
        dv = (dxh - xh * cm) * rv
        dy_ref[...] = dv * sz
        dz_ref[...] = dv * yv * s * (1.0 + zv * (1.0 - s))
        _acc_rows(dg_ref, pl.program_id(0) == 0, jnp.sum(dov * xh, axis=0, keepdims=True))

    row = pl.BlockSpec((tr, c), lambda i: (i, 0))
    dy, dz, dg = pl.pallas_call(
        body, name=name, grid=(t // tr,),
        in_specs=[row, row, pl.BlockSpec((1, c), lambda i: (0, 0)), pl.BlockSpec((tr, 1), lambda i: (i, 0)), row],
        out_specs=[row, row, pl.BlockSpec((8, c), lambda i: (0, 0))],
        out_shape=[jax.ShapeDtypeStruct((t, c), F32), jax.ShapeDtypeStruct((t, c), F32), jax.ShapeDtypeStruct((8, c), F32)],
        compiler_params=_params(1))(y, z, g, r, dout)
    return dy, dz, dg[0]


def _swiglu_fwd(gate, up, name):
    t, c = gate.shape
    tr, tc = _rows(t), _tile(c, (2816, 1408, 1024, 512, 256, 128))

    def body(g_ref, u_ref, o_ref):
        gv = g_ref[...].astype(F32)
        o_ref[...] = (gv * _sigmoid(gv) * u_ref[...].astype(F32)).astype(o_ref.dtype)

    blk = pl.BlockSpec((tr, tc), lambda i, j: (i, j))
    return pl.pallas_call(body, name=name, grid=(t // tr, c // tc), in_specs=[blk, blk], out_specs=blk,
                          out_shape=jax.ShapeDtypeStruct((t, c), BF16), compiler_params=_params(2))(gate, up)


def _swiglu_bwd(gate, up, dact, name):
    t, c = gate.shape
    tr, tc = _rows(t), _tile(c, (2816, 1408, 1024, 512, 256, 128))

    def body(g_ref, u_ref, d_ref, dg_ref, du_ref):
        gv, dv = g_ref[...].astype(F32), d_ref[...].astype(F32)
        s = _sigmoid(gv)
        dg_ref[...] = (dv * u_ref[...].astype(F32) * s * (1.0 + gv * (1.0 - s))).astype(dg_ref.dtype)
        du_ref[...] = (dv * gv * s).astype(du_ref.dtype)

    blk = pl.BlockSpec((tr, tc), lambda i, j: (i, j))
    return pl.pallas_call(body, name=name, grid=(t // tr, c // tc), in_specs=[blk, blk, blk], out_specs=[blk, blk],
                          out_shape=[jax.ShapeDtypeStruct((t, c), BF16)] * 2, compiler_params=_params(2))(gate, up, dact)


def _loss_fwd_bwd(x, g, tgt, name):
    t, c = x.shape
    tr = _rows(t)

    def body(x_ref, g_ref, t_ref, l_ref, dx_ref, dxb_ref, dg_ref):
        xv, gv = x_ref[...], g_ref[...]
        r = lax.rsqrt(jnp.mean(xv * xv, axis=-1, keepdims=True) + RMS_EPS)
        xh = xv * r
        err = xh * gv - t_ref[...]
        per_row = jnp.mean(err * err, axis=-1, keepdims=True)
        dy = err * (1.0 / c)
        dxh = dy * gv
        cm = jnp.mean(dxh * xh, axis=-1, keepdims=True)
        dx = (dxh - xh * cm) * r
        dx_ref[...] = dx
        dxb_ref[...] = dx.astype(BF16)
        first = pl.program_id(0) == 0
        _acc_rows(dg_ref, first, jnp.sum(dy * xh, axis=0, keepdims=True))
        _acc_rows(l_ref, first, jnp.broadcast_to(0.5 * jnp.sum(per_row, axis=0, keepdims=True), (1, LANES)))

    row = pl.BlockSpec((tr, c), lambda i: (i, 0))
    lo, dx, dxb, dg = pl.pallas_call(
        body, name=name, grid=(t // tr,), in_specs=[row, pl.BlockSpec((1, c), lambda i: (0, 0)), row],
        out_specs=[pl.BlockSpec((8, LANES), lambda i: (0, 0)), row, row, pl.BlockSpec((8, c), lambda i: (0, 0))],
        out_shape=[jax.ShapeDtypeStruct((8, LANES), F32), jax.ShapeDtypeStruct((t, c), F32), jax.ShapeDtypeStruct((t, c), BF16),
                   jax.ShapeDtypeStruct((8, c), F32)],
        compiler_params=_params(1))(x, g, tgt)
    return lo[0, 0], dx, dxb, dg[0]


def _conv_specs(t, c):
    tr = _rows(t)
    tc = _tile(c, (1024, 768, 512, 256, 128))
    h8 = tr // 8
    tile = pl.BlockSpec((tr, tc), lambda j, i: (i, j))
    prev = pl.BlockSpec((8, tc), lambda j, i: (jnp.maximum(i * h8 - 1, 0), j))
    nxt = pl.BlockSpec((8, tc), lambda j, i: (jnp.minimum((i + 1) * h8, t // 8 - 1), j))
    return tr, tc, tile, prev, nxt


def _conv_fwd(xbc, w, b, name):
    t, c = xbc.shape
    tr, tc, tile, prev, _ = _conv_specs(t, c)

    def body(x_ref, p_ref, w_ref, b_ref, o_ref, buf):
        i = pl.program_id(1)
        buf[0:8, :] = jnp.where(i > 0, p_ref[...], 0.0)
        buf[8:, :] = x_ref[...]
        pre = b_ref[...]
        for k in range(SSD_CONV):
            pre = pre + w_ref[k:k + 1, :] * buf[pl.ds(8 - (SSD_CONV - 1) + k, tr), :]
        o_ref[...] = pre * _sigmoid(pre)

    return pl.pallas_call(
        body, name=name, grid=(c // tc, t // tr),
        in_specs=[tile, prev, pl.BlockSpec((8, tc), lambda j, i: (0, j)), pl.BlockSpec((1, tc), lambda j, i: (0, j))],
        out_specs=tile, out_shape=jax.ShapeDtypeStruct((t, c), F32),
        scratch_shapes=[pltpu.VMEM((tr + 8, tc), F32)], compiler_params=_params(2))(xbc, xbc, w, b)


def _conv_bwd(xbc, w, b, dout, name):
    t, c = xbc.shape
    tr, tc, tile, prev, nxt = _conv_specs(t, c)
    nt = t // tr
    kc = SSD_CONV

    def body(x_ref, p_ref, n_ref, w_ref, b_ref, d_ref, dn_ref, dx_ref, dw_ref, db_ref, buf, dbuf):
        i = pl.program_id(1)
        last = i == nt - 1
        buf[0:8, :] = jnp.where(i > 0, p_ref[...], 0.0)
        buf[8:8 + tr, :] = x_ref[...]
        buf[8 + tr:, :] = jnp.where(last, 0.0, n_ref[...])
        pre = b_ref[...]
        for k in range(kc):
            pre = pre + w_ref[k:k + 1, :] * buf[pl.ds(8 - (kc - 1) + k, tr + 8), :]
        s = _sigmoid(pre)
        dsilu = s * (1.0 + pre * (1.0 - s))
        dbuf[0:tr, :] = d_ref[...] * dsilu[0:tr, :]
        dbuf[tr:, :] = jnp.where(last, 0.0, dn_ref[...]) * dsilu[tr:, :]
        dpre = dbuf[0:tr, :]
        dx = jnp.zeros((tr, tc), F32)
        first = i == 0
        for k in range(kc):
            dx = dx + w_ref[k:k + 1, :] * dbuf[pl.ds(kc - 1 - k, tr), :]
        dx_ref[...] = dx

        @pl.when(first)
        def _():
            dw_ref[...] = jnp.zeros_like(dw_ref)
        for k in range(kc):
            dw_ref[k:k + 1, :] += jnp.sum(dpre * buf[pl.ds(8 - (kc - 1) + k, tr), :], axis=0, keepdims=True)
        _acc_rows(db_ref, first, jnp.sum(dpre, axis=0, keepdims=True))

    par = pl.BlockSpec((8, tc), lambda j, i: (0, j))
    dx, dw, db = pl.pallas_call(
        body, name=name, grid=(c // tc, nt),
        in_specs=[tile, prev, nxt, par, pl.BlockSpec((1, tc), lambda j, i: (0, j)), tile, nxt],
        out_specs=[tile, par, par],
        out_shape=[jax.ShapeDtypeStruct((t, c), F32), jax.ShapeDtypeStruct((8, c), F32), jax.ShapeDtypeStruct((8, c), F32)],
        scratch_shapes=[pltpu.VMEM((tr + 16, tc), F32), pltpu.VMEM((tr + 8, tc), F32)],
        compiler_params=_params(2))(xbc, xbc, xbc, w, b, dout, dout)
    return dx, dw[:kc], db[0]


def _ssd_consts():
    h, p, ln = SSD_HEADS, SSD_HEAD_DIM, SSD_CHUNK
    w = h * p
    hrow, jcol = _iota((LANES, w), 0), _iota((LANES, w), 1)
    e = ((jcol >= hrow * p) & (jcol < (hrow + 1) * p)).astype(BF16)
    jrow, hcol = _iota((w, LANES), 0), _iota((w, LANES), 1)
    et = ((jrow >= hcol * p) & (jrow < (hcol + 1) * p)).astype(BF16)
    row, col = _iota((ln, ln), 0), _iota((ln, ln), 1)
    return e, et, row, col


def _ssd_common(dtraw_ref, dtb_ref, alog_ref, e):
    ln = SSD_CHUNK
    raw = dtraw_ref[...] + dtb_ref[...]
    dt = _softplus(raw)
    a = -jnp.exp(alog_ref[...])
    adt = dt * a
    row, col = _iota((ln, ln), 0), _iota((ln, ln), 1)
    cs = _xdot_l((col <= row).astype(BF16), adt)
    cl = jnp.sum(adt, axis=0, keepdims=True)
    ex = _xdot(jnp.concatenate([dt, cs, jnp.broadcast_to(cl, (ln, LANES))], axis=0), e)
    return raw, dt, a, cs, ex[:ln], ex[ln:2 * ln], ex[2 * ln:]


def _head_decay(cs, h, causal):
    ln = SSD_CHUNK
    lane = _iota((1, LANES), 1)
    colv = jnp.sum(jnp.where(lane == h, cs, 0.0), axis=1, keepdims=True)
    cb = jnp.broadcast_to(colv, (ln, ln))
    return jnp.exp(jnp.where(causal, cb - cb.T, NEG))


def _ssd_fwd(dtraw, xc, dtb, alog, dskip_x, name):
    t = dtraw.shape[0]
    h, p, g, n, ln = SSD_HEADS, SSD_HEAD_DIM, SSD_GROUPS, SSD_STATE, SSD_CHUNK
    w, gn, gw, hpg = h * p, g * n, (h // g) * p, h // g
    assert n == ln and gw % LANES == 0 and w % gn == 0
    nc = t // ln

    def body(dtraw_ref, x_ref, b_ref, c_ref, dtb_ref, alog_ref, dsk_ref, y_ref, sp_ref, s_ref):
        @pl.when(pl.program_id(0) == 0)
        def _():
            s_ref[...] = jnp.zeros_like(s_ref)
        e, _, row, col = _ssd_consts()
        causal = col <= row
        _, _, _, cs, dt_x, cs_x, cl_x = _ssd_common(dtraw_ref, dtb_ref, alog_ref, e)
        xv = x_ref[...]
        xd = xv * dt_x
        xdb = xd.astype(BF16)
        sv = s_ref[...]
        sp_ref[0] = sv
        el_x = jnp.exp(cs_x)
        zb = (xd * jnp.exp(cl_x - cs_x)).astype(BF16)
        cd_x = jnp.exp(cl_x)
        dsk = dsk_ref[...]
        half = _iota((1, LANES), 1) >= p
        for gi in range(g):
            gs = slice(gi * gw, (gi + 1) * gw)
            bg = b_ref[:, gi * n:(gi + 1) * n].astype(BF16)
            cg = c_ref[:, gi * n:(gi + 1) * n].astype(BF16)
            gm = _dot(cg, bg, "nt")
            sg = sv[:, gs]
            yoff = _dot(cg, sg.astype(BF16)) * el_x[:, gs]
            s_ref[:, gs] = sg * cd_x[:, gs] + _dot(bg, zb[:, gs], "tn")
            for pp in range(gw // LANES):
                ls = slice(gi * gw + pp * LANES, gi * gw + (pp + 1) * LANES)
                xp = xdb[:, ls]
                yp = yoff[:, pp * LANES:(pp + 1) * LANES] + dsk[:, ls] * xv[:, ls]
                for hh in range(LANES // p):
                    hd = gi * hpg + pp * (LANES // p) + hh
                    wm = (gm * _head_decay(cs, hd, causal)).astype(BF16)
                    yp = yp + _dot(wm, jnp.where(half == (hh == 1), xp, jnp.zeros_like(xp)))
                y_ref[:, ls] = yp

    nar = pl.BlockSpec((ln, LANES), lambda c: (c, 0))
    one = pl.BlockSpec((1, LANES), lambda c: (0, 0))
    return pl.pallas_call(
        body, name=name, grid=(nc,),
        in_specs=[nar, pl.BlockSpec((ln, w), lambda c: (c, 0)), pl.BlockSpec((ln, gn), lambda c: (c, w // gn)),
                  pl.BlockSpec((ln, gn), lambda c: (c, w // gn + 1)), one, one, pl.BlockSpec((1, w), lambda c: (0, 0))],
        out_specs=[pl.BlockSpec((ln, w), lambda c: (c, 0)), pl.BlockSpec((1, n, w), lambda c: (c, 0, 0))],
        out_shape=[jax.ShapeDtypeStruct((t, w), F32), jax.ShapeDtypeStruct((nc, n, w), F32)],
        scratch_shapes=[pltpu.VMEM((n, w), F32)], compiler_params=_params(1))(dtraw, xc, xc, xc, dtb, alog, dskip_x)


def _ssd_bwd(dtraw, xc, dtb, alog, dskip_x, sprev, dy, name):
    t = dtraw.shape[0]
    h, p, g, n, ln = SSD_HEADS, SSD_HEAD_DIM, SSD_GROUPS, SSD_STATE, SSD_CHUNK
    w, gn, gw, hpg = h * p, g * n, (h // g) * p, h // g
    nc = t // ln

    def body(dtraw_ref, x_ref, b_ref, c_ref, dtb_ref, alog_ref, dsk_ref, sp_ref, dy_ref,
             dxc_ref, ddt_ref, dbias_ref, dalog_ref, ddsk_ref, ds_ref, dxd_ref, qcs_ref):
        first = pl.program_id(0) == 0

        @pl.when(first)
        def _():
            ds_ref[...] = jnp.zeros_like(ds_ref)
        e, et, row, col = _ssd_consts()
        causal = col <= row
        raw, dt, a, cs, dt_x, cs_x, cl_x = _ssd_common(dtraw_ref, dtb_ref, alog_ref, e)
        xv = x_ref[...]
        xd = xv * dt_x
        xdb = xd.astype(BF16)
        sv = sp_ref[0]
        dyv = dy_ref[...]
        dyb = dyv.astype(BF16)
        dsn = ds_ref[...]
        el_x = jnp.exp(cs_x)
        dte_x = jnp.exp(cl_x - cs_x)
        cd_x = jnp.exp(cl_x)
        zf = xd * dte_x
        lane = _iota((1, LANES), 1)
        half = lane >= p
        lastrow = _iota((ln, 1), 0) == ln - 1
        dcs = jnp.zeros((ln, LANES), F32)
        for gi in range(g):
            gs = slice(gi * gw, (gi + 1) * gw)
            ns = slice(gi * n, (gi + 1) * n)
            bg = b_ref[:, ns].astype(BF16)
            cg = c_ref[:, ns].astype(BF16)
            gm = _dot(cg, bg, "nt")
            sgb = sv[:, gs].astype(BF16)
            dsg = dsn[:, gs]
            dsgb = dsg.astype(BF16)
            yoff = _dot(cg, sgb) * el_x[:, gs]
            drb = (el_x[:, gs] * dyv[:, gs]).astype(BF16)
            dcg = _dot(drb, sgb, "nt")
            ds_ref[:, gs] = cd_x[:, gs] * dsg + _dot(cg, drb, "tn")
            dz = _dot(bg, dsgb)
            zg = zf[:, gs]
            dbg = _dot(zg.astype(BF16), dsgb, "nt")
            dzz = dz * zg
            qcl = jnp.sum(dzz + cd_x[:, gs] * dsg * sv[:, gs], axis=0, keepdims=True)
            qcs_ref[:, gs] = dyv[:, gs] * yoff - dzz + jnp.where(lastrow, jnp.broadcast_to(qcl, (ln, gw)), 0.0)
            dgm = jnp.zeros((ln, ln), F32)
            for pp in range(gw // LANES):
                ls = slice(gi * gw + pp * LANES, gi * gw + (pp + 1) * LANES)
                xp = xdb[:, ls]
                dxp = dz[:, pp * LANES:(pp + 1) * LANES] * dte_x[:, ls]
                for hh in range(LANES // p):
                    hd = gi * hpg + pp * (LANES // p) + hh
                    dm = _head_decay(cs, hd, causal)
                    wf = gm * dm
                    dym = jnp.where(half == (hh == 1), dyb[:, ls], jnp.zeros_like(xp))
                    dw = _dot(dym, xp, "nt")
                    dxp = dxp + _dot(wf.astype(BF16), dym, "tn")
                    dgm = dgm + dw * dm
                    mm = dw * wf
                    rc = jnp.sum(mm, axis=1, keepdims=True) - jnp.sum(mm.T, axis=1, keepdims=True)
                    dcs = dcs + rc * (lane == hd).astype(F32)
                dxd_ref[:, ls] = dxp
            dgb = dgm.astype(BF16)
            dxc_ref[:, w + gn + gi * n:w + gn + (gi + 1) * n] = dcg + _dot(dgb, bg)
            dxc_ref[:, w + gi * n:w + (gi + 1) * n] = dbg + _dot(dgb, cg, "tn")
        dxd = dxd_ref[...]
        dxc_ref[:, :w] = dxd * dt_x + dsk_ref[...] * dyv
        red = _xdot(jnp.concatenate([qcs_ref[...], dxd * xv, dyv * xv], axis=0), et)
        dcs = dcs + red[:ln]
        dadt = _xdot_l((row <= col).astype(BF16), dcs)
        ddt = red[ln:2 * ln] + dadt * a
        draw = ddt * _sigmoid(raw)
        ddt_ref[...] = draw
        _acc_rows(dbias_ref, first, jnp.sum(draw, axis=0, keepdims=True))
        _acc_rows(dalog_ref, first, jnp.sum(dadt * dt, axis=0, keepdims=True) * a)
        _acc_rows(ddsk_ref, first, jnp.sum(red[2 * ln:], axis=0, keepdims=True))

    rev = lambda c: nc - 1 - c
    nar = pl.BlockSpec((ln, LANES), lambda c: (rev(c), 0))
    one = pl.BlockSpec((1, LANES), lambda c: (0, 0))
    wide = pl.BlockSpec((ln, w), lambda c: (rev(c), 0))
    acc = pl.BlockSpec((8, LANES), lambda c: (0, 0))
    outs = pl.pallas_call(
        body, name=name, grid=(nc,),
        in_specs=[nar, wide, pl.BlockSpec((ln, gn), lambda c: (rev(c), w // gn)), pl.BlockSpec((ln, gn), lambda c: (rev(c), w // gn + 1)),
                  one, one, pl.BlockSpec((1, w), lambda c: (0, 0)), pl.BlockSpec((1, n, w), lambda c: (rev(c), 0, 0)), wide],
        out_specs=[pl.BlockSpec((ln, w + 2 * gn), lambda c: (rev(c), 0)), nar, acc, acc, acc],
        out_shape=[jax.ShapeDtypeStruct((t, w + 2 * gn), F32), jax.ShapeDtypeStruct((t, LANES), F32)]
        + [jax.ShapeDtypeStruct((8, LANES), F32)] * 3,
        scratch_shapes=[pltpu.VMEM((n, w), F32), pltpu.VMEM((ln, w), F32), pltpu.VMEM((ln, w), F32)],
        compiler_params=_params(1))(dtraw, xc, xc, xc, dtb, alog, dskip_x, sprev, dy)
    dxc, ddt, dbias, dalog, ddsk = outs
    return dxc, ddt, dbias[0], dalog[0], ddsk[0]


def _rope_tables(t):
    half = MLA_ROPE // 2
    inv_freq = ROPE_THETA ** (-jnp.arange(half, dtype=F32) / half)
    ang = jnp.arange(t, dtype=F32)[:, None] * inv_freq[None, :]
    cos, sin = jnp.cos(ang), jnp.sin(ang)
    pad = LANES - MLA_ROPE
    cos_r = jnp.concatenate([cos, cos, jnp.ones((t, pad), F32)], axis=1)
    sin_r = jnp.concatenate([sin, sin, jnp.zeros((t, pad), F32)], axis=1)
    return cos_r, sin_r


def _rot_matrix():
    half = MLA_ROPE // 2
    i, j = _iota((LANES, LANES), 0), _iota((LANES, LANES), 1)
    neg = (j < half) & (i == j + half)
    pos = (j >= half) & (j < 2 * half) & (i == j - half)
    return (pos.astype(F32) - neg.astype(F32)).astype(BF16)


def _rope(x, cos, sin, every, transpose, out_dtype, name):
    t, w = x.shape
    tr = _rows(t)

    def body(x_ref, c_ref, s_ref, o_ref):
        rot = _rot_matrix()
        cv, sv = c_ref[...], s_ref[...]
        for j in range(w // LANES):
            ls = slice(j * LANES, (j + 1) * LANES)
            xv = x_ref[:, ls]
            if j % every != every - 1:
                o_ref[:, ls] = xv.astype(o_ref.dtype)
            elif transpose:
                o_ref[:, ls] = (xv * cv - _xdot(xv * sv, rot, 2)).astype(o_ref.dtype)
            else:
                o_ref[:, ls] = (xv * cv + _xdot(xv, rot, 2) * sv).astype(o_ref.dtype)

    wide = pl.BlockSpec((tr, w), lambda i: (i, 0))
    tab = pl.BlockSpec((tr, LANES), lambda i: (i, 0))
    return pl.pallas_call(
        body, name=name, grid=(t // tr,), in_specs=[wide, tab, tab], out_specs=wide,
        out_shape=jax.ShapeDtypeStruct((t, w), out_dtype), compiler_params=_params(1))(x, cos, sin)


def _att_masks(blk):
    return _iota((blk, blk), 0), _iota((blk, blk), 1)


def _lanes(j):
    return slice(j * LANES, (j + 1) * LANES)


def _mla_fwd(q, kr, kv, name):
    t = q.shape[0]
    nh, blk = MLA_HEADS, min(ATT_BLK, t)
    scale = (MLA_NOPE + MLA_ROPE) ** -0.5
    hps = math.gcd(nh, MLA_FWD_HEADS)

    def body(q_ref, kv_ref, kr_ref, o_ref, lse_ref):
        i = pl.program_id(1)
        row, col = _att_masks(blk)
        qs = [q_ref[:, 2 * hh * LANES:(2 * hh + 2) * LANES] for hh in range(hps)]

        def scores(kb, hh):
            ks = pl.ds(pl.multiple_of(kb * blk, blk), blk)
            kfull = jnp.concatenate([kv_ref[ks, _lanes(2 * hh)], kr_ref[ks, :]], axis=1)
            return _dot(qs[hh], kfull, "nt") * scale, kv_ref[ks, _lanes(2 * hh + 1)]
        init = []
        for hh in range(hps):
            s, v = scores(i, hh)
            s = jnp.where(col <= row, s, NEG)
            m = jnp.max(s, axis=1, keepdims=True)
            pr = jnp.exp(s - m)
            init += [m, jnp.sum(pr, axis=1, keepdims=True), _dot(pr.astype(BF16), v)]

        def step(kb, carry):
            sv = [scores(kb, hh) for hh in range(hps)]
            out, prs = [], []
            for hh in range(hps):
                m, l, acc = carry[3 * hh:3 * hh + 3]
                s = sv[hh][0]
                m2 = jnp.maximum(m, jnp.max(s, axis=1, keepdims=True))
                al = jnp.exp(m - m2)
                pr = jnp.exp(s - m2)
                prs.append(pr.astype(BF16))
                out += [m2, al * l + jnp.sum(pr, axis=1, keepdims=True), al * acc]
            for hh in range(hps):
                out[3 * hh + 2] = out[3 * hh + 2] + _dot(prs[hh], sv[hh][1])
            return tuple(out)
        res = lax.fori_loop(0, i, step, tuple(init))
        for hh in range(hps):
            m, l, acc = res[3 * hh:3 * hh + 3]
            o_ref[:, _lanes(hh)] = acc / l
            lse_ref[hh] = m + jnp.log(l)

    return pl.pallas_call(
        body, name=name, grid=(nh // hps, t // blk),
        in_specs=[pl.BlockSpec((blk, 2 * hps * LANES), lambda h, i: (i, h)),
                  pl.BlockSpec((t, 2 * hps * LANES), lambda h, i: (0, h), pipeline_mode=pl.Buffered(1)),
                  pl.BlockSpec((t, LANES), lambda h, i: (0, 0), pipeline_mode=pl.Buffered(1))],
        out_specs=[pl.BlockSpec((blk, hps * LANES), lambda h, i: (i, h)), pl.BlockSpec((hps, blk, 1), lambda h, i: (h, i, 0))],
        out_shape=[jax.ShapeDtypeStruct((t, nh * LANES), F32), jax.ShapeDtypeStruct((nh, t, 1), F32)],
        compiler_params=_params(2))(q, kv, kr)


def _mla_bwd(q, kr, kv, o, do, lse, name):
    t = q.shape[0]
    nh, blk = MLA_HEADS, min(ATT_BLK, t)
    hps = math.gcd(nh, MLA_BWD_HEADS)
    scale = (MLA_NOPE + MLA_ROPE) ** -0.5

    def body(q_ref, kv_ref, kr_ref, o_ref, do_ref, lse_ref, dq_ref, dkv_ref, dkr_ref):
        hp, i = pl.program_id(0), pl.program_id(1)

        @pl.when(i == 0)
        def _():
            dkv_ref[...] = jnp.zeros_like(dkv_ref)

        @pl.when((i == 0) & (hp == 0))
        def _():
            dkr_ref[...] = jnp.zeros_like(dkr_ref)
        row, col = _att_masks(blk)
        qs = [q_ref[:, 2 * hh * LANES:(2 * hh + 2) * LANES] for hh in range(hps)]
        dobs = [do_ref[:, _lanes(hh)].astype(BF16) for hh in range(hps)]
        deltas = [jnp.sum(do_ref[:, _lanes(hh)] * o_ref[:, _lanes(hh)], axis=1, keepdims=True) for hh in range(hps)]
        lses = [lse_ref[hh] for hh in range(hps)]

        def tile(kb, carry, masked):
            ks = pl.ds(pl.multiple_of(kb * blk, blk), blk)
            krv = kr_ref[ks, :]
            hs = range(hps)
            kfull = [jnp.concatenate([kv_ref[ks, _lanes(2 * hh)], krv], axis=1) for hh in hs]
            ss = [_dot(qs[hh], kfull[hh], "nt") for hh in hs]
            dps = [_dot(dobs[hh], kv_ref[ks, _lanes(2 * hh + 1)], "nt") for hh in hs]
            prb, dsb = [], []
            for hh in hs:
                pr = jnp.exp(ss[hh] * scale - lses[hh])
                if masked:
                    pr = jnp.where(col <= row, pr, 0.0)
                prb.append(pr.astype(BF16))
                dsb.append((pr * (dps[hh] - deltas[hh]) * scale).astype(BF16))
            out, dkr = [], None
            for hh in hs:
                dkv_ref[ks, _lanes(2 * hh + 1)] += _dot(prb[hh], dobs[hh], "tn")
                dk = _dot(dsb[hh], qs[hh], "tn")
                dkv_ref[ks, _lanes(2 * hh)] += dk[:, :LANES]
                dkr = dk[:, LANES:] if dkr is None else dkr + dk[:, LANES:]
                out.append(carry[hh] + _dot(dsb[hh], kfull[hh]))
            dkr_ref[ks, :] += dkr
            return tuple(out)
        zero = jnp.zeros((blk, 2 * LANES), F32)
        carry = lax.fori_loop(0, i, lambda kb, c: tile(kb, c, False), (zero,) * hps)
        res = tile(i, carry, True)
        for hh in range(hps):
            dq_ref[:, 2 * hh * LANES:(2 * hh + 2) * LANES] = res[hh]

    qb = lambda w: pl.BlockSpec((blk, w * LANES), lambda h, i: (i, h))
    seq = lambda w, f: pl.BlockSpec((t, w * LANES), f, pipeline_mode=pl.Buffered(1))
    wide = jax.ShapeDtypeStruct((t, nh * 2 * LANES), F32)
    return pl.pallas_call(
        body, name=name, grid=(nh // hps, t // blk),
        in_specs=[qb(2 * hps), seq(2 * hps, lambda h, i: (0, h)), seq(1, lambda h, i: (0, 0)),
                  qb(hps), qb(hps), pl.BlockSpec((hps, blk, 1), lambda h, i: (h, i, 0))],
        out_specs=[qb(2 * hps), seq(2 * hps, lambda h, i: (0, h)), seq(1, lambda h, i: (0, 0))],
        out_shape=[wide, wide, jax.ShapeDtypeStruct((t, LANES), F32)],
        compiler_params=_params(2))(q, kv, kr, o, do, lse)


def _sb_fwd(qkv, name):
    t = qkv.shape[0]
    nh, blk = SB_HEADS, min(ATT_BLK, t)
    hps = math.gcd(nh, SB_FWD_HEADS)
    nq = t // blk
    scale = SB_HEAD_DIM ** -0.5

    def body(q_ref, k_ref, v_ref, o_ref, lt_ref):
        i = pl.program_id(1)
        row, col = _att_masks(blk)
        usuf = (row > col).astype(BF16)
        qs = [q_ref[:, _lanes(hh)] for hh in range(hps)]

        def tile(kb, carry, masked):
            ks = pl.ds(pl.multiple_of(kb * blk, blk), blk)
            hs = range(hps)
            zs = [_dot(qs[hh], k_ref[ks, _lanes(hh)], "nt") for hh in hs]
            lks, lss = [], []
            for hh in hs:
                z = zs[hh] * scale
                lk = -_softplus(z)
                lss.append(lk + z)
                lks.append(jnp.where(col < row, lk, 0.0) if masked else lk)
            pieces = [_split(lks[hh], 2) for hh in hs]
            later = [_dot(pieces[hh][0], usuf) + _dot(pieces[hh][1], usuf) for hh in hs]
            out = []
            for hh in hs:
                wt = jnp.exp(lss[hh] + later[hh] + carry[2 * hh + 1])
                if masked:
                    wt = jnp.where(col < row, wt, 0.0)
                out += [wt.astype(BF16), carry[2 * hh + 1] + jnp.sum(lks[hh], axis=1, keepdims=True)]
            for hh in hs:
                out[2 * hh] = carry[2 * hh] + _dot(out[2 * hh], v_ref[ks, _lanes(hh)])
            return tuple(out)
        za, zr = jnp.zeros((blk, LANES), F32), jnp.zeros((blk, 1), F32)
        carry = tile(i, (za, zr) * hps, True)
        carry = lax.fori_loop(0, i, lambda j, c: tile(i - 1 - j, c, False), carry)
        for hh in range(hps):
            o_ref[:, _lanes(hh)] = carry[2 * hh]
            lt_ref[hh] = carry[2 * hh + 1]

    ng = nh // hps
    full = lambda f: pl.BlockSpec((t, hps * LANES), f)
    return pl.pallas_call(
        body, name=name, grid=(ng, nq),
        in_specs=[pl.BlockSpec((blk, hps * LANES), lambda h, i: (i, h)), full(lambda h, i: (0, ng + h)), full(lambda h, i: (0, 2 * ng + h))],
        out_specs=[pl.BlockSpec((blk, hps * LANES), lambda h, i: (i, h)), pl.BlockSpec((hps, blk, 1), lambda h, i: (h, i, 0))],
        out_shape=[jax.ShapeDtypeStruct((t, nh * LANES), F32), jax.ShapeDtypeStruct((nh, t, 1), F32)],
        compiler_params=_params(2))(qkv, qkv, qkv)


def _sb_bwd(qkv, do, ltot, name):
    t = qkv.shape[0]
    nh, blk = SB_HEADS, min(ATT_BLK, t)
    hps = math.gcd(nh, SB_BWD_HEADS)
    nq = t // blk
    scale = SB_HEAD_DIM ** -0.5

    def body(q_ref, k_ref, v_ref, do_ref, lt_ref, dq_ref, dk_ref, dv_ref):
        i = pl.program_id(1)

        @pl.when(i == 0)
        def _():
            dk_ref[...] = jnp.zeros_like(dk_ref)
            dv_ref[...] = jnp.zeros_like(dv_ref)
        row, col = _att_masks(blk)
        uinc = (row <= col).astype(BF16)
        uexc = (row < col).astype(BF16)
        qs = [q_ref[:, _lanes(hh)] for hh in range(hps)]
        dobs = [do_ref[:, _lanes(hh)].astype(BF16) for hh in range(hps)]
        lts = [lt_ref[hh] for hh in range(hps)]

        def tile(kb, carry, masked):
            ks = pl.ds(pl.multiple_of(kb * blk, blk), blk)
            hs = range(hps)
            kvs = [k_ref[ks, _lanes(hh)] for hh in hs]
            zs = [_dot(qs[hh], kvs[hh], "nt") for hh in hs]
            dws = [_dot(dobs[hh], v_ref[ks, _lanes(hh)], "nt") for hh in hs]
            lks, lss = [], []
            for hh in hs:
                z = zs[hh] * scale
                lk = -_softplus(z)
                lss.append(lk + z)
                lks.append(jnp.where(col < row, lk, 0.0) if masked else lk)
            pieces = [_split(lks[hh], 2) for hh in hs]
            css = [_dot(pieces[hh][0], uinc) + _dot(pieces[hh][1], uinc) for hh in hs]
            wts, evs = [], []
            for hh in hs:
                wt = jnp.exp(lss[hh] + (lts[hh] - (carry[3 * hh + 1] + css[hh])))
                if masked:
                    wt = jnp.where(col < row, wt, 0.0)
                wts.append(wt.astype(BF16))
                evs.append(dws[hh] * wt)
            ecss = [_dot(evs[hh].astype(BF16), uexc) for hh in hs]
            for hh in hs:
                dv_ref[ks, _lanes(hh)] += _dot(wts[hh], dobs[hh], "tn")
            dzbs = []
            for hh in hs:
                sig = jnp.exp(lss[hh])
                dz = evs[hh] * (1.0 - sig) - (ecss[hh] + carry[3 * hh + 2]) * sig
                if masked:
                    dz = jnp.where(col < row, dz, 0.0)
                dzbs.append((dz * scale).astype(BF16))
            out = []
            for hh in hs:
                dk_ref[ks, _lanes(hh)] += _dot(dzbs[hh], qs[hh], "tn")
                out += [carry[3 * hh] + _dot(dzbs[hh], kvs[hh]), carry[3 * hh + 1] + jnp.sum(lks[hh], axis=1, keepdims=True),
                        carry[3 * hh + 2] + jnp.sum(evs[hh], axis=1, keepdims=True)]
            return tuple(out)
        za, z1 = jnp.zeros((blk, LANES), F32), jnp.zeros((blk, 1), F32)
        carry = lax.fori_loop(0, i, lambda kb, c: tile(kb, c, False), (za, z1, z1) * hps)
        res = tile(i, carry, True)
        for hh in range(hps):
            dq_ref[:, _lanes(hh)] = res[3 * hh]

    ng = nh // hps
    full = lambda f: pl.BlockSpec((t, hps * LANES), f, pipeline_mode=pl.Buffered(1))
    qb = pl.BlockSpec((blk, hps * LANES), lambda h, i: (i, h))
    wide = jax.ShapeDtypeStruct((t, nh * LANES), F32)
    return pl.pallas_call(
        body, name=name, grid=(ng, nq),
        in_specs=[qb, full(lambda h, i: (0, ng + h)), full(lambda h, i: (0, 2 * ng + h)), qb,
                  pl.BlockSpec((hps, blk, 1), lambda h, i: (h, i, 0))],
        out_specs=[qb, full(lambda h, i: (0, h)), full(lambda h, i: (0, h))],
        out_shape=[wide, wide, wide], compiler_params=_params(2))(qkv, qkv, qkv, do, ltot)


def _as2d(a):
    return a.reshape((-1, a.shape[-1]))


def _sum4(parts, name):
    _, r, c = parts.shape
    tr = _tile(r, (256, 128, 64, 32, 16, 8))

    def body(p_ref, o_ref):
        acc = p_ref[0].astype(F32)
        for j in range(1, 4):
            acc = acc + p_ref[j].astype(F32)
        o_ref[...] = acc

    return pl.pallas_call(body, name=name, grid=(r // tr,), in_specs=[pl.BlockSpec((4, tr, c), lambda i: (0, i, 0))],
                          out_specs=pl.BlockSpec((tr, c), lambda i: (i, 0)), out_shape=jax.ShapeDtypeStruct((r, c), F32),
                          compiler_params=_params(1))(parts)


def _adamw(w, gas, gbs, m, v, name):
    r, c = w.shape
    npc = len(gas)
    tr = _tile(r // npc, (128, 64, 32, 16, 8))
    nb = r // npc // tr
    c1, c2 = 1.0 - ADAM_B1 ** ADAM_STEP, 1.0 - ADAM_B2 ** ADAM_STEP
    two = gbs is not None
    ng = npc * (2 if two else 1)

    def body(*refs):
        w_ref, g_refs = refs[0], refs[1:1 + ng]
        m_ref, v_ref, g_out, d_out, m_out, v_out = refs[1 + ng:]
        pieces = [g_refs[p][...] + g_refs[npc + p][...] if two else g_refs[p][...] for p in range(npc)]
        gv = pieces[0]
        for p in range(1, npc):
            gv = jnp.where(pl.program_id(0) >= p * nb, pieces[p], gv)
        mn = ADAM_B1 * m_ref[...] + (1.0 - ADAM_B1) * gv
        vn = ADAM_B2 * v_ref[...] + (1.0 - ADAM_B2) * (gv * gv)
        g_out[...] = gv
        m_out[...] = mn
        v_out[...] = vn
        d_out[...] = -ADAM_LR * ((mn / c1) / (jnp.sqrt(vn / c2) + ADAM_EPS) + ADAM_WD * w_ref[...])

    blk = pl.BlockSpec((tr, c), lambda i: (i, 0))
    piece = lambda p: pl.BlockSpec((tr, c), lambda i: (jnp.clip(i - p * nb, 0, nb - 1), 0))
    gspecs = [piece(p) for p in range(npc)] * (2 if two else 1)
    ins = [w] + list(gas) + (list(gbs) if two else []) + [m, v]
    return pl.pallas_call(body, name=name, grid=(r // tr,), in_specs=[blk] + gspecs + [blk, blk], out_specs=[blk] * 4,
                          out_shape=[jax.ShapeDtypeStruct((r, c), F32)] * 4, compiler_params=_params(1))(*ins)


HBM_SPEC = pl.BlockSpec(memory_space=pltpu.HBM)
SEM_SPEC = pl.BlockSpec(memory_space=pltpu.SEMAPHORE)
EFFECT = pltpu.SideEffectType.DATAFLOW_SIDE_EFFECTING


def _chip_copies(ins, lands, send_sems, recv_sems, whole):
    x, y, c = lax.axis_index("x"), lax.axis_index("y"), lax.axis_index("c")
    me = 2 * x + y
    out = []
    for wi in range(len(ins)):
        for k, (px, py) in enumerate([(1 - x, y), (x, 1 - y), (1 - x, 1 - y)]):
            sems = dict(send_sem=send_sems[wi * 3 + k], recv_sem=recv_sems[wi * 3 + k], device_id=(px, py, c), device_id_type=MESH_T)
            peer = 2 * px + py
            sent = pltpu.make_async_remote_copy(src_ref=ins[wi] if whole else ins[wi].at[peer], dst_ref=lands[wi].at[me], **sems)
            got = functools.partial(pltpu.make_async_remote_copy, src_ref=ins[wi] if whole else ins[wi].at[me],
                                    dst_ref=lands[wi].at[peer], **sems)
            out.append((sent, got))
    return out


def _xstart(arrs, whole, after, name):
    n, na = len(arrs), len(after)
    me = 2 * lax.axis_index("x") + lax.axis_index("y")
    lands = []
    for a in arrs:
        own = a[None] if whole else lax.dynamic_slice_in_dim(a, me, 1, axis=0)
        empty = lax.empty(((4,) + a.shape) if whole else a.shape, a.dtype)
        lands.append(lax.dynamic_update_slice_in_dim(empty, own, me, axis=0))

    def body(*refs):
        ins, lands_in = refs[:n], refs[n:2 * n]
        outs = refs[2 * n + na:]
        for sent, _ in _chip_copies(ins, lands_in, outs[:3 * n], outs[3 * n:6 * n], whole):
            sent.start()
        outs[8 * n][...] = jnp.zeros((8, LANES), F32)

    hbm = lambda a: pltpu.HBM(a.shape, a.dtype)
    res = pl.pallas_call(
        body, name=name,
        out_shape=[pltpu.SemaphoreType.DMA(())] * (6 * n) + [hbm(a) for a in arrs] + [hbm(a) for a in lands]
        + [jax.ShapeDtypeStruct((8, LANES), F32)],
        in_specs=[HBM_SPEC] * (2 * n) + [pl.BlockSpec(memory_space=pl.ANY)] * na,
        out_specs=[SEM_SPEC] * (6 * n) + [HBM_SPEC] * (2 * n) + [pl.BlockSpec(memory_space=pltpu.VMEM)],
        input_output_aliases={i: 6 * n + i for i in range(2 * n)},
        compiler_params=pltpu.CompilerParams(has_side_effects=EFFECT),
    )(*[pltpu.with_memory_space_constraint(a, pltpu.HBM) for a in list(arrs) + lands], *after)
    return res


def _xwait(handle, whole, after, name):
    n = (len(handle) - 1) // 8
    sems, thru = handle[:6 * n], handle[6 * n:8 * n]

    def body(*refs):
        ins, lands_in = refs[:n], refs[n:2 * n]
        for sent, got in _chip_copies(ins, lands_in, refs[2 * n:5 * n], refs[5 * n:8 * n], whole):
            sent.wait_send()
            got().wait_recv()

    hbm = lambda a: pltpu.HBM(a.shape, a.dtype)
    res = pl.pallas_call(
        body, name=name, out_shape=[hbm(a) for a in thru],
        in_specs=[HBM_SPEC] * (2 * n) + [SEM_SPEC] * (6 * n) + [pl.BlockSpec(memory_space=pl.ANY)],
        out_specs=[HBM_SPEC] * (2 * n), input_output_aliases={i: i for i in range(2 * n)},
        compiler_params=pltpu.CompilerParams(has_side_effects=EFFECT),
    )(*thru, *sems, after)
    return res[n:]


def _sibling_copies(ins, lands, send_sems, recv_sems):
    sib = (lax.axis_index("x"), lax.axis_index("y"), 1 - lax.axis_index("c"))
    return [pltpu.make_async_remote_copy(src_ref=ins[wi], dst_ref=lands[wi], send_sem=send_sems[wi], recv_sem=recv_sems[wi],
                                         device_id=sib, device_id_type=MESH_T) for wi in range(len(ins))]


def _sib_start(arrs, after, name):
    n, na = len(arrs), len(after)
    lands = [lax.empty(a.shape, a.dtype) for a in arrs]

    def body(*refs):
        outs = refs[2 * n + na:]
        for cp in _sibling_copies(refs[:n], refs[n:2 * n], outs[:n], outs[n:2 * n]):
            cp.start()
        outs[4 * n][...] = jnp.zeros((8, LANES), F32)

    hbm = lambda a: pltpu.HBM(a.shape, a.dtype)
    return pl.pallas_call(
        body, name=name,
        out_shape=[pltpu.SemaphoreType.DMA(())] * (2 * n) + [hbm(a) for a in arrs] * 2 + [jax.ShapeDtypeStruct((8, LANES), F32)],
        in_specs=[HBM_SPEC] * (2 * n) + [pl.BlockSpec(memory_space=pl.ANY)] * na,
        out_specs=[SEM_SPEC] * (2 * n) + [HBM_SPEC] * (2 * n) + [pl.BlockSpec(memory_space=pltpu.VMEM)],
        input_output_aliases={i: 2 * n + i for i in range(2 * n)},
        compiler_params=pltpu.CompilerParams(has_side_effects=EFFECT),
    )(*[pltpu.with_memory_space_constraint(a, pltpu.HBM) for a in list(arrs) + lands], *after)


def _sib_wait(handle, after, name):
    n = (len(handle) - 1) // 4
    sems, thru = handle[:2 * n], handle[2 * n:4 * n]

    def body(*refs):
        for cp in _sibling_copies(refs[:n], refs[n:2 * n], refs[2 * n:3 * n], refs[3 * n:4 * n]):
            cp.wait_send()
            cp.wait_recv()

    res = pl.pallas_call(
        body, name=name, out_shape=[pltpu.HBM(a.shape, a.dtype) for a in thru],
        in_specs=[HBM_SPEC] * (2 * n) + [SEM_SPEC] * (2 * n) + [pl.BlockSpec(memory_space=pl.ANY)],
        out_specs=[HBM_SPEC] * (2 * n), input_output_aliases={i: i for i in range(2 * n)},
        compiler_params=pltpu.CompilerParams(has_side_effects=EFFECT),
    )(*thru, *sems, after)
    return res[n:]


def _sibling_exchange(arrs, name):
    n = len(arrs)

    def body(*refs):
        ins, outs = refs[:n], refs[n:2 * n]
        send_sems, recv_sems = refs[2 * n:]
        sib = (lax.axis_index("x"), lax.axis_index("y"), 1 - lax.axis_index("c"))
        cps = [pltpu.make_async_remote_copy(src_ref=ins[wi], dst_ref=outs[wi], send_sem=send_sems.at[wi], recv_sem=recv_sems.at[wi],
                                            device_id=sib, device_id_type=MESH_T) for wi in range(n)]
        for cp in cps:
            cp.start()
        for cp in cps:
            cp.wait_recv()
        for cp in cps:
            cp.wait_send()

    anyspec = pl.BlockSpec(memory_space=pl.ANY)
    return pl.pallas_call(
        body, name=name, in_specs=[anyspec] * n, out_specs=[anyspec] * n,
        out_shape=[jax.ShapeDtypeStruct(a.shape, a.dtype) for a in arrs],
        scratch_shapes=[pltpu.SemaphoreType.DMA((n,)), pltpu.SemaphoreType.DMA((n,))],
        compiler_params=pltpu.CompilerParams(has_side_effects=True))(*arrs)


def _allreduce_small(packed, name):
    r = packed.shape[0]

    def body(in_ref, out_ref, land, send_sems, recv_sems):
        x, y, c = lax.axis_index("x"), lax.axis_index("y"), lax.axis_index("c")
        me = 4 * x + 2 * y + c
        land[me] = in_ref[...]
        rel = [(dx, dy, dc) for dx in (0, 1) for dy in (0, 1) for dc in (0, 1)][1:]
        peers = [((1 - x) if dx else x, (1 - y) if dy else y, (1 - c) if dc else c) for dx, dy, dc in rel]
        sends = []
        for k, peer in enumerate(peers):
            cp = pltpu.make_async_remote_copy(src_ref=in_ref, dst_ref=land.at[me], send_sem=send_sems.at[k], recv_sem=recv_sems.at[k],
                                              device_id=peer, device_id_type=MESH_T)
            cp.start()
            sends.append(cp)
        for k, (px, py, pc) in enumerate(peers):
            pltpu.make_async_remote_copy(src_ref=in_ref, dst_ref=land.at[4 * px + 2 * py + pc], send_sem=send_sems.at[k],
                                         recv_sem=recv_sems.at[k], device_id=(px, py, pc), device_id_type=MESH_T).wait_recv()
        for cp in sends:
            cp.wait_send()
        acc = land[0]
        for j in range(1, 8):
            acc = acc + land[j]
        out_ref[...] = acc

    vm = pl.BlockSpec(memory_space=pltpu.VMEM)
    return pl.pallas_call(
        body, name=name, in_specs=[vm], out_specs=vm, out_shape=jax.ShapeDtypeStruct((r, LANES), F32),
        scratch_shapes=[pltpu.VMEM((8, r, LANES), F32), pltpu.SemaphoreType.DMA((7,)), pltpu.SemaphoreType.DMA((7,))],
        compiler_params=pltpu.CompilerParams(has_side_effects=True))(packed)


def _in_splits():
    w = SSD_HEADS * SSD_HEAD_DIM
    cc = w + 2 * SSD_GROUPS * SSD_STATE
    return [w, cc, SSD_HEADS, MLA_Q_RANK, MLA_KV_RANK, MLA_ROPE]


def _padc(a, n):
    return jnp.pad(a, ((0, 0), (0, n - a.shape[1])))


def _win_pad(wm):
    offs = np.cumsum(_in_splits())[:-1]
    z, xbc, dt, cq, ckv, kr = jnp.split(wm, offs, axis=1)
    return jnp.concatenate([z, xbc, cq, ckv, _padc(kr, LANES), _padc(dt, LANES)], axis=1)


def _win_unpad(g):
    w, cc, nh, qr, kvr, rp = _in_splits()
    offs = np.cumsum([w, cc, qr, kvr, LANES])
    z, xbc, cq, ckv, kr, dt = jnp.split(g, offs, axis=1)
    return jnp.concatenate([z, xbc, dt[:, :nh], cq, ckv, kr[:, :rp]], axis=1)


def _wuq_pad(wm):
    r = wm.shape[0]
    w3 = wm.reshape(r, MLA_HEADS, MLA_NOPE + MLA_ROPE)
    return jnp.pad(w3, ((0, 0), (0, 0), (0, 2 * LANES - MLA_NOPE - MLA_ROPE))).reshape(r, MLA_HEADS * 2 * LANES)


def _wuq_unpad(g):
    r = g.shape[0]
    return g.reshape(r, MLA_HEADS, 2 * LANES)[:, :, :MLA_NOPE + MLA_ROPE].reshape(r, MLA_HEADS * (MLA_NOPE + MLA_ROPE))


def _full_from_gather(name, g):
    layers = g.shape[1]
    if name not in COL_SHARDED:
        return [g[:, l].reshape(-1, g.shape[-1]) for l in range(layers)]
    return [jnp.concatenate([g[j, l] for j in range(4)], axis=1) for l in range(layers)]


def _gathered_t(g):
    return g.transpose(0, 1, 3, 2).reshape(-1, g.shape[2])


def _shards_from_full(name, mats):
    col = name in COL_SHARDED
    per = []
    for mt in mats:
        r, c = mt.shape
        per.append(mt.reshape(r, 4, c // 4).transpose(1, 0, 2) if col else mt.reshape(4, r // 4, c))
    return jnp.stack(per, axis=1).astype(BF16)


def _pack_small(vals, extra=None):
    flat = jnp.concatenate([vals[n].reshape(-1).astype(F32) for n in SMALL] + ([extra.reshape(1)] if extra is not None else []))
    rows = -(-flat.shape[0] // (8 * LANES)) * 8
    return jnp.pad(flat, (0, rows * LANES - flat.shape[0])).reshape(rows, LANES)


def _unpack_small(packed, like):
    flat, out, off = packed.reshape(-1), {}, 0
    for n in SMALL:
        sz = like[n].size
        out[n] = flat[off:off + sz].reshape(like[n].shape)
        off += sz
    return out


def _ffn_fwd(x, norm_g, wg, wu, wd, tag):
    h, r = _rms_fwd(x, norm_g, f"ffn_norm_{tag}")
    gate = _mm(h, wg, "nn", f"ffn_gate_{tag}", out_dtype=BF16, b_shards=True)
    up = _mm(h, wu, "nn", f"ffn_up_{tag}", out_dtype=BF16, b_shards=True)
    act = _swiglu_fwd(gate, up, f"swiglu_{tag}")
    out = _mm(act, wd, "nn", f"ffn_down_{tag}", add=x)
    return out, (x, r, h, gate, up, act)


def _ffn_bwd(dout, doutb, norm_g, wg, wu, wd, saved, tag):
    x, r, h, gate, up, act = saved
    dact = _mm(doutb, wd.T, "nn", f"ffn_dact_{tag}", out_dtype=BF16)
    dgate, dup = _swiglu_bwd(gate, up, dact, f"swiglu_bwd_{tag}")
    ht = h.T
    d_wd = _mm(act.T, doutb, "nn", f"ffn_dwd_{tag}", out_dtype=BF16)
    d_wg = _mm(ht, dgate, "nn", f"ffn_dwg_{tag}", out_dtype=BF16, out_shards=True)
    d_wu = _mm(ht, dup, "nn", f"ffn_dwu_{tag}", out_dtype=BF16, out_shards=True)
    dh = _mm(dgate, _gathered_t(wg), "nn", f"ffn_dh1_{tag}")
    dh = _mm(dup, _gathered_t(wu), "nn", f"ffn_dh2_{tag}", add=dh)
    dx, dxb, dnorm = _rms_bwd(x, norm_g, r, dh, f"ffn_norm_bwd_{tag}", dres=dout)
    return dx, dxb, dnorm, d_wg, d_wu, d_wd


def kernel(x, mix_norm, ffn_norm, w_in, conv_w, conv_b, dt_bias, a_log, d_skip, ssd_norm, q_norm, kv_norm, w_uq, w_ukv, w_out_even, w_qkv, w_out_odd, w_gate, w_up, w_down, final_norm, loss_target, m_mix_norm, m_ffn_norm, m_w_in, m_conv_w, m_conv_b, m_dt_bias, m_a_log, m_d_skip, m_ssd_norm, m_q_norm, m_kv_norm, m_w_uq, m_w_ukv, m_w_out_even, m_w_qkv, m_w_out_odd, m_w_gate, m_w_up, m_w_down, m_final_norm, v_mix_norm, v_ffn_norm, v_w_in, v_conv_w, v_conv_b, v_dt_bias, v_a_log, v_d_skip, v_ssd_norm, v_q_norm, v_kv_norm, v_w_uq, v_w_ukv, v_w_out_even, v_w_qkv, v_w_out_odd, v_w_gate, v_w_up, v_w_down, v_final_norm):
    given = dict(locals())
    wts = {n: given[n] for n in WEIGHTS}
    x0 = x[0]
    tgt = loss_target[0]
    t, d = x0.shape
    hw = SSD_HEADS * SSD_HEAD_DIM
    gn = SSD_GROUPS * SSD_STATE
    cc = hw + 2 * gn
    qr, kvr = MLA_Q_RANK, MLA_KV_RANK

    def shard(n, layer=None):
        a = wts[n] if layer is None else wts[n][layer:layer + 1]
        return a if n == 'conv_w' else a.astype(BF16)
    g0_names = ['w_in', 'conv_w']
    gq_names = ['w_uq', 'w_ukv']
    g1_names = [('w_out_even', None), ('w_gate', 0), ('w_up', 0), ('w_down', 0)]
    g2_names = [('w_qkv', None), ('w_out_odd', None), ('w_gate', 1), ('w_up', 1), ('w_down', 1)]
    hg0 = _xstart([shard(n) for n in g0_names], True, [], "gather0_start")
    hgq = _xstart([shard(n) for n in gq_names], True, [hg0[-1]], "gatherq_start")
    hg1 = _xstart([shard(n, l) for n, l in g1_names], True, [hgq[-1]], "gather1_start")
    hg2 = _xstart([shard(n, l) for n, l in g2_names], True, [hg1[-1]], "gather2_start")
    full = lambda n, g: _full_from_gather(n, g)[0]
    g0 = dict(zip(g0_names, _xwait(hg0, True, hg2[-1], "gather0_wait")))
    win = _win_pad(full('w_in', g0['w_in']))
    cw = _padc(full('conv_w', g0['conv_w']).T, 8).T
    o_cq, o_ckv, o_kr, o_dt = hw + cc, hw + cc + qr, hw + cc + qr + kvr, hw + cc + qr + kvr + LANES

    row = lambda v: v.reshape(1, -1)
    narrow = lambda v: _padc(v.reshape(1, -1), LANES)
    dtb, alog = narrow(dt_bias[0]), narrow(a_log[0])
    dsk_x = jnp.repeat(d_skip[0], SSD_HEAD_DIM).reshape(1, hw)
    cos, sin = _rope_tables(t)

    h0, r0 = _rms_fwd(x0, row(mix_norm[0]), "mix_norm_0")
    u = _mm(h0, win, "nn", "in_proj")
    z, xbc, c_q, c_kv = u[:, :hw], u[:, hw:hw + cc], u[:, o_cq:o_ckv], u[:, o_ckv:o_kr]
    kr_raw, dtraw = u[:, o_kr:o_dt], u[:, o_dt:]
    xc = _conv_fwd(xbc, cw, row(conv_b[0]), "conv")
    y_ssd, sprev = _ssd_fwd(dtraw, xc, dtb, alog, dsk_x, "ssd")
    yg, r_g = _gated_fwd(y_ssd, z, row(ssd_norm[0]), "ssd_gate_norm")
    q_lat, r_q = _rms_fwd(c_q, row(q_norm[0]), "q_norm")
    kv_lat, r_kv = _rms_fwd(c_kv, row(kv_norm[0]), "kv_norm")
    gq = dict(zip(gq_names, _xwait(hgq, True, r_kv, "gatherq_wait")))
    wuq = _wuq_pad(full('w_uq', gq['w_uq']))
    wukv = gq['w_ukv']
    qf = _mm(q_lat, wuq, "nn", "q_up")
    kvb = _mm(kv_lat, wukv, "nn", "kv_up", out_dtype=BF16, b_shards=True)
    q_r = _rope(qf, cos, sin, 2, False, BF16, "rope_q")
    k_r = _rope(kr_raw, cos, sin, 1, False, BF16, "rope_k")
    o_mla, lse = _mla_fwd(q_r, k_r, kvb, "mla")
    g1 = dict(zip(g1_names, _xwait(hg1, True, lse, "gather1_wait")))
    woe, wd0 = full('w_out_even', g1[g1_names[0]]), full('w_down', g1[g1_names[3]])
    wg0, wu0 = g1[g1_names[1]], g1[g1_names[2]]
    cat = jnp.concatenate([yg, o_mla.astype(BF16)], axis=1)
    x1 = _mm(cat, woe, "nn", "mix_out_0", add=x0)
    x2, ffn0 = _ffn_fwd(x1, row(ffn_norm[0]), wg0, wu0, wd0, "0")

    g2 = dict(zip(g2_names, _xwait(hg2, True, x2, "gather2_wait")))
    woo, wd1 = full('w_out_odd', g2[g2_names[1]]), full('w_down', g2[g2_names[4]])
    wqkv, wg1, wu1 = g2[g2_names[0]], g2[g2_names[2]], g2[g2_names[3]]
    h1, r1 = _rms_fwd(x2, row(mix_norm[1]), "mix_norm_1")
    qkv = _mm(h1, wqkv, "nn", "qkv_proj", out_dtype=BF16, b_shards=True)
    o_sb, ltot = _sb_fwd(qkv, "sb")
    o_sbb = o_sb.astype(BF16)
    x3 = _mm(o_sbb, woo, "nn", "mix_out_1", add=x2)
    x4, ffn1 = _ffn_fwd(x3, row(ffn_norm[1]), wg1, wu1, wd1, "1")

    loss_part, dx4, dx4b, d_final = _loss_fwd_bwd(x4, row(final_norm), tgt, "loss")

    def grad_shards(pairs):
        return [g[:, None] if g.ndim == 3 else _shards_from_full(n, [g]) for n, g in pairs]

    dx3, dx3b, d_ffn1, d_wg1, d_wu1, d_wd1 = _ffn_bwd(dx4, dx4b, row(ffn_norm[1]), wg1, wu1, wd1, ffn1, "1")
    do_sb = _mm(dx3b, woo.T, "nn", "sb_dout")
    d_woo = _mm(o_sbb.T, dx3b, "nn", "d_w_out_odd", out_dtype=BF16)
    dq, dk, dv = _sb_bwd(qkv, do_sb, ltot, "sb_bwd")
    dqkv = jnp.concatenate([dq, dk, dv], axis=1).astype(BF16)
    d_wqkv = _mm(h1.T, dqkv, "nn", "d_w_qkv", out_dtype=BF16, out_shards=True)
    x2_names = ['w_qkv', 'w_out_odd', 'w_gate', 'w_up', 'w_down']
    hx2 = _xstart(grad_shards(zip(x2_names, [d_wqkv, d_woo, d_wg1, d_wu1, d_wd1])), False, [], "grads2_start")
    dh1 = _mm(dqkv, _gathered_t(wqkv), "nn", "qkv_dh")
    dx2, dx2b, d_mix1 = _rms_bwd(x2, row(mix_norm[1]) + hx2[-1][0:1, 0:1], r1, dh1, "mix_norm_bwd_1", dres=dx3)

    dx1, dx1b, d_ffn0, d_wg0, d_wu0, d_wd0 = _ffn_bwd(dx2, dx2b, row(ffn_norm[0]), wg0, wu0, wd0, ffn0, "0")
    d_woe = _mm(cat.T, dx1b, "nn", "d_w_out_even", out_dtype=BF16)
    x1_names = ['w_out_even', 'w_gate', 'w_up', 'w_down']
    hx1 = _xstart(grad_shards(zip(x1_names, [d_woe, d_wg0, d_wu0, d_wd0])), False, [], "grads1_start")
    tok1 = hx1[-1][0:1, 0:1]
    dcat = _mm(dx1b, woe.T, "nn", "mix_dcat")
    dy_ssd, dz, d_ssdn = _gated_bwd(y_ssd, z, row(ssd_norm[0]) + tok1, r_g, dcat[:, :hw], "ssd_gate_norm_bwd")
    dxc, ddtraw, d_dtb, d_alog, d_dsk = _ssd_bwd(dtraw, xc, dtb, alog, dsk_x, sprev, dy_ssd, "ssd_bwd")
    dxbc, d_cw, d_cb = _conv_bwd(xbc, cw, row(conv_b[0]), dxc, "conv_bwd")
    dqm, dkvm, dkr = _mla_bwd(q_r, k_r, kvb, o_mla, dcat[:, hw:] + tok1, lse, "mla_bwd")
    dqf = _rope(dqm, cos, sin, 2, True, BF16, "rope_q_bwd")
    dkr_raw = _rope(dkr, cos, sin, 1, True, F32, "rope_k_bwd")
    dkvf = dkvm.astype(BF16)
    d_wuq = _mm(q_lat.T, dqf, "nn", "d_w_uq", out_dtype=BF16)
    d_wukv = _mm(kv_lat.T, dkvf, "nn", "d_w_ukv", out_dtype=BF16, out_shards=True)
    dq_lat = _mm(dqf, wuq.T, "nn", "q_up_bwd")
    dkv_lat = _mm(dkvf, _gathered_t(wukv), "nn", "kv_up_bwd")
    dc_q, _, d_qn = _rms_bwd(c_q, row(q_norm[0]), r_q, dq_lat, "q_norm_bwd")
    dc_kv, _, d_kvn = _rms_bwd(c_kv, row(kv_norm[0]), r_kv, dkv_lat, "kv_norm_bwd")
    du = jnp.concatenate([dz, dxbc, dc_q, dc_kv, dkr_raw, ddtraw], axis=1).astype(BF16)
    d_win = _mm(h0.T, du, "nn", "d_w_in", out_dtype=BF16)
    x0_names = ['w_in', 'conv_w', 'w_uq', 'w_ukv']
    hx0 = _xstart(grad_shards(zip(x0_names, [_win_unpad(d_win), d_cw, _wuq_unpad(d_wuq), d_wukv])), False, [], "grads0_start")
    dh0 = _mm(du, win.T, "nn", "in_proj_bwd")
    grad_x, _, d_mix0 = _rms_bwd(x0, row(mix_norm[0]) + hx0[-1][0:1, 0:1], r0, dh0, "mix_norm_bwd_0", dres=dx1)

    def chip_sums(handle, names, after, tag):
        lands = _xwait(handle, False, after, f"grads{tag}_wait")
        return {n: _sum4(p.reshape(4, -1, p.shape[-1]), f"sum{tag}_{n}") for n, p in zip(names, lands)}
    grads, deltas, new_m, new_v = {}, {}, {}, {}

    def adamw(n, ga, gb):
        res = _adamw(_as2d(wts[n]), ga, gb, _as2d(given['m_' + n]), _as2d(given['v_' + n]), f"adamw_{n}")
        grads[n], deltas[n], new_m[n], new_v[n] = [r.reshape(wts[n].shape) for r in res]
        return res[0]
    s2 = chip_sums(hx2, x2_names, grad_x, "2")
    hs2 = _sib_start([s2[n] for n in x2_names], [], "sibling2_start")
    s1 = chip_sums(hx1, x1_names, hs2[-1], "1")
    hs1 = _sib_start([s1[n] for n in x1_names], [], "sibling1_start")
    r2 = dict(zip(x2_names, _sib_wait(hs2, hs1[-1], "sibling2_wait")))
    r1 = dict(zip(x1_names, _sib_wait(hs1, r2['w_qkv'], "sibling1_wait")))
    last = None
    for n in ('w_qkv', 'w_out_odd'):
        last = adamw(n, [s2[n]], [r2[n]])
    last = adamw('w_out_even', [s1['w_out_even']], [r1['w_out_even']])
    for n in ('w_gate', 'w_up', 'w_down'):
        last = adamw(n, [s1[n], s2[n]], [r1[n], r2[n]])
    s0 = chip_sums(hx0, x0_names, last, "0")
    r0_ = _sibling_exchange([s0[n] for n in x0_names], "exchange_sibling0")
    for n, gb in zip(x0_names, r0_):
        adamw(n, [s0[n]], [gb])

    nhs = SSD_HEADS
    small_g = {'mix_norm': jnp.stack([d_mix0, d_mix1]), 'ffn_norm': jnp.stack([d_ffn0, d_ffn1]), 'conv_b': d_cb[None],
               'dt_bias': d_dtb[None, :nhs], 'a_log': d_alog[None, :nhs], 'd_skip': d_dsk[None, :nhs], 'ssd_norm': d_ssdn[None],
               'q_norm': d_qn[None], 'kv_norm': d_kvn[None], 'final_norm': d_final}
    packed, _ = lax.optimization_barrier((_pack_small(small_g, loss_part), r0_[0]))
    g_small = _allreduce_small(packed, "allreduce_small")
    loss = g_small.reshape(-1)[sum(wts[n].size for n in SMALL)]
    pk = lambda pre: _pack_small({n: given[pre + n] for n in SMALL}, jnp.zeros((), F32))
    sg, sd, sm, sv = _adamw(pk(''), [g_small], None, pk('m_'), pk('v_'), "adamw_small")
    for dst, src in ((grads, sg), (deltas, sd), (new_m, sm), (new_v, sv)):
        dst.update(_unpack_small(src, wts))

    outs = [loss, grad_x[None]]
    for dct in (grads, deltas, new_m, new_v):
        outs += [dct[n] for n in WEIGHTS]
    return tuple(outs)
```
